```python
import math
import jax
import jax.numpy as jnp
from jax import lax
import numpy as np

D_MODEL = 1024
BATCH = 8
SEQ = 4096
DEPTH = 1

F32 = jnp.float32
RMS_EPS = 1e-6
N_ATTN_HEADS = 8
ATTN_HEAD_DIM = 64
ATTN_W = N_ATTN_HEADS * ATTN_HEAD_DIM
IDX_HEADS = 16
IDX_DIM = 32
IDX_Q = IDX_HEADS * IDX_DIM
TOPK_MAX = 256
Q_BLOCK = 64
N_BUCKETS = 32
MAX_DISTANCE = 128
RWKV_HEADS = 8
RWKV_HEAD = 64
RWKV_W = RWKV_HEADS * RWKV_HEAD
DECAY_LORA = 64
ICLR_LORA = 64
GATE_LORA = 128
RWKV_IN = 3 * RWKV_W + DECAY_LORA + ICLR_LORA + GATE_LORA
GN_EPS = 64e-5
RWKV_START = 3 * ATTN_W + IDX_Q + IDX_DIM + IDX_HEADS
N_IN = RWKV_START + RWKV_IN + 2 * D_MODEL
IN_SPLITS = (ATTN_W, 2 * ATTN_W, 3 * ATTN_W, 3 * ATTN_W + IDX_Q,
             3 * ATTN_W + IDX_Q + IDX_DIM, RWKV_START,
             RWKV_START + RWKV_IN, RWKV_START + RWKV_IN + D_MODEL)
RWKV_SPLITS = (RWKV_W, 2 * RWKV_W, 3 * RWKV_W, 3 * RWKV_W + DECAY_LORA,
               3 * RWKV_W + DECAY_LORA + ICLR_LORA)
N_EXPERTS = 64
N_GROUPS = 8
TOPK_GROUPS = 4
MOE_TOPK = 8
EXPERT_FF = 256
ROUTED_SCALE = 2.5
EXPERT_BLOCK = 128

kernel_name = 'hybrid_dsa_rwkv7_moe_block'


def rms_norm(x, g):
    xf = x.astype(F32)
    y = xf * lax.rsqrt(jnp.mean(xf * xf, axis=-1, keepdims=True) + RMS_EPS)
    return (y * g.astype(F32)).astype(x.dtype)


def t5_bucket(rel):
    n = jnp.maximum(rel, 0)
    max_exact = N_BUCKETS // 2
    nf = jnp.maximum(n, 1).astype(F32)
    large = max_exact + (jnp.log(nf / max_exact) / math.log(MAX_DISTANCE / max_exact)
                         * (N_BUCKETS - max_exact)).astype(jnp.int32)
    large = jnp.minimum(large, N_BUCKETS - 1)
    return jnp.where(n < max_exact, n, large)


def dsa_sparse_attention(q, k, v, q_idx, k_idx, w_idx, rel_bias):
    B, S = q.shape[0], q.shape[1]
    top_k = min(TOPK_MAX, S // 4)
    nb = S // Q_BLOCK
    k_flat = k.reshape(B, S, ATTN_W)
    v_flat = v.reshape(B, S, ATTN_W)
    key_pos = jnp.arange(S, dtype=jnp.int32)
    idx_scale = (IDX_HEADS ** -0.5) * (IDX_DIM ** -0.5)

    def to_blocks(t):
        return jnp.moveaxis(t.reshape((B, nb, Q_BLOCK) + t.shape[2:]), 1, 0)

    def block(args):
        qb, qib, wib, start = args
        q_pos = start + jnp.arange(Q_BLOCK, dtype=jnp.int32)
        dots = jnp.einsum('bqhd,bsd->bqhs', qib, k_idx, preferred_element_type=F32)
        idx_score = jnp.einsum('bqhs,bqh->bqs', jax.nn.relu(dots), wib.astype(F32)) * idx_scale
        causal = key_pos[None, :] <= q_pos[:, None]
        idx_score = jnp.where(causal[None], idx_score, -jnp.inf)
        _, sel = lax.top_k(idx_score, top_k)
        kg = jax.vmap(lambda kf, i: kf[i])(k_flat, sel).reshape(B, Q_BLOCK, top_k, N_ATTN_HEADS, ATTN_HEAD_DIM)
        vg = jax.vmap(lambda vf, i: vf[i])(v_flat, sel).reshape(B, Q_BLOCK, top_k, N_ATTN_HEADS, ATTN_HEAD_DIM)
        logits = jnp.einsum('bqhd,bqkhd->bqhk', qb, kg, preferred_element_type=F32) * (ATTN_HEAD_DIM ** -0.5)
        rel = q_pos[None, :, None] - sel
        bias = rel_bias[t5_bucket(rel)].astype(F32)
        logits = logits + jnp.moveaxis(bias, -1, 2)
        valid = sel <= q_pos[None, :, None]
        logits = jnp.where(valid[:, :, None, :], logits, -jnp.inf)
        p = jax.nn.softmax(logits, axis=-1).astype(vg.dtype)
        return jnp.einsum('bqhk,bqkhd->bqhd', p, vg)

    starts = jnp.arange(nb, dtype=jnp.int32) * Q_BLOCK
    out = lax.map(block, (to_blocks(q), to_blocks(q_idx), to_blocks(w_idx), starts))
    return jnp.moveaxis(out, 0, 1).reshape(B, S, ATTN_W)


def wkv7_scan(r, decay, k, v, a, b):
    B, S, H, N = r.shape

    def step(state, inp):
        r_t, w_t, k_t, v_t, a_t, b_t = inp
        sa = jnp.einsum('bhij,bhj->bhi', state, a_t)
        state = state * w_t[:, :, None, :] + sa[..., None] * b_t[:, :, None, :] + v_t[..., None] * k_t[:, :, None, :]
        y = jnp.einsum('bhij,bhj->bhi', state, r_t)
        return state, y

    xs = tuple(jnp.moveaxis(t, 1, 0) for t in (r, decay, k, v, a, b))
    _, ys = lax.scan(step, jnp.zeros((B, H, N, N), F32), xs)
    return jnp.moveaxis(ys, 0, 1)


def rwkv7_branch(z, tshift_mu, decay_w0, decay_up, iclr_a0, iclr_up, gate_up, k_k, k_a, r_k, lnx_g, lnx_b):
    B, S = z.shape[0], z.shape[1]
    z_prev = jnp.pad(z[:, :-1], ((0, 0), (1, 0), (0, 0)))
    z = (z + tshift_mu * (z_prev - z)).astype(F32)
    r, k, v, wd, ad, gd = jnp.split(z, RWKV_SPLITS, axis=-1)
    w_log = -jax.nn.softplus(-(decay_w0.astype(F32) + jnp.tanh(wd) @ decay_up.astype(F32))) - 0.5
    decay = jnp.exp(-jnp.exp(w_log))
    a = jax.nn.sigmoid(iclr_a0.astype(F32) + ad @ iclr_up.astype(F32))
    g = jax.nn.sigmoid(gd) @ gate_up.astype(F32)
    heads = lambda t: t.reshape(B, S, RWKV_HEADS, RWKV_HEAD)
    kk = heads(k * k_k.astype(F32))
    kk = kk / jnp.maximum(jnp.sqrt(jnp.sum(kk * kk, axis=-1, keepdims=True)), 1e-12)
    k = k * (1.0 + (a - 1.0) * k_a.astype(F32))
    r, k, v, decay, a = heads(r), heads(k), heads(v), heads(decay), heads(a)
    y = wkv7_scan(r, decay, k, v, -kk, kk * a)
    mu = jnp.mean(y, axis=-1, keepdims=True)
    var = jnp.mean(jnp.square(y - mu), axis=-1, keepdims=True)
    yn = ((y - mu) * lax.rsqrt(var + GN_EPS)).reshape(B, S, RWKV_W)
    yn = yn * lnx_g.astype(F32) + lnx_b.astype(F32)
    bonus = (jnp.sum(r * k * r_k.astype(F32), axis=-1, keepdims=True) * v).reshape(B, S, RWKV_W)
    return (yn + bonus) * g


def swiglu(h, wg, wu, wd):
    return (jax.nn.silu(h @ wg) * (h @ wu)) @ wd


def moe_ffn(h, router_w, router_bias, exp_gate, exp_up, exp_down, sh_gate, sh_up, sh_down):
    B, S, D = h.shape
    T = B * S
    hf = h.reshape(T, D)
    scores = jax.nn.sigmoid(jnp.einsum('td,de->te', hf, router_w, preferred_element_type=F32))
    choice = scores + router_bias.astype(F32)
    grp_score = lax.top_k(choice.reshape(T, N_GROUPS, N_EXPERTS // N_GROUPS), 2)[0].sum(-1)
    _, top_g = lax.top_k(grp_score, TOPK_GROUPS)
    gmask = jnp.any(top_g[:, :, None] == jnp.arange(N_GROUPS)[None, None, :], axis=1)
    emask = jnp.repeat(gmask, N_EXPERTS // N_GROUPS, axis=1)
    _, top_e = lax.top_k(jnp.where(emask, choice, -jnp.inf), MOE_TOPK)
    gw = jnp.take_along_axis(scores, top_e, axis=1)
    gw = gw / jnp.sum(gw, axis=-1, keepdims=True) * ROUTED_SCALE
    A = T * MOE_TOPK
    flat_e = top_e.reshape(A)
    order = jnp.argsort(flat_e)
    se = flat_e[order]
    counts = jnp.bincount(flat_e, length=N_EXPERTS)
    padded = (counts + EXPERT_BLOCK - 1) // EXPERT_BLOCK * EXPERT_BLOCK
    start = jnp.cumsum(counts) - counts
    pend = jnp.cumsum(padded)
    pstart = pend - padded
    dest = pstart[se] + jnp.arange(A, dtype=jnp.int32) - start[se]
    n_blocks = -(-A // EXPERT_BLOCK) + N_EXPERTS
    P = n_blocks * EXPERT_BLOCK
    slot_tok = jnp.zeros((P,), jnp.int32).at[dest].set((order // MOE_TOPK).astype(jnp.int32))
    slot_w = jnp.zeros((P,), F32).at[dest].set(gw.reshape(A)[order])
    block_start = jnp.arange(n_blocks, dtype=jnp.int32) * EXPERT_BLOCK
    block_e = jnp.minimum(jnp.sum(pend[None, :] <= block_start[:, None], axis=1), N_EXPERTS - 1)

    def step(y, blk):
        tok, wt, e = blk
        out = swiglu(hf[tok], exp_gate[e], exp_up[e], exp_down[e])
        return y.at[tok].add(out.astype(F32) * wt[:, None]), None

    y, _ = lax.scan(step, jnp.zeros((T, D), F32),
                    (slot_tok.reshape(n_blocks, EXPERT_BLOCK), slot_w.reshape(n_blocks, EXPERT_BLOCK), block_e))
    shared = swiglu(hf, sh_gate, sh_up, sh_down)
    return (y.astype(h.dtype) + shared).reshape(B, S, D)


def hybrid_layer(x, c, ada_w, ada_b, norm1_g, w_in, rel_bias, tshift_mu, decay_w0, decay_up,
                 iclr_a0, iclr_up, gate_up, k_k, k_a, r_k, lnx_g, lnx_b, w_attn_br, w_rwkv_br,
                 w_out, norm2_g, router_w, router_bias, exp_gate, exp_up, exp_down,
                 sh_gate, sh_up, sh_down):
    B, S, D = x.shape
    mod = (jax.nn.silu(c.astype(F32)) @ ada_w.astype(F32) + ada_b.astype(F32)).astype(x.dtype)
    sh1, sc1, g1, sh2, sc2, g2 = [m[:, None, :] for m in jnp.split(mod, 6, axis=-1)]
    h = rms_norm(x, norm1_g) * (1 + sc1) + sh1
    z = h @ w_in
    q, k, v, iq, ik, iw, zr, ga, gr = jnp.split(z, IN_SPLITS, axis=-1)
    hs = lambda t: t.reshape(B, S, N_ATTN_HEADS, ATTN_HEAD_DIM)
    attn = dsa_sparse_attention(hs(q), hs(k), hs(v), iq.reshape(B, S, IDX_HEADS, IDX_DIM), ik, iw, rel_bias)
    rw = rwkv7_branch(zr, tshift_mu, decay_w0, decay_up, iclr_a0, iclr_up, gate_up,
                      k_k, k_a, r_k, lnx_g, lnx_b).astype(x.dtype)
    mixed = jax.nn.sigmoid(ga) * (attn @ w_attn_br) + jax.nn.sigmoid(gr) * (rw @ w_rwkv_br)
    x = x + g1 * (mixed @ w_out)
    h2 = rms_norm(x, norm2_g) * (1 + sc2) + sh2
    x = x + g2 * moe_ffn(h2, router_w, router_bias, exp_gate, exp_up, exp_down, sh_gate, sh_up, sh_down)
    return x


def setup_inputs(seed: int = 0) -> dict:
    key = jax.random.key(seed)
    ks = iter(jax.random.split(key, 40))
    L, D = DEPTH, D_MODEL

    def nrm(shape, scale):
        return jax.random.normal(next(ks), shape, F32) * scale

    def uni(shape, lo, hi):
        return jax.random.uniform(next(ks), shape, F32, lo, hi)

    return {
        'x': nrm((BATCH, SEQ, D), 1.0),
        'c': nrm((BATCH, D), 1.0),
        'ada_w': nrm((L, D, 6 * D), 0.5 * D ** -0.5),
        'ada_b': nrm((L, 6 * D), 0.02),
        'norm1_g': 1.0 + nrm((L, D), 0.02),
        'w_in': nrm((L, D, N_IN), D ** -0.5),
        'rel_bias': nrm((N_BUCKETS, N_ATTN_HEADS), 0.5),
        'tshift_mu': uni((L, RWKV_IN), 0.0, 1.0),
        'decay_w0': uni((L, RWKV_W), -3.0, 1.0),
        'decay_up': nrm((L, DECAY_LORA, RWKV_W), 0.1),
        'iclr_a0': nrm((L, RWKV_W), 0.5),
        'iclr_up': nrm((L, ICLR_LORA, RWKV_W), ICLR_LORA ** -0.5),
        'gate_up': nrm((L, GATE_LORA, RWKV_W), GATE_LORA ** -0.5),
        'k_k': 0.85 + nrm((L, RWKV_W), 0.02),
        'k_a': 1.0 + nrm((L, RWKV_W), 0.02),
        'r_k': nrm((L, RWKV_HEADS, RWKV_HEAD), 0.1),
        'lnx_g': 1.0 + nrm((L, RWKV_W), 0.02),
        'lnx_b': nrm((L, RWKV_W), 0.02),
        'w_attn_br': nrm((L, ATTN_W, D), ATTN_W ** -0.5),
        'w_rwkv_br': nrm((L, RWKV_W, D), RWKV_W ** -0.5),
        'w_out': nrm((L, D, D), D ** -0.5),
        'norm2_g': 1.0 + nrm((L, D), 0.02),
        'router_w': nrm((L, D, N_EXPERTS), D ** -0.5),
        'router_bias': nrm((L, N_EXPERTS), 0.01),
        'exp_gate': nrm((L, N_EXPERTS, D, EXPERT_FF), D ** -0.5),
        'exp_up': nrm((L, N_EXPERTS, D, EXPERT_FF), D ** -0.5),
        'exp_down': nrm((L, N_EXPERTS, EXPERT_FF, D), EXPERT_FF ** -0.5),
        'sh_gate': nrm((L, D, EXPERT_FF), D ** -0.5),
        'sh_up': nrm((L, D, EXPERT_FF), D ** -0.5),
        'sh_down': nrm((L, EXPERT_FF, D), EXPERT_FF ** -0.5),
        'final_g': 1.0 + nrm((D,), 0.02),
    }


def reference(x, c, ada_w, ada_b, norm1_g, w_in, rel_bias, tshift_mu, decay_w0, decay_up,
              iclr_a0, iclr_up, gate_up, k_k, k_a, r_k, lnx_g, lnx_b, w_attn_br, w_rwkv_br,
              w_out, norm2_g, router_w, router_bias, exp_gate, exp_up, exp_down,
              sh_gate, sh_up, sh_down, final_g):
    for l in range(DEPTH):
        x = hybrid_layer(x, c, ada_w[l], ada_b[l], norm1_g[l], w_in[l], rel_bias, tshift_mu[l],
                         decay_w0[l], decay_up[l], iclr_a0[l], iclr_up[l], gate_up[l], k_k[l],
                         k_a[l], r_k[l], lnx_g[l], lnx_b[l], w_attn_br[l], w_rwkv_br[l], w_out[l],
                         norm2_g[l], router_w[l], router_bias[l], exp_gate[l], exp_up[l],
                         exp_down[l], sh_gate[l], sh_up[l], sh_down[l])
    return rms_norm(x, final_g)
```

```python
import functools
import math

import jax
import jax.numpy as jnp
from jax import lax
from jax.experimental import pallas as pl
from jax.experimental.pallas import tpu as pltpu

F32 = jnp.float32
BF16 = jnp.bfloat16
I32 = jnp.int32

RMS_EPS = 1e-6
D_MODEL = 1024
N_ATTN_HEADS = 8
ATTN_HEAD_DIM = 64
ATTN_W = 512
IDX_HEADS = 16
IDX_DIM = 32
IDX_Q = 512
TOPK_MAX = 256
N_BUCKETS = 32
MAX_DISTANCE = 128
RWKV_HEADS = 8
RWKV_HEAD = 64
RWKV_W = 512
DECAY_LORA = 64
ICLR_LORA = 64
GATE_LORA = 128
RWKV_IN = 1792
GN_EPS = 64e-5
N_EXPERTS = 64
N_GROUPS = 8
TOPK_GROUPS = 4
MOE_TOPK = 8
EXPERT_FF = 256
ROUTED_SCALE = 2.5

LANES = 128
VMEM_LIMIT = 56 * 1024 * 1024
CHUNK = 64
INT_MIN = -2147483648
KEY_NEG_INF = -2139095041

NT_DIMS = (((1,), (1,)), ((), ()))


def _cparams(sem):
    return pltpu.CompilerParams(dimension_semantics=sem, vmem_limit_bytes=VMEM_LIMIT)


def _dot(a, b):
    return jnp.dot(a, b, preferred_element_type=F32)


def _dot_nt(a, b):
    return lax.dot_general(a, b, NT_DIMS, preferred_element_type=F32)


def _split2(x):
    hi = x.astype(BF16)
    lo = (x - hi.astype(F32)).astype(BF16)
    return hi, lo


def _split3(x):
    hi = x.astype(BF16)
    r1 = x - hi.astype(F32)
    mid = r1.astype(BF16)
    lo = (r1 - mid.astype(F32)).astype(BF16)
    return hi, mid, lo


def _dot_exact_rhs(x, ones_bf16, terms=2):
    parts = _split3(x) if terms == 3 else _split2(x)
    out = _dot(parts[0], ones_bf16)
    for p in parts[1:]:
        out = out + _dot(p, ones_bf16)
    return out


def _dot3(a, b, nt=False):
    ah, al = _split2(a)
    bh, bl = _split2(b)
    f = _dot_nt if nt else _dot
    return f(ah, bh) + f(ah, bl) + f(al, bh)


def _sigmoid(x):
    return 1.0 / (1.0 + jnp.exp(-x))


def _mod_kernel(c_ref, w_ref, b_ref, o_ref):
    c = c_ref[...]
    s = c * _sigmoid(c)
    o_ref[...] = _dot3(s, w_ref[...]) + b_ref[...]


def _mod_call(c, ada_w, ada_b):
    B, D = c.shape
    N = ada_w.shape[1]
    tn = 1024
    return pl.pallas_call(
        _mod_kernel,
        grid=(N // tn,),
        in_specs=[pl.BlockSpec((B, D), lambda j: (0, 0)),
                  pl.BlockSpec((D, tn), lambda j: (0, j)),
                  pl.BlockSpec((1, tn), lambda j: (0, j))],
        out_specs=pl.BlockSpec((B, tn), lambda j: (0, j)),
        out_shape=jax.ShapeDtypeStruct((B, N), F32),
        compiler_params=_cparams(("arbitrary",)),
        name="mod",
    )(c, ada_w, ada_b.reshape(1, N))


_OFF_Q, _OFF_K, _OFF_V, _OFF_IQ, _OFF_IKW, _OFF_ZR, _OFF_GA, _OFF_GR, _N_PACK = (
    0, 512, 1024, 1536, 2048, 2176, 3968, 4992, 6016)


def _pack_w_in(w_in):
    D = w_in.shape[0]
    ikw = jnp.pad(w_in[:, 2048:2096], ((0, 0), (0, LANES - 48)))
    return jnp.concatenate([w_in[:, :2048], ikw, w_in[:, 2096:]], axis=1).astype(BF16)


def _inproj_kernel(x_ref, g_ref, sc_ref, sh_ref, w_ref,
                   q_ref, k_ref, v_ref, iq_ref, ikw_ref, zr_ref, ga_ref, gr_ref):
    x = x_ref[...]
    ms = jnp.mean(x * x, axis=-1, keepdims=True)
    h = x * lax.rsqrt(ms + RMS_EPS) * g_ref[...]
    h = h * (1.0 + sc_ref[0]) + sh_ref[0]
    hb = h.astype(BF16)

    def proj(lo, hi):
        return _dot(hb, w_ref[:, lo:hi])

    q_ref[...] = (proj(_OFF_Q, _OFF_K) * (ATTN_HEAD_DIM ** -0.5)).astype(BF16)
    k_ref[...] = proj(_OFF_K, _OFF_V).astype(BF16)
    v_ref[...] = proj(_OFF_V, _OFF_IQ).astype(BF16)
    iq_ref[...] = proj(_OFF_IQ, _OFF_IKW).astype(BF16)
    ikw_ref[...] = proj(_OFF_IKW, _OFF_ZR)
    zr_ref[...] = proj(_OFF_ZR, _OFF_GA)
    ga_ref[...] = proj(_OFF_GA, _OFF_GR).astype(BF16)
    gr_ref[...] = proj(_OFF_GR, _N_PACK).astype(BF16)


def _inproj_call(x2, norm_g, sc, sh, w_pack, S):
    T, D = x2.shape
    B = T // S
    tm = min(512, S)
    tpb = S // tm
    row = lambda i: (i, 0)
    per_b = lambda i: (i // tpb, 0, 0)
    widths = (512, 512, 512, 512, LANES, RWKV_IN, D, D)
    dtypes = (BF16, BF16, BF16, BF16, F32, F32, BF16, BF16)
    return pl.pallas_call(
        _inproj_kernel,
        grid=(T // tm,),
        in_specs=[pl.BlockSpec((tm, D), row),
                  pl.BlockSpec((1, D), lambda i: (0, 0)),
                  pl.BlockSpec((1, 1, D), per_b),
                  pl.BlockSpec((1, 1, D), per_b),
                  pl.BlockSpec((D, _N_PACK), lambda i: (0, 0), pipeline_mode=pl.Buffered(1))],
        out_specs=[pl.BlockSpec((tm, w), row) for w in widths],
        out_shape=[jax.ShapeDtypeStruct((T, w), dt) for w, dt in zip(widths, dtypes)],
        compiler_params=_cparams(("parallel",)),
        name="inproj",
    )(x2, norm_g.reshape(1, D), sc.reshape(B, 1, D), sh.reshape(B, 1, D), w_pack)


def _t5_bucket(rel):
    n = jnp.maximum(rel, 0)
    max_exact = N_BUCKETS // 2
    nf = jnp.maximum(n, 1).astype(F32)
    large = max_exact + (jnp.log(nf / max_exact) / math.log(MAX_DISTANCE / max_exact)
                         * (N_BUCKETS - max_exact)).astype(I32)
    large = jnp.minimum(large, N_BUCKETS - 1)
    return jnp.where(n < max_exact, n, large)


def _bias_kernel(bucket_ref, rb_ref, o_ref):
    h = pl.program_id(0)
    bk = bucket_ref[...]
    out = jnp.zeros(bk.shape, F32)
    for b in range(N_BUCKETS):
        out = jnp.where(bk == b, rb_ref[b, h], out)
    o_ref[0] = out


def _bias_call(rel_bias, tq):
    r = jnp.arange(tq, dtype=I32)[:, None]
    c = jnp.arange(tq, dtype=I32)[None, :]
    buckets = jnp.stack([_t5_bucket(r - c), _t5_bucket(tq + r - c)])
    return pl.pallas_call(
        _bias_kernel,
        grid=(N_ATTN_HEADS,),
        in_specs=[pl.BlockSpec((2, tq, tq), lambda h: (0, 0, 0)),
                  pl.BlockSpec(memory_space=pltpu.SMEM)],
        out_specs=pl.BlockSpec((1, 2, tq, tq), lambda h: (h, 0, 0, 0)),
        out_shape=jax.ShapeDtypeStruct((N_ATTN_HEADS, 2, tq, tq), F32),
        compiler_params=_cparams(("arbitrary",)),
        name="bias",
    )(buckets, rel_bias)


def _index_kernel(iq_ref, iw_ref, ik_ref, mask_ref, wb_ref, key_ref, *, tq, tk, nk, top_k, scale):
    qi = pl.program_id(1)
    nkt = (qi * tq + tq + tk - 1) // tk
    qsub = min(tq, 128)
    ksub = min(tk, 256)

    for h in range(IDX_HEADS):
        wb_ref[h] = jnp.broadcast_to(iw_ref[0, :, h:h + 1], (tq, LANES))

    def score_tile(kt, carry):
        kbase = pl.multiple_of(kt * tk, tk)
        for qs in range(tq // qsub):
            rows = slice(qs * qsub, (qs + 1) * qsub)
            for ks in range(tk // ksub):
                ikt = ik_ref[0, pl.ds(kbase + ks * ksub, ksub), :]
                acc = jnp.zeros((qsub, ksub), F32)
                for h in range(IDX_HEADS):
                    d = _dot_nt(iq_ref[0, h, rows, :], ikt)
                    w = wb_ref[h, rows, :]
                    wfull = jnp.concatenate([w] * (ksub // LANES), axis=1) if ksub > LANES else w[:, :ksub]
                    acc = acc + jnp.maximum(d, 0.0) * wfull
                s = acc * scale
                qpos = qi * tq + qs * qsub + lax.broadcasted_iota(I32, (qsub, ksub), 0)
                kpos = kt * tk + ks * ksub + lax.broadcasted_iota(I32, (qsub, ksub), 1)
                s = jnp.where(kpos <= qpos, s, -jnp.inf)
                bits = pltpu.bitcast(s, I32)
                key_ref[kt, rows, ks * ksub:(ks + 1) * ksub] = bits ^ ((bits >> 31) & 0x7FFFFFFF)
        return carry

    lax.fori_loop(0, nkt, score_tile, 0)

    def bit_body(i, ans):
        cand = ans | lax.shift_left(jnp.int32(1), 31 - i)
        cand_s = cand ^ INT_MIN

        def cnt_body(kt, acc):
            one = jnp.where(key_ref[kt] >= cand_s, 1.0, 0.0)
            for l in range(tk // LANES):
                acc = acc + one[:, l * LANES:(l + 1) * LANES]
            return acc

        acc = lax.fori_loop(0, nkt, cnt_body, jnp.zeros((tq, LANES), F32))
        cnt = jnp.sum(acc, axis=1, keepdims=True)
        return jnp.where(cnt >= float(top_k), cand, ans)

    ans = lax.fori_loop(0, 32, bit_body, jnp.zeros((tq, 1), I32))
    thr = ans ^ INT_MIN

    def mask_body(kt, carry):
        keys = key_ref[kt]
        sel = (keys >= thr) & (keys > KEY_NEG_INF)
        mask_ref[0, 0, kt] = jnp.where(sel, 0.0, -jnp.inf).astype(BF16)
        return carry

    lax.fori_loop(0, nkt, mask_body, 0)

    def fill_body(kt, carry):
        mask_ref[0, 0, kt] = jnp.full((tq, tk), -jnp.inf, BF16)
        return carry

    lax.fori_loop(nkt, nk, fill_body, 0)


def _index_call(iq_h, iw, ik, tq, tk, top_k):
    B, _, S, _ = iq_h.shape
    nq, nk = S // tq, S // tk
    scale = (IDX_HEADS ** -0.5) * (IDX_DIM ** -0.5)
    kern = functools.partial(_index_kernel, tq=tq, tk=tk, nk=nk, top_k=top_k, scale=scale)
    return pl.pallas_call(
        kern,
        grid=(B, nq),
        in_specs=[pl.BlockSpec((1, IDX_HEADS, tq, IDX_DIM), lambda b, i: (b, 0, i, 0)),
                  pl.BlockSpec((1, tq, IDX_HEADS), lambda b, i: (b, i, 0)),
                  pl.BlockSpec((1, S, IDX_DIM), lambda b, i: (b, 0, 0))],
        out_specs=pl.BlockSpec((1, 1, nk, tq, tk), lambda b, i: (b, i, 0, 0, 0)),
        out_shape=jax.ShapeDtypeStruct((B, nq, nk, tq, tk), BF16),
        scratch_shapes=[pltpu.VMEM((IDX_HEADS, tq, LANES), F32),
                        pltpu.VMEM((nk, tq, tk), I32)],
        compiler_params=_cparams(("parallel", "arbitrary")),
        name="index",
    )(iq_h, iw, ik)


def _attn_kernel(q_ref, k_ref, v_ref, mask_ref, bias_ref, rb_ref, o_ref, m_ref, l_ref, acc_ref, *, nk):
    qi = pl.program_id(1)
    kt = pl.program_id(2)

    @pl.when(kt == 0)
    def _():
        m_ref[...] = jnp.full(m_ref.shape, -jnp.inf, F32)
        l_ref[...] = jnp.zeros(l_ref.shape, F32)
        acc_ref[...] = jnp.zeros(acc_ref.shape, F32)

    def step(bias_of_head):
        maskf = mask_ref[0, 0, 0].astype(F32)
        for h in range(N_ATTN_HEADS):
            s = _dot_nt(q_ref[0, h], k_ref[0, h]) + bias_of_head(h) + maskf
            m_old = m_ref[h]
            m_new = jnp.maximum(m_old, jnp.max(s, axis=1, keepdims=True))
            m_safe = jnp.where(m_new == -jnp.inf, 0.0, m_new)
            alpha = jnp.exp(m_old - m_safe)
            p = jnp.exp(s - m_safe)
            l_ref[h] = alpha * l_ref[h] + jnp.sum(p, axis=1, keepdims=True)
            acc_ref[h] = alpha * acc_ref[h] + _dot(p.astype(BF16), v_ref[0, h])
            m_ref[h] = m_new

    @pl.when(kt == qi)
    def _():
        step(lambda h: bias_ref[h, 0])

    @pl.when(kt == qi - 1)
    def _():
        step(lambda h: bias_ref[h, 1])

    @pl.when(kt < qi - 1)
    def _():
        step(lambda h: rb_ref[N_BUCKETS - 1, h])

    @pl.when(kt == nk - 1)
    def _():
        for h in range(N_ATTN_HEADS):
            o_ref[0, h] = (acc_ref[h] / l_ref[h]).astype(BF16)


def _attn_call(q_h, k_h, v_h, mask, bias_tiles, rel_bias, t):
    B, H, S, dh = q_h.shape
    n = S // t
    kern = functools.partial(_attn_kernel, nk=n)
    kv_map = lambda b, i, j: (b, 0, jnp.minimum(j, i), 0)
    return pl.pallas_call(
        kern,
        grid=(B, n, n),
        in_specs=[pl.BlockSpec((1, H, t, dh), lambda b, i, j: (b, 0, i, 0)),
                  pl.BlockSpec((1, H, t, dh), kv_map),
                  pl.BlockSpec((1, H, t, dh), kv_map),
                  pl.BlockSpec((1, 1, 1, t, t), lambda b, i, j: (b, i, jnp.minimum(j, i), 0, 0)),
                  pl.BlockSpec((H, 2, t, t), lambda b, i, j: (0, 0, 0, 0), pipeline_mode=pl.Buffered(1)),
                  pl.BlockSpec(memory_space=pltpu.SMEM)],
        out_specs=pl.BlockSpec((1, H, t, dh), lambda b, i, j: (b, 0, i, 0)),
        out_shape=jax.ShapeDtypeStruct((B, H, S, dh), BF16),
        scratch_shapes=[pltpu.VMEM((H, t, 1), F32),
                        pltpu.VMEM((H, t, 1), F32),
                        pltpu.VMEM((H, t, dh), F32)],
        compiler_params=_cparams(("parallel", "parallel", "arbitrary")),
        name="attn",
    )(q_h, k_h, v_h, mask, bias_tiles, rel_bias)


def _blockdiag_rows(x):
    lane = lax.broadcasted_iota(I32, x.shape, 1)
    zero = jnp.zeros_like(x)
    return jnp.concatenate([jnp.where(lane < RWKV_HEAD, x, zero),
                            jnp.where(lane >= RWKV_HEAD, x, zero)], axis=0)


def _rwkv_kernel(z_ref, mu_ref, w0_ref, dup_ref, a0_ref, iup_ref, gup_ref, kk_ref, ka_ref, rk_ref,
                 lng_ref, lnb_ref, seg_ref, tri_ref, blk_ref, o_ref,
                 prev_ref, st_ref, at_ref, rt_ref, bt_ref, kt_ref, bh_ref, kh_ref, v_ref, pc_ref, y_ref,
                 *, tt):
    j = pl.program_id(1)
    W = RWKV_W
    C = CHUNK
    nchunk = tt // C
    npair = RWKV_HEADS // 2

    @pl.when(j == 0)
    def _():
        prev_ref[...] = jnp.zeros(prev_ref.shape, F32)
        st_ref[...] = jnp.zeros(st_ref.shape, F32)

    z = z_ref[0]
    row = lax.broadcasted_iota(I32, z.shape, 0)
    z_prev = jnp.where(row == 0, prev_ref[...], pltpu.roll(z, 1, axis=0))
    prev_ref[...] = z[tt - 1:tt, :]
    z = z + mu_ref[...] * (z_prev - z)

    r = z[:, 0:W]
    k = z[:, W:2 * W]
    v = z[:, 2 * W:3 * W]
    wdad = z[:, 3 * W:3 * W + 2 * DECAY_LORA]
    gd = z[:, 3 * W + 2 * DECAY_LORA:]

    w_pre = w0_ref[...] + _dot3(jnp.tanh(wdad), dup_ref[...])
    neg = -w_pre
    softplus = jnp.maximum(neg, 0.0) + jnp.log(1.0 + jnp.exp(-jnp.abs(neg)))
    lw = -jnp.exp(-softplus - 0.5)
    a_lr = _sigmoid(a0_ref[...] + _dot3(wdad, iup_ref[...]))
    g = _dot3(_sigmoid(gd), gup_ref[...])

    seg = seg_ref[...]
    kk = k * kk_ref[...]
    kk = kk / jnp.maximum(jnp.sqrt(_dot_exact_rhs(kk * kk, seg)), 1e-12)
    k2 = k * (1.0 + (a_lr - 1.0) * ka_ref[...])
    a_vec = -kk
    b_vec = kk * a_lr

    cum = _dot_exact_rhs_lhs(tri_ref[...], lw)
    tot = _dot_exact_rhs_lhs(blk_ref[...], lw)
    p_in = jnp.exp(cum)
    p_inv = jnp.exp(-cum)
    p_out = jnp.exp(tot - cum)
    at_ref[...] = a_vec * jnp.exp(cum - lw)
    rt_ref[...] = r * p_in
    bt_ref[...] = (b_vec * p_inv).astype(BF16)
    kt_ref[...] = (k2 * p_inv).astype(BF16)
    bh_ref[...] = b_vec * p_out
    kh_ref[...] = k2 * p_out
    v_ref[...] = v
    pc_ref[...] = jnp.exp(tot)

    t_i = lax.broadcasted_iota(I32, (C, LANES), 0)
    s_i = lax.broadcasted_iota(I32, (C, LANES), 1) % C
    strict = s_i < t_i
    incl = s_i <= t_i
    r_i = lax.broadcasted_iota(I32, (LANES, LANES), 0)
    c_i = lax.broadcasted_iota(I32, (LANES, LANES), 1)
    same_head = (r_i < RWKV_HEAD) == (c_i < RWKV_HEAD)
    diag = r_i == c_i
    nstage = int(math.log2(C))

    def chunk_body(c, carry):
        rows = pl.ds(pl.multiple_of(c * C, C), C)
        for p in range(npair):
            cols = slice(p * LANES, (p + 1) * LANES)
            at = at_ref[rows, cols]
            rt = rt_ref[rows, cols]
            vv = v_ref[rows, cols]
            vb = vv.astype(BF16)
            lhs = jnp.concatenate([at, rt], axis=0).astype(BF16)
            rhs = jnp.concatenate([_blockdiag_rows(bt_ref[rows, cols]),
                                   _blockdiag_rows(kt_ref[rows, cols])], axis=0)
            prod = _dot_nt(lhs, rhs)
            zero = jnp.zeros((C, LANES), F32)
            m_ab = jnp.where(strict, prod[:C, :LANES], zero)
            m_ak = jnp.where(strict, prod[:C, LANES:], zero)
            a_rb = jnp.where(incl, prod[C:, :LANES], zero).astype(BF16)
            a_rk = jnp.where(incl, prod[C:, LANES:], zero).astype(BF16)
            v_bd = _blockdiag_rows(vb)
            w1 = at
            w2 = _dot(m_ak.astype(BF16), v_bd)
            lmat = m_ab
            for s in range(nstage):
                lb = lmat.astype(BF16)
                w_bd = jnp.concatenate([_blockdiag_rows(w1.astype(BF16)),
                                        _blockdiag_rows(w2.astype(BF16))], axis=1)
                upd = _dot(lb, w_bd)
                w1 = w1 + upd[:, :LANES]
                w2 = w2 + upd[:, LANES:]
                if s < nstage - 1:
                    lmat = _dot(lb, _blockdiag_rows(lb))
            w1b = w1.astype(BF16)
            w2b = w2.astype(BF16)
            w_bd = jnp.concatenate([_blockdiag_rows(w1b), _blockdiag_rows(w2b)], axis=1)
            gg = _dot(a_rb, w_bd)
            g1 = rt + gg[:, :LANES]
            g2 = gg[:, LANES:] + _dot(a_rk, v_bd)
            bk_t = jnp.concatenate([bh_ref[rows, cols], kh_ref[rows, cols]], axis=0).T
            hrhs = jnp.concatenate(
                [jnp.concatenate([w1b, w2b], axis=1),
                 jnp.concatenate([jnp.zeros((C, LANES), BF16), vb], axis=1)], axis=0)
            hh = _dot(bk_t.astype(BF16), hrhs)
            zsq = jnp.zeros((LANES, LANES), F32)
            pc = pc_ref[pl.ds(c * C, 1), cols]
            h1 = jnp.where(same_head, hh[:, :LANES], zsq) + jnp.where(diag, jnp.broadcast_to(pc, (LANES, LANES)), zsq)
            h2 = jnp.where(same_head, hh[:, LANES:], zsq)
            st = st_ref[p]
            stb = st.astype(BF16)
            y_ref[rows, cols] = _dot(g1.astype(BF16), stb) + g2
            st_ref[p] = _dot(h1.astype(BF16), stb) + h2
        return carry

    lax.fori_loop(0, nchunk, chunk_body, 0)

    y = y_ref[...]
    inv_n = 1.0 / RWKV_HEAD
    mean = _dot_exact_rhs(y, seg) * inv_n
    yc = y - mean
    var = _dot_exact_rhs(yc * yc, seg) * inv_n
    yn = yc * lax.rsqrt(var + GN_EPS) * lng_ref[...] + lnb_ref[...]
    bonus = _dot_exact_rhs(r * k2 * rk_ref[...], seg) * v
    o_ref[0] = ((yn + bonus) * g).astype(BF16)


def _dot_exact_rhs_lhs(ones_bf16, x):
    hi, mid, lo = _split3(x)
    return _dot(ones_bf16, hi) + _dot(ones_bf16, mid) + _dot(ones_bf16, lo)


def _rwkv_call(zr3, tshift_mu, decay_w0, decay_up, iclr_a0, iclr_up, gate_up, k_k, k_a, r_k, lnx_g, lnx_b):
    B, S, _ = zr3.shape
    tt = min(256, S)
    W = RWKV_W
    row = lambda a: a.reshape(1, -1).astype(F32)
    dup = jnp.concatenate([decay_up, jnp.zeros((ICLR_LORA, W), F32)], axis=0)
    iup = jnp.concatenate([jnp.zeros((DECAY_LORA, W), F32), iclr_up], axis=0)
    idx = jnp.arange(W)
    seg = (idx[:, None] // RWKV_HEAD == idx[None, :] // RWKV_HEAD).astype(BF16)
    t = jnp.arange(tt)
    same_chunk = t[:, None] // CHUNK == t[None, :] // CHUNK
    tri = (same_chunk & (t[None, :] <= t[:, None])).astype(BF16)
    blk = same_chunk.astype(BF16)
    const = lambda shape: pl.BlockSpec(shape, lambda b, j: (0,) * len(shape))
    kern = functools.partial(_rwkv_kernel, tt=tt)
    return pl.pallas_call(
        kern,
        grid=(B, S // tt),
        in_specs=[pl.BlockSpec((1, tt, RWKV_IN), lambda b, j: (b, j, 0)),
                  const((1, RWKV_IN)), const((1, W)), const((2 * DECAY_LORA, W)), const((1, W)),
                  const((2 * ICLR_LORA, W)), const((GATE_LORA, W)), const((1, W)), const((1, W)),
                  const((1, W)), const((1, W)), const((1, W)),
                  const((W, W)), const((tt, tt)), const((tt, tt))],
        out_specs=pl.BlockSpec((1, tt, W), lambda b, j: (b, j, 0)),
        out_shape=jax.ShapeDtypeStruct((B, S, W), BF16),
        scratch_shapes=[pltpu.VMEM((1, RWKV_IN), F32),
                        pltpu.VMEM((RWKV_HEADS // 2, LANES, LANES), F32),
                        pltpu.VMEM((tt, W), F32),
                        pltpu.VMEM((tt, W), F32),
                        pltpu.VMEM((tt, W), BF16),
                        pltpu.VMEM((tt, W), BF16),
                        pltpu.VMEM((tt, W), F32),
                        pltpu.VMEM((tt, W), F32),
                        pltpu.VMEM((tt, W), F32),
                        pltpu.VMEM((tt, W), F32),
                        pltpu.VMEM((tt, W), F32)],
        compiler_params=_cparams(("parallel", "arbitrary")),
        name="rwkv",
    )(zr3, row(tshift_mu), row(decay_w0), dup, row(iclr_a0), iup, gate_up.astype(F32), row(k_k), row(k_a),
      row(r_k), row(lnx_g), row(lnx_b), seg, tri, blk)


def _merge_kernel(x_ref, attn_ref, rw_ref, ga_ref, gr_ref, wa_ref, wr_ref, wo_ref, g1_ref,
                  n2_ref, sc_ref, sh_ref, rwt_ref, rb_ref, x1_ref, h2_ref, gate_ref):
    a = _dot(attn_ref[...], wa_ref[...])
    rr = _dot(rw_ref[...], wr_ref[...])
    mixed = _sigmoid(ga_ref[...].astype(F32)) * a + _sigmoid(gr_ref[...].astype(F32)) * rr
    x1 = x_ref[...] + g1_ref[0] * _dot(mixed.astype(BF16), wo_ref[...])
    x1_ref[...] = x1
    ms = jnp.mean(x1 * x1, axis=-1, keepdims=True)
    h2 = x1 * lax.rsqrt(ms + RMS_EPS) * n2_ref[...]
    h2 = h2 * (1.0 + sc_ref[0]) + sh_ref[0]
    h2_ref[...] = h2.astype(BF16)

    tm = x1.shape[0]
    E, G, EG = N_EXPERTS, N_GROUPS, N_EXPERTS // N_GROUPS
    scores = _sigmoid(_dot3(rwt_ref[...], h2, nt=True))
    choice = scores + rb_ref[...]
    c3 = choice.reshape(G, EG, tm)
    e_i = lax.broadcasted_iota(I32, (G, EG, tm), 1)
    m1 = jnp.max(c3, axis=1, keepdims=True)
    first = jnp.min(jnp.where(c3 == m1, e_i, EG), axis=1, keepdims=True)
    m2 = jnp.max(jnp.where(e_i == first, -jnp.inf, c3), axis=1, keepdims=True)
    grp = (m1 + m2).reshape(G, tm)
    g_i = lax.broadcasted_iota(I32, (G, tm), 0)
    rank = jnp.zeros((G, tm), I32)
    for o in range(G):
        other = grp[o:o + 1, :]
        rank = rank + jnp.where((other > grp) | ((other == grp) & (o < g_i)), 1, 0)
    gsel = rank < TOPK_GROUPS
    esel = jnp.broadcast_to(gsel.reshape(G, 1, tm), (G, EG, tm)).reshape(E, tm)
    mc = jnp.where(esel, choice, -jnp.inf)
    x_i = lax.broadcasted_iota(I32, (E, tm), 0)
    erank = jnp.zeros((E, tm), I32)
    for o in range(E):
        other = mc[o:o + 1, :]
        erank = erank + jnp.where((other > mc) | ((other == mc) & (o < x_i)), 1, 0)
    top = erank < MOE_TOPK
    gw = jnp.where(top, scores, 0.0)
    gw = gw / jnp.sum(gw, axis=0, keepdims=True) * ROUTED_SCALE
    gpad = jnp.concatenate([gw, jnp.zeros((LANES - E, tm), F32)], axis=0)
    gate_ref[...] = gpad.T


def _merge_call(x2, attn, rw, ga, gr, wa, wr, wo, g1, norm2_g, sc2, sh2, router_w, router_bias, S):
    T, D = x2.shape
    B = T // S
    tm = min(512, S)
    tpb = S // tm
    row = lambda i: (i, 0)
    per_b = lambda i: (i // tpb, 0, 0)
    const = lambda shape: pl.BlockSpec(shape, lambda i: (0,) * len(shape))
    return pl.pallas_call(
        _merge_kernel,
        grid=(T // tm,),
        in_specs=[pl.BlockSpec((tm, D), row), pl.BlockSpec((tm, ATTN_W), row), pl.BlockSpec((tm, RWKV_W), row),
                  pl.BlockSpec((tm, D), row), pl.BlockSpec((tm, D), row),
                  const((ATTN_W, D)), const((RWKV_W, D)), const((D, D)),
                  pl.BlockSpec((1, 1, D), per_b), const((1, D)),
                  pl.BlockSpec((1, 1, D), per_b), pl.BlockSpec((1, 1, D), per_b),
                  const((N_EXPERTS, D)), const((N_EXPERTS, 1))],
        out_specs=[pl.BlockSpec((tm, D), row), pl.BlockSpec((tm, D), row), pl.BlockSpec((tm, LANES), row)],
        out_shape=[jax.ShapeDtypeStruct((T, D), F32), jax.ShapeDtypeStruct((T, D), BF16),
                   jax.ShapeDtypeStruct((T, LANES), F32)],
        compiler_params=_cparams(("parallel",)),
        name="merge",
    )(x2, attn, rw, ga, gr, wa, wr, wo, g1.reshape(B, 1, D), norm2_g.reshape(1, D),
      sc2.reshape(B, 1, D), sh2.reshape(B, 1, D), router_w.T, router_bias.reshape(N_EXPERTS, 1))


def _moe_kernel(h_ref, gate_ref, x1_ref, g2_ref, fg_ref, eg_ref, eu_ref, ed_ref, sg_ref, su_ref, sd_ref,
                o_ref, acc_ref, *, eg_per_step, n_steps):
    j = pl.program_id(1)
    h = h_ref[...]

    @pl.when(j == 0)
    def _():
        a = _dot(h, sg_ref[...])
        u = _dot(h, su_ref[...])
        acc_ref[...] = _dot((a * _sigmoid(a) * u).astype(BF16), sd_ref[...])

    gates = gate_ref[...]
    lane = lax.broadcasted_iota(I32, gates.shape, 1)
    acts = []
    for i in range(eg_per_step):
        e = j * eg_per_step + i
        gcol = jnp.sum(jnp.where(lane == e, gates, 0.0), axis=1, keepdims=True)
        a = _dot(h, eg_ref[i])
        u = _dot(h, eu_ref[i])
        acts.append((a * _sigmoid(a) * u * gcol).astype(BF16))
    act = jnp.concatenate(acts, axis=1)
    acc_ref[...] += _dot(act, ed_ref[...])

    @pl.when(j == n_steps - 1)
    def _():
        x2 = x1_ref[...] + g2_ref[0] * acc_ref[...]
        ms = jnp.mean(x2 * x2, axis=-1, keepdims=True)
        o_ref[...] = x2 * lax.rsqrt(ms + RMS_EPS) * fg_ref[...]


def _moe_call(h2, gate, x1, g2, final_g, eg, eu, ed, sg, su, sd, S):
    T, D = h2.shape
    B = T // S
    tm = min(1024, S)
    tpb = S // tm
    eps = 4
    n_steps = N_EXPERTS // eps
    FF = EXPERT_FF
    kern = functools.partial(_moe_kernel, eg_per_step=eps, n_steps=n_steps)
    row = lambda i, j: (i, 0)
    const = lambda shape: pl.BlockSpec(shape, lambda i, j: (0,) * len(shape))
    return pl.pallas_call(
        kern,
        grid=(T // tm, n_steps),
        in_specs=[pl.BlockSpec((tm, D), row), pl.BlockSpec((tm, LANES), row), pl.BlockSpec((tm, D), row),
                  pl.BlockSpec((1, 1, D), lambda i, j: (i // tpb, 0, 0)), const((1, D)),
                  pl.BlockSpec((eps, D, FF), lambda i, j: (j, 0, 0)),
                  pl.BlockSpec((eps, D, FF), lambda i, j: (j, 0, 0)),
                  pl.BlockSpec((eps * FF, D), lambda i, j: (j, 0)),
                  const((D, FF)), const((D, FF)), const((FF, D))],
        out_specs=pl.BlockSpec((tm, D), row),
        out_shape=jax.ShapeDtypeStruct((T, D), F32),
        scratch_shapes=[pltpu.VMEM((tm, D), F32)],
        compiler_params=_cparams(("parallel", "arbitrary")),
        name="moe",
    )(h2, gate, x1, g2.reshape(B, 1, D), final_g.reshape(1, D), eg, eu, ed.reshape(N_EXPERTS * FF, D), sg, su, sd)


def _layer(x2, c, S, ada_w, ada_b, norm1_g, w_in, rel_bias, tshift_mu, decay_w0, decay_up, iclr_a0, iclr_up,
           gate_up, k_k, k_a, r_k, lnx_g, lnx_b, w_attn_br, w_rwkv_br, w_out, norm2_g, router_w, router_bias,
           exp_gate, exp_up, exp_down, sh_gate, sh_up, sh_down, final_g):
    T, D = x2.shape
    B = T // S
    mod = _mod_call(c, ada_w, ada_b)
    sh1, sc1, g1, sh2, sc2, g2 = jnp.split(mod, 6, axis=-1)

    q, k, v, iq, ikw, zr, ga, gr = _inproj_call(x2, norm1_g, sc1, sh1, _pack_w_in(w_in), S)

    heads = lambda t, n, d: t.reshape(B, S, n, d).transpose(0, 2, 1, 3)
    q_h, k_h, v_h = (heads(t, N_ATTN_HEADS, ATTN_HEAD_DIM) for t in (q, k, v))
    iq_h = heads(iq, IDX_HEADS, IDX_DIM)
    ik = ikw[:, :IDX_DIM].astype(BF16).reshape(B, S, IDX_DIM)
    iw = ikw[:, IDX_DIM:IDX_DIM + IDX_HEADS].reshape(B, S, IDX_HEADS)

    ta = min(256, S)
    assert ta >= LANES and S % ta == 0
    top_k = min(TOPK_MAX, S // 4)
    mask = _index_call(iq_h, iw, ik, ta, ta, top_k)
    bias_tiles = _bias_call(rel_bias, ta)
    attn_h = _attn_call(q_h, k_h, v_h, mask, bias_tiles, rel_bias, ta)
    attn = attn_h.transpose(0, 2, 1, 3).reshape(T, ATTN_W)

    rw = _rwkv_call(zr.reshape(B, S, RWKV_IN), tshift_mu, decay_w0, decay_up, iclr_a0, iclr_up, gate_up,
                    k_k, k_a, r_k, lnx_g, lnx_b).reshape(T, RWKV_W)

    x1, h2, gate = _merge_call(x2, attn, rw, ga, gr, w_attn_br.astype(BF16), w_rwkv_br.astype(BF16),
                               w_out.astype(BF16), g1, norm2_g, sc2, sh2, router_w, router_bias, S)
    return _moe_call(h2, gate, x1, g2, final_g, exp_gate.astype(BF16), exp_up.astype(BF16),
                     exp_down.astype(BF16), sh_gate.astype(BF16), sh_up.astype(BF16), sh_down.astype(BF16), S)


def kernel(x, c, ada_w, ada_b, norm1_g, w_in, rel_bias, tshift_mu, decay_w0, decay_up, iclr_a0, iclr_up, gate_up, k_k, k_a, r_k, lnx_g, lnx_b, w_attn_br, w_rwkv_br, w_out, norm2_g, router_w, router_bias, exp_gate, exp_up, exp_down, sh_gate, sh_up, sh_down, final_g):
    B, S, D = x.shape
    depth = ada_w.shape[0]
    assert depth == 1, "the final RMSNorm is fused into the (single) layer's MoE kernel"
    out = _layer(x.reshape(B * S, D), c, S, ada_w[0], ada_b[0], norm1_g[0], w_in[0], rel_bias, tshift_mu[0],
                 decay_w0[0], decay_up[0], iclr_a0[0], iclr_up[0], gate_up[0], k_k[0], k_a[0], r_k[0],
                 lnx_g[0], lnx_b[0], w_attn_br[0], w_rwkv_br[0], w_out[0], norm2_g[0], router_w[0],
                 router_bias[0], exp_gate[0], exp_up[0], exp_down[0], sh_gate[0], sh_up[0], sh_down[0], final_g)
    return out.reshape(B, S, D)
```

```python
import functools
import math

import jax
import jax.numpy as jnp
from jax import lax
from jax.experimental import pallas as pl
from jax.experimental.pallas import tpu as pltpu

F32 = jnp.float32
BF16 = jnp.bfloat16
I32 = jnp.int32
I16 = jnp.int16

RMS_EPS = 1e-6
D_MODEL = 1024
N_ATTN_HEADS = 8
ATTN_HEAD_DIM = 64
ATTN_W = 512
IDX_HEADS = 16
IDX_DIM = 32
IDX_Q = 512
TOPK_MAX = 256
N_BUCKETS = 32
MAX_DISTANCE = 128
RWKV_HEADS = 8
RWKV_HEAD = 64
RWKV_W = 512
DECAY_LORA = 64
ICLR_LORA = 64
GATE_LORA = 128
RWKV_IN = 1792
GN_EPS = 64e-5
N_EXPERTS = 64
N_GROUPS = 8
TOPK_GROUPS = 4
MOE_TOPK = 8
EXPERT_FF = 256
ROUTED_SCALE = 2.5

LANES = 128
VMEM_LIMIT = 56 * 1024 * 1024
CHUNK = 64
INT_MIN = -2147483648
KEY_NEG_INF = -2139095041

NT_DIMS = (((1,), (1,)), ((), ()))


def _cparams(sem):
    return pltpu.CompilerParams(dimension_semantics=sem, vmem_limit_bytes=VMEM_LIMIT)


def _dot(a, b):
    return jnp.dot(a, b, preferred_element_type=F32)


def _dot_nt(a, b):
    return lax.dot_general(a, b, NT_DIMS, preferred_element_type=F32)


def _split2(x):
    hi = x.astype(BF16)
    lo = (x - hi.astype(F32)).astype(BF16)
    return hi, lo


def _split3(x):
    hi = x.astype(BF16)
    r1 = x - hi.astype(F32)
    mid = r1.astype(BF16)
    lo = (r1 - mid.astype(F32)).astype(BF16)
    return hi, mid, lo


def _dot_exact_rhs(x, ones_bf16, terms=2):
    parts = _split3(x) if terms == 3 else _split2(x)
    out = _dot(parts[0], ones_bf16)
    for p in parts[1:]:
        out = out + _dot(p, ones_bf16)
    return out


def _dot3(a, b, nt=False):
    ah, al = _split2(a)
    bh, bl = _split2(b)
    f = _dot_nt if nt else _dot
    return f(ah, bh) + f(ah, bl) + f(al, bh)


def _sigmoid(x):
    return 1.0 / (1.0 + jnp.exp(-x))


def _mod_kernel(c_ref, w_ref, b_ref, o_ref):
    c = c_ref[...]
    s = c * _sigmoid(c)
    o_ref[...] = _dot3(s, w_ref[...]) + b_ref[...]


def _mod_call(c, ada_w, ada_b):
    B, D = c.shape
    N = ada_w.shape[1]
    tn = 1024
    return pl.pallas_call(
        _mod_kernel,
        grid=(N // tn,),
        in_specs=[pl.BlockSpec((B, D), lambda j: (0, 0)),
                  pl.BlockSpec((D, tn), lambda j: (0, j)),
                  pl.BlockSpec((1, tn), lambda j: (0, j))],
        out_specs=pl.BlockSpec((B, tn), lambda j: (0, j)),
        out_shape=jax.ShapeDtypeStruct((B, N), F32),
        compiler_params=_cparams(("arbitrary",)),
        name="mod",
    )(c, ada_w, ada_b.reshape(1, N))


_OFF_Q, _OFF_K, _OFF_IQ, _OFF_IK4, _OFF_ZR, _OFF_GA, _OFF_GR, _N_PACK = (
    0, 512, 1024, 1536, 2048, 3840, 4864, 5888)
IDX_PER_BLOCK = LANES // IDX_DIM


def _pack_w_in(w_in):
    D = w_in.shape[0]
    w_ik = w_in[:, 2048:2080]
    ik4 = jnp.zeros((D, IDX_PER_BLOCK * LANES), w_in.dtype)
    for j in range(IDX_PER_BLOCK):
        ik4 = lax.dynamic_update_slice(ik4, w_ik, (0, j * LANES + j * IDX_DIM))
    w_pack = jnp.concatenate([w_in[:, 0:1024], w_in[:, 1536:2048], ik4, w_in[:, 2096:]], axis=1).astype(BF16)
    return w_pack, w_in[:, 1024:1536].T.astype(BF16), w_in[:, 2080:2096].T.astype(BF16)


def _inproj_kernel(x_ref, g_ref, sc_ref, sh_ref, w_ref, wvt_ref, wiwt_ref,
                   q_ref, k_ref, vt_ref, iq_ref, ik4_ref, iwt_ref, zr_ref, ga_ref, gr_ref):
    x = x_ref[...]
    ms = jnp.mean(x * x, axis=-1, keepdims=True)
    h = x * lax.rsqrt(ms + RMS_EPS) * g_ref[...]
    h = h * (1.0 + sc_ref[0]) + sh_ref[0]
    hb = h.astype(BF16)

    def proj(lo, hi):
        return _dot(hb, w_ref[:, lo:hi])

    q_ref[...] = (proj(_OFF_Q, _OFF_K) * (ATTN_HEAD_DIM ** -0.5)).astype(BF16)
    k_ref[...] = proj(_OFF_K, _OFF_IQ).astype(BF16)
    iq_ref[...] = proj(_OFF_IQ, _OFF_IK4).astype(BF16)
    ik4_ref[...] = proj(_OFF_IK4, _OFF_ZR).astype(BF16)
    zr_ref[...] = proj(_OFF_ZR, _OFF_GA)
    ga_ref[...] = proj(_OFF_GA, _OFF_GR).astype(BF16)
    gr_ref[...] = proj(_OFF_GR, _N_PACK).astype(BF16)
    vt_ref[0] = _dot_nt(wvt_ref[...], hb).astype(BF16)
    iwt_ref[0] = _dot_nt(wiwt_ref[...], hb)


def _inproj_call(x2, norm_g, sc, sh, w_in, S):
    T, D = x2.shape
    B = T // S
    tm = min(512, S)
    tpb = S // tm
    w_pack, wvt, wiwt = _pack_w_in(w_in)
    row = lambda i: (i, 0)
    per_b = lambda i: (i // tpb, 0, 0)
    colblk = lambda i: (i // tpb, 0, i % tpb)
    const = lambda shape: pl.BlockSpec(shape, lambda i: (0,) * len(shape), pipeline_mode=pl.Buffered(1))
    rows_out = ((512, BF16), (512, BF16), (512, BF16), (512, BF16), (RWKV_IN, F32), (D, BF16), (D, BF16))
    out_specs = [pl.BlockSpec((tm, w), row) for w, _ in rows_out]
    out_shape = [jax.ShapeDtypeStruct((T, w), dt) for w, dt in rows_out]
    out_specs[2:2] = [pl.BlockSpec((1, ATTN_W, tm), colblk)]
    out_shape[2:2] = [jax.ShapeDtypeStruct((B, ATTN_W, S), BF16)]
    out_specs[5:5] = [pl.BlockSpec((1, IDX_HEADS, tm), colblk)]
    out_shape[5:5] = [jax.ShapeDtypeStruct((B, IDX_HEADS, S), F32)]
    return pl.pallas_call(
        _inproj_kernel,
        grid=(T // tm,),
        in_specs=[pl.BlockSpec((tm, D), row),
                  pl.BlockSpec((1, D), lambda i: (0, 0)),
                  pl.BlockSpec((1, 1, D), per_b),
                  pl.BlockSpec((1, 1, D), per_b),
                  const((D, _N_PACK)), const((ATTN_W, D)), const((IDX_HEADS, D))],
        out_specs=out_specs,
        out_shape=out_shape,
        compiler_params=_cparams(("parallel",)),
        name="inproj",
    )(x2, norm_g.reshape(1, D), sc.reshape(B, 1, D), sh.reshape(B, 1, D), w_pack, wvt, wiwt)


def _t5_bucket(rel):
    n = jnp.maximum(rel, 0)
    max_exact = N_BUCKETS // 2
    nf = jnp.maximum(n, 1).astype(F32)
    large = max_exact + (jnp.log(nf / max_exact) / math.log(MAX_DISTANCE / max_exact)
                         * (N_BUCKETS - max_exact)).astype(I32)
    large = jnp.minimum(large, N_BUCKETS - 1)
    return jnp.where(n < max_exact, n, large)


def _bias_kernel(bucket_ref, rb_ref, o_ref):
    h = pl.program_id(0)
    bk = bucket_ref[...]
    out = jnp.zeros(bk.shape, F32)
    for b in range(N_BUCKETS):
        out = jnp.where(bk == b, rb_ref[b, h], out)
    o_ref[0] = out


def _bias_call(rel_bias, tq):
    r = jnp.arange(tq, dtype=I32)[None, :]
    c = jnp.arange(tq, dtype=I32)[:, None]
    buckets = jnp.stack([_t5_bucket(r - c), _t5_bucket(tq + r - c)])
    return pl.pallas_call(
        _bias_kernel,
        grid=(N_ATTN_HEADS,),
        in_specs=[pl.BlockSpec((2, tq, tq), lambda h: (0, 0, 0)),
                  pl.BlockSpec(memory_space=pltpu.SMEM)],
        out_specs=pl.BlockSpec((1, 2, tq, tq), lambda h: (h, 0, 0, 0)),
        out_shape=jax.ShapeDtypeStruct((N_ATTN_HEADS, 2, tq, tq), F32),
        compiler_params=_cparams(("arbitrary",)),
        name="bias",
    )(buckets, rel_bias)


def _index_kernel(iq_ref, iwt_ref, ik4_ref, mask_ref, key_ref, k16_ref, *, t, nk, top_k, scale):
    qi = pl.program_id(1)
    nkt = qi + 1
    ksub = LANES
    qpos = qi * t + lax.broadcasted_iota(I32, (ksub, t), 1)

    def score_tile(kt, carry):
        kbase = pl.multiple_of(kt * t, t)
        for ks in range(t // ksub):
            acc = jnp.zeros((ksub, t), F32)
            ik_rows = ik4_ref[0, pl.ds(kbase + ks * ksub, ksub), :]
            ik_stack = jnp.concatenate([ik_rows[:, j * LANES:(j + 1) * LANES] for j in range(IDX_PER_BLOCK)],
                                       axis=0)
            for g in range(IDX_HEADS // IDX_PER_BLOCK):
                d4 = _dot_nt(ik_stack, iq_ref[0, :, g * LANES:(g + 1) * LANES])
                for j in range(IDX_PER_BLOCK):
                    h = g * IDX_PER_BLOCK + j
                    acc = acc + jnp.maximum(d4[j * ksub:(j + 1) * ksub], 0.0) * iwt_ref[0, h:h + 1, :]
            s = acc * scale
            kpos = kt * t + ks * ksub + lax.broadcasted_iota(I32, (ksub, t), 0)
            s = jnp.where(kpos <= qpos, s, -jnp.inf)
            bits = pltpu.bitcast(s, I32)
            key = bits ^ ((bits >> 31) & 0x7FFFFFFF)
            key_ref[kt, ks * ksub:(ks + 1) * ksub, :] = key
            k16_ref[kt, ks * ksub:(ks + 1) * ksub, :] = (key >> 16).astype(I16)
        return carry

    lax.fori_loop(0, nkt, score_tile, 0)

    pack = 16

    def search16():
        def bit_body(i, ans):
            cand = ans | lax.shift_left(jnp.int32(1), 15 - i)
            cand16 = (cand - 32768).astype(I16)

            def cnt_body(kt, acc):
                one = jnp.where(k16_ref[kt] >= cand16, jnp.int16(1), jnp.int16(0))
                for r in range(t // pack):
                    acc = acc + one[r * pack:(r + 1) * pack, :]
                return acc

            acc = lax.fori_loop(0, nkt, cnt_body, jnp.zeros((pack, t), I16))
            cnt = jnp.sum(acc.astype(I32), axis=0, keepdims=True)
            return jnp.where(cnt >= top_k, cand, ans)

        return lax.fori_loop(0, 16, bit_body, jnp.zeros((1, t), I32))

    hi = search16() - 32768

    def remap_body(kt, carry):
        key = key_ref[kt]
        khi = key >> 16
        lo = (key & 0xFFFF) - 32768
        k16_ref[kt] = jnp.where(khi > hi, 32767, jnp.where(khi == hi, lo, -32768)).astype(I16)
        return carry

    lax.fori_loop(0, nkt, remap_body, 0)
    thr = hi * 65536 + search16()

    def mask_body(kt, carry):
        keys = key_ref[kt]
        sel = (keys >= thr) & (keys > KEY_NEG_INF)
        mask_ref[0, 0, kt] = jnp.where(sel, 0.0, -jnp.inf).astype(BF16)
        return carry

    lax.fori_loop(0, nkt, mask_body, 0)

    def fill_body(kt, carry):
        mask_ref[0, 0, kt] = jnp.full((t, t), -jnp.inf, BF16)
        return carry

    lax.fori_loop(nkt, nk, fill_body, 0)


def _index_call(iq, iwt, ik4, t, top_k):
    B, S, _ = iq.shape
    n = S // t
    scale = (IDX_HEADS ** -0.5) * (IDX_DIM ** -0.5)
    kern = functools.partial(_index_kernel, t=t, nk=n, top_k=top_k, scale=scale)
    return pl.pallas_call(
        kern,
        grid=(B, n),
        in_specs=[pl.BlockSpec((1, t, IDX_Q), lambda b, i: (b, i, 0)),
                  pl.BlockSpec((1, IDX_HEADS, t), lambda b, i: (b, 0, i)),
                  pl.BlockSpec((1, S, IDX_PER_BLOCK * LANES), lambda b, i: (b, 0, 0))],
        out_specs=pl.BlockSpec((1, 1, n, t, t), lambda b, i: (b, i, 0, 0, 0)),
        out_shape=jax.ShapeDtypeStruct((B, n, n, t, t), BF16),
        scratch_shapes=[pltpu.VMEM((n, t, t), I32), pltpu.VMEM((n, t, t), I16)],
        compiler_params=_cparams(("parallel", "arbitrary")),
        name="index",
    )(iq, iwt, ik4)


ONES_ROWS = 16


def _attn_kernel(qi_tab, kt_tab, q_ref, k_ref, vt_ref, mask_ref, bias_ref, rb_ref, o_ref,
                 qz_ref, m_ref, acc_ref, s_ref, *, t):
    s_id = pl.program_id(1)
    qi = qi_tab[s_id]
    kt = kt_tab[s_id]
    dh = ATTN_HEAD_DIM

    @pl.when(kt == 0)
    def _():
        m_ref[...] = jnp.full(m_ref.shape, -jnp.inf, F32)
        acc_ref[...] = jnp.zeros(acc_ref.shape, F32)
        lane = lax.broadcasted_iota(I32, (t, LANES), 1)
        for h in range(N_ATTN_HEADS):
            blk = q_ref[0, :, (h // 2) * LANES:(h // 2 + 1) * LANES]
            keep = (lane < dh) if h % 2 == 0 else (lane >= dh)
            qz_ref[h] = jnp.where(keep, blk, jnp.zeros_like(blk))

    def step(bias_of_head):
        maskf = mask_ref[0, 0, 0].astype(F32)
        ones = jnp.ones((ONES_ROWS, t), BF16)

        for h in range(N_ATTN_HEADS):
            k_blk = k_ref[0, :, (h // 2) * LANES:(h // 2 + 1) * LANES]
            s_ref[h] = _dot_nt(k_blk, qz_ref[h]) + bias_of_head(h) + maskf
        for h in range(N_ATTN_HEADS):
            s = s_ref[h]
            m_old = m_ref[h:h + 1, :]
            m_cur = jnp.max(jnp.max(s.reshape(t // 8, 8, t), axis=0), axis=0, keepdims=True)
            m_new = jnp.maximum(m_old, m_cur)
            m_safe = jnp.where(m_new == -jnp.inf, 0.0, m_new)
            alpha = jnp.exp(m_old - m_safe)
            p = jnp.exp(s - m_safe).astype(BF16)
            v_aug = jnp.concatenate([vt_ref[0, h * dh:(h + 1) * dh, :], ones], axis=0)
            acc_ref[h] = alpha * acc_ref[h] + _dot(v_aug, p)
            m_ref[h:h + 1, :] = m_new

    @pl.when(kt == qi)
    def _():
        step(lambda h: bias_ref[h, 0])

    @pl.when(kt == qi - 1)
    def _():
        step(lambda h: bias_ref[h, 1])

    @pl.when(kt < qi - 1)
    def _():
        step(lambda h: rb_ref[N_BUCKETS - 1, h])

    @pl.when(kt == qi)
    def _():
        outs = []
        for h in range(N_ATTN_HEADS):
            a = acc_ref[h]
            outs.append(a[:dh, :] / a[dh:dh + 1, :])
        o_ref[0] = jnp.concatenate(outs, axis=0).T.astype(BF16)


def _attn_call(q, k, vt, mask, bias_tiles, rel_bias, t):
    B, S, W = q.shape
    n = S // t
    H = N_ATTN_HEADS
    qi_tab = jnp.asarray([i for i in range(n) for _ in range(i + 1)], I32)
    kt_tab = jnp.asarray([j for i in range(n) for j in range(i + 1)], I32)
    kern = functools.partial(_attn_kernel, t=t)
    grid_spec = pltpu.PrefetchScalarGridSpec(
        num_scalar_prefetch=2,
        grid=(B, int(qi_tab.shape[0])),
        in_specs=[pl.BlockSpec((1, t, W), lambda b, s, qt, kt: (b, qt[s], 0)),
                  pl.BlockSpec((1, t, W), lambda b, s, qt, kt: (b, kt[s], 0)),
                  pl.BlockSpec((1, W, t), lambda b, s, qt, kt: (b, 0, kt[s])),
                  pl.BlockSpec((1, 1, 1, t, t), lambda b, s, qt, kt: (b, qt[s], kt[s], 0, 0)),
                  pl.BlockSpec((H, 2, t, t), lambda b, s, qt, kt: (0, 0, 0, 0), pipeline_mode=pl.Buffered(1)),
                  pl.BlockSpec(memory_space=pltpu.SMEM)],
        out_specs=pl.BlockSpec((1, t, W), lambda b, s, qt, kt: (b, qt[s], 0)),
        scratch_shapes=[pltpu.VMEM((H, t, LANES), BF16),
                        pltpu.VMEM((H, t), F32),
                        pltpu.VMEM((H, ATTN_HEAD_DIM + ONES_ROWS, t), F32),
                        pltpu.VMEM((H, t, t), F32)])
    return pl.pallas_call(
        kern,
        grid_spec=grid_spec,
        out_shape=jax.ShapeDtypeStruct((B, S, W), BF16),
        compiler_params=_cparams(("parallel", "arbitrary")),
        name="attn",
    )(qi_tab, kt_tab, q, k, vt, mask, bias_tiles, rel_bias)


def _blockdiag_rows(x):
    lane = lax.broadcasted_iota(I32, x.shape, 1)
    zero = jnp.zeros_like(x)
    return jnp.concatenate([jnp.where(lane < RWKV_HEAD, x, zero),
                            jnp.where(lane >= RWKV_HEAD, x, zero)], axis=0)


def _rwkv_kernel(z_ref, mu_ref, w0_ref, dup_ref, a0_ref, iup_ref, gup_ref, kk_ref, ka_ref, rk_ref,
                 lng_ref, lnb_ref, seg_ref, tri_ref, blk_ref, o_ref,
                 prev_ref, st_ref, at_ref, rt_ref, bt_ref, kt_ref, bh_ref, kh_ref, v_ref, pc_ref, y_ref,
                 *, tt):
    j = pl.program_id(1)
    W = RWKV_W
    C = CHUNK
    nchunk = tt // C
    npair = RWKV_HEADS // 2

    @pl.when(j == 0)
    def _():
        prev_ref[...] = jnp.zeros(prev_ref.shape, F32)
        st_ref[...] = jnp.zeros(st_ref.shape, F32)

    z = z_ref[0]
    row = lax.broadcasted_iota(I32, z.shape, 0)
    z_prev = jnp.where(row == 0, prev_ref[...], pltpu.roll(z, 1, axis=0))
    prev_ref[...] = z[tt - 1:tt, :]
    z = z + mu_ref[...] * (z_prev - z)

    r = z[:, 0:W]
    k = z[:, W:2 * W]
    v = z[:, 2 * W:3 * W]
    wdad = z[:, 3 * W:3 * W + 2 * DECAY_LORA]
    gd = z[:, 3 * W + 2 * DECAY_LORA:]

    w_pre = w0_ref[...] + _dot3(jnp.tanh(wdad), dup_ref[...])
    neg = -w_pre
    softplus = jnp.maximum(neg, 0.0) + jnp.log(1.0 + jnp.exp(-jnp.abs(neg)))
    lw = -jnp.exp(-softplus - 0.5)
    a_lr = _sigmoid(a0_ref[...] + _dot3(wdad, iup_ref[...]))
    g = _dot3(_sigmoid(gd), gup_ref[...])

    seg = seg_ref[...]
    kk = k * kk_ref[...]
    kk = kk / jnp.maximum(jnp.sqrt(_dot_exact_rhs(kk * kk, seg)), 1e-12)
    k2 = k * (1.0 + (a_lr - 1.0) * ka_ref[...])
    a_vec = -kk
    b_vec = kk * a_lr

    cum = _dot_exact_rhs_lhs(tri_ref[...], lw)
    tot = _dot_exact_rhs_lhs(blk_ref[...], lw)
    p_in = jnp.exp(cum)
    p_inv = jnp.exp(-cum)
    p_out = jnp.exp(tot - cum)
    at_ref[...] = a_vec * jnp.exp(cum - lw)
    rt_ref[...] = r * p_in
    bt_ref[...] = (b_vec * p_inv).astype(BF16)
    kt_ref[...] = (k2 * p_inv).astype(BF16)
    bh_ref[...] = b_vec * p_out
    kh_ref[...] = k2 * p_out
    v_ref[...] = v
    pc_ref[...] = jnp.exp(tot)

    t_i = lax.broadcasted_iota(I32, (C, LANES), 0)
    s_i = lax.broadcasted_iota(I32, (C, LANES), 1) % C
    strict = s_i < t_i
    incl = s_i <= t_i
    r_i = lax.broadcasted_iota(I32, (LANES, LANES), 0)
    c_i = lax.broadcasted_iota(I32, (LANES, LANES), 1)
    same_head = (r_i < RWKV_HEAD) == (c_i < RWKV_HEAD)
    diag = r_i == c_i
    nstage = int(math.log2(C))

    def chunk_body(c, carry):
        rows = pl.ds(pl.multiple_of(c * C, C), C)
        for p in range(npair):
            cols = slice(p * LANES, (p + 1) * LANES)
            at = at_ref[rows, cols]
            rt = rt_ref[rows, cols]
            vv = v_ref[rows, cols]
            vb = vv.astype(BF16)
            lhs = jnp.concatenate([at, rt], axis=0).astype(BF16)
            rhs = jnp.concatenate([_blockdiag_rows(bt_ref[rows, cols]),
                                   _blockdiag_rows(kt_ref[rows, cols])], axis=0)
            prod = _dot_nt(lhs, rhs)
            zero = jnp.zeros((C, LANES), F32)
            m_ab = jnp.where(strict, prod[:C, :LANES], zero)
            m_ak = jnp.where(strict, prod[:C, LANES:], zero)
            a_rb = jnp.where(incl, prod[C:, :LANES], zero).astype(BF16)
            a_rk = jnp.where(incl, prod[C:, LANES:], zero).astype(BF16)
            v_bd = _blockdiag_rows(vb)
            w1 = at
            w2 = _dot(m_ak.astype(BF16), v_bd)
            lmat = m_ab
            for s in range(nstage):
                lb = lmat.astype(BF16)
                w_bd = jnp.concatenate([_blockdiag_rows(w1.astype(BF16)),
                                        _blockdiag_rows(w2.astype(BF16))], axis=1)
                upd = _dot(lb, w_bd)
                w1 = w1 + upd[:, :LANES]
                w2 = w2 + upd[:, LANES:]
                if s < nstage - 1:
                    lmat = _dot(lb, _blockdiag_rows(lb))
            w1b = w1.astype(BF16)
            w2b = w2.astype(BF16)
            w_bd = jnp.concatenate([_blockdiag_rows(w1b), _blockdiag_rows(w2b)], axis=1)
            gg = _dot(a_rb, w_bd)
            g1 = rt + gg[:, :LANES]
            g2 = gg[:, LANES:] + _dot(a_rk, v_bd)
            bk_t = jnp.concatenate([bh_ref[rows, cols], kh_ref[rows, cols]], axis=0).T
            hrhs = jnp.concatenate(
                [jnp.concatenate([w1b, w2b], axis=1),
                 jnp.concatenate([jnp.zeros((C, LANES), BF16), vb], axis=1)], axis=0)
            hh = _dot(bk_t.astype(BF16), hrhs)
            zsq = jnp.zeros((LANES, LANES), F32)
            pc = pc_ref[pl.ds(c * C, 1), cols]
            h1 = jnp.where(same_head, hh[:, :LANES], zsq) + jnp.where(diag, jnp.broadcast_to(pc, (LANES, LANES)), zsq)
            h2 = jnp.where(same_head, hh[:, LANES:], zsq)
            st = st_ref[p]
            stb = st.astype(BF16)
            y_ref[rows, cols] = _dot(g1.astype(BF16), stb) + g2
            st_ref[p] = _dot(h1.astype(BF16), stb) + h2
        return carry

    lax.fori_loop(0, nchunk, chunk_body, 0)

    y = y_ref[...]
    inv_n = 1.0 / RWKV_HEAD
    mean = _dot_exact_rhs(y, seg) * inv_n
    yc = y - mean
    var = _dot_exact_rhs(yc * yc, seg) * inv_n
    yn = yc * lax.rsqrt(var + GN_EPS) * lng_ref[...] + lnb_ref[...]
    bonus = _dot_exact_rhs(r * k2 * rk_ref[...], seg) * v
    o_ref[0] = ((yn + bonus) * g).astype(BF16)


def _dot_exact_rhs_lhs(ones_bf16, x):
    hi, mid, lo = _split3(x)
    return _dot(ones_bf16, hi) + _dot(ones_bf16, mid) + _dot(ones_bf16, lo)


def _rwkv_call(zr3, tshift_mu, decay_w0, decay_up, iclr_a0, iclr_up, gate_up, k_k, k_a, r_k, lnx_g, lnx_b):
    B, S, _ = zr3.shape
    tt = min(256, S)
    W = RWKV_W
    row = lambda a: a.reshape(1, -1).astype(F32)
    dup = jnp.concatenate([decay_up, jnp.zeros((ICLR_LORA, W), F32)], axis=0)
    iup = jnp.concatenate([jnp.zeros((DECAY_LORA, W), F32), iclr_up], axis=0)
    idx = jnp.arange(W)
    seg = (idx[:, None] // RWKV_HEAD == idx[None, :] // RWKV_HEAD).astype(BF16)
    t = jnp.arange(tt)
    same_chunk = t[:, None] // CHUNK == t[None, :] // CHUNK
    tri = (same_chunk & (t[None, :] <= t[:, None])).astype(BF16)
    blk = same_chunk.astype(BF16)
    const = lambda shape: pl.BlockSpec(shape, lambda b, j: (0,) * len(shape))
    kern = functools.partial(_rwkv_kernel, tt=tt)
    return pl.pallas_call(
        kern,
        grid=(B, S // tt),
        in_specs=[pl.BlockSpec((1, tt, RWKV_IN), lambda b, j: (b, j, 0)),
                  const((1, RWKV_IN)), const((1, W)), const((2 * DECAY_LORA, W)), const((1, W)),
                  const((2 * ICLR_LORA, W)), const((GATE_LORA, W)), const((1, W)), const((1, W)),
                  const((1, W)), const((1, W)), const((1, W)),
                  const((W, W)), const((tt, tt)), const((tt, tt))],
        out_specs=pl.BlockSpec((1, tt, W), lambda b, j: (b, j, 0)),
        out_shape=jax.ShapeDtypeStruct((B, S, W), BF16),
        scratch_shapes=[pltpu.VMEM((1, RWKV_IN), F32),
                        pltpu.VMEM((RWKV_HEADS // 2, LANES, LANES), F32),
                        pltpu.VMEM((tt, W), F32),
                        pltpu.VMEM((tt, W), F32),
                        pltpu.VMEM((tt, W), BF16),
                        pltpu.VMEM((tt, W), BF16),
                        pltpu.VMEM((tt, W), F32),
                        pltpu.VMEM((tt, W), F32),
                        pltpu.VMEM((tt, W), F32),
                        pltpu.VMEM((tt, W), F32),
                        pltpu.VMEM((tt, W), F32)],
        compiler_params=_cparams(("parallel", "arbitrary")),
        name="rwkv",
    )(zr3, row(tshift_mu), row(decay_w0), dup, row(iclr_a0), iup, gate_up.astype(F32), row(k_k), row(k_a),
      row(r_k), row(lnx_g), row(lnx_b), seg, tri, blk)


def _merge_kernel(x_ref, attn_ref, rw_ref, ga_ref, gr_ref, wa_ref, wr_ref, wo_ref, g1_ref,
                  n2_ref, sc_ref, sh_ref, rwt_ref, rb_ref, x1_ref, h2_ref, gate_ref):
    a = _dot(attn_ref[...], wa_ref[...])
    rr = _dot(rw_ref[...], wr_ref[...])
    mixed = _sigmoid(ga_ref[...].astype(F32)) * a + _sigmoid(gr_ref[...].astype(F32)) * rr
    x1 = x_ref[...] + g1_ref[0] * _dot(mixed.astype(BF16), wo_ref[...])
    x1_ref[...] = x1
    ms = jnp.mean(x1 * x1, axis=-1, keepdims=True)
    h2 = x1 * lax.rsqrt(ms + RMS_EPS) * n2_ref[...]
    h2 = h2 * (1.0 + sc_ref[0]) + sh_ref[0]
    h2_ref[...] = h2.astype(BF16)

    tm = x1.shape[0]
    E, G, EG = N_EXPERTS, N_GROUPS, N_EXPERTS // N_GROUPS
    scores = _sigmoid(_dot3(rwt_ref[...], h2, nt=True))
    choice = scores + rb_ref[...]
    c3 = choice.reshape(G, EG, tm)
    e_i = lax.broadcasted_iota(I32, (G, EG, tm), 1)
    m1 = jnp.max(c3, axis=1, keepdims=True)
    first = jnp.min(jnp.where(c3 == m1, e_i, EG), axis=1, keepdims=True)
    m2 = jnp.max(jnp.where(e_i == first, -jnp.inf, c3), axis=1, keepdims=True)
    grp = (m1 + m2).reshape(G, tm)
    g_i = lax.broadcasted_iota(I32, (G, tm), 0)
    rank = jnp.zeros((G, tm), I32)
    for o in range(G):
        other = grp[o:o + 1, :]
        rank = rank + jnp.where((other > grp) | ((other == grp) & (o < g_i)), 1, 0)
    gsel = rank < TOPK_GROUPS
    esel = jnp.broadcast_to(gsel.reshape(G, 1, tm), (G, EG, tm)).reshape(E, tm)
    mc = jnp.where(esel, choice, -jnp.inf)
    x_i = lax.broadcasted_iota(I32, (E, tm), 0)
    erank = jnp.zeros((E, tm), I32)
    for o in range(E):
        other = mc[o:o + 1, :]
        erank = erank + jnp.where((other > mc) | ((other == mc) & (o < x_i)), 1, 0)
    top = erank < MOE_TOPK
    gw = jnp.where(top, scores, 0.0)
    gw = gw / jnp.sum(gw, axis=0, keepdims=True) * ROUTED_SCALE
    gpad = jnp.concatenate([gw, jnp.zeros((LANES - E, tm), F32)], axis=0)
    gate_ref[...] = gpad.T


def _merge_call(x2, attn, rw, ga, gr, wa, wr, wo, g1, norm2_g, sc2, sh2, router_w, router_bias, S):
    T, D = x2.shape
    B = T // S
    tm = min(512, S)
    tpb = S // tm
    row = lambda i: (i, 0)
    per_b = lambda i: (i // tpb, 0, 0)
    const = lambda shape: pl.BlockSpec(shape, lambda i: (0,) * len(shape))
    return pl.pallas_call(
        _merge_kernel,
        grid=(T // tm,),
        in_specs=[pl.BlockSpec((tm, D), row), pl.BlockSpec((tm, ATTN_W), row), pl.BlockSpec((tm, RWKV_W), row),
                  pl.BlockSpec((tm, D), row), pl.BlockSpec((tm, D), row),
                  const((ATTN_W, D)), const((RWKV_W, D)), const((D, D)),
                  pl.BlockSpec((1, 1, D), per_b), const((1, D)),
                  pl.BlockSpec((1, 1, D), per_b), pl.BlockSpec((1, 1, D), per_b),
                  const((N_EXPERTS, D)), const((N_EXPERTS, 1))],
        out_specs=[pl.BlockSpec((tm, D), row), pl.BlockSpec((tm, D), row), pl.BlockSpec((tm, LANES), row)],
        out_shape=[jax.ShapeDtypeStruct((T, D), F32), jax.ShapeDtypeStruct((T, D), BF16),
                   jax.ShapeDtypeStruct((T, LANES), F32)],
        compiler_params=_cparams(("parallel",)),
        name="merge",
    )(x2, attn, rw, ga, gr, wa, wr, wo, g1.reshape(B, 1, D), norm2_g.reshape(1, D),
      sc2.reshape(B, 1, D), sh2.reshape(B, 1, D), router_w.T, router_bias.reshape(N_EXPERTS, 1))


def _moe_kernel(h_ref, gate_ref, x1_ref, g2_ref, fg_ref, eg_ref, eu_ref, ed_ref, sg_ref, su_ref, sd_ref,
                o_ref, acc_ref, *, eg_per_step, n_steps):
    j = pl.program_id(1)
    h = h_ref[...]

    @pl.when(j == 0)
    def _():
        a = _dot(h, sg_ref[...])
        u = _dot(h, su_ref[...])
        acc_ref[...] = _dot((a * _sigmoid(a) * u).astype(BF16), sd_ref[...])

    gates = gate_ref[...]
    lane = lax.broadcasted_iota(I32, gates.shape, 1)
    acts = []
    for i in range(eg_per_step):
        e = j * eg_per_step + i
        gcol = jnp.sum(jnp.where(lane == e, gates, 0.0), axis=1, keepdims=True)
        a = _dot(h, eg_ref[i])
        u = _dot(h, eu_ref[i])
        acts.append((a * _sigmoid(a) * u * gcol).astype(BF16))
    act = jnp.concatenate(acts, axis=1)
    acc_ref[...] += _dot(act, ed_ref[...])

    @pl.when(j == n_steps - 1)
    def _():
        x2 = x1_ref[...] + g2_ref[0] * acc_ref[...]
        ms = jnp.mean(x2 * x2, axis=-1, keepdims=True)
        o_ref[...] = x2 * lax.rsqrt(ms + RMS_EPS) * fg_ref[...]


def _moe_call(h2, gate, x1, g2, final_g, eg, eu, ed, sg, su, sd, S):
    T, D = h2.shape
    B = T // S
    tm = min(1024, S)
    tpb = S // tm
    eps = 4
    n_steps = N_EXPERTS // eps
    FF = EXPERT_FF
    kern = functools.partial(_moe_kernel, eg_per_step=eps, n_steps=n_steps)
    row = lambda i, j: (i, 0)
    const = lambda shape: pl.BlockSpec(shape, lambda i, j: (0,) * len(shape))
    return pl.pallas_call(
        kern,
        grid=(T // tm, n_steps),
        in_specs=[pl.BlockSpec((tm, D), row), pl.BlockSpec((tm, LANES), row), pl.BlockSpec((tm, D), row),
                  pl.BlockSpec((1, 1, D), lambda i, j: (i // tpb, 0, 0)), const((1, D)),
                  pl.BlockSpec((eps, D, FF), lambda i, j: (j, 0, 0)),
                  pl.BlockSpec((eps, D, FF), lambda i, j: (j, 0, 0)),
                  pl.BlockSpec((eps * FF, D), lambda i, j: (j, 0)),
                  const((D, FF)), const((D, FF)), const((FF, D))],
        out_specs=pl.BlockSpec((tm, D), row),
        out_shape=jax.ShapeDtypeStruct((T, D), F32),
        scratch_shapes=[pltpu.VMEM((tm, D), F32)],
        compiler_params=_cparams(("parallel", "arbitrary")),
        name="moe",
    )(h2, gate, x1, g2.reshape(B, 1, D), final_g.reshape(1, D), eg, eu, ed.reshape(N_EXPERTS * FF, D), sg, su, sd)


def _layer(x2, c, S, ada_w, ada_b, norm1_g, w_in, rel_bias, tshift_mu, decay_w0, decay_up, iclr_a0, iclr_up,
           gate_up, k_k, k_a, r_k, lnx_g, lnx_b, w_attn_br, w_rwkv_br, w_out, norm2_g, router_w, router_bias,
           exp_gate, exp_up, exp_down, sh_gate, sh_up, sh_down, final_g):
    T, D = x2.shape
    B = T // S
    mod = _mod_call(c, ada_w, ada_b)
    sh1, sc1, g1, sh2, sc2, g2 = jnp.split(mod, 6, axis=-1)

    q, k, vt, iq, ik4, iwt, zr, ga, gr = _inproj_call(x2, norm1_g, sc1, sh1, w_in, S)

    ta = min(256, S)
    assert ta >= LANES and S % ta == 0
    top_k = min(TOPK_MAX, S // 4)
    seq = lambda a: a.reshape(B, S, a.shape[-1])
    mask = _index_call(seq(iq), iwt, seq(ik4), ta, top_k)
    bias_tiles = _bias_call(rel_bias, ta)
    attn = _attn_call(seq(q), seq(k), vt, mask, bias_tiles, rel_bias, ta).reshape(T, ATTN_W)

    rw = _rwkv_call(zr.reshape(B, S, RWKV_IN), tshift_mu, decay_w0, decay_up, iclr_a0, iclr_up, gate_up,
                    k_k, k_a, r_k, lnx_g, lnx_b).reshape(T, RWKV_W)

    x1, h2, gate = _merge_call(x2, attn, rw, ga, gr, w_attn_br.astype(BF16), w_rwkv_br.astype(BF16),
                               w_out.astype(BF16), g1, norm2_g, sc2, sh2, router_w, router_bias, S)
    return _moe_call(h2, gate, x1, g2, final_g, exp_gate.astype(BF16), exp_up.astype(BF16),
                     exp_down.astype(BF16), sh_gate.astype(BF16), sh_up.astype(BF16), sh_down.astype(BF16), S)


def kernel(x, c, ada_w, ada_b, norm1_g, w_in, rel_bias, tshift_mu, decay_w0, decay_up, iclr_a0, iclr_up, gate_up, k_k, k_a, r_k, lnx_g, lnx_b, w_attn_br, w_rwkv_br, w_out, norm2_g, router_w, router_bias, exp_gate, exp_up, exp_down, sh_gate, sh_up, sh_down, final_g):
    B, S, D = x.shape
    depth = ada_w.shape[0]
    assert depth == 1, "the final RMSNorm is fused into the (single) layer's MoE kernel"
    out = _layer(x.reshape(B * S, D), c, S, ada_w[0], ada_b[0], norm1_g[0], w_in[0], rel_bias, tshift_mu[0],
                 decay_w0[0], decay_up[0], iclr_a0[0], iclr_up[0], gate_up[0], k_k[0], k_a[0], r_k[0],
                 lnx_g[0], lnx_b[0], w_attn_br[0], w_rwkv_br[0], w_out[0], norm2_g[0], router_w[0],
                 router_bias[0], exp_gate[0], exp_up[0], exp_down[0], sh_gate[0], sh_up[0], sh_down[0], final_g)
    return out.reshape(B, S, D)
```

```python
import functools
import math

import jax
import jax.numpy as jnp
from jax import lax
from jax.experimental import pallas as pl
from jax.experimental.pallas import tpu as pltpu

F32 = jnp.float32
BF16 = jnp.bfloat16
I32 = jnp.int32
I16 = jnp.int16

RMS_EPS = 1e-6
D_MODEL = 1024
N_ATTN_HEADS = 8
ATTN_HEAD_DIM = 64
ATTN_W = 512
IDX_HEADS = 16
IDX_DIM = 32
IDX_Q = 512
TOPK_MAX = 256
N_BUCKETS = 32
MAX_DISTANCE = 128
RWKV_HEADS = 8
RWKV_HEAD = 64
RWKV_W = 512
DECAY_LORA = 64
ICLR_LORA = 64
GATE_LORA = 128
RWKV_IN = 1792
GN_EPS = 64e-5
N_EXPERTS = 64
N_GROUPS = 8
TOPK_GROUPS = 4
MOE_TOPK = 8
EXPERT_FF = 256
ROUTED_SCALE = 2.5

LANES = 128
VMEM_LIMIT = 56 * 1024 * 1024
CHUNK = 64
INT_MIN = -2147483648
KEY_NEG_INF = -2139095041

NT_DIMS = (((1,), (1,)), ((), ()))


def _cparams(sem):
    return pltpu.CompilerParams(dimension_semantics=sem, vmem_limit_bytes=VMEM_LIMIT)


def _dot(a, b):
    return jnp.dot(a, b, preferred_element_type=F32)


def _dot_nt(a, b):
    return lax.dot_general(a, b, NT_DIMS, preferred_element_type=F32)


def _split2(x):
    hi = x.astype(BF16)
    lo = (x - hi.astype(F32)).astype(BF16)
    return hi, lo


def _split3(x):
    hi = x.astype(BF16)
    r1 = x - hi.astype(F32)
    mid = r1.astype(BF16)
    lo = (r1 - mid.astype(F32)).astype(BF16)
    return hi, mid, lo


def _dot_exact_rhs(x, ones_bf16, terms=2):
    parts = _split3(x) if terms == 3 else _split2(x)
    out = _dot(parts[0], ones_bf16)
    for p in parts[1:]:
        out = out + _dot(p, ones_bf16)
    return out


def _dot3(a, b, nt=False):
    ah, al = _split2(a)
    bh, bl = _split2(b)
    f = _dot_nt if nt else _dot
    return f(ah, bh) + f(ah, bl) + f(al, bh)


def _sigmoid(x):
    return 1.0 / (1.0 + jnp.exp(-x))


def _mod_kernel(c_ref, w_ref, b_ref, o_ref):
    c = c_ref[...]
    s = c * _sigmoid(c)
    o_ref[...] = _dot3(s, w_ref[...]) + b_ref[...]


def _mod_call(c, ada_w, ada_b):
    B, D = c.shape
    N = ada_w.shape[1]
    tn = 1024
    return pl.pallas_call(
        _mod_kernel,
        grid=(N // tn,),
        in_specs=[pl.BlockSpec((B, D), lambda j: (0, 0)),
                  pl.BlockSpec((D, tn), lambda j: (0, j)),
                  pl.BlockSpec((1, tn), lambda j: (0, j))],
        out_specs=pl.BlockSpec((B, tn), lambda j: (0, j)),
        out_shape=jax.ShapeDtypeStruct((B, N), F32),
        compiler_params=_cparams(("arbitrary",)),
        name="mod",
    )(c, ada_w, ada_b.reshape(1, N))


_OFF_Q, _OFF_K, _OFF_IQ, _OFF_IK4, _OFF_ZR, _OFF_GA, _OFF_GR, _N_PACK = (
    0, 512, 1024, 1536, 2048, 3840, 4864, 5888)
IDX_PER_BLOCK = LANES // IDX_DIM


def _pack_w_in(w_in):
    D = w_in.shape[0]
    w_ik = w_in[:, 2048:2080]
    ik4 = jnp.zeros((D, IDX_PER_BLOCK * LANES), w_in.dtype)
    for j in range(IDX_PER_BLOCK):
        ik4 = lax.dynamic_update_slice(ik4, w_ik, (0, j * LANES + j * IDX_DIM))
    w_pack = jnp.concatenate([w_in[:, 0:1024], w_in[:, 1536:2048], ik4, w_in[:, 2096:]], axis=1).astype(BF16)
    return w_pack, w_in[:, 1024:1536].T.astype(BF16), w_in[:, 2080:2096].T.astype(BF16)


def _inproj_kernel(x_ref, g_ref, sc_ref, sh_ref, w_ref, wvt_ref, wiwt_ref,
                   q_ref, k_ref, vt_ref, iq_ref, ik4_ref, iwt_ref, zr_ref, ga_ref, gr_ref):
    x = x_ref[...]
    ms = jnp.mean(x * x, axis=-1, keepdims=True)
    h = x * lax.rsqrt(ms + RMS_EPS) * g_ref[...]
    h = h * (1.0 + sc_ref[0]) + sh_ref[0]
    hb = h.astype(BF16)

    def proj(lo, hi):
        return _dot(hb, w_ref[:, lo:hi])

    q_ref[...] = (proj(_OFF_Q, _OFF_K) * (ATTN_HEAD_DIM ** -0.5)).astype(BF16)
    k_ref[...] = proj(_OFF_K, _OFF_IQ).astype(BF16)
    iq_ref[...] = proj(_OFF_IQ, _OFF_IK4).astype(BF16)
    ik4_ref[...] = proj(_OFF_IK4, _OFF_ZR).astype(BF16)
    zr_ref[...] = proj(_OFF_ZR, _OFF_GA)
    ga_ref[...] = proj(_OFF_GA, _OFF_GR).astype(BF16)
    gr_ref[...] = proj(_OFF_GR, _N_PACK).astype(BF16)
    vt_ref[0] = _dot_nt(wvt_ref[...], hb).astype(BF16)
    iwt_ref[0] = _dot_nt(wiwt_ref[...], hb)


def _inproj_call(x2, norm_g, sc, sh, w_in, S):
    T, D = x2.shape
    B = T // S
    tm = min(512, S)
    tpb = S // tm
    w_pack, wvt, wiwt = _pack_w_in(w_in)
    row = lambda i: (i, 0)
    per_b = lambda i: (i // tpb, 0, 0)
    colblk = lambda i: (i // tpb, 0, i % tpb)
    const = lambda shape: pl.BlockSpec(shape, lambda i: (0,) * len(shape), pipeline_mode=pl.Buffered(1))
    rows_out = ((512, BF16), (512, BF16), (512, BF16), (512, BF16), (RWKV_IN, F32), (D, BF16), (D, BF16))
    out_specs = [pl.BlockSpec((tm, w), row) for w, _ in rows_out]
    out_shape = [jax.ShapeDtypeStruct((T, w), dt) for w, dt in rows_out]
    out_specs[2:2] = [pl.BlockSpec((1, ATTN_W, tm), colblk)]
    out_shape[2:2] = [jax.ShapeDtypeStruct((B, ATTN_W, S), BF16)]
    out_specs[5:5] = [pl.BlockSpec((1, IDX_HEADS, tm), colblk)]
    out_shape[5:5] = [jax.ShapeDtypeStruct((B, IDX_HEADS, S), F32)]
    return pl.pallas_call(
        _inproj_kernel,
        grid=(T // tm,),
        in_specs=[pl.BlockSpec((tm, D), row),
                  pl.BlockSpec((1, D), lambda i: (0, 0)),
                  pl.BlockSpec((1, 1, D), per_b),
                  pl.BlockSpec((1, 1, D), per_b),
                  const((D, _N_PACK)), const((ATTN_W, D)), const((IDX_HEADS, D))],
        out_specs=out_specs,
        out_shape=out_shape,
        compiler_params=_cparams(("parallel",)),
        name="inproj",
    )(x2, norm_g.reshape(1, D), sc.reshape(B, 1, D), sh.reshape(B, 1, D), w_pack, wvt, wiwt)


def _t5_bucket(rel):
    n = jnp.maximum(rel, 0)
    max_exact = N_BUCKETS // 2
    nf = jnp.maximum(n, 1).astype(F32)
    large = max_exact + (jnp.log(nf / max_exact) / math.log(MAX_DISTANCE / max_exact)
                         * (N_BUCKETS - max_exact)).astype(I32)
    large = jnp.minimum(large, N_BUCKETS - 1)
    return jnp.where(n < max_exact, n, large)


def _bias_kernel(bucket_ref, rb_ref, o_ref):
    h = pl.program_id(0)
    bk = bucket_ref[...]
    out = jnp.zeros(bk.shape, F32)
    for b in range(N_BUCKETS):
        out = jnp.where(bk == b, rb_ref[b, h], out)
    o_ref[0] = out


def _bias_call(rel_bias, tq):
    r = jnp.arange(tq, dtype=I32)[None, :]
    c = jnp.arange(tq, dtype=I32)[:, None]
    buckets = jnp.stack([_t5_bucket(r - c), _t5_bucket(tq + r - c)])
    return pl.pallas_call(
        _bias_kernel,
        grid=(N_ATTN_HEADS,),
        in_specs=[pl.BlockSpec((2, tq, tq), lambda h: (0, 0, 0)),
                  pl.BlockSpec(memory_space=pltpu.SMEM)],
        out_specs=pl.BlockSpec((1, 2, tq, tq), lambda h: (h, 0, 0, 0)),
        out_shape=jax.ShapeDtypeStruct((N_ATTN_HEADS, 2, tq, tq), F32),
        compiler_params=_cparams(("arbitrary",)),
        name="bias",
    )(buckets, rel_bias)


def _index_kernel(iq_ref, iwt_ref, ik4_ref, mask_ref, key_ref, k16_ref, *, t, nk, top_k, scale):
    qi = pl.program_id(1)
    nkt = qi + 1
    ksub = LANES
    qpos = qi * t + lax.broadcasted_iota(I32, (ksub, t), 1)

    def score_tile(kt, carry):
        kbase = pl.multiple_of(kt * t, t)
        for ks in range(t // ksub):
            acc = jnp.zeros((ksub, t), F32)
            ik_rows = ik4_ref[0, pl.ds(kbase + ks * ksub, ksub), :]
            ik_stack = jnp.concatenate([ik_rows[:, j * LANES:(j + 1) * LANES] for j in range(IDX_PER_BLOCK)],
                                       axis=0)
            for g in range(IDX_HEADS // IDX_PER_BLOCK):
                d4 = _dot_nt(ik_stack, iq_ref[0, :, g * LANES:(g + 1) * LANES])
                for j in range(IDX_PER_BLOCK):
                    h = g * IDX_PER_BLOCK + j
                    acc = acc + jnp.maximum(d4[j * ksub:(j + 1) * ksub], 0.0) * iwt_ref[0, h:h + 1, :]
            s = acc * scale
            kpos = kt * t + ks * ksub + lax.broadcasted_iota(I32, (ksub, t), 0)
            s = jnp.where(kpos <= qpos, s, -jnp.inf)
            bits = pltpu.bitcast(s, I32)
            key = bits ^ ((bits >> 31) & 0x7FFFFFFF)
            key_ref[kt, ks * ksub:(ks + 1) * ksub, :] = key
            k16_ref[kt, ks * ksub:(ks + 1) * ksub, :] = (key >> 16).astype(I16)
        return carry

    lax.fori_loop(0, nkt, score_tile, 0)

    pack = 16

    def search16():
        def bit_body(i, ans):
            cand = ans | lax.shift_left(jnp.int32(1), 15 - i)
            cand16 = (cand - 32768).astype(I16)

            def cnt_body(kt, acc):
                one = jnp.where(k16_ref[kt] >= cand16, jnp.int16(1), jnp.int16(0))
                for r in range(t // pack):
                    acc = acc + one[r * pack:(r + 1) * pack, :]
                return acc

            acc = lax.fori_loop(0, nkt, cnt_body, jnp.zeros((pack, t), I16))
            cnt = jnp.sum(acc.astype(I32), axis=0, keepdims=True)
            return jnp.where(cnt >= top_k, cand, ans)

        return lax.fori_loop(0, 16, bit_body, jnp.zeros((1, t), I32))

    hi = search16() - 32768

    def remap_body(kt, carry):
        key = key_ref[kt]
        khi = key >> 16
        lo = (key & 0xFFFF) - 32768
        k16_ref[kt] = jnp.where(khi > hi, 32767, jnp.where(khi == hi, lo, -32768)).astype(I16)
        return carry

    lax.fori_loop(0, nkt, remap_body, 0)
    thr = hi * 65536 + search16()

    def mask_body(kt, carry):
        keys = key_ref[kt]
        sel = (keys >= thr) & (keys > KEY_NEG_INF)
        mask_ref[0, 0, kt] = jnp.where(sel, 0.0, -jnp.inf).astype(BF16)
        return carry

    lax.fori_loop(0, nkt, mask_body, 0)

    def fill_body(kt, carry):
        mask_ref[0, 0, kt] = jnp.full((t, t), -jnp.inf, BF16)
        return carry

    lax.fori_loop(nkt, nk, fill_body, 0)


def _index_call(iq, iwt, ik4, t, top_k):
    B, S, _ = iq.shape
    n = S // t
    scale = (IDX_HEADS ** -0.5) * (IDX_DIM ** -0.5)
    kern = functools.partial(_index_kernel, t=t, nk=n, top_k=top_k, scale=scale)
    return pl.pallas_call(
        kern,
        grid=(B, n),
        in_specs=[pl.BlockSpec((1, t, IDX_Q), lambda b, i: (b, i, 0)),
                  pl.BlockSpec((1, IDX_HEADS, t), lambda b, i: (b, 0, i)),
                  pl.BlockSpec((1, S, IDX_PER_BLOCK * LANES), lambda b, i: (b, 0, 0))],
        out_specs=pl.BlockSpec((1, 1, n, t, t), lambda b, i: (b, i, 0, 0, 0)),
        out_shape=jax.ShapeDtypeStruct((B, n, n, t, t), BF16),
        scratch_shapes=[pltpu.VMEM((n, t, t), I32), pltpu.VMEM((n, t, t), I16)],
        compiler_params=_cparams(("parallel", "arbitrary")),
        name="index",
    )(iq, iwt, ik4)


ONES_ROWS = 16


def _attn_kernel(qi_tab, kt_tab, q_ref, k_ref, vt_ref, mask_ref, bias_ref, rb_ref, o_ref,
                 qz_ref, m_ref, acc_ref, s_ref, *, t):
    s_id = pl.program_id(1)
    qi = qi_tab[s_id]
    kt = kt_tab[s_id]
    dh = ATTN_HEAD_DIM

    @pl.when(kt == 0)
    def _():
        m_ref[...] = jnp.full(m_ref.shape, -jnp.inf, F32)
        acc_ref[...] = jnp.zeros(acc_ref.shape, F32)
        lane = lax.broadcasted_iota(I32, (t, LANES), 1)
        for h in range(N_ATTN_HEADS):
            blk = q_ref[0, :, (h // 2) * LANES:(h // 2 + 1) * LANES]
            keep = (lane < dh) if h % 2 == 0 else (lane >= dh)
            qz_ref[h] = jnp.where(keep, blk, jnp.zeros_like(blk))

    def step(bias_of_head):
        maskf = mask_ref[0, 0, 0].astype(F32)
        ones = jnp.ones((ONES_ROWS, t), BF16)

        for h in range(N_ATTN_HEADS):
            k_blk = k_ref[0, :, (h // 2) * LANES:(h // 2 + 1) * LANES]
            s_ref[h] = _dot_nt(k_blk, qz_ref[h]) + bias_of_head(h) + maskf
        for h in range(N_ATTN_HEADS):
            s = s_ref[h]
            m_old = m_ref[h:h + 1, :]
            m_cur = jnp.max(jnp.max(s.reshape(t // 8, 8, t), axis=0), axis=0, keepdims=True)
            m_new = jnp.maximum(m_old, m_cur)
            m_safe = jnp.where(m_new == -jnp.inf, 0.0, m_new)
            alpha = jnp.exp(m_old - m_safe)
            p = jnp.exp(s - m_safe).astype(BF16)
            v_aug = jnp.concatenate([vt_ref[0, h * dh:(h + 1) * dh, :], ones], axis=0)
            acc_ref[h] = alpha * acc_ref[h] + _dot(v_aug, p)
            m_ref[h:h + 1, :] = m_new

    @pl.when(kt == qi)
    def _():
        step(lambda h: bias_ref[h, 0])

    @pl.when(kt == qi - 1)
    def _():
        step(lambda h: bias_ref[h, 1])

    @pl.when(kt < qi - 1)
    def _():
        step(lambda h: rb_ref[N_BUCKETS - 1, h])

    @pl.when(kt == qi)
    def _():
        outs = []
        for h in range(N_ATTN_HEADS):
            a = acc_ref[h]
            outs.append(a[:dh, :] / a[dh:dh + 1, :])
        o_ref[0] = jnp.concatenate(outs, axis=0).T.astype(BF16)


def _attn_call(q, k, vt, mask, bias_tiles, rel_bias, t):
    B, S, W = q.shape
    n = S // t
    H = N_ATTN_HEADS
    qi_tab = jnp.asarray([i for i in range(n) for _ in range(i + 1)], I32)
    kt_tab = jnp.asarray([j for i in range(n) for j in range(i + 1)], I32)
    kern = functools.partial(_attn_kernel, t=t)
    grid_spec = pltpu.PrefetchScalarGridSpec(
        num_scalar_prefetch=2,
        grid=(B, int(qi_tab.shape[0])),
        in_specs=[pl.BlockSpec((1, t, W), lambda b, s, qt, kt: (b, qt[s], 0)),
                  pl.BlockSpec((1, t, W), lambda b, s, qt, kt: (b, kt[s], 0)),
                  pl.BlockSpec((1, W, t), lambda b, s, qt, kt: (b, 0, kt[s])),
                  pl.BlockSpec((1, 1, 1, t, t), lambda b, s, qt, kt: (b, qt[s], kt[s], 0, 0)),
                  pl.BlockSpec((H, 2, t, t), lambda b, s, qt, kt: (0, 0, 0, 0), pipeline_mode=pl.Buffered(1)),
                  pl.BlockSpec(memory_space=pltpu.SMEM)],
        out_specs=pl.BlockSpec((1, t, W), lambda b, s, qt, kt: (b, qt[s], 0)),
        scratch_shapes=[pltpu.VMEM((H, t, LANES), BF16),
                        pltpu.VMEM((H, t), F32),
                        pltpu.VMEM((H, ATTN_HEAD_DIM + ONES_ROWS, t), F32),
                        pltpu.VMEM((H, t, t), F32)])
    return pl.pallas_call(
        kern,
        grid_spec=grid_spec,
        out_shape=jax.ShapeDtypeStruct((B, S, W), BF16),
        compiler_params=_cparams(("parallel", "arbitrary")),
        name="attn",
    )(qi_tab, kt_tab, q, k, vt, mask, bias_tiles, rel_bias)


def _blockdiag_rows(x):
    lane = lax.broadcasted_iota(I32, x.shape, 1)
    zero = jnp.zeros_like(x)
    return jnp.concatenate([jnp.where(lane < RWKV_HEAD, x, zero),
                            jnp.where(lane >= RWKV_HEAD, x, zero)], axis=0)


def _rwkv_kernel(z_ref, mu_ref, w0_ref, dup_ref, a0_ref, iup_ref, gup_ref, kk_ref, ka_ref, rk_ref,
                 lng_ref, lnb_ref, seg_ref, tri_ref, o_ref,
                 prev_ref, st_ref, at_ref, rt_ref, bt_ref, kt_ref, bh_ref, kh_ref, v_ref, pc_ref, y_ref,
                 la_ref, lb_ref, mak_ref, arb_ref, ark_ref, wa_ref, wb_ref, g1_ref, g2_ref, h1_ref, h2_ref,
                 *, tt):
    j = pl.program_id(1)
    W = RWKV_W
    C = CHUNK
    nchunk = tt // C
    npair = RWKV_HEADS // 2

    @pl.when(j == 0)
    def _():
        prev_ref[...] = jnp.zeros(prev_ref.shape, F32)
        st_ref[...] = jnp.zeros(st_ref.shape, F32)

    z = z_ref[0]
    row = lax.broadcasted_iota(I32, z.shape, 0)
    z_prev = jnp.where(row == 0, prev_ref[...], pltpu.roll(z, 1, axis=0))
    prev_ref[...] = z[tt - 1:tt, :]
    z = z + mu_ref[...] * (z_prev - z)

    r = z[:, 0:W]
    k = z[:, W:2 * W]
    v = z[:, 2 * W:3 * W]
    wdad = z[:, 3 * W:3 * W + 2 * DECAY_LORA]
    gd = z[:, 3 * W + 2 * DECAY_LORA:]

    w_pre = w0_ref[...] + _dot3(jnp.tanh(wdad), dup_ref[...])
    neg = -w_pre
    softplus = jnp.maximum(neg, 0.0) + jnp.log(1.0 + jnp.exp(-jnp.abs(neg)))
    lw = -jnp.exp(-softplus - 0.5)
    a_lr = _sigmoid(a0_ref[...] + _dot(wdad.astype(BF16), iup_ref[...]))
    g = _dot(_sigmoid(gd).astype(BF16), gup_ref[...])

    seg = seg_ref[...]
    kk = k * kk_ref[...]
    kk = kk / jnp.maximum(jnp.sqrt(_dot_exact_rhs(kk * kk, seg)), 1e-12)
    k2 = k * (1.0 + (a_lr - 1.0) * ka_ref[...])
    a_vec = -kk
    b_vec = kk * a_lr

    cum = _dot_exact_rhs_lhs(tri_ref[...], lw)
    tot = jnp.concatenate([jnp.broadcast_to(cum[(c + 1) * C - 1:(c + 1) * C, :], (C, W)) for c in range(nchunk)],
                          axis=0)
    p_inv = jnp.exp(-cum)
    p_out = jnp.exp(tot - cum)
    at_ref[...] = a_vec * jnp.exp(cum - lw)
    rt_ref[...] = r * jnp.exp(cum)
    bt_ref[...] = (b_vec * p_inv).astype(BF16)
    kt_ref[...] = (k2 * p_inv).astype(BF16)
    bh_ref[...] = b_vec * p_out
    kh_ref[...] = k2 * p_out
    v_ref[...] = v
    pc_ref[...] = jnp.exp(tot)

    t_i = lax.broadcasted_iota(I32, (C, LANES), 0)
    s_i = lax.broadcasted_iota(I32, (C, LANES), 1) % C
    strict = s_i < t_i
    incl = s_i <= t_i
    r_i = lax.broadcasted_iota(I32, (LANES, LANES), 0)
    c_i = lax.broadcasted_iota(I32, (LANES, LANES), 1)
    same_head = (r_i < RWKV_HEAD) == (c_i < RWKV_HEAD)
    diag = r_i == c_i
    nstage = int(math.log2(C))
    zero = jnp.zeros((C, LANES), F32)
    zsq = jnp.zeros((LANES, LANES), F32)
    units = [(c, p) for c in range(nchunk) for p in range(npair)]

    def sl(c, p):
        return slice(c * C, (c + 1) * C), slice(p * LANES, (p + 1) * LANES)

    def bd2(w):
        wb = w.astype(BF16)
        return jnp.concatenate([_blockdiag_rows(wb[:, :LANES]), _blockdiag_rows(wb[:, LANES:])], axis=1)

    for i, (c, p) in enumerate(units):
        rows, cols = sl(c, p)
        lhs = jnp.concatenate([at_ref[rows, cols], rt_ref[rows, cols]], axis=0).astype(BF16)
        rhs = jnp.concatenate([_blockdiag_rows(bt_ref[rows, cols]),
                               _blockdiag_rows(kt_ref[rows, cols])], axis=0)
        prod = _dot_nt(lhs, rhs)
        la_ref[i] = jnp.where(strict, prod[:C, :LANES], zero).astype(BF16)
        mak_ref[i] = jnp.where(strict, prod[:C, LANES:], zero).astype(BF16)
        arb_ref[i] = jnp.where(incl, prod[C:, :LANES], zero).astype(BF16)
        ark_ref[i] = jnp.where(incl, prod[C:, LANES:], zero).astype(BF16)
    for i, (c, p) in enumerate(units):
        rows, cols = sl(c, p)
        w2 = _dot(mak_ref[i], _blockdiag_rows(v_ref[rows, cols].astype(BF16)))
        wa_ref[i] = jnp.concatenate([at_ref[rows, cols], w2], axis=1)
    l_bufs, w_bufs = (la_ref, lb_ref), (wa_ref, wb_ref)
    for s in range(nstage):
        l_in, l_out = l_bufs[s % 2], l_bufs[(s + 1) % 2]
        w_in, w_out = w_bufs[s % 2], w_bufs[(s + 1) % 2]
        for i in range(len(units)):
            lmat = l_in[i]
            w = w_in[i]
            w_out[i] = w + _dot(lmat, bd2(w))
            if s < nstage - 1:
                l_out[i] = _dot(lmat, _blockdiag_rows(lmat)).astype(BF16)
    w_fin = w_bufs[nstage % 2]
    for i, (c, p) in enumerate(units):
        rows, cols = sl(c, p)
        w = w_fin[i]
        wb = w.astype(BF16)
        vb = v_ref[rows, cols].astype(BF16)
        gg = _dot(arb_ref[i], bd2(w))
        g1_ref[i] = (rt_ref[rows, cols] + gg[:, :LANES]).astype(BF16)
        g2_ref[i] = gg[:, LANES:] + _dot(ark_ref[i], _blockdiag_rows(vb))
        bk_t = jnp.concatenate([bh_ref[rows, cols], kh_ref[rows, cols]], axis=0).T
        hrhs = jnp.concatenate([wb, jnp.concatenate([jnp.zeros((C, LANES), BF16), vb], axis=1)], axis=0)
        hh = _dot(bk_t.astype(BF16), hrhs)
        pc = pc_ref[c * C:c * C + 1, cols]
        h1 = jnp.where(same_head, hh[:, :LANES], zsq) + jnp.where(diag, jnp.broadcast_to(pc, (LANES, LANES)), zsq)
        h1_ref[i] = h1.astype(BF16)
        h2_ref[i] = jnp.where(same_head, hh[:, LANES:], zsq)
    for c in range(nchunk):
        sts = [st_ref[p].astype(BF16) for p in range(npair)]
        for p in range(npair):
            i = c * npair + p
            rows, cols = sl(c, p)
            y_ref[rows, cols] = _dot(g1_ref[i], sts[p]) + g2_ref[i]
            st_ref[p] = _dot(h1_ref[i], sts[p]) + h2_ref[i]

    y = y_ref[...]
    inv_n = 1.0 / RWKV_HEAD
    mean = _dot(y.astype(BF16), seg) * inv_n
    yc = y - mean
    var = _dot((yc * yc).astype(BF16), seg) * inv_n
    yn = yc * lax.rsqrt(var + GN_EPS) * lng_ref[...] + lnb_ref[...]
    bonus = _dot((r * k2 * rk_ref[...]).astype(BF16), seg) * v
    o_ref[0] = ((yn + bonus) * g).astype(BF16)


def _dot_exact_rhs_lhs(ones_bf16, x):
    hi, mid, lo = _split3(x)
    return _dot(ones_bf16, hi) + _dot(ones_bf16, mid) + _dot(ones_bf16, lo)


def _rwkv_call(zr3, tshift_mu, decay_w0, decay_up, iclr_a0, iclr_up, gate_up, k_k, k_a, r_k, lnx_g, lnx_b):
    B, S, _ = zr3.shape
    tt = min(256, S)
    W = RWKV_W
    row = lambda a: a.reshape(1, -1).astype(F32)
    dup = jnp.concatenate([decay_up, jnp.zeros((ICLR_LORA, W), F32)], axis=0)
    iup = jnp.concatenate([jnp.zeros((DECAY_LORA, W), F32), iclr_up], axis=0)
    idx = jnp.arange(W)
    seg = (idx[:, None] // RWKV_HEAD == idx[None, :] // RWKV_HEAD).astype(BF16)
    t = jnp.arange(tt)
    same_chunk = t[:, None] // CHUNK == t[None, :] // CHUNK
    tri = (same_chunk & (t[None, :] <= t[:, None])).astype(BF16)
    const = lambda shape: pl.BlockSpec(shape, lambda b, j: (0,) * len(shape))
    kern = functools.partial(_rwkv_kernel, tt=tt)
    nu = (tt // CHUNK) * (RWKV_HEADS // 2)
    return pl.pallas_call(
        kern,
        grid=(B, S // tt),
        in_specs=[pl.BlockSpec((1, tt, RWKV_IN), lambda b, j: (b, j, 0)),
                  const((1, RWKV_IN)), const((1, W)), const((2 * DECAY_LORA, W)), const((1, W)),
                  const((2 * ICLR_LORA, W)), const((GATE_LORA, W)), const((1, W)), const((1, W)),
                  const((1, W)), const((1, W)), const((1, W)),
                  const((W, W)), const((tt, tt))],
        out_specs=pl.BlockSpec((1, tt, W), lambda b, j: (b, j, 0)),
        out_shape=jax.ShapeDtypeStruct((B, S, W), BF16),
        scratch_shapes=[pltpu.VMEM((1, RWKV_IN), F32),
                        pltpu.VMEM((RWKV_HEADS // 2, LANES, LANES), F32),
                        pltpu.VMEM((tt, W), F32),
                        pltpu.VMEM((tt, W), F32),
                        pltpu.VMEM((tt, W), BF16),
                        pltpu.VMEM((tt, W), BF16),
                        pltpu.VMEM((tt, W), F32),
                        pltpu.VMEM((tt, W), F32),
                        pltpu.VMEM((tt, W), F32),
                        pltpu.VMEM((tt, W), F32),
                        pltpu.VMEM((tt, W), F32),
                        pltpu.VMEM((nu, CHUNK, LANES), BF16),
                        pltpu.VMEM((nu, CHUNK, LANES), BF16),
                        pltpu.VMEM((nu, CHUNK, LANES), BF16),
                        pltpu.VMEM((nu, CHUNK, LANES), BF16),
                        pltpu.VMEM((nu, CHUNK, LANES), BF16),
                        pltpu.VMEM((nu, CHUNK, 2 * LANES), F32),
                        pltpu.VMEM((nu, CHUNK, 2 * LANES), F32),
                        pltpu.VMEM((nu, CHUNK, LANES), BF16),
                        pltpu.VMEM((nu, CHUNK, LANES), F32),
                        pltpu.VMEM((nu, LANES, LANES), BF16),
                        pltpu.VMEM((nu, LANES, LANES), F32)],
        compiler_params=_cparams(("parallel", "arbitrary")),
        name="rwkv",
    )(zr3, row(tshift_mu), row(decay_w0), dup, row(iclr_a0), iup.astype(BF16), gate_up.astype(BF16), row(k_k),
      row(k_a), row(r_k), row(lnx_g), row(lnx_b), seg, tri)


def _merge_kernel(x_ref, attn_ref, rw_ref, ga_ref, gr_ref, wa_ref, wr_ref, wo_ref, g1_ref,
                  n2_ref, sc_ref, sh_ref, rwt_ref, rb_ref, x1_ref, h2_ref, gate_ref):
    a = _dot(attn_ref[...], wa_ref[...])
    rr = _dot(rw_ref[...], wr_ref[...])
    mixed = _sigmoid(ga_ref[...].astype(F32)) * a + _sigmoid(gr_ref[...].astype(F32)) * rr
    x1 = x_ref[...] + g1_ref[0] * _dot(mixed.astype(BF16), wo_ref[...])
    x1_ref[...] = x1
    ms = jnp.mean(x1 * x1, axis=-1, keepdims=True)
    h2 = x1 * lax.rsqrt(ms + RMS_EPS) * n2_ref[...]
    h2 = h2 * (1.0 + sc_ref[0]) + sh_ref[0]
    h2_ref[...] = h2.astype(BF16)

    tm = x1.shape[0]
    E, G, EG = N_EXPERTS, N_GROUPS, N_EXPERTS // N_GROUPS
    scores = _sigmoid(_dot3(rwt_ref[...], h2, nt=True))
    choice = scores + rb_ref[...]
    c3 = choice.reshape(G, EG, tm)
    e_i = lax.broadcasted_iota(I32, (G, EG, tm), 1)
    m1 = jnp.max(c3, axis=1, keepdims=True)
    first = jnp.min(jnp.where(c3 == m1, e_i, EG), axis=1, keepdims=True)
    m2 = jnp.max(jnp.where(e_i == first, -jnp.inf, c3), axis=1, keepdims=True)
    grp = (m1 + m2).reshape(G, tm)
    g_i = lax.broadcasted_iota(I32, (G, tm), 0)
    rank = jnp.zeros((G, tm), I32)
    for o in range(G):
        other = grp[o:o + 1, :]
        rank = rank + jnp.where((other > grp) | ((other == grp) & (o < g_i)), 1, 0)
    gsel = rank < TOPK_GROUPS
    esel = jnp.broadcast_to(gsel.reshape(G, 1, tm), (G, EG, tm)).reshape(E, tm)
    mc = jnp.where(esel, choice, -jnp.inf)
    x_i = lax.broadcasted_iota(I32, (E, tm), 0)
    erank = jnp.zeros((E, tm), I32)
    for o in range(E):
        other = mc[o:o + 1, :]
        erank = erank + jnp.where((other > mc) | ((other == mc) & (o < x_i)), 1, 0)
    top = erank < MOE_TOPK
    gw = jnp.where(top, scores, 0.0)
    gw = gw / jnp.sum(gw, axis=0, keepdims=True) * ROUTED_SCALE
    gpad = jnp.concatenate([gw, jnp.zeros((LANES - E, tm), F32)], axis=0)
    gate_ref[...] = gpad.T


def _merge_call(x2, attn, rw, ga, gr, wa, wr, wo, g1, norm2_g, sc2, sh2, router_w, router_bias, S):
    T, D = x2.shape
    B = T // S
    tm = min(512, S)
    tpb = S // tm
    row = lambda i: (i, 0)
    per_b = lambda i: (i // tpb, 0, 0)
    const = lambda shape: pl.BlockSpec(shape, lambda i: (0,) * len(shape))
    return pl.pallas_call(
        _merge_kernel,
        grid=(T // tm,),
        in_specs=[pl.BlockSpec((tm, D), row), pl.BlockSpec((tm, ATTN_W), row), pl.BlockSpec((tm, RWKV_W), row),
                  pl.BlockSpec((tm, D), row), pl.BlockSpec((tm, D), row),
                  const((ATTN_W, D)), const((RWKV_W, D)), const((D, D)),
                  pl.BlockSpec((1, 1, D), per_b), const((1, D)),
                  pl.BlockSpec((1, 1, D), per_b), pl.BlockSpec((1, 1, D), per_b),
                  const((N_EXPERTS, D)), const((N_EXPERTS, 1))],
        out_specs=[pl.BlockSpec((tm, D), row), pl.BlockSpec((tm, D), row), pl.BlockSpec((tm, LANES), row)],
        out_shape=[jax.ShapeDtypeStruct((T, D), F32), jax.ShapeDtypeStruct((T, D), BF16),
                   jax.ShapeDtypeStruct((T, LANES), F32)],
        compiler_params=_cparams(("parallel",)),
        name="merge",
    )(x2, attn, rw, ga, gr, wa, wr, wo, g1.reshape(B, 1, D), norm2_g.reshape(1, D),
      sc2.reshape(B, 1, D), sh2.reshape(B, 1, D), router_w.T, router_bias.reshape(N_EXPERTS, 1))


def _moe_kernel(h_ref, gate_ref, x1_ref, g2_ref, fg_ref, eg_ref, eu_ref, ed_ref, sg_ref, su_ref, sd_ref,
                o_ref, acc_ref, *, eg_per_step, n_steps):
    j = pl.program_id(1)
    h = h_ref[...]

    @pl.when(j == 0)
    def _():
        a = _dot(h, sg_ref[...])
        u = _dot(h, su_ref[...])
        acc_ref[...] = _dot((a * _sigmoid(a) * u).astype(BF16), sd_ref[...])

    gates = gate_ref[...]
    lane = lax.broadcasted_iota(I32, gates.shape, 1)
    acts = []
    for i in range(eg_per_step):
        e = j * eg_per_step + i
        gcol = jnp.sum(jnp.where(lane == e, gates, 0.0), axis=1, keepdims=True)
        a = _dot(h, eg_ref[i])
        u = _dot(h, eu_ref[i])
        acts.append((a * _sigmoid(a) * u * gcol).astype(BF16))
    act = jnp.concatenate(acts, axis=1)
    acc_ref[...] += _dot(act, ed_ref[...])

    @pl.when(j == n_steps - 1)
    def _():
        x2 = x1_ref[...] + g2_ref[0] * acc_ref[...]
        ms = jnp.mean(x2 * x2, axis=-1, keepdims=True)
        o_ref[...] = x2 * lax.rsqrt(ms + RMS_EPS) * fg_ref[...]


def _moe_call(h2, gate, x1, g2, final_g, eg, eu, ed, sg, su, sd, S):
    T, D = h2.shape
    B = T // S
    tm = min(1024, S)
    tpb = S // tm
    eps = 4
    n_steps = N_EXPERTS // eps
    FF = EXPERT_FF
    kern = functools.partial(_moe_kernel, eg_per_step=eps, n_steps=n_steps)
    row = lambda i, j: (i, 0)
    const = lambda shape: pl.BlockSpec(shape, lambda i, j: (0,) * len(shape))
    return pl.pallas_call(
        kern,
        grid=(T // tm, n_steps),
        in_specs=[pl.BlockSpec((tm, D), row), pl.BlockSpec((tm, LANES), row), pl.BlockSpec((tm, D), row),
                  pl.BlockSpec((1, 1, D), lambda i, j: (i // tpb, 0, 0)), const((1, D)),
                  pl.BlockSpec((eps, D, FF), lambda i, j: (j, 0, 0)),
                  pl.BlockSpec((eps, D, FF), lambda i, j: (j, 0, 0)),
                  pl.BlockSpec((eps * FF, D), lambda i, j: (j, 0)),
                  const((D, FF)), const((D, FF)), const((FF, D))],
        out_specs=pl.BlockSpec((tm, D), row),
        out_shape=jax.ShapeDtypeStruct((T, D), F32),
        scratch_shapes=[pltpu.VMEM((tm, D), F32)],
        compiler_params=_cparams(("parallel", "arbitrary")),
        name="moe",
    )(h2, gate, x1, g2.reshape(B, 1, D), final_g.reshape(1, D), eg, eu, ed.reshape(N_EXPERTS * FF, D), sg, su, sd)


def _layer(x2, c, S, ada_w, ada_b, norm1_g, w_in, rel_bias, tshift_mu, decay_w0, decay_up, iclr_a0, iclr_up,
           gate_up, k_k, k_a, r_k, lnx_g, lnx_b, w_attn_br, w_rwkv_br, w_out, norm2_g, router_w, router_bias,
           exp_gate, exp_up, exp_down, sh_gate, sh_up, sh_down, final_g):
    T, D = x2.shape
    B = T // S
    mod = _mod_call(c, ada_w, ada_b)
    sh1, sc1, g1, sh2, sc2, g2 = jnp.split(mod, 6, axis=-1)

    q, k, vt, iq, ik4, iwt, zr, ga, gr = _inproj_call(x2, norm1_g, sc1, sh1, w_in, S)

    ta = min(256, S)
    assert ta >= LANES and S % ta == 0
    top_k = min(TOPK_MAX, S // 4)
    seq = lambda a: a.reshape(B, S, a.shape[-1])
    mask = _index_call(seq(iq), iwt, seq(ik4), ta, top_k)
    bias_tiles = _bias_call(rel_bias, ta)
    attn = _attn_call(seq(q), seq(k), vt, mask, bias_tiles, rel_bias, ta).reshape(T, ATTN_W)

    rw = _rwkv_call(zr.reshape(B, S, RWKV_IN), tshift_mu, decay_w0, decay_up, iclr_a0, iclr_up, gate_up,
                    k_k, k_a, r_k, lnx_g, lnx_b).reshape(T, RWKV_W)

    x1, h2, gate = _merge_call(x2, attn, rw, ga, gr, w_attn_br.astype(BF16), w_rwkv_br.astype(BF16),
                               w_out.astype(BF16), g1, norm2_g, sc2, sh2, router_w, router_bias, S)
    return _moe_call(h2, gate, x1, g2, final_g, exp_gate.astype(BF16), exp_up.astype(BF16),
                     exp_down.astype(BF16), sh_gate.astype(BF16), sh_up.astype(BF16), sh_down.astype(BF16), S)


def kernel(x, c, ada_w, ada_b, norm1_g, w_in, rel_bias, tshift_mu, decay_w0, decay_up, iclr_a0, iclr_up, gate_up, k_k, k_a, r_k, lnx_g, lnx_b, w_attn_br, w_rwkv_br, w_out, norm2_g, router_w, router_bias, exp_gate, exp_up, exp_down, sh_gate, sh_up, sh_down, final_g):
    B, S, D = x.shape
    depth = ada_w.shape[0]
    assert depth == 1, "the final RMSNorm is fused into the (single) layer's MoE kernel"
    out = _layer(x.reshape(B * S, D), c, S, ada_w[0], ada_b[0], norm1_g[0], w_in[0], rel_bias, tshift_mu[0],
                 decay_w0[0], decay_up[0], iclr_a0[0], iclr_up[0], gate_up[0], k_k[0], k_a[0], r_k[0],
                 lnx_g[0], lnx_b[0], w_attn_br[0], w_rwkv_br[0], w_out[0], norm2_g[0], router_w[0],
                 router_bias[0], exp_gate[0], exp_up[0], exp_down[0], sh_gate[0], sh_up[0], sh_down[0], final_g)
    return out.reshape(B, S, D)
```

```python
import functools
import math

import jax
import jax.numpy as jnp
from jax import lax
from jax.experimental import pallas as pl
from jax.experimental.pallas import tpu as pltpu
from jax.experimental.pallas import tpu_sc as plsc

F32 = jnp.float32
BF16 = jnp.bfloat16
I32 = jnp.int32
I16 = jnp.int16

RMS_EPS = 1e-6
D_MODEL = 1024
N_ATTN_HEADS = 8
ATTN_HEAD_DIM = 64
ATTN_W = 512
IDX_HEADS = 16
IDX_DIM = 32
IDX_Q = 512
TOPK_MAX = 256
N_BUCKETS = 32
MAX_DISTANCE = 128
RWKV_HEADS = 8
RWKV_HEAD = 64
RWKV_W = 512
DECAY_LORA = 64
ICLR_LORA = 64
GATE_LORA = 128
RWKV_IN = 1792
GN_EPS = 64e-5
N_EXPERTS = 64
N_GROUPS = 8
TOPK_GROUPS = 4
MOE_TOPK = 8
EXPERT_FF = 256
ROUTED_SCALE = 2.5

LANES = 128
VMEM_LIMIT = 56 * 1024 * 1024
CHUNK = 64
INT_MIN = -2147483648
KEY_NEG_INF = -2139095041

NT_DIMS = (((1,), (1,)), ((), ()))


def _cparams(sem):
    return pltpu.CompilerParams(dimension_semantics=sem, vmem_limit_bytes=VMEM_LIMIT)


def _dot(a, b):
    return jnp.dot(a, b, preferred_element_type=F32)


def _dot_nt(a, b):
    return lax.dot_general(a, b, NT_DIMS, preferred_element_type=F32)


def _split2(x):
    hi = x.astype(BF16)
    lo = (x - hi.astype(F32)).astype(BF16)
    return hi, lo


def _split3(x):
    hi = x.astype(BF16)
    r1 = x - hi.astype(F32)
    mid = r1.astype(BF16)
    lo = (r1 - mid.astype(F32)).astype(BF16)
    return hi, mid, lo


def _dot_exact_rhs(x, ones_bf16, terms=2):
    parts = _split3(x) if terms == 3 else _split2(x)
    out = _dot(parts[0], ones_bf16)
    for p in parts[1:]:
        out = out + _dot(p, ones_bf16)
    return out


def _dot3(a, b, nt=False):
    ah, al = _split2(a)
    bh, bl = _split2(b)
    f = _dot_nt if nt else _dot
    return f(ah, bh) + f(ah, bl) + f(al, bh)


def _sigmoid(x):
    return 1.0 / (1.0 + jnp.exp(-x))


def _mod_kernel(c_ref, w_ref, b_ref, o_ref):
    c = c_ref[...]
    s = c * _sigmoid(c)
    o_ref[...] = _dot3(s, w_ref[...]) + b_ref[...]


def _mod_call(c, ada_w, ada_b):
    B, D = c.shape
    N = ada_w.shape[1]
    tn = 1024
    return pl.pallas_call(
        _mod_kernel,
        grid=(N // tn,),
        in_specs=[pl.BlockSpec((B, D), lambda j: (0, 0)),
                  pl.BlockSpec((D, tn), lambda j: (0, j)),
                  pl.BlockSpec((1, tn), lambda j: (0, j))],
        out_specs=pl.BlockSpec((B, tn), lambda j: (0, j)),
        out_shape=jax.ShapeDtypeStruct((B, N), F32),
        compiler_params=_cparams(("arbitrary",)),
        name="mod",
    )(c, ada_w, ada_b.reshape(1, N))


_OFF_Q, _OFF_K, _OFF_IQ, _OFF_IK4, _OFF_ZR, _OFF_GA, _OFF_GR, _N_PACK = (
    0, 512, 1024, 1536, 2048, 3840, 4864, 5888)
IDX_PER_BLOCK = LANES // IDX_DIM


def _pack_w_in(w_in):
    D = w_in.shape[0]
    w_ik = w_in[:, 2048:2080]
    ik4 = jnp.zeros((D, IDX_PER_BLOCK * LANES), w_in.dtype)
    for j in range(IDX_PER_BLOCK):
        ik4 = lax.dynamic_update_slice(ik4, w_ik, (0, j * LANES + j * IDX_DIM))
    w_pack = jnp.concatenate([w_in[:, 0:1024], w_in[:, 1536:2048], ik4, w_in[:, 2096:]], axis=1).astype(BF16)
    return w_pack, w_in[:, 1024:1536].T.astype(BF16), w_in[:, 2080:2096].T.astype(BF16)


def _inproj_kernel(x_ref, g_ref, sc_ref, sh_ref, w_ref, wvt_ref, wiwt_ref,
                   q_ref, k_ref, vt_ref, iq_ref, ik4_ref, iwt_ref, zr_ref, ga_ref, gr_ref):
    x = x_ref[...]
    ms = jnp.mean(x * x, axis=-1, keepdims=True)
    h = x * lax.rsqrt(ms + RMS_EPS) * g_ref[...]
    h = h * (1.0 + sc_ref[0]) + sh_ref[0]
    hb = h.astype(BF16)

    def proj(lo, hi):
        return _dot(hb, w_ref[:, lo:hi])

    q_ref[...] = (proj(_OFF_Q, _OFF_K) * (ATTN_HEAD_DIM ** -0.5)).astype(BF16)
    k_ref[...] = proj(_OFF_K, _OFF_IQ).astype(BF16)
    iq_ref[...] = proj(_OFF_IQ, _OFF_IK4).astype(BF16)
    ik4_ref[...] = proj(_OFF_IK4, _OFF_ZR).astype(BF16)
    zr_ref[...] = proj(_OFF_ZR, _OFF_GA)
    ga_ref[...] = proj(_OFF_GA, _OFF_GR).astype(BF16)
    gr_ref[...] = proj(_OFF_GR, _N_PACK).astype(BF16)
    vt_ref[0] = _dot_nt(wvt_ref[...], hb).astype(BF16)
    iwt_ref[0] = _dot_nt(wiwt_ref[...], hb)


def _inproj_call(x2, norm_g, sc, sh, w_in, S):
    T, D = x2.shape
    B = T // S
    tm = min(512, S)
    tpb = S // tm
    w_pack, wvt, wiwt = _pack_w_in(w_in)
    row = lambda i: (i, 0)
    per_b = lambda i: (i // tpb, 0, 0)
    colblk = lambda i: (i // tpb, 0, i % tpb)
    const = lambda shape: pl.BlockSpec(shape, lambda i: (0,) * len(shape), pipeline_mode=pl.Buffered(1))
    rows_out = ((512, BF16), (512, BF16), (512, BF16), (512, BF16), (RWKV_IN, F32), (D, BF16), (D, BF16))
    out_specs = [pl.BlockSpec((tm, w), row) for w, _ in rows_out]
    out_shape = [jax.ShapeDtypeStruct((T, w), dt) for w, dt in rows_out]
    out_specs[2:2] = [pl.BlockSpec((1, ATTN_W, tm), colblk)]
    out_shape[2:2] = [jax.ShapeDtypeStruct((B, ATTN_W, S), BF16)]
    out_specs[5:5] = [pl.BlockSpec((1, IDX_HEADS, tm), colblk)]
    out_shape[5:5] = [jax.ShapeDtypeStruct((B, IDX_HEADS, S), F32)]
    return pl.pallas_call(
        _inproj_kernel,
        grid=(T // tm,),
        in_specs=[pl.BlockSpec((tm, D), row),
                  pl.BlockSpec((1, D), lambda i: (0, 0)),
                  pl.BlockSpec((1, 1, D), per_b),
                  pl.BlockSpec((1, 1, D), per_b),
                  const((D, _N_PACK)), const((ATTN_W, D)), const((IDX_HEADS, D))],
        out_specs=out_specs,
        out_shape=out_shape,
        compiler_params=_cparams(("parallel",)),
        name="inproj",
    )(x2, norm_g.reshape(1, D), sc.reshape(B, 1, D), sh.reshape(B, 1, D), w_pack, wvt, wiwt)


def _t5_bucket(rel):
    n = jnp.maximum(rel, 0)
    max_exact = N_BUCKETS // 2
    nf = jnp.maximum(n, 1).astype(F32)
    large = max_exact + (jnp.log(nf / max_exact) / math.log(MAX_DISTANCE / max_exact)
                         * (N_BUCKETS - max_exact)).astype(I32)
    large = jnp.minimum(large, N_BUCKETS - 1)
    return jnp.where(n < max_exact, n, large)


def _bias_kernel(bucket_ref, rb_ref, o_ref):
    h = pl.program_id(0)
    bk = bucket_ref[...]
    out = jnp.zeros(bk.shape, F32)
    for b in range(N_BUCKETS):
        out = jnp.where(bk == b, rb_ref[b, h], out)
    o_ref[0] = out


def _bias_call(rel_bias, tq):
    r = jnp.arange(tq, dtype=I32)[None, :]
    c = jnp.arange(tq, dtype=I32)[:, None]
    buckets = jnp.stack([_t5_bucket(r - c), _t5_bucket(tq + r - c)])
    return pl.pallas_call(
        _bias_kernel,
        grid=(N_ATTN_HEADS,),
        in_specs=[pl.BlockSpec((2, tq, tq), lambda h: (0, 0, 0)),
                  pl.BlockSpec(memory_space=pltpu.SMEM)],
        out_specs=pl.BlockSpec((1, 2, tq, tq), lambda h: (h, 0, 0, 0)),
        out_shape=jax.ShapeDtypeStruct((N_ATTN_HEADS, 2, tq, tq), F32),
        compiler_params=_cparams(("arbitrary",)),
        name="bias",
    )(buckets, rel_bias)


def _index_kernel(iq_ref, iwt_ref, ik4_ref, mask_ref, key_ref, k16_ref, *, t, nk, top_k, scale):
    qi = pl.program_id(1)
    nkt = qi + 1
    ksub = LANES
    qpos = qi * t + lax.broadcasted_iota(I32, (ksub, t), 1)

    def score_tile(kt, carry):
        kbase = pl.multiple_of(kt * t, t)
        for ks in range(t // ksub):
            acc = jnp.zeros((ksub, t), F32)
            ik_rows = ik4_ref[0, pl.ds(kbase + ks * ksub, ksub), :]
            ik_stack = jnp.concatenate([ik_rows[:, j * LANES:(j + 1) * LANES] for j in range(IDX_PER_BLOCK)],
                                       axis=0)
            for g in range(IDX_HEADS // IDX_PER_BLOCK):
                d4 = _dot_nt(ik_stack, iq_ref[0, :, g * LANES:(g + 1) * LANES])
                for j in range(IDX_PER_BLOCK):
                    h = g * IDX_PER_BLOCK + j
                    acc = acc + jnp.maximum(d4[j * ksub:(j + 1) * ksub], 0.0) * iwt_ref[0, h:h + 1, :]
            s = acc * scale
            kpos = kt * t + ks * ksub + lax.broadcasted_iota(I32, (ksub, t), 0)
            s = jnp.where(kpos <= qpos, s, -jnp.inf)
            bits = pltpu.bitcast(s, I32)
            key = bits ^ ((bits >> 31) & 0x7FFFFFFF)
            key_ref[kt, ks * ksub:(ks + 1) * ksub, :] = key
            k16_ref[kt, ks * ksub:(ks + 1) * ksub, :] = (key >> 16).astype(I16)
        return carry

    lax.fori_loop(0, nkt, score_tile, 0)

    pack = 16

    def search16():
        def bit_body(i, ans):
            cand = ans | lax.shift_left(jnp.int32(1), 15 - i)
            cand16 = (cand - 32768).astype(I16)

            def cnt_body(kt, acc):
                one = jnp.where(k16_ref[kt] >= cand16, jnp.int16(1), jnp.int16(0))
                for r in range(t // pack):
                    acc = acc + one[r * pack:(r + 1) * pack, :]
                return acc

            acc = lax.fori_loop(0, nkt, cnt_body, jnp.zeros((pack, t), I16))
            cnt = jnp.sum(acc.astype(I32), axis=0, keepdims=True)
            return jnp.where(cnt >= top_k, cand, ans)

        return lax.fori_loop(0, 16, bit_body, jnp.zeros((1, t), I32))

    hi = search16() - 32768

    def remap_body(kt, carry):
        key = key_ref[kt]
        khi = key >> 16
        lo = (key & 0xFFFF) - 32768
        k16_ref[kt] = jnp.where(khi > hi, 32767, jnp.where(khi == hi, lo, -32768)).astype(I16)
        return carry

    lax.fori_loop(0, nkt, remap_body, 0)
    thr = hi * 65536 + search16()

    def mask_body(kt, carry):
        keys = key_ref[kt]
        sel = (keys >= thr) & (keys > KEY_NEG_INF)
        mask_ref[0, 0, kt] = jnp.where(sel, 0.0, -jnp.inf).astype(BF16)
        return carry

    lax.fori_loop(0, nkt, mask_body, 0)

    def fill_body(kt, carry):
        mask_ref[0, 0, kt] = jnp.full((t, t), -jnp.inf, BF16)
        return carry

    lax.fori_loop(nkt, nk, fill_body, 0)


def _index_call(iq, iwt, ik4, t, top_k):
    B, S, _ = iq.shape
    n = S // t
    scale = (IDX_HEADS ** -0.5) * (IDX_DIM ** -0.5)
    kern = functools.partial(_index_kernel, t=t, nk=n, top_k=top_k, scale=scale)
    return pl.pallas_call(
        kern,
        grid=(B, n),
        in_specs=[pl.BlockSpec((1, t, IDX_Q), lambda b, i: (b, i, 0)),
                  pl.BlockSpec((1, IDX_HEADS, t), lambda b, i: (b, 0, i)),
                  pl.BlockSpec((1, S, IDX_PER_BLOCK * LANES), lambda b, i: (b, 0, 0))],
        out_specs=pl.BlockSpec((1, 1, n, t, t), lambda b, i: (b, i, 0, 0, 0)),
        out_shape=jax.ShapeDtypeStruct((B, n, n, t, t), BF16),
        scratch_shapes=[pltpu.VMEM((n, t, t), I32), pltpu.VMEM((n, t, t), I16)],
        compiler_params=_cparams(("parallel", "arbitrary")),
        name="index",
    )(iq, iwt, ik4)


ONES_ROWS = 16


def _attn_kernel(qi_tab, kt_tab, q_ref, k_ref, vt_ref, mask_ref, bias_ref, rb_ref, o_ref,
                 qz_ref, m_ref, acc_ref, s_ref, *, t):
    s_id = pl.program_id(1)
    qi = qi_tab[s_id]
    kt = kt_tab[s_id]
    dh = ATTN_HEAD_DIM

    @pl.when(kt == 0)
    def _():
        m_ref[...] = jnp.full(m_ref.shape, -jnp.inf, F32)
        acc_ref[...] = jnp.zeros(acc_ref.shape, F32)
        lane = lax.broadcasted_iota(I32, (t, LANES), 1)
        for h in range(N_ATTN_HEADS):
            blk = q_ref[0, :, (h // 2) * LANES:(h // 2 + 1) * LANES]
            keep = (lane < dh) if h % 2 == 0 else (lane >= dh)
            qz_ref[h] = jnp.where(keep, blk, jnp.zeros_like(blk))

    def step(bias_of_head):
        maskf = mask_ref[0, 0, 0].astype(F32)
        ones = jnp.ones((ONES_ROWS, t), BF16)

        for h in range(N_ATTN_HEADS):
            k_blk = k_ref[0, :, (h // 2) * LANES:(h // 2 + 1) * LANES]
            s_ref[h] = _dot_nt(k_blk, qz_ref[h]) + bias_of_head(h) + maskf
        for h in range(N_ATTN_HEADS):
            s = s_ref[h]
            m_old = m_ref[h:h + 1, :]
            m_cur = jnp.max(jnp.max(s.reshape(t // 8, 8, t), axis=0), axis=0, keepdims=True)
            m_new = jnp.maximum(m_old, m_cur)
            m_safe = jnp.where(m_new == -jnp.inf, 0.0, m_new)
            alpha = jnp.exp(m_old - m_safe)
            p = jnp.exp(s - m_safe).astype(BF16)
            v_aug = jnp.concatenate([vt_ref[0, h * dh:(h + 1) * dh, :], ones], axis=0)
            acc_ref[h] = alpha * acc_ref[h] + _dot(v_aug, p)
            m_ref[h:h + 1, :] = m_new

    @pl.when(kt == qi)
    def _():
        step(lambda h: bias_ref[h, 0])

    @pl.when(kt == qi - 1)
    def _():
        step(lambda h: bias_ref[h, 1])

    @pl.when(kt < qi - 1)
    def _():
        step(lambda h: rb_ref[N_BUCKETS - 1, h])

    @pl.when(kt == qi)
    def _():
        outs = []
        for h in range(N_ATTN_HEADS):
            a = acc_ref[h]
            outs.append(a[:dh, :] / a[dh:dh + 1, :])
        o_ref[0] = jnp.concatenate(outs, axis=0).T.astype(BF16)


def _attn_call(q, k, vt, mask, bias_tiles, rel_bias, t):
    B, S, W = q.shape
    n = S // t
    H = N_ATTN_HEADS
    qi_tab = jnp.asarray([i for i in range(n) for _ in range(i + 1)], I32)
    kt_tab = jnp.asarray([j for i in range(n) for j in range(i + 1)], I32)
    kern = functools.partial(_attn_kernel, t=t)
    grid_spec = pltpu.PrefetchScalarGridSpec(
        num_scalar_prefetch=2,
        grid=(B, int(qi_tab.shape[0])),
        in_specs=[pl.BlockSpec((1, t, W), lambda b, s, qt, kt: (b, qt[s], 0)),
                  pl.BlockSpec((1, t, W), lambda b, s, qt, kt: (b, kt[s], 0)),
                  pl.BlockSpec((1, W, t), lambda b, s, qt, kt: (b, 0, kt[s])),
                  pl.BlockSpec((1, 1, 1, t, t), lambda b, s, qt, kt: (b, qt[s], kt[s], 0, 0)),
                  pl.BlockSpec((H, 2, t, t), lambda b, s, qt, kt: (0, 0, 0, 0), pipeline_mode=pl.Buffered(1)),
                  pl.BlockSpec(memory_space=pltpu.SMEM)],
        out_specs=pl.BlockSpec((1, t, W), lambda b, s, qt, kt: (b, qt[s], 0)),
        scratch_shapes=[pltpu.VMEM((H, t, LANES), BF16),
                        pltpu.VMEM((H, t), F32),
                        pltpu.VMEM((H, ATTN_HEAD_DIM + ONES_ROWS, t), F32),
                        pltpu.VMEM((H, t, t), F32)])
    return pl.pallas_call(
        kern,
        grid_spec=grid_spec,
        out_shape=jax.ShapeDtypeStruct((B, S, W), BF16),
        compiler_params=_cparams(("parallel", "arbitrary")),
        name="attn",
    )(qi_tab, kt_tab, q, k, vt, mask, bias_tiles, rel_bias)


def _blockdiag_rows(x):
    lane = lax.broadcasted_iota(I32, x.shape, 1)
    zero = jnp.zeros_like(x)
    return jnp.concatenate([jnp.where(lane < RWKV_HEAD, x, zero),
                            jnp.where(lane >= RWKV_HEAD, x, zero)], axis=0)


def _rwkv_kernel(z_ref, mu_ref, w0_ref, dup_ref, a0_ref, iup_ref, gup_ref, kk_ref, ka_ref, rk_ref,
                 lng_ref, lnb_ref, seg_ref, tri_ref, o_ref,
                 prev_ref, st_ref, at_ref, rt_ref, bt_ref, kt_ref, bh_ref, kh_ref, v_ref, pc_ref, y_ref,
                 la_ref, lb_ref, mak_ref, arb_ref, ark_ref, wa_ref, wb_ref, g1_ref, g2_ref, h1_ref, h2_ref,
                 *, tt):
    j = pl.program_id(1)
    W = RWKV_W
    C = CHUNK
    nchunk = tt // C
    npair = RWKV_HEADS // 2

    @pl.when(j == 0)
    def _():
        prev_ref[...] = jnp.zeros(prev_ref.shape, F32)
        st_ref[...] = jnp.zeros(st_ref.shape, F32)

    z = z_ref[0]
    row = lax.broadcasted_iota(I32, z.shape, 0)
    z_prev = jnp.where(row == 0, prev_ref[...], pltpu.roll(z, 1, axis=0))
    prev_ref[...] = z[tt - 1:tt, :]
    z = z + mu_ref[...] * (z_prev - z)

    r = z[:, 0:W]
    k = z[:, W:2 * W]
    v = z[:, 2 * W:3 * W]
    wdad = z[:, 3 * W:3 * W + 2 * DECAY_LORA]
    gd = z[:, 3 * W + 2 * DECAY_LORA:]

    w_pre = w0_ref[...] + _dot3(jnp.tanh(wdad), dup_ref[...])
    neg = -w_pre
    softplus = jnp.maximum(neg, 0.0) + jnp.log(1.0 + jnp.exp(-jnp.abs(neg)))
    lw = -jnp.exp(-softplus - 0.5)
    a_lr = _sigmoid(a0_ref[...] + _dot(wdad.astype(BF16), iup_ref[...]))
    g = _dot(_sigmoid(gd).astype(BF16), gup_ref[...])

    seg = seg_ref[...]
    kk = k * kk_ref[...]
    kk = kk / jnp.maximum(jnp.sqrt(_dot_exact_rhs(kk * kk, seg)), 1e-12)
    k2 = k * (1.0 + (a_lr - 1.0) * ka_ref[...])
    a_vec = -kk
    b_vec = kk * a_lr

    cum = _dot_exact_rhs_lhs(tri_ref[...], lw)
    tot = jnp.concatenate([jnp.broadcast_to(cum[(c + 1) * C - 1:(c + 1) * C, :], (C, W)) for c in range(nchunk)],
                          axis=0)
    p_inv = jnp.exp(-cum)
    p_out = jnp.exp(tot - cum)
    at_ref[...] = a_vec * jnp.exp(cum - lw)
    rt_ref[...] = r * jnp.exp(cum)
    bt_ref[...] = (b_vec * p_inv).astype(BF16)
    kt_ref[...] = (k2 * p_inv).astype(BF16)
    bh_ref[...] = b_vec * p_out
    kh_ref[...] = k2 * p_out
    v_ref[...] = v
    pc_ref[...] = jnp.exp(tot)

    t_i = lax.broadcasted_iota(I32, (C, LANES), 0)
    s_i = lax.broadcasted_iota(I32, (C, LANES), 1) % C
    strict = s_i < t_i
    incl = s_i <= t_i
    r_i = lax.broadcasted_iota(I32, (LANES, LANES), 0)
    c_i = lax.broadcasted_iota(I32, (LANES, LANES), 1)
    same_head = (r_i < RWKV_HEAD) == (c_i < RWKV_HEAD)
    diag = r_i == c_i
    nstage = int(math.log2(C))
    zero = jnp.zeros((C, LANES), F32)
    zsq = jnp.zeros((LANES, LANES), F32)
    units = [(c, p) for c in range(nchunk) for p in range(npair)]

    def sl(c, p):
        return slice(c * C, (c + 1) * C), slice(p * LANES, (p + 1) * LANES)

    def bd2(w):
        wb = w.astype(BF16)
        return jnp.concatenate([_blockdiag_rows(wb[:, :LANES]), _blockdiag_rows(wb[:, LANES:])], axis=1)

    for i, (c, p) in enumerate(units):
        rows, cols = sl(c, p)
        lhs = jnp.concatenate([at_ref[rows, cols], rt_ref[rows, cols]], axis=0).astype(BF16)
        rhs = jnp.concatenate([_blockdiag_rows(bt_ref[rows, cols]),
                               _blockdiag_rows(kt_ref[rows, cols])], axis=0)
        prod = _dot_nt(lhs, rhs)
        la_ref[i] = jnp.where(strict, prod[:C, :LANES], zero).astype(BF16)
        mak_ref[i] = jnp.where(strict, prod[:C, LANES:], zero).astype(BF16)
        arb_ref[i] = jnp.where(incl, prod[C:, :LANES], zero).astype(BF16)
        ark_ref[i] = jnp.where(incl, prod[C:, LANES:], zero).astype(BF16)
    for i, (c, p) in enumerate(units):
        rows, cols = sl(c, p)
        w2 = _dot(mak_ref[i], _blockdiag_rows(v_ref[rows, cols].astype(BF16)))
        wa_ref[i] = jnp.concatenate([at_ref[rows, cols], w2], axis=1)
    l_bufs, w_bufs = (la_ref, lb_ref), (wa_ref, wb_ref)
    for s in range(nstage):
        l_in, l_out = l_bufs[s % 2], l_bufs[(s + 1) % 2]
        w_in, w_out = w_bufs[s % 2], w_bufs[(s + 1) % 2]
        for i in range(len(units)):
            lmat = l_in[i]
            w = w_in[i]
            w_out[i] = w + _dot(lmat, bd2(w))
            if s < nstage - 1:
                l_out[i] = _dot(lmat, _blockdiag_rows(lmat)).astype(BF16)
    w_fin = w_bufs[nstage % 2]
    for i, (c, p) in enumerate(units):
        rows, cols = sl(c, p)
        w = w_fin[i]
        wb = w.astype(BF16)
        vb = v_ref[rows, cols].astype(BF16)
        gg = _dot(arb_ref[i], bd2(w))
        g1_ref[i] = (rt_ref[rows, cols] + gg[:, :LANES]).astype(BF16)
        g2_ref[i] = gg[:, LANES:] + _dot(ark_ref[i], _blockdiag_rows(vb))
        bk_t = jnp.concatenate([bh_ref[rows, cols], kh_ref[rows, cols]], axis=0).T
        hrhs = jnp.concatenate([wb, jnp.concatenate([jnp.zeros((C, LANES), BF16), vb], axis=1)], axis=0)
        hh = _dot(bk_t.astype(BF16), hrhs)
        pc = pc_ref[c * C:c * C + 1, cols]
        h1 = jnp.where(same_head, hh[:, :LANES], zsq) + jnp.where(diag, jnp.broadcast_to(pc, (LANES, LANES)), zsq)
        h1_ref[i] = h1.astype(BF16)
        h2_ref[i] = jnp.where(same_head, hh[:, LANES:], zsq)
    for c in range(nchunk):
        sts = [st_ref[p].astype(BF16) for p in range(npair)]
        for p in range(npair):
            i = c * npair + p
            rows, cols = sl(c, p)
            y_ref[rows, cols] = _dot(g1_ref[i], sts[p]) + g2_ref[i]
            st_ref[p] = _dot(h1_ref[i], sts[p]) + h2_ref[i]

    y = y_ref[...]
    inv_n = 1.0 / RWKV_HEAD
    mean = _dot(y.astype(BF16), seg) * inv_n
    yc = y - mean
    var = _dot((yc * yc).astype(BF16), seg) * inv_n
    yn = yc * lax.rsqrt(var + GN_EPS) * lng_ref[...] + lnb_ref[...]
    bonus = _dot((r * k2 * rk_ref[...]).astype(BF16), seg) * v
    o_ref[0] = ((yn + bonus) * g).astype(BF16)


def _dot_exact_rhs_lhs(ones_bf16, x):
    hi, mid, lo = _split3(x)
    return _dot(ones_bf16, hi) + _dot(ones_bf16, mid) + _dot(ones_bf16, lo)


def _rwkv_call(zr3, tshift_mu, decay_w0, decay_up, iclr_a0, iclr_up, gate_up, k_k, k_a, r_k, lnx_g, lnx_b):
    B, S, _ = zr3.shape
    tt = min(256, S)
    W = RWKV_W
    row = lambda a: a.reshape(1, -1).astype(F32)
    dup = jnp.concatenate([decay_up, jnp.zeros((ICLR_LORA, W), F32)], axis=0)
    iup = jnp.concatenate([jnp.zeros((DECAY_LORA, W), F32), iclr_up], axis=0)
    idx = jnp.arange(W)
    seg = (idx[:, None] // RWKV_HEAD == idx[None, :] // RWKV_HEAD).astype(BF16)
    t = jnp.arange(tt)
    same_chunk = t[:, None] // CHUNK == t[None, :] // CHUNK
    tri = (same_chunk & (t[None, :] <= t[:, None])).astype(BF16)
    const = lambda shape: pl.BlockSpec(shape, lambda b, j: (0,) * len(shape))
    kern = functools.partial(_rwkv_kernel, tt=tt)
    nu = (tt // CHUNK) * (RWKV_HEADS // 2)
    return pl.pallas_call(
        kern,
        grid=(B, S // tt),
        in_specs=[pl.BlockSpec((1, tt, RWKV_IN), lambda b, j: (b, j, 0)),
                  const((1, RWKV_IN)), const((1, W)), const((2 * DECAY_LORA, W)), const((1, W)),
                  const((2 * ICLR_LORA, W)), const((GATE_LORA, W)), const((1, W)), const((1, W)),
                  const((1, W)), const((1, W)), const((1, W)),
                  const((W, W)), const((tt, tt))],
        out_specs=pl.BlockSpec((1, tt, W), lambda b, j: (b, j, 0)),
        out_shape=jax.ShapeDtypeStruct((B, S, W), BF16),
        scratch_shapes=[pltpu.VMEM((1, RWKV_IN), F32),
                        pltpu.VMEM((RWKV_HEADS // 2, LANES, LANES), F32),
                        pltpu.VMEM((tt, W), F32),
                        pltpu.VMEM((tt, W), F32),
                        pltpu.VMEM((tt, W), BF16),
                        pltpu.VMEM((tt, W), BF16),
                        pltpu.VMEM((tt, W), F32),
                        pltpu.VMEM((tt, W), F32),
                        pltpu.VMEM((tt, W), F32),
                        pltpu.VMEM((tt, W), F32),
                        pltpu.VMEM((tt, W), F32),
                        pltpu.VMEM((nu, CHUNK, LANES), BF16),
                        pltpu.VMEM((nu, CHUNK, LANES), BF16),
                        pltpu.VMEM((nu, CHUNK, LANES), BF16),
                        pltpu.VMEM((nu, CHUNK, LANES), BF16),
                        pltpu.VMEM((nu, CHUNK, LANES), BF16),
                        pltpu.VMEM((nu, CHUNK, 2 * LANES), F32),
                        pltpu.VMEM((nu, CHUNK, 2 * LANES), F32),
                        pltpu.VMEM((nu, CHUNK, LANES), BF16),
                        pltpu.VMEM((nu, CHUNK, LANES), F32),
                        pltpu.VMEM((nu, LANES, LANES), BF16),
                        pltpu.VMEM((nu, LANES, LANES), F32)],
        compiler_params=_cparams(("parallel", "arbitrary")),
        name="rwkv",
    )(zr3, row(tshift_mu), row(decay_w0), dup, row(iclr_a0), iup.astype(BF16), gate_up.astype(BF16), row(k_k),
      row(k_a), row(r_k), row(lnx_g), row(lnx_b), seg, tri)


def _merge_kernel(x_ref, attn_ref, rw_ref, ga_ref, gr_ref, wa_ref, wr_ref, wo_ref, g1_ref,
                  n2_ref, sc_ref, sh_ref, rwt_ref, rb_ref, x1_ref, h2_ref, gt_ref, cnt_ref):
    a = _dot(attn_ref[...], wa_ref[...])
    rr = _dot(rw_ref[...], wr_ref[...])
    mixed = _sigmoid(ga_ref[...].astype(F32)) * a + _sigmoid(gr_ref[...].astype(F32)) * rr
    x1 = x_ref[...] + g1_ref[0] * _dot(mixed.astype(BF16), wo_ref[...])
    x1_ref[...] = x1
    ms = jnp.mean(x1 * x1, axis=-1, keepdims=True)
    h2 = x1 * lax.rsqrt(ms + RMS_EPS) * n2_ref[...]
    h2 = h2 * (1.0 + sc_ref[0]) + sh_ref[0]
    h2_ref[...] = _pack_bf16_pairs(h2)

    tm = x1.shape[0]
    E, G, EG = N_EXPERTS, N_GROUPS, N_EXPERTS // N_GROUPS
    scores = _sigmoid(_dot3(rwt_ref[...], h2, nt=True))
    choice = scores + rb_ref[...]
    c3 = choice.reshape(G, EG, tm)
    e_i = lax.broadcasted_iota(I32, (G, EG, tm), 1)
    m1 = jnp.max(c3, axis=1, keepdims=True)
    first = jnp.min(jnp.where(c3 == m1, e_i, EG), axis=1, keepdims=True)
    m2 = jnp.max(jnp.where(e_i == first, -jnp.inf, c3), axis=1, keepdims=True)
    grp = (m1 + m2).reshape(G, tm)
    g_i = lax.broadcasted_iota(I32, (G, tm), 0)
    rank = jnp.zeros((G, tm), I32)
    for o in range(G):
        other = grp[o:o + 1, :]
        rank = rank + jnp.where((other > grp) | ((other == grp) & (o < g_i)), 1, 0)
    gsel = rank < TOPK_GROUPS
    esel = jnp.broadcast_to(gsel.reshape(G, 1, tm), (G, EG, tm)).reshape(E, tm)
    mc = jnp.where(esel, choice, -jnp.inf)
    x_i = lax.broadcasted_iota(I32, (E, tm), 0)
    erank = jnp.zeros((E, tm), I32)
    for o in range(E):
        other = mc[o:o + 1, :]
        erank = erank + jnp.where((other > mc) | ((other == mc) & (o < x_i)), 1, 0)
    top = erank < MOE_TOPK
    gw = jnp.where(top, scores, 0.0)
    gw = gw / jnp.sum(gw, axis=0, keepdims=True) * ROUTED_SCALE
    gt_ref[...] = gw
    sel = jnp.where(gw > 0.0, 1.0, 0.0)
    cnt_ref[0] = jnp.broadcast_to(jnp.sum(sel, axis=1, keepdims=True), (E, LANES))


def _pack_bf16_pairs(x):
    n = x.shape[1] // 2
    bits = pltpu.bitcast(x.astype(BF16).astype(F32), I32)
    return bits[:, :n] | lax.shift_right_logical(bits[:, n:], 16)


def _unpack_bf16_pairs(p):
    hi = pltpu.bitcast(p & jnp.int32(-65536), F32)
    lo = pltpu.bitcast(lax.shift_left(p, 16), F32)
    return jnp.concatenate([hi, lo], axis=1).astype(BF16)


def _merge_call(x2, attn, rw, ga, gr, wa, wr, wo, g1, norm2_g, sc2, sh2, router_w, router_bias, S):
    T, D = x2.shape
    B = T // S
    tm = min(512, S)
    tpb = S // tm
    nt = T // tm
    E = N_EXPERTS
    row = lambda i: (i, 0)
    per_b = lambda i: (i // tpb, 0, 0)
    const = lambda shape: pl.BlockSpec(shape, lambda i: (0,) * len(shape))
    return pl.pallas_call(
        _merge_kernel,
        grid=(nt,),
        in_specs=[pl.BlockSpec((tm, D), row), pl.BlockSpec((tm, ATTN_W), row), pl.BlockSpec((tm, RWKV_W), row),
                  pl.BlockSpec((tm, D), row), pl.BlockSpec((tm, D), row),
                  const((ATTN_W, D)), const((RWKV_W, D)), const((D, D)),
                  pl.BlockSpec((1, 1, D), per_b), const((1, D)),
                  pl.BlockSpec((1, 1, D), per_b), pl.BlockSpec((1, 1, D), per_b),
                  const((E, D)), const((E, 1))],
        out_specs=[pl.BlockSpec((tm, D), row), pl.BlockSpec((tm, D // 2), row),
                   pl.BlockSpec((E, tm), lambda i: (0, i)), pl.BlockSpec((1, E, LANES), lambda i: (i, 0, 0))],
        out_shape=[jax.ShapeDtypeStruct((T, D), F32), jax.ShapeDtypeStruct((T, D // 2), I32),
                   jax.ShapeDtypeStruct((E, T), F32), jax.ShapeDtypeStruct((nt, E, LANES), F32)],
        compiler_params=_cparams(("parallel",)),
        name="merge",
    )(x2, attn, rw, ga, gr, wa, wr, wo, g1.reshape(B, 1, D), norm2_g.reshape(1, D),
      sc2.reshape(B, 1, D), sh2.reshape(B, 1, D), router_w.T, router_bias.reshape(E, 1))


MOE_BLOCK = 512
SC_CORES, SC_SUBCORES = 2, 16
SC_WORKERS = SC_CORES * SC_SUBCORES
SC_ROWS = 128


def _plan_kernel(gt_ref, cnt_ref, upper_ref, lowe_ref, dest_ref, gw_ref, be_ref, off_ref, *, tm, n_blocks):
    i = pl.program_id(0)
    E = N_EXPERTS
    lowe = lowe_ref[...]

    @pl.when(i == 0)
    def _():
        total = jnp.sum(cnt_ref[...], axis=0)
        nblk = jnp.floor((total + (MOE_BLOCK - 1)) * (1.0 / MOE_BLOCK))
        start_blk = _dot_exact_rhs_lhs(lowe, nblk)
        off_ref[...] = start_blk * MOE_BLOCK
        end_blk = start_blk + nblk
        b_i = lax.broadcasted_iota(I32, (E, n_blocks), 1).astype(F32)
        e_of_b = jnp.sum(jnp.where(end_blk[:, :1] <= b_i, 1.0, 0.0), axis=0, keepdims=True)
        be_ref[...] = jnp.minimum(e_of_b, E - 1.0).astype(I32)

    gt = gt_ref[...]
    sel = gt > 0.0
    selb = jnp.where(sel, 1.0, 0.0).astype(BF16)
    rank = _dot(selb, upper_ref[...])
    dest = off_ref[:, :1] + rank
    off_ref[...] = off_ref[...] + cnt_ref[i]
    kth = _dot(lowe, selb)
    dests, gws = [], []
    for k in range(MOE_TOPK):
        m = sel & (kth == float(k))
        have = jnp.sum(jnp.where(m, 1.0, 0.0), axis=0, keepdims=True)
        d = jnp.sum(jnp.where(m, dest, 0.0), axis=0, keepdims=True)
        dests.append(jnp.where(have > 0.0, d, float(n_blocks * MOE_BLOCK)))
        gws.append(jnp.sum(jnp.where(m, gt, 0.0), axis=0, keepdims=True))
    dest_ref[...] = jnp.concatenate(dests, axis=0).astype(I32)
    gpad = jnp.concatenate(gws + [jnp.zeros((LANES - MOE_TOPK, tm), F32)], axis=0)
    gw_ref[...] = gpad.T


def _plan_call(gate_t, cnt, n_blocks):
    E, T = gate_t.shape
    nt = cnt.shape[0]
    tm = T // nt
    idx = jnp.arange(tm)
    upper = (idx[:, None] < idx[None, :]).astype(BF16)
    ei = jnp.arange(E)
    lowe = (ei[None, :] < ei[:, None]).astype(BF16)
    kern = functools.partial(_plan_kernel, tm=tm, n_blocks=n_blocks)
    const = lambda shape: pl.BlockSpec(shape, lambda i: (0,) * len(shape))
    return pl.pallas_call(
        kern,
        grid=(nt,),
        in_specs=[pl.BlockSpec((E, tm), lambda i: (0, i)), const((nt, E, LANES)), const((tm, tm)), const((E, E))],
        out_specs=[pl.BlockSpec((MOE_TOPK, tm), lambda i: (0, i)), pl.BlockSpec((tm, LANES), lambda i: (i, 0)),
                   const((1, n_blocks))],
        out_shape=[jax.ShapeDtypeStruct((MOE_TOPK, T), I32), jax.ShapeDtypeStruct((T, LANES), F32),
                   jax.ShapeDtypeStruct((1, n_blocks), I32)],
        scratch_shapes=[pltpu.VMEM((E, LANES), F32)],
        compiler_params=_cparams(("arbitrary",)),
        name="plan",
    )(gate_t, cnt, upper, lowe)


def _sc_index_layout(dest_t):
    K, T = dest_t.shape
    n_ch = T // (SC_WORKERS * SC_ROWS)
    return dest_t.reshape(K, SC_WORKERS, n_ch, SC_ROWS).transpose(1, 2, 0, 3).reshape(SC_WORKERS, n_ch * K, SC_ROWS)


def _sc_dispatch(rows, idx, n_slots):
    T, W = rows.shape
    n_ch = T // (SC_WORKERS * SC_ROWS)
    tpw = T // SC_WORKERS
    mesh = plsc.VectorSubcoreMesh(core_axis_name="c", subcore_axis_name="s")

    @functools.partial(
        pl.kernel, mesh=mesh,
        out_type=jax.ShapeDtypeStruct((n_slots, W), I32),
        scratch_types=[pltpu.VMEM((n_ch * MOE_TOPK, SC_ROWS), I32), pltpu.VMEM((SC_ROWS, W), I32),
                       pltpu.SemaphoreType.DMA])
    def kern(x_hbm, idx_hbm, o_hbm, idx_v, rows_v, sem):
        wid = lax.axis_index("s") * SC_CORES + lax.axis_index("c")
        pltpu.sync_copy(idx_hbm.at[wid], idx_v)

        @pl.loop(0, n_ch)
        def _(j):
            pltpu.sync_copy(x_hbm.at[pl.ds(wid * tpw + j * SC_ROWS, SC_ROWS)], rows_v)
            copies = [pltpu.async_copy(rows_v, o_hbm.at[idx_v.at[j * MOE_TOPK + k]], sem)
                      for k in range(MOE_TOPK)]
            for cp in copies:
                cp.wait()

    return kern(rows, idx)


def _sc_combine(slots, idx, T):
    _, W = slots.shape
    n_ch = T // (SC_WORKERS * SC_ROWS)
    tpw = T // SC_WORKERS
    mesh = plsc.VectorSubcoreMesh(core_axis_name="c", subcore_axis_name="s")

    @functools.partial(
        pl.kernel, mesh=mesh,
        out_type=jax.ShapeDtypeStruct((MOE_TOPK, T, W), I32),
        scratch_types=[pltpu.VMEM((n_ch * MOE_TOPK, SC_ROWS), I32), pltpu.VMEM((SC_ROWS, W), I32),
                       pltpu.SemaphoreType.DMA])
    def kern(s_hbm, idx_hbm, o_hbm, idx_v, rows_v, sem):
        wid = lax.axis_index("s") * SC_CORES + lax.axis_index("c")
        pltpu.sync_copy(idx_hbm.at[wid], idx_v)

        @pl.loop(0, n_ch)
        def _(j):
            for k in range(MOE_TOPK):
                pltpu.async_copy(s_hbm.at[idx_v.at[j * MOE_TOPK + k]], rows_v, sem).wait()
                pltpu.sync_copy(rows_v, o_hbm.at[k, pl.ds(wid * tpw + j * SC_ROWS, SC_ROWS)])

    return kern(slots, idx)


def _ffn_kernel(be_ref, x_ref, eg_ref, eu_ref, ed_ref, o_ref):
    x = _unpack_bf16_pairs(x_ref[...])
    a = _dot(x, eg_ref[0])
    u = _dot(x, eu_ref[0])
    o_ref[...] = _pack_bf16_pairs(_dot((a * _sigmoid(a) * u).astype(BF16), ed_ref[0]))


def _ffn_call(xs, block_e, eg, eu, ed, n_blocks):
    P, W = xs.shape
    D, FF = 2 * W, EXPERT_FF
    grid_spec = pltpu.PrefetchScalarGridSpec(
        num_scalar_prefetch=1,
        grid=(n_blocks,),
        in_specs=[pl.BlockSpec((MOE_BLOCK, W), lambda b, be: (b, 0)),
                  pl.BlockSpec((1, D, FF), lambda b, be: (be[b], 0, 0)),
                  pl.BlockSpec((1, D, FF), lambda b, be: (be[b], 0, 0)),
                  pl.BlockSpec((1, FF, D), lambda b, be: (be[b], 0, 0))],
        out_specs=pl.BlockSpec((MOE_BLOCK, W), lambda b, be: (b, 0)))
    return pl.pallas_call(
        _ffn_kernel,
        grid_spec=grid_spec,
        out_shape=jax.ShapeDtypeStruct((P, W), I32),
        compiler_params=_cparams(("parallel",)),
        name="ffn",
    )(block_e, xs, eg, eu, ed)


def _final_kernel(h_ref, c_ref, gw_ref, x1_ref, g2_ref, fg_ref, sg_ref, su_ref, sd_ref, o_ref):
    h = _unpack_bf16_pairs(h_ref[...])
    a = _dot(h, sg_ref[...])
    u = _dot(h, su_ref[...])
    moe = _dot((a * _sigmoid(a) * u).astype(BF16), sd_ref[...])
    gw = gw_ref[...]
    for k in range(MOE_TOPK):
        w = gw[:, k:k + 1]
        y = _unpack_bf16_pairs(c_ref[k]).astype(F32)
        moe = moe + jnp.where(w > 0.0, w * y, 0.0)
    x2 = x1_ref[...] + g2_ref[0] * moe
    ms = jnp.mean(x2 * x2, axis=-1, keepdims=True)
    o_ref[...] = x2 * lax.rsqrt(ms + RMS_EPS) * fg_ref[...]


def _final_call(h2p, comb, gw, x1, g2, final_g, sg, su, sd, S):
    T, W = h2p.shape
    D, FF = 2 * W, EXPERT_FF
    B = T // S
    tm = min(512, S)
    tpb = S // tm
    row = lambda i: (i, 0)
    const = lambda shape: pl.BlockSpec(shape, lambda i: (0,) * len(shape))
    return pl.pallas_call(
        _final_kernel,
        grid=(T // tm,),
        in_specs=[pl.BlockSpec((tm, W), row), pl.BlockSpec((MOE_TOPK, tm, W), lambda i: (0, i, 0)),
                  pl.BlockSpec((tm, LANES), row), pl.BlockSpec((tm, D), row),
                  pl.BlockSpec((1, 1, D), lambda i: (i // tpb, 0, 0)), const((1, D)),
                  const((D, FF)), const((D, FF)), const((FF, D))],
        out_specs=pl.BlockSpec((tm, D), row),
        out_shape=jax.ShapeDtypeStruct((T, D), F32),
        compiler_params=_cparams(("parallel",)),
        name="final",
    )(h2p, comb, gw, x1, g2.reshape(B, 1, D), final_g.reshape(1, D), sg, su, sd)


def _moe_call(h2p, gate_t, cnt, x1, g2, final_g, eg, eu, ed, sg, su, sd, S):
    T = h2p.shape[0]
    n_blocks = (T * MOE_TOPK) // MOE_BLOCK + N_EXPERTS
    n_slots = (n_blocks + 1) * MOE_BLOCK
    dest_t, gw, block_e = _plan_call(gate_t, cnt, n_blocks)
    idx = _sc_index_layout(dest_t)
    xs = _sc_dispatch(h2p, idx, n_slots)
    ys = _ffn_call(xs, block_e.reshape(n_blocks), eg, eu, ed, n_blocks)
    comb = _sc_combine(ys, idx, T)
    return _final_call(h2p, comb, gw, x1, g2, final_g, sg, su, sd, S)


def _layer(x2, c, S, ada_w, ada_b, norm1_g, w_in, rel_bias, tshift_mu, decay_w0, decay_up, iclr_a0, iclr_up,
           gate_up, k_k, k_a, r_k, lnx_g, lnx_b, w_attn_br, w_rwkv_br, w_out, norm2_g, router_w, router_bias,
           exp_gate, exp_up, exp_down, sh_gate, sh_up, sh_down, final_g):
    T, D = x2.shape
    B = T // S
    mod = _mod_call(c, ada_w, ada_b)
    sh1, sc1, g1, sh2, sc2, g2 = jnp.split(mod, 6, axis=-1)

    q, k, vt, iq, ik4, iwt, zr, ga, gr = _inproj_call(x2, norm1_g, sc1, sh1, w_in, S)

    ta = min(256, S)
    assert ta >= LANES and S % ta == 0
    top_k = min(TOPK_MAX, S // 4)
    seq = lambda a: a.reshape(B, S, a.shape[-1])
    mask = _index_call(seq(iq), iwt, seq(ik4), ta, top_k)
    bias_tiles = _bias_call(rel_bias, ta)
    attn = _attn_call(seq(q), seq(k), vt, mask, bias_tiles, rel_bias, ta).reshape(T, ATTN_W)

    rw = _rwkv_call(zr.reshape(B, S, RWKV_IN), tshift_mu, decay_w0, decay_up, iclr_a0, iclr_up, gate_up,
                    k_k, k_a, r_k, lnx_g, lnx_b).reshape(T, RWKV_W)

    x1, h2p, gate_t, cnt = _merge_call(x2, attn, rw, ga, gr, w_attn_br.astype(BF16), w_rwkv_br.astype(BF16),
                                       w_out.astype(BF16), g1, norm2_g, sc2, sh2, router_w, router_bias, S)
    return _moe_call(h2p, gate_t, cnt, x1, g2, final_g, exp_gate.astype(BF16), exp_up.astype(BF16),
                     exp_down.astype(BF16), sh_gate.astype(BF16), sh_up.astype(BF16), sh_down.astype(BF16), S)


def kernel(x, c, ada_w, ada_b, norm1_g, w_in, rel_bias, tshift_mu, decay_w0, decay_up, iclr_a0, iclr_up, gate_up, k_k, k_a, r_k, lnx_g, lnx_b, w_attn_br, w_rwkv_br, w_out, norm2_g, router_w, router_bias, exp_gate, exp_up, exp_down, sh_gate, sh_up, sh_down, final_g):
    B, S, D = x.shape
    depth = ada_w.shape[0]
    assert depth == 1, "the final RMSNorm is fused into the (single) layer's MoE kernel"
    out = _layer(x.reshape(B * S, D), c, S, ada_w[0], ada_b[0], norm1_g[0], w_in[0], rel_bias, tshift_mu[0],
                 decay_w0[0], decay_up[0], iclr_a0[0], iclr_up[0], gate_up[0], k_k[0], k_a[0], r_k[0],
                 lnx_g[0], lnx_b[0], w_attn_br[0], w_rwkv_br[0], w_out[0], norm2_g[0], router_w[0],
                 router_bias[0], exp_gate[0], exp_up[0], exp_down[0], sh_gate[0], sh_up[0], sh_down[0], final_g)
    return out.reshape(B, S, D)
```

```python
import functools
import math

import jax
import jax.numpy as jnp
from jax import lax
from jax.experimental import pallas as pl
from jax.experimental.pallas import tpu as pltpu
from jax.experimental.pallas import tpu_sc as plsc

F32 = jnp.float32
BF16 = jnp.bfloat16
I32 = jnp.int32
I16 = jnp.int16

RMS_EPS = 1e-6
D_MODEL = 1024
N_ATTN_HEADS = 8
ATTN_HEAD_DIM = 64
ATTN_W = 512
IDX_HEADS = 16
IDX_DIM = 32
IDX_Q = 512
TOPK_MAX = 256
N_BUCKETS = 32
MAX_DISTANCE = 128
RWKV_HEADS = 8
RWKV_HEAD = 64
RWKV_W = 512
DECAY_LORA = 64
ICLR_LORA = 64
GATE_LORA = 128
RWKV_IN = 1792
GN_EPS = 64e-5
N_EXPERTS = 64
N_GROUPS = 8
TOPK_GROUPS = 4
MOE_TOPK = 8
EXPERT_FF = 256
ROUTED_SCALE = 2.5

LANES = 128
VMEM_LIMIT = 56 * 1024 * 1024
CHUNK = 64
INT_MIN = -2147483648
KEY_NEG_INF = -2139095041

NT_DIMS = (((1,), (1,)), ((), ()))


def _cparams(sem):
    return pltpu.CompilerParams(dimension_semantics=sem, vmem_limit_bytes=VMEM_LIMIT)


def _dot(a, b):
    return jnp.dot(a, b, preferred_element_type=F32)


def _dot_nt(a, b):
    return lax.dot_general(a, b, NT_DIMS, preferred_element_type=F32)


def _split2(x):
    hi = x.astype(BF16)
    lo = (x - hi.astype(F32)).astype(BF16)
    return hi, lo


def _split3(x):
    hi = x.astype(BF16)
    r1 = x - hi.astype(F32)
    mid = r1.astype(BF16)
    lo = (r1 - mid.astype(F32)).astype(BF16)
    return hi, mid, lo


def _dot_exact_rhs(x, ones_bf16, terms=2):
    parts = _split3(x) if terms == 3 else _split2(x)
    out = _dot(parts[0], ones_bf16)
    for p in parts[1:]:
        out = out + _dot(p, ones_bf16)
    return out


def _dot3(a, b, nt=False):
    ah, al = _split2(a)
    bh, bl = _split2(b)
    f = _dot_nt if nt else _dot
    return f(ah, bh) + f(ah, bl) + f(al, bh)


def _sigmoid(x):
    return 1.0 / (1.0 + jnp.exp(-x))


def _mod_kernel(c_ref, w_ref, b_ref, o_ref):
    c = c_ref[...]
    s = c * _sigmoid(c)
    o_ref[...] = _dot3(s, w_ref[...]) + b_ref[...]


def _mod_call(c, ada_w, ada_b):
    B, D = c.shape
    N = ada_w.shape[1]
    tn = 1024
    return pl.pallas_call(
        _mod_kernel,
        grid=(N // tn,),
        in_specs=[pl.BlockSpec((B, D), lambda j: (0, 0)),
                  pl.BlockSpec((D, tn), lambda j: (0, j)),
                  pl.BlockSpec((1, tn), lambda j: (0, j))],
        out_specs=pl.BlockSpec((B, tn), lambda j: (0, j)),
        out_shape=jax.ShapeDtypeStruct((B, N), F32),
        compiler_params=_cparams(("arbitrary",)),
        name="mod",
    )(c, ada_w, ada_b.reshape(1, N))


_OFF_Q, _OFF_K, _OFF_IQ, _OFF_IK4, _OFF_ZR, _OFF_GA, _OFF_GR, _N_PACK = (
    0, 512, 1024, 1536, 2048, 3840, 4864, 5888)
IDX_PER_BLOCK = LANES // IDX_DIM


def _pack_w_in(w_in):
    D = w_in.shape[0]
    w_ik = w_in[:, 2048:2080]
    ik4 = jnp.zeros((D, IDX_PER_BLOCK * LANES), w_in.dtype)
    for j in range(IDX_PER_BLOCK):
        ik4 = lax.dynamic_update_slice(ik4, w_ik, (0, j * LANES + j * IDX_DIM))
    w_pack = jnp.concatenate([w_in[:, 0:1024], w_in[:, 1536:2048], ik4, w_in[:, 2096:]], axis=1).astype(BF16)
    return w_pack, w_in[:, 1024:1536].T.astype(BF16), w_in[:, 2080:2096].T.astype(BF16)


def _inproj_kernel(x_ref, g_ref, sc_ref, sh_ref, w_ref, wvt_ref, wiwt_ref,
                   q_ref, k_ref, vt_ref, iq_ref, ik4_ref, iwt_ref, zr_ref, ga_ref, gr_ref):
    x = x_ref[...]
    ms = jnp.mean(x * x, axis=-1, keepdims=True)
    h = x * lax.rsqrt(ms + RMS_EPS) * g_ref[...]
    h = h * (1.0 + sc_ref[0]) + sh_ref[0]
    hb = h.astype(BF16)

    def proj(lo, hi):
        return _dot(hb, w_ref[:, lo:hi])

    q_ref[...] = (proj(_OFF_Q, _OFF_K) * (ATTN_HEAD_DIM ** -0.5)).astype(BF16)
    k_ref[...] = proj(_OFF_K, _OFF_IQ).astype(BF16)
    iq_ref[...] = proj(_OFF_IQ, _OFF_IK4).astype(BF16)
    ik4_ref[...] = proj(_OFF_IK4, _OFF_ZR).astype(BF16)
    zr_ref[...] = proj(_OFF_ZR, _OFF_GA)
    ga_ref[...] = proj(_OFF_GA, _OFF_GR).astype(BF16)
    gr_ref[...] = proj(_OFF_GR, _N_PACK).astype(BF16)
    vt_ref[0] = _dot_nt(wvt_ref[...], hb).astype(BF16)
    iwt_ref[0] = _dot_nt(wiwt_ref[...], hb)


def _inproj_call(x2, norm_g, sc, sh, w_in, S):
    T, D = x2.shape
    B = T // S
    tm = min(512, S)
    tpb = S // tm
    w_pack, wvt, wiwt = _pack_w_in(w_in)
    row = lambda i: (i, 0)
    per_b = lambda i: (i // tpb, 0, 0)
    colblk = lambda i: (i // tpb, 0, i % tpb)
    const = lambda shape: pl.BlockSpec(shape, lambda i: (0,) * len(shape), pipeline_mode=pl.Buffered(1))
    rows_out = ((512, BF16), (512, BF16), (512, BF16), (512, BF16), (RWKV_IN, F32), (D, BF16), (D, BF16))
    out_specs = [pl.BlockSpec((tm, w), row) for w, _ in rows_out]
    out_shape = [jax.ShapeDtypeStruct((T, w), dt) for w, dt in rows_out]
    out_specs[2:2] = [pl.BlockSpec((1, ATTN_W, tm), colblk)]
    out_shape[2:2] = [jax.ShapeDtypeStruct((B, ATTN_W, S), BF16)]
    out_specs[5:5] = [pl.BlockSpec((1, IDX_HEADS, tm), colblk)]
    out_shape[5:5] = [jax.ShapeDtypeStruct((B, IDX_HEADS, S), F32)]
    return pl.pallas_call(
        _inproj_kernel,
        grid=(T // tm,),
        in_specs=[pl.BlockSpec((tm, D), row),
                  pl.BlockSpec((1, D), lambda i: (0, 0)),
                  pl.BlockSpec((1, 1, D), per_b),
                  pl.BlockSpec((1, 1, D), per_b),
                  const((D, _N_PACK)), const((ATTN_W, D)), const((IDX_HEADS, D))],
        out_specs=out_specs,
        out_shape=out_shape,
        compiler_params=_cparams(("parallel",)),
        name="inproj",
    )(x2, norm_g.reshape(1, D), sc.reshape(B, 1, D), sh.reshape(B, 1, D), w_pack, wvt, wiwt)


def _t5_bucket(rel):
    n = jnp.maximum(rel, 0)
    max_exact = N_BUCKETS // 2
    nf = jnp.maximum(n, 1).astype(F32)
    large = max_exact + (jnp.log(nf / max_exact) / math.log(MAX_DISTANCE / max_exact)
                         * (N_BUCKETS - max_exact)).astype(I32)
    large = jnp.minimum(large, N_BUCKETS - 1)
    return jnp.where(n < max_exact, n, large)


def _bias_kernel(bucket_ref, rb_ref, o_ref):
    h = pl.program_id(0)
    bk = bucket_ref[...]
    out = jnp.zeros(bk.shape, F32)
    for b in range(N_BUCKETS):
        out = jnp.where(bk == b, rb_ref[b, h], out)
    o_ref[0] = out


def _bias_call(rel_bias, tq):
    r = jnp.arange(tq, dtype=I32)[None, :]
    c = jnp.arange(tq, dtype=I32)[:, None]
    buckets = jnp.stack([_t5_bucket(r - c), _t5_bucket(tq + r - c)])
    return pl.pallas_call(
        _bias_kernel,
        grid=(N_ATTN_HEADS,),
        in_specs=[pl.BlockSpec((2, tq, tq), lambda h: (0, 0, 0)),
                  pl.BlockSpec(memory_space=pltpu.SMEM)],
        out_specs=pl.BlockSpec((1, 2, tq, tq), lambda h: (h, 0, 0, 0)),
        out_shape=jax.ShapeDtypeStruct((N_ATTN_HEADS, 2, tq, tq), F32),
        compiler_params=_cparams(("arbitrary",)),
        name="bias",
    )(buckets, rel_bias)


def _index_kernel(iq_ref, iwt_ref, ik4_ref, mask_ref, key_ref, k16_ref, *, t, nk, top_k, scale):
    qi = pl.program_id(1)
    nkt = qi + 1
    ksub = LANES
    qpos = qi * t + lax.broadcasted_iota(I32, (ksub, t), 1)

    def score_tile(kt, carry):
        kbase = pl.multiple_of(kt * t, t)
        for ks in range(t // ksub):
            acc = jnp.zeros((ksub, t), F32)
            ik_rows = ik4_ref[0, pl.ds(kbase + ks * ksub, ksub), :]
            ik_stack = jnp.concatenate([ik_rows[:, j * LANES:(j + 1) * LANES] for j in range(IDX_PER_BLOCK)],
                                       axis=0)
            for g in range(IDX_HEADS // IDX_PER_BLOCK):
                d4 = _dot_nt(ik_stack, iq_ref[0, :, g * LANES:(g + 1) * LANES])
                for j in range(IDX_PER_BLOCK):
                    h = g * IDX_PER_BLOCK + j
                    acc = acc + jnp.maximum(d4[j * ksub:(j + 1) * ksub], 0.0) * iwt_ref[0, h:h + 1, :]
            s = acc * scale
            kpos = kt * t + ks * ksub + lax.broadcasted_iota(I32, (ksub, t), 0)
            s = jnp.where(kpos <= qpos, s, -jnp.inf)
            bits = pltpu.bitcast(s, I32)
            key = bits ^ ((bits >> 31) & 0x7FFFFFFF)
            key_ref[kt, ks * ksub:(ks + 1) * ksub, :] = key
            k16_ref[kt, ks * ksub:(ks + 1) * ksub, :] = (key >> 16).astype(I16)
        return carry

    lax.fori_loop(0, nkt, score_tile, 0)

    pack = 16

    def search16():
        def bit_body(i, ans):
            cand = ans | lax.shift_left(jnp.int32(1), 15 - i)
            cand16 = (cand - 32768).astype(I16)

            def cnt_body(kt, acc):
                one = jnp.where(k16_ref[kt] >= cand16, jnp.int16(1), jnp.int16(0))
                for r in range(t // pack):
                    acc = acc + one[r * pack:(r + 1) * pack, :]
                return acc

            acc = lax.fori_loop(0, nkt, cnt_body, jnp.zeros((pack, t), I16))
            cnt = jnp.sum(acc.astype(I32), axis=0, keepdims=True)
            return jnp.where(cnt >= top_k, cand, ans)

        return lax.fori_loop(0, 16, bit_body, jnp.zeros((1, t), I32))

    hi = search16() - 32768

    def remap_body(kt, carry):
        key = key_ref[kt]
        khi = key >> 16
        lo = (key & 0xFFFF) - 32768
        k16_ref[kt] = jnp.where(khi > hi, 32767, jnp.where(khi == hi, lo, -32768)).astype(I16)
        return carry

    lax.fori_loop(0, nkt, remap_body, 0)
    thr = hi * 65536 + search16()

    def mask_body(kt, carry):
        keys = key_ref[kt]
        sel = (keys >= thr) & (keys > KEY_NEG_INF)
        mask_ref[0, 0, kt] = jnp.where(sel, 0.0, -jnp.inf).astype(BF16)
        return carry

    lax.fori_loop(0, nkt, mask_body, 0)

    def fill_body(kt, carry):
        mask_ref[0, 0, kt] = jnp.full((t, t), -jnp.inf, BF16)
        return carry

    lax.fori_loop(nkt, nk, fill_body, 0)


def _index_call(iq, iwt, ik4, t, top_k):
    B, S, _ = iq.shape
    n = S // t
    scale = (IDX_HEADS ** -0.5) * (IDX_DIM ** -0.5)
    kern = functools.partial(_index_kernel, t=t, nk=n, top_k=top_k, scale=scale)
    return pl.pallas_call(
        kern,
        grid=(B, n),
        in_specs=[pl.BlockSpec((1, t, IDX_Q), lambda b, i: (b, i, 0)),
                  pl.BlockSpec((1, IDX_HEADS, t), lambda b, i: (b, 0, i)),
                  pl.BlockSpec((1, S, IDX_PER_BLOCK * LANES), lambda b, i: (b, 0, 0))],
        out_specs=pl.BlockSpec((1, 1, n, t, t), lambda b, i: (b, i, 0, 0, 0)),
        out_shape=jax.ShapeDtypeStruct((B, n, n, t, t), BF16),
        scratch_shapes=[pltpu.VMEM((n, t, t), I32), pltpu.VMEM((n, t, t), I16)],
        compiler_params=_cparams(("parallel", "arbitrary")),
        name="index",
    )(iq, iwt, ik4)


ONES_ROWS = 16


def _attn_kernel(qi_tab, kt_tab, q_ref, k_ref, vt_ref, mask_ref, bias_ref, rb_ref, o_ref,
                 qz_ref, m_ref, acc_ref, s_ref, *, t):
    s_id = pl.program_id(1)
    qi = qi_tab[s_id]
    kt = kt_tab[s_id]
    dh = ATTN_HEAD_DIM

    @pl.when(kt == 0)
    def _():
        m_ref[...] = jnp.full(m_ref.shape, -jnp.inf, F32)
        acc_ref[...] = jnp.zeros(acc_ref.shape, F32)
        lane = lax.broadcasted_iota(I32, (t, LANES), 1)
        for h in range(N_ATTN_HEADS):
            blk = q_ref[0, :, (h // 2) * LANES:(h // 2 + 1) * LANES]
            keep = (lane < dh) if h % 2 == 0 else (lane >= dh)
            qz_ref[h] = jnp.where(keep, blk, jnp.zeros_like(blk))

    def step(bias_of_head):
        maskf = mask_ref[0, 0, 0].astype(F32)
        ones = jnp.ones((ONES_ROWS, t), BF16)

        for h in range(N_ATTN_HEADS):
            k_blk = k_ref[0, :, (h // 2) * LANES:(h // 2 + 1) * LANES]
            s_ref[h] = _dot_nt(k_blk, qz_ref[h]) + bias_of_head(h) + maskf
        for h in range(N_ATTN_HEADS):
            s = s_ref[h]
            m_old = m_ref[h:h + 1, :]
            m_cur = jnp.max(jnp.max(s.reshape(t // 8, 8, t), axis=0), axis=0, keepdims=True)
            m_new = jnp.maximum(m_old, m_cur)
            m_safe = jnp.where(m_new == -jnp.inf, 0.0, m_new)
            alpha = jnp.exp(m_old - m_safe)
            p = jnp.exp(s - m_safe).astype(BF16)
            v_aug = jnp.concatenate([vt_ref[0, h * dh:(h + 1) * dh, :], ones], axis=0)
            acc_ref[h] = alpha * acc_ref[h] + _dot(v_aug, p)
            m_ref[h:h + 1, :] = m_new

    @pl.when(kt == qi)
    def _():
        step(lambda h: bias_ref[h, 0])

    @pl.when(kt == qi - 1)
    def _():
        step(lambda h: bias_ref[h, 1])

    @pl.when(kt < qi - 1)
    def _():
        step(lambda h: rb_ref[N_BUCKETS - 1, h])

    @pl.when(kt == qi)
    def _():
        outs = []
        for h in range(N_ATTN_HEADS):
            a = acc_ref[h]
            outs.append(a[:dh, :] / a[dh:dh + 1, :])
        o_ref[0] = jnp.concatenate(outs, axis=0).T.astype(BF16)


def _attn_call(q, k, vt, mask, bias_tiles, rel_bias, t):
    B, S, W = q.shape
    n = S // t
    H = N_ATTN_HEADS
    qi_tab = jnp.asarray([i for i in range(n) for _ in range(i + 1)], I32)
    kt_tab = jnp.asarray([j for i in range(n) for j in range(i + 1)], I32)
    kern = functools.partial(_attn_kernel, t=t)
    grid_spec = pltpu.PrefetchScalarGridSpec(
        num_scalar_prefetch=2,
        grid=(B, int(qi_tab.shape[0])),
        in_specs=[pl.BlockSpec((1, t, W), lambda b, s, qt, kt: (b, qt[s], 0)),
                  pl.BlockSpec((1, t, W), lambda b, s, qt, kt: (b, kt[s], 0)),
                  pl.BlockSpec((1, W, t), lambda b, s, qt, kt: (b, 0, kt[s])),
                  pl.BlockSpec((1, 1, 1, t, t), lambda b, s, qt, kt: (b, qt[s], kt[s], 0, 0)),
                  pl.BlockSpec((H, 2, t, t), lambda b, s, qt, kt: (0, 0, 0, 0), pipeline_mode=pl.Buffered(1)),
                  pl.BlockSpec(memory_space=pltpu.SMEM)],
        out_specs=pl.BlockSpec((1, t, W), lambda b, s, qt, kt: (b, qt[s], 0)),
        scratch_shapes=[pltpu.VMEM((H, t, LANES), BF16),
                        pltpu.VMEM((H, t), F32),
                        pltpu.VMEM((H, ATTN_HEAD_DIM + ONES_ROWS, t), F32),
                        pltpu.VMEM((H, t, t), F32)])
    return pl.pallas_call(
        kern,
        grid_spec=grid_spec,
        out_shape=jax.ShapeDtypeStruct((B, S, W), BF16),
        compiler_params=_cparams(("parallel", "arbitrary")),
        name="attn",
    )(qi_tab, kt_tab, q, k, vt, mask, bias_tiles, rel_bias)


def _blockdiag_rows(x):
    lane = lax.broadcasted_iota(I32, x.shape, 1)
    zero = jnp.zeros_like(x)
    return jnp.concatenate([jnp.where(lane < RWKV_HEAD, x, zero),
                            jnp.where(lane >= RWKV_HEAD, x, zero)], axis=0)


def _rwkv_kernel(z_ref, mu_ref, w0_ref, dup_ref, a0_ref, iup_ref, gup_ref, kk_ref, ka_ref, rk_ref,
                 lng_ref, lnb_ref, seg_ref, tri_ref, o_ref,
                 prev_ref, st_ref, at_ref, rt_ref, bt_ref, kt_ref, bh_ref, kh_ref, v_ref, pc_ref, y_ref,
                 la_ref, lb_ref, mak_ref, arb_ref, ark_ref, wa_ref, wb_ref, g1_ref, g2_ref, h1_ref, h2_ref,
                 *, tt):
    j = pl.program_id(1)
    W = RWKV_W
    C = CHUNK
    nchunk = tt // C
    npair = RWKV_HEADS // 2

    @pl.when(j == 0)
    def _():
        prev_ref[...] = jnp.zeros(prev_ref.shape, F32)
        st_ref[...] = jnp.zeros(st_ref.shape, F32)

    z = z_ref[0]
    row = lax.broadcasted_iota(I32, z.shape, 0)
    z_prev = jnp.where(row == 0, prev_ref[...], pltpu.roll(z, 1, axis=0))
    prev_ref[...] = z[tt - 1:tt, :]
    z = z + mu_ref[...] * (z_prev - z)

    r = z[:, 0:W]
    k = z[:, W:2 * W]
    v = z[:, 2 * W:3 * W]
    wdad = z[:, 3 * W:3 * W + 2 * DECAY_LORA]
    gd = z[:, 3 * W + 2 * DECAY_LORA:]

    w_pre = w0_ref[...] + _dot3(jnp.tanh(wdad), dup_ref[...])
    neg = -w_pre
    softplus = jnp.maximum(neg, 0.0) + jnp.log(1.0 + jnp.exp(-jnp.abs(neg)))
    lw = -jnp.exp(-softplus - 0.5)
    a_lr = _sigmoid(a0_ref[...] + _dot(wdad.astype(BF16), iup_ref[...]))
    g = _dot(_sigmoid(gd).astype(BF16), gup_ref[...])

    seg = seg_ref[...]
    kk = k * kk_ref[...]
    kk = kk / jnp.maximum(jnp.sqrt(_dot_exact_rhs(kk * kk, seg)), 1e-12)
    k2 = k * (1.0 + (a_lr - 1.0) * ka_ref[...])
    a_vec = -kk
    b_vec = kk * a_lr

    cum = _dot_exact_rhs_lhs(tri_ref[...], lw)
    tot = jnp.concatenate([jnp.broadcast_to(cum[(c + 1) * C - 1:(c + 1) * C, :], (C, W)) for c in range(nchunk)],
                          axis=0)
    p_inv = jnp.exp(-cum)
    p_out = jnp.exp(tot - cum)
    at_ref[...] = a_vec * jnp.exp(cum - lw)
    rt_ref[...] = r * jnp.exp(cum)
    bt_ref[...] = (b_vec * p_inv).astype(BF16)
    kt_ref[...] = (k2 * p_inv).astype(BF16)
    bh_ref[...] = b_vec * p_out
    kh_ref[...] = k2 * p_out
    v_ref[...] = v
    pc_ref[...] = jnp.exp(tot)

    t_i = lax.broadcasted_iota(I32, (C, LANES), 0)
    s_i = lax.broadcasted_iota(I32, (C, LANES), 1) % C
    strict = s_i < t_i
    incl = s_i <= t_i
    r_i = lax.broadcasted_iota(I32, (LANES, LANES), 0)
    c_i = lax.broadcasted_iota(I32, (LANES, LANES), 1)
    same_head = (r_i < RWKV_HEAD) == (c_i < RWKV_HEAD)
    diag = r_i == c_i
    nstage = int(math.log2(C))
    zero = jnp.zeros((C, LANES), F32)
    zsq = jnp.zeros((LANES, LANES), F32)
    units = [(c, p) for c in range(nchunk) for p in range(npair)]

    def sl(c, p):
        return slice(c * C, (c + 1) * C), slice(p * LANES, (p + 1) * LANES)

    def bd2(w):
        wb = w.astype(BF16)
        return jnp.concatenate([_blockdiag_rows(wb[:, :LANES]), _blockdiag_rows(wb[:, LANES:])], axis=1)

    for i, (c, p) in enumerate(units):
        rows, cols = sl(c, p)
        lhs = jnp.concatenate([at_ref[rows, cols], rt_ref[rows, cols]], axis=0).astype(BF16)
        rhs = jnp.concatenate([_blockdiag_rows(bt_ref[rows, cols]),
                               _blockdiag_rows(kt_ref[rows, cols])], axis=0)
        prod = _dot_nt(lhs, rhs)
        la_ref[i] = jnp.where(strict, prod[:C, :LANES], zero).astype(BF16)
        mak_ref[i] = jnp.where(strict, prod[:C, LANES:], zero).astype(BF16)
        arb_ref[i] = jnp.where(incl, prod[C:, :LANES], zero).astype(BF16)
        ark_ref[i] = jnp.where(incl, prod[C:, LANES:], zero).astype(BF16)
    for i, (c, p) in enumerate(units):
        rows, cols = sl(c, p)
        w2 = _dot(mak_ref[i], _blockdiag_rows(v_ref[rows, cols].astype(BF16)))
        wa_ref[i] = jnp.concatenate([at_ref[rows, cols], w2], axis=1)
    l_bufs, w_bufs = (la_ref, lb_ref), (wa_ref, wb_ref)
    for s in range(nstage):
        l_in, l_out = l_bufs[s % 2], l_bufs[(s + 1) % 2]
        w_in, w_out = w_bufs[s % 2], w_bufs[(s + 1) % 2]
        for i in range(len(units)):
            lmat = l_in[i]
            w = w_in[i]
            w_out[i] = w + _dot(lmat, bd2(w))
            if s < nstage - 1:
                l_out[i] = _dot(lmat, _blockdiag_rows(lmat)).astype(BF16)
    w_fin = w_bufs[nstage % 2]
    for i, (c, p) in enumerate(units):
        rows, cols = sl(c, p)
        w = w_fin[i]
        wb = w.astype(BF16)
        vb = v_ref[rows, cols].astype(BF16)
        gg = _dot(arb_ref[i], bd2(w))
        g1_ref[i] = (rt_ref[rows, cols] + gg[:, :LANES]).astype(BF16)
        g2_ref[i] = gg[:, LANES:] + _dot(ark_ref[i], _blockdiag_rows(vb))
        bk_t = jnp.concatenate([bh_ref[rows, cols], kh_ref[rows, cols]], axis=0).T
        hrhs = jnp.concatenate([wb, jnp.concatenate([jnp.zeros((C, LANES), BF16), vb], axis=1)], axis=0)
        hh = _dot(bk_t.astype(BF16), hrhs)
        pc = pc_ref[c * C:c * C + 1, cols]
        h1 = jnp.where(same_head, hh[:, :LANES], zsq) + jnp.where(diag, jnp.broadcast_to(pc, (LANES, LANES)), zsq)
        h1_ref[i] = h1.astype(BF16)
        h2_ref[i] = jnp.where(same_head, hh[:, LANES:], zsq)
    for c in range(nchunk):
        sts = [st_ref[p].astype(BF16) for p in range(npair)]
        for p in range(npair):
            i = c * npair + p
            rows, cols = sl(c, p)
            y_ref[rows, cols] = _dot(g1_ref[i], sts[p]) + g2_ref[i]
            st_ref[p] = _dot(h1_ref[i], sts[p]) + h2_ref[i]

    y = y_ref[...]
    inv_n = 1.0 / RWKV_HEAD
    mean = _dot(y.astype(BF16), seg) * inv_n
    yc = y - mean
    var = _dot((yc * yc).astype(BF16), seg) * inv_n
    yn = yc * lax.rsqrt(var + GN_EPS) * lng_ref[...] + lnb_ref[...]
    bonus = _dot((r * k2 * rk_ref[...]).astype(BF16), seg) * v
    o_ref[0] = ((yn + bonus) * g).astype(BF16)


def _dot_exact_rhs_lhs(ones_bf16, x):
    hi, mid, lo = _split3(x)
    return _dot(ones_bf16, hi) + _dot(ones_bf16, mid) + _dot(ones_bf16, lo)


def _rwkv_call(zr3, tshift_mu, decay_w0, decay_up, iclr_a0, iclr_up, gate_up, k_k, k_a, r_k, lnx_g, lnx_b):
    B, S, _ = zr3.shape
    tt = min(256, S)
    W = RWKV_W
    row = lambda a: a.reshape(1, -1).astype(F32)
    dup = jnp.concatenate([decay_up, jnp.zeros((ICLR_LORA, W), F32)], axis=0)
    iup = jnp.concatenate([jnp.zeros((DECAY_LORA, W), F32), iclr_up], axis=0)
    idx = jnp.arange(W)
    seg = (idx[:, None] // RWKV_HEAD == idx[None, :] // RWKV_HEAD).astype(BF16)
    t = jnp.arange(tt)
    same_chunk = t[:, None] // CHUNK == t[None, :] // CHUNK
    tri = (same_chunk & (t[None, :] <= t[:, None])).astype(BF16)
    const = lambda shape: pl.BlockSpec(shape, lambda b, j: (0,) * len(shape))
    kern = functools.partial(_rwkv_kernel, tt=tt)
    nu = (tt // CHUNK) * (RWKV_HEADS // 2)
    return pl.pallas_call(
        kern,
        grid=(B, S // tt),
        in_specs=[pl.BlockSpec((1, tt, RWKV_IN), lambda b, j: (b, j, 0)),
                  const((1, RWKV_IN)), const((1, W)), const((2 * DECAY_LORA, W)), const((1, W)),
                  const((2 * ICLR_LORA, W)), const((GATE_LORA, W)), const((1, W)), const((1, W)),
                  const((1, W)), const((1, W)), const((1, W)),
                  const((W, W)), const((tt, tt))],
        out_specs=pl.BlockSpec((1, tt, W), lambda b, j: (b, j, 0)),
        out_shape=jax.ShapeDtypeStruct((B, S, W), BF16),
        scratch_shapes=[pltpu.VMEM((1, RWKV_IN), F32),
                        pltpu.VMEM((RWKV_HEADS // 2, LANES, LANES), F32),
                        pltpu.VMEM((tt, W), F32),
                        pltpu.VMEM((tt, W), F32),
                        pltpu.VMEM((tt, W), BF16),
                        pltpu.VMEM((tt, W), BF16),
                        pltpu.VMEM((tt, W), F32),
                        pltpu.VMEM((tt, W), F32),
                        pltpu.VMEM((tt, W), F32),
                        pltpu.VMEM((tt, W), F32),
                        pltpu.VMEM((tt, W), F32),
                        pltpu.VMEM((nu, CHUNK, LANES), BF16),
                        pltpu.VMEM((nu, CHUNK, LANES), BF16),
                        pltpu.VMEM((nu, CHUNK, LANES), BF16),
                        pltpu.VMEM((nu, CHUNK, LANES), BF16),
                        pltpu.VMEM((nu, CHUNK, LANES), BF16),
                        pltpu.VMEM((nu, CHUNK, 2 * LANES), F32),
                        pltpu.VMEM((nu, CHUNK, 2 * LANES), F32),
                        pltpu.VMEM((nu, CHUNK, LANES), BF16),
                        pltpu.VMEM((nu, CHUNK, LANES), F32),
                        pltpu.VMEM((nu, LANES, LANES), BF16),
                        pltpu.VMEM((nu, LANES, LANES), F32)],
        compiler_params=_cparams(("parallel", "arbitrary")),
        name="rwkv",
    )(zr3, row(tshift_mu), row(decay_w0), dup, row(iclr_a0), iup.astype(BF16), gate_up.astype(BF16), row(k_k),
      row(k_a), row(r_k), row(lnx_g), row(lnx_b), seg, tri)


def _merge_kernel(x_ref, attn_ref, rw_ref, ga_ref, gr_ref, wa_ref, wr_ref, wo_ref, g1_ref,
                  n2_ref, sc_ref, sh_ref, rwt_ref, rb_ref, x1_ref, h2_ref, gt_ref, cnt_ref):
    a = _dot(attn_ref[...], wa_ref[...])
    rr = _dot(rw_ref[...], wr_ref[...])
    mixed = _sigmoid(ga_ref[...].astype(F32)) * a + _sigmoid(gr_ref[...].astype(F32)) * rr
    x1 = x_ref[...] + g1_ref[0] * _dot(mixed.astype(BF16), wo_ref[...])
    x1_ref[...] = x1
    ms = jnp.mean(x1 * x1, axis=-1, keepdims=True)
    h2 = x1 * lax.rsqrt(ms + RMS_EPS) * n2_ref[...]
    h2 = h2 * (1.0 + sc_ref[0]) + sh_ref[0]
    h2_ref[...] = _pack_bf16_pairs(h2)

    tm = x1.shape[0]
    E, G, EG = N_EXPERTS, N_GROUPS, N_EXPERTS // N_GROUPS
    scores = _sigmoid(_dot3(rwt_ref[...], h2, nt=True))
    choice = scores + rb_ref[...]
    c3 = choice.reshape(G, EG, tm)
    e_i = lax.broadcasted_iota(I32, (G, EG, tm), 1)
    m1 = jnp.max(c3, axis=1, keepdims=True)
    first = jnp.min(jnp.where(c3 == m1, e_i, EG), axis=1, keepdims=True)
    m2 = jnp.max(jnp.where(e_i == first, -jnp.inf, c3), axis=1, keepdims=True)
    grp = (m1 + m2).reshape(G, tm)
    g_i = lax.broadcasted_iota(I32, (G, tm), 0)
    rank = jnp.zeros((G, tm), I32)
    for o in range(G):
        other = grp[o:o + 1, :]
        rank = rank + jnp.where((other > grp) | ((other == grp) & (o < g_i)), 1, 0)
    gsel = rank < TOPK_GROUPS
    esel = jnp.broadcast_to(gsel.reshape(G, 1, tm), (G, EG, tm)).reshape(E, tm)
    mc = jnp.where(esel, choice, -jnp.inf)
    x_i = lax.broadcasted_iota(I32, (E, tm), 0)
    erank = jnp.zeros((E, tm), I32)
    for o in range(E):
        other = mc[o:o + 1, :]
        erank = erank + jnp.where((other > mc) | ((other == mc) & (o < x_i)), 1, 0)
    top = erank < MOE_TOPK
    gw = jnp.where(top, scores, 0.0)
    gw = gw / jnp.sum(gw, axis=0, keepdims=True) * ROUTED_SCALE
    gt_ref[...] = gw
    sel = jnp.where(gw > 0.0, 1.0, 0.0)
    cnt_ref[0] = jnp.broadcast_to(jnp.sum(sel, axis=1, keepdims=True), (E, LANES))


def _pack_bf16_pairs(x):
    n = x.shape[1] // 2
    bits = pltpu.bitcast(x.astype(BF16).astype(F32), I32)
    return bits[:, :n] | lax.shift_right_logical(bits[:, n:], 16)


def _unpack_bf16_pairs(p):
    hi = pltpu.bitcast(p & jnp.int32(-65536), F32)
    lo = pltpu.bitcast(lax.shift_left(p, 16), F32)
    return jnp.concatenate([hi, lo], axis=1).astype(BF16)


def _merge_call(x2, attn, rw, ga, gr, wa, wr, wo, g1, norm2_g, sc2, sh2, router_w, router_bias, S):
    T, D = x2.shape
    B = T // S
    tm = min(512, S)
    tpb = S // tm
    nt = T // tm
    E = N_EXPERTS
    row = lambda i: (i, 0)
    per_b = lambda i: (i // tpb, 0, 0)
    const = lambda shape: pl.BlockSpec(shape, lambda i: (0,) * len(shape))
    return pl.pallas_call(
        _merge_kernel,
        grid=(nt,),
        in_specs=[pl.BlockSpec((tm, D), row), pl.BlockSpec((tm, ATTN_W), row), pl.BlockSpec((tm, RWKV_W), row),
                  pl.BlockSpec((tm, D), row), pl.BlockSpec((tm, D), row),
                  const((ATTN_W, D)), const((RWKV_W, D)), const((D, D)),
                  pl.BlockSpec((1, 1, D), per_b), const((1, D)),
                  pl.BlockSpec((1, 1, D), per_b), pl.BlockSpec((1, 1, D), per_b),
                  const((E, D)), const((E, 1))],
        out_specs=[pl.BlockSpec((tm, D), row), pl.BlockSpec((tm, D // 2), row),
                   pl.BlockSpec((E, tm), lambda i: (0, i)), pl.BlockSpec((1, E, LANES), lambda i: (i, 0, 0))],
        out_shape=[jax.ShapeDtypeStruct((T, D), F32), jax.ShapeDtypeStruct((T, D // 2), I32),
                   jax.ShapeDtypeStruct((E, T), F32), jax.ShapeDtypeStruct((nt, E, LANES), F32)],
        compiler_params=_cparams(("parallel",)),
        name="merge",
    )(x2, attn, rw, ga, gr, wa, wr, wo, g1.reshape(B, 1, D), norm2_g.reshape(1, D),
      sc2.reshape(B, 1, D), sh2.reshape(B, 1, D), router_w.T, router_bias.reshape(E, 1))


MOE_BLOCK = 1024
SC_CORES, SC_SUBCORES = 2, 16
SC_WORKERS = SC_CORES * SC_SUBCORES
SC_ROWS = 128


def _plan_kernel(gt_ref, cnt_ref, upper_ref, lowe_ref, dest_ref, gw_ref, be_ref, off_ref, *, tm, n_blocks):
    i = pl.program_id(0)
    E = N_EXPERTS
    lowe = lowe_ref[...]

    @pl.when(i == 0)
    def _():
        total = jnp.sum(cnt_ref[...], axis=0)
        nblk = jnp.floor((total + (MOE_BLOCK - 1)) * (1.0 / MOE_BLOCK))
        start_blk = _dot_exact_rhs_lhs(lowe, nblk)
        off_ref[...] = start_blk * MOE_BLOCK
        end_blk = start_blk + nblk
        b_i = lax.broadcasted_iota(I32, (E, n_blocks), 1).astype(F32)
        e_of_b = jnp.sum(jnp.where(end_blk[:, :1] <= b_i, 1.0, 0.0), axis=0, keepdims=True)
        be_ref[...] = e_of_b.astype(I32)

    gt = gt_ref[...]
    sel = gt > 0.0
    selb = jnp.where(sel, 1.0, 0.0).astype(BF16)
    rank = _dot(selb, upper_ref[...])
    dest = off_ref[:, :1] + rank
    off_ref[...] = off_ref[...] + cnt_ref[i]
    kth = _dot(lowe, selb)
    dests, gws = [], []
    for k in range(MOE_TOPK):
        m = sel & (kth == float(k))
        have = jnp.sum(jnp.where(m, 1.0, 0.0), axis=0, keepdims=True)
        d = jnp.sum(jnp.where(m, dest, 0.0), axis=0, keepdims=True)
        dests.append(jnp.where(have > 0.0, d, float((n_blocks - 1) * MOE_BLOCK)))
        gws.append(jnp.sum(jnp.where(m, gt, 0.0), axis=0, keepdims=True))
    dest_ref[...] = jnp.concatenate(dests, axis=0).astype(I32)
    gpad = jnp.concatenate(gws + [jnp.zeros((LANES - MOE_TOPK, tm), F32)], axis=0)
    gw_ref[...] = gpad.T


def _plan_call(gate_t, cnt, n_blocks):
    E, T = gate_t.shape
    nt = cnt.shape[0]
    tm = T // nt
    idx = jnp.arange(tm)
    upper = (idx[:, None] < idx[None, :]).astype(BF16)
    ei = jnp.arange(E)
    lowe = (ei[None, :] < ei[:, None]).astype(BF16)
    kern = functools.partial(_plan_kernel, tm=tm, n_blocks=n_blocks)
    const = lambda shape: pl.BlockSpec(shape, lambda i: (0,) * len(shape))
    return pl.pallas_call(
        kern,
        grid=(nt,),
        in_specs=[pl.BlockSpec((E, tm), lambda i: (0, i)), const((nt, E, LANES)), const((tm, tm)), const((E, E))],
        out_specs=[pl.BlockSpec((MOE_TOPK, tm), lambda i: (0, i)), pl.BlockSpec((tm, LANES), lambda i: (i, 0)),
                   const((1, n_blocks))],
        out_shape=[jax.ShapeDtypeStruct((MOE_TOPK, T), I32), jax.ShapeDtypeStruct((T, LANES), F32),
                   jax.ShapeDtypeStruct((1, n_blocks), I32)],
        scratch_shapes=[pltpu.VMEM((E, LANES), F32)],
        compiler_params=_cparams(("arbitrary",)),
        name="plan",
    )(gate_t, cnt, upper, lowe)


def _sc_index_layout(dest_t):
    K, T = dest_t.shape
    n_ch = T // (SC_WORKERS * SC_ROWS)
    return dest_t.reshape(K, SC_WORKERS, n_ch, SC_ROWS).transpose(1, 2, 0, 3).reshape(SC_WORKERS, n_ch * K, SC_ROWS)


def _sc_dispatch(rows, idx, n_slots):
    T, W = rows.shape
    n_ch = T // (SC_WORKERS * SC_ROWS)
    tpw = T // SC_WORKERS
    mesh = plsc.VectorSubcoreMesh(core_axis_name="c", subcore_axis_name="s")

    @functools.partial(
        pl.kernel, mesh=mesh,
        out_type=jax.ShapeDtypeStruct((n_slots, W), I32),
        scratch_types=[pltpu.VMEM((n_ch * MOE_TOPK, SC_ROWS), I32), pltpu.VMEM((SC_ROWS, W), I32),
                       pltpu.SemaphoreType.DMA])
    def kern(x_hbm, idx_hbm, o_hbm, idx_v, rows_v, sem):
        wid = lax.axis_index("s") * SC_CORES + lax.axis_index("c")
        pltpu.sync_copy(idx_hbm.at[wid], idx_v)

        @pl.loop(0, n_ch)
        def _(j):
            pltpu.sync_copy(x_hbm.at[pl.ds(wid * tpw + j * SC_ROWS, SC_ROWS)], rows_v)
            copies = [pltpu.async_copy(rows_v, o_hbm.at[idx_v.at[j * MOE_TOPK + k]], sem)
                      for k in range(MOE_TOPK)]
            for cp in copies:
                cp.wait()

    return kern(rows, idx)


def _sc_combine(slots, idx, T):
    _, W = slots.shape
    n_ch = T // (SC_WORKERS * SC_ROWS)
    tpw = T // SC_WORKERS
    mesh = plsc.VectorSubcoreMesh(core_axis_name="c", subcore_axis_name="s")

    @functools.partial(
        pl.kernel, mesh=mesh,
        out_type=jax.ShapeDtypeStruct((MOE_TOPK, T, W), I32),
        scratch_types=[pltpu.VMEM((n_ch * MOE_TOPK, SC_ROWS), I32), pltpu.VMEM((SC_ROWS, W), I32),
                       pltpu.SemaphoreType.DMA])
    def kern(s_hbm, idx_hbm, o_hbm, idx_v, rows_v, sem):
        wid = lax.axis_index("s") * SC_CORES + lax.axis_index("c")
        pltpu.sync_copy(idx_hbm.at[wid], idx_v)

        @pl.loop(0, n_ch)
        def _(j):
            for k in range(MOE_TOPK):
                pltpu.async_copy(s_hbm.at[idx_v.at[j * MOE_TOPK + k]], rows_v, sem).wait()
                pltpu.sync_copy(rows_v, o_hbm.at[k, pl.ds(wid * tpw + j * SC_ROWS, SC_ROWS)])

    return kern(slots, idx)


def _ffn_kernel(be_ref, x_ref, eg_ref, eu_ref, ed_ref, o_ref):
    used = be_ref[pl.program_id(0)] < N_EXPERTS

    @pl.when(used)
    def _():
        x = _unpack_bf16_pairs(x_ref[...])
        a = _dot(x, eg_ref[0])
        u = _dot(x, eu_ref[0])
        o_ref[...] = _pack_bf16_pairs(_dot((a * _sigmoid(a) * u).astype(BF16), ed_ref[0]))

    @pl.when(jnp.logical_not(used))
    def _():
        o_ref[...] = jnp.zeros(o_ref.shape, I32)


def _ffn_call(xs, block_e, eg, eu, ed, n_blocks):
    P, W = xs.shape
    D, FF = 2 * W, EXPERT_FF
    grid_spec = pltpu.PrefetchScalarGridSpec(
        num_scalar_prefetch=1,
        grid=(n_blocks,),
        in_specs=[pl.BlockSpec((MOE_BLOCK, W), lambda b, be: (b, 0)),
                  pl.BlockSpec((1, D, FF), lambda b, be: (jnp.minimum(be[b], N_EXPERTS - 1), 0, 0)),
                  pl.BlockSpec((1, D, FF), lambda b, be: (jnp.minimum(be[b], N_EXPERTS - 1), 0, 0)),
                  pl.BlockSpec((1, FF, D), lambda b, be: (jnp.minimum(be[b], N_EXPERTS - 1), 0, 0))],
        out_specs=pl.BlockSpec((MOE_BLOCK, W), lambda b, be: (b, 0)))
    return pl.pallas_call(
        _ffn_kernel,
        grid_spec=grid_spec,
        out_shape=jax.ShapeDtypeStruct((P, W), I32),
        compiler_params=_cparams(("parallel",)),
        name="ffn",
    )(block_e, xs, eg, eu, ed)


def _final_kernel(h_ref, c_ref, gw_ref, x1_ref, g2_ref, fg_ref, sg_ref, su_ref, sd_ref, o_ref):
    h = _unpack_bf16_pairs(h_ref[...])
    a = _dot(h, sg_ref[...])
    u = _dot(h, su_ref[...])
    moe = _dot((a * _sigmoid(a) * u).astype(BF16), sd_ref[...])
    gw = gw_ref[...]
    for k in range(MOE_TOPK):
        w = gw[:, k:k + 1]
        y = _unpack_bf16_pairs(c_ref[k]).astype(F32)
        moe = moe + jnp.where(w > 0.0, w * y, 0.0)
    x2 = x1_ref[...] + g2_ref[0] * moe
    ms = jnp.mean(x2 * x2, axis=-1, keepdims=True)
    o_ref[...] = x2 * lax.rsqrt(ms + RMS_EPS) * fg_ref[...]


def _final_call(h2p, comb, gw, x1, g2, final_g, sg, su, sd, S):
    T, W = h2p.shape
    D, FF = 2 * W, EXPERT_FF
    B = T // S
    tm = min(512, S)
    tpb = S // tm
    row = lambda i: (i, 0)
    const = lambda shape: pl.BlockSpec(shape, lambda i: (0,) * len(shape))
    return pl.pallas_call(
        _final_kernel,
        grid=(T // tm,),
        in_specs=[pl.BlockSpec((tm, W), row), pl.BlockSpec((MOE_TOPK, tm, W), lambda i: (0, i, 0)),
                  pl.BlockSpec((tm, LANES), row), pl.BlockSpec((tm, D), row),
                  pl.BlockSpec((1, 1, D), lambda i: (i // tpb, 0, 0)), const((1, D)),
                  const((D, FF)), const((D, FF)), const((FF, D))],
        out_specs=pl.BlockSpec((tm, D), row),
        out_shape=jax.ShapeDtypeStruct((T, D), F32),
        compiler_params=_cparams(("parallel",)),
        name="final",
    )(h2p, comb, gw, x1, g2.reshape(B, 1, D), final_g.reshape(1, D), sg, su, sd)


def _moe_call(h2p, gate_t, cnt, x1, g2, final_g, eg, eu, ed, sg, su, sd, S):
    T = h2p.shape[0]
    n_blocks = (T * MOE_TOPK) // MOE_BLOCK + N_EXPERTS + 1
    dest_t, gw, block_e = _plan_call(gate_t, cnt, n_blocks)
    idx = _sc_index_layout(dest_t)
    xs = _sc_dispatch(h2p, idx, n_blocks * MOE_BLOCK)
    ys = _ffn_call(xs, block_e.reshape(n_blocks), eg, eu, ed, n_blocks)
    comb = _sc_combine(ys, idx, T)
    return _final_call(h2p, comb, gw, x1, g2, final_g, sg, su, sd, S)


def _layer(x2, c, S, ada_w, ada_b, norm1_g, w_in, rel_bias, tshift_mu, decay_w0, decay_up, iclr_a0, iclr_up,
           gate_up, k_k, k_a, r_k, lnx_g, lnx_b, w_attn_br, w_rwkv_br, w_out, norm2_g, router_w, router_bias,
           exp_gate, exp_up, exp_down, sh_gate, sh_up, sh_down, final_g):
    T, D = x2.shape
    B = T // S
    mod = _mod_call(c, ada_w, ada_b)
    sh1, sc1, g1, sh2, sc2, g2 = jnp.split(mod, 6, axis=-1)

    q, k, vt, iq, ik4, iwt, zr, ga, gr = _inproj_call(x2, norm1_g, sc1, sh1, w_in, S)

    ta = min(512, S)
    assert ta >= LANES and S % ta == 0
    top_k = min(TOPK_MAX, S // 4)
    seq = lambda a: a.reshape(B, S, a.shape[-1])
    mask = _index_call(seq(iq), iwt, seq(ik4), ta, top_k)
    bias_tiles = _bias_call(rel_bias, ta)
    attn = _attn_call(seq(q), seq(k), vt, mask, bias_tiles, rel_bias, ta).reshape(T, ATTN_W)

    rw = _rwkv_call(zr.reshape(B, S, RWKV_IN), tshift_mu, decay_w0, decay_up, iclr_a0, iclr_up, gate_up,
                    k_k, k_a, r_k, lnx_g, lnx_b).reshape(T, RWKV_W)

    x1, h2p, gate_t, cnt = _merge_call(x2, attn, rw, ga, gr, w_attn_br.astype(BF16), w_rwkv_br.astype(BF16),
                                       w_out.astype(BF16), g1, norm2_g, sc2, sh2, router_w, router_bias, S)
    return _moe_call(h2p, gate_t, cnt, x1, g2, final_g, exp_gate.astype(BF16), exp_up.astype(BF16),
                     exp_down.astype(BF16), sh_gate.astype(BF16), sh_up.astype(BF16), sh_down.astype(BF16), S)


def kernel(x, c, ada_w, ada_b, norm1_g, w_in, rel_bias, tshift_mu, decay_w0, decay_up, iclr_a0, iclr_up, gate_up, k_k, k_a, r_k, lnx_g, lnx_b, w_attn_br, w_rwkv_br, w_out, norm2_g, router_w, router_bias, exp_gate, exp_up, exp_down, sh_gate, sh_up, sh_down, final_g):
    B, S, D = x.shape
    depth = ada_w.shape[0]
    assert depth == 1, "the final RMSNorm is fused into the (single) layer's MoE kernel"
    out = _layer(x.reshape(B * S, D), c, S, ada_w[0], ada_b[0], norm1_g[0], w_in[0], rel_bias, tshift_mu[0],
                 decay_w0[0], decay_up[0], iclr_a0[0], iclr_up[0], gate_up[0], k_k[0], k_a[0], r_k[0],
                 lnx_g[0], lnx_b[0], w_attn_br[0], w_rwkv_br[0], w_out[0], norm2_g[0], router_w[0],
                 router_bias[0], exp_gate[0], exp_up[0], exp_down[0], sh_gate[0], sh_up[0], sh_down[0], final_g)
    return out.reshape(B, S, D)
```

```python
import functools
import math

import jax
import jax.numpy as jnp
from jax import lax
from jax.experimental import pallas as pl
from jax.experimental.pallas import tpu as pltpu
from jax.experimental.pallas import tpu_sc as plsc

F32 = jnp.float32
BF16 = jnp.bfloat16
I32 = jnp.int32
I16 = jnp.int16

RMS_EPS = 1e-6
D_MODEL = 1024
N_ATTN_HEADS = 8
ATTN_HEAD_DIM = 64
ATTN_W = 512
IDX_HEADS = 16
IDX_DIM = 32
IDX_Q = 512
TOPK_MAX = 256
N_BUCKETS = 32
MAX_DISTANCE = 128
RWKV_HEADS = 8
RWKV_HEAD = 64
RWKV_W = 512
DECAY_LORA = 64
ICLR_LORA = 64
GATE_LORA = 128
RWKV_IN = 1792
GN_EPS = 64e-5
N_EXPERTS = 64
N_GROUPS = 8
TOPK_GROUPS = 4
MOE_TOPK = 8
EXPERT_FF = 256
ROUTED_SCALE = 2.5

LANES = 128
VMEM_LIMIT = 56 * 1024 * 1024
CHUNK = 64
LOG2E = 1.4426950408889634
INT_MIN = -2147483648
KEY_NEG_INF = -2139095041

NT_DIMS = (((1,), (1,)), ((), ()))


def _cparams(sem):
    return pltpu.CompilerParams(dimension_semantics=sem, vmem_limit_bytes=VMEM_LIMIT)


def _dot(a, b):
    return jnp.dot(a, b, preferred_element_type=F32)


def _dot_nt(a, b):
    return lax.dot_general(a, b, NT_DIMS, preferred_element_type=F32)


def _split2(x):
    hi = x.astype(BF16)
    lo = (x - hi.astype(F32)).astype(BF16)
    return hi, lo


def _split3(x):
    hi = x.astype(BF16)
    r1 = x - hi.astype(F32)
    mid = r1.astype(BF16)
    lo = (r1 - mid.astype(F32)).astype(BF16)
    return hi, mid, lo


def _dot_exact_rhs(x, ones_bf16, terms=2):
    parts = _split3(x) if terms == 3 else _split2(x)
    out = _dot(parts[0], ones_bf16)
    for p in parts[1:]:
        out = out + _dot(p, ones_bf16)
    return out


def _dot3(a, b, nt=False):
    ah, al = _split2(a)
    bh, bl = _split2(b)
    f = _dot_nt if nt else _dot
    return f(ah, bh) + f(ah, bl) + f(al, bh)


def _sigmoid(x):
    return 1.0 / (1.0 + jnp.exp(-x))


def _mod_kernel(c_ref, w_ref, b_ref, o_ref):
    c = c_ref[...]
    s = c * _sigmoid(c)
    o_ref[...] = _dot3(s, w_ref[...]) + b_ref[...]


def _mod_call(c, ada_w, ada_b):
    B, D = c.shape
    N = ada_w.shape[1]
    tn = 1024
    return pl.pallas_call(
        _mod_kernel,
        grid=(N // tn,),
        in_specs=[pl.BlockSpec((B, D), lambda j: (0, 0)),
                  pl.BlockSpec((D, tn), lambda j: (0, j)),
                  pl.BlockSpec((1, tn), lambda j: (0, j))],
        out_specs=pl.BlockSpec((B, tn), lambda j: (0, j)),
        out_shape=jax.ShapeDtypeStruct((B, N), F32),
        compiler_params=_cparams(("arbitrary",)),
        name="mod",
    )(c, ada_w, ada_b.reshape(1, N))


_OFF_Q, _OFF_K, _OFF_IQ, _OFF_IK4, _OFF_ZR, _OFF_GA, _OFF_GR, _N_PACK = (
    0, 512, 1024, 1536, 2048, 3840, 4864, 5888)
IDX_PER_BLOCK = LANES // IDX_DIM


def _pack_w_in(w_in):
    D = w_in.shape[0]
    w_ik = w_in[:, 2048:2080]
    ik4 = jnp.zeros((D, IDX_PER_BLOCK * LANES), w_in.dtype)
    for j in range(IDX_PER_BLOCK):
        ik4 = lax.dynamic_update_slice(ik4, w_ik, (0, j * LANES + j * IDX_DIM))
    w_pack = jnp.concatenate([w_in[:, 0:1024], w_in[:, 1536:2048], ik4, w_in[:, 2096:]], axis=1).astype(BF16)
    return w_pack, w_in[:, 1024:1536].T.astype(BF16), w_in[:, 2080:2096].T.astype(BF16)


def _inproj_kernel(x_ref, g_ref, sc_ref, sh_ref, w_ref, wvt_ref, wiwt_ref,
                   q_ref, k_ref, vt_ref, iq_ref, ik4_ref, iwt_ref, zr_ref, ga_ref, gr_ref):
    x = x_ref[...]
    ms = jnp.mean(x * x, axis=-1, keepdims=True)
    h = x * lax.rsqrt(ms + RMS_EPS) * g_ref[...]
    h = h * (1.0 + sc_ref[0]) + sh_ref[0]
    hb = h.astype(BF16)

    def proj(lo, hi):
        return _dot(hb, w_ref[:, lo:hi])

    q_ref[...] = (proj(_OFF_Q, _OFF_K) * (ATTN_HEAD_DIM ** -0.5 * LOG2E)).astype(BF16)
    k_ref[...] = proj(_OFF_K, _OFF_IQ).astype(BF16)
    iq_ref[...] = proj(_OFF_IQ, _OFF_IK4).astype(BF16)
    ik4_ref[...] = proj(_OFF_IK4, _OFF_ZR).astype(BF16)
    zr_ref[...] = proj(_OFF_ZR, _OFF_GA)
    ga_ref[...] = proj(_OFF_GA, _OFF_GR).astype(BF16)
    gr_ref[...] = proj(_OFF_GR, _N_PACK).astype(BF16)
    vt_ref[0] = _dot_nt(wvt_ref[...], hb).astype(BF16)
    iwt_ref[0] = _dot_nt(wiwt_ref[...], hb)


def _inproj_call(x2, norm_g, sc, sh, w_in, S):
    T, D = x2.shape
    B = T // S
    tm = min(512, S)
    tpb = S // tm
    w_pack, wvt, wiwt = _pack_w_in(w_in)
    row = lambda i: (i, 0)
    per_b = lambda i: (i // tpb, 0, 0)
    colblk = lambda i: (i // tpb, 0, i % tpb)
    const = lambda shape: pl.BlockSpec(shape, lambda i: (0,) * len(shape), pipeline_mode=pl.Buffered(1))
    rows_out = ((512, BF16), (512, BF16), (512, BF16), (512, BF16), (RWKV_IN, F32), (D, BF16), (D, BF16))
    out_specs = [pl.BlockSpec((tm, w), row) for w, _ in rows_out]
    out_shape = [jax.ShapeDtypeStruct((T, w), dt) for w, dt in rows_out]
    out_specs[2:2] = [pl.BlockSpec((1, ATTN_W, tm), colblk)]
    out_shape[2:2] = [jax.ShapeDtypeStruct((B, ATTN_W, S), BF16)]
    out_specs[5:5] = [pl.BlockSpec((1, IDX_HEADS, tm), colblk)]
    out_shape[5:5] = [jax.ShapeDtypeStruct((B, IDX_HEADS, S), F32)]
    return pl.pallas_call(
        _inproj_kernel,
        grid=(T // tm,),
        in_specs=[pl.BlockSpec((tm, D), row),
                  pl.BlockSpec((1, D), lambda i: (0, 0)),
                  pl.BlockSpec((1, 1, D), per_b),
                  pl.BlockSpec((1, 1, D), per_b),
                  const((D, _N_PACK)), const((ATTN_W, D)), const((IDX_HEADS, D))],
        out_specs=out_specs,
        out_shape=out_shape,
        compiler_params=_cparams(("parallel",)),
        name="inproj",
    )(x2, norm_g.reshape(1, D), sc.reshape(B, 1, D), sh.reshape(B, 1, D), w_pack, wvt, wiwt)


def _t5_bucket(rel):
    n = jnp.maximum(rel, 0)
    max_exact = N_BUCKETS // 2
    nf = jnp.maximum(n, 1).astype(F32)
    large = max_exact + (jnp.log(nf / max_exact) / math.log(MAX_DISTANCE / max_exact)
                         * (N_BUCKETS - max_exact)).astype(I32)
    large = jnp.minimum(large, N_BUCKETS - 1)
    return jnp.where(n < max_exact, n, large)


def _bias_kernel(bucket_ref, rb_ref, o_ref):
    h = pl.program_id(0)
    bk = bucket_ref[...]
    out = jnp.zeros(bk.shape, F32)
    for b in range(N_BUCKETS):
        out = jnp.where(bk == b, rb_ref[b, h] * LOG2E, out)
    o_ref[0] = out


def _bias_call(rel_bias, tq):
    r = jnp.arange(tq, dtype=I32)[None, :]
    c = jnp.arange(tq, dtype=I32)[:, None]
    buckets = jnp.stack([_t5_bucket(r - c), _t5_bucket(tq + r - c)])
    return pl.pallas_call(
        _bias_kernel,
        grid=(N_ATTN_HEADS,),
        in_specs=[pl.BlockSpec((2, tq, tq), lambda h: (0, 0, 0)),
                  pl.BlockSpec(memory_space=pltpu.SMEM)],
        out_specs=pl.BlockSpec((1, 2, tq, tq), lambda h: (h, 0, 0, 0)),
        out_shape=jax.ShapeDtypeStruct((N_ATTN_HEADS, 2, tq, tq), F32),
        compiler_params=_cparams(("arbitrary",)),
        name="bias",
    )(buckets, rel_bias)


def _index_kernel(iq_ref, iwt_ref, ik4_ref, lower_ref, mask_ref, key_ref, k16_ref, *, t, nk, top_k, scale):
    qi = pl.program_id(1)
    nkt = qi + 1
    ksub = LANES
    qpos = qi * t + lax.broadcasted_iota(I32, (ksub, t), 1)

    def score_tile(kt, carry):
        kbase = pl.multiple_of(kt * t, t)
        for ks in range(t // ksub):
            acc = jnp.zeros((ksub, t), F32)
            ik_rows = ik4_ref[0, pl.ds(kbase + ks * ksub, ksub), :]
            ik_stack = jnp.concatenate([ik_rows[:, j * LANES:(j + 1) * LANES] for j in range(IDX_PER_BLOCK)],
                                       axis=0)
            for g in range(IDX_HEADS // IDX_PER_BLOCK):
                d4 = _dot_nt(ik_stack, iq_ref[0, :, g * LANES:(g + 1) * LANES])
                for j in range(IDX_PER_BLOCK):
                    h = g * IDX_PER_BLOCK + j
                    acc = acc + jnp.maximum(d4[j * ksub:(j + 1) * ksub], 0.0) * iwt_ref[0, h:h + 1, :]
            s = acc * scale
            kpos = kt * t + ks * ksub + lax.broadcasted_iota(I32, (ksub, t), 0)
            s = jnp.where(kpos <= qpos, s, -jnp.inf)
            bits = pltpu.bitcast(s, I32)
            key = bits ^ ((bits >> 31) & 0x7FFFFFFF)
            key_ref[kt, ks * ksub:(ks + 1) * ksub, :] = key
            k16_ref[kt, ks * ksub:(ks + 1) * ksub, :] = (key >> 16).astype(I16)
        return carry

    lax.fori_loop(0, nkt, score_tile, 0)

    pack = 16

    def search16():
        def bit_body(i, ans):
            cand = ans | lax.shift_left(jnp.int32(1), 15 - i)
            cand16 = (cand - 32768).astype(I16)

            def cnt_body(kt, acc):
                one = jnp.where(k16_ref[kt] >= cand16, jnp.int16(1), jnp.int16(0))
                for r in range(t // pack):
                    acc = acc + one[r * pack:(r + 1) * pack, :]
                return acc

            acc = lax.fori_loop(0, nkt, cnt_body, jnp.zeros((pack, t), I16))
            cnt = jnp.sum(acc.astype(I32), axis=0, keepdims=True)
            return jnp.where(cnt >= top_k, cand, ans)

        return lax.fori_loop(0, 16, bit_body, jnp.zeros((1, t), I32))

    hi = search16() - 32768

    def remap_body(kt, carry):
        key = key_ref[kt]
        khi = key >> 16
        lo = (key & 0xFFFF) - 32768
        k16_ref[kt] = jnp.where(khi > hi, 32767, jnp.where(khi == hi, lo, -32768)).astype(I16)
        return carry

    lax.fori_loop(0, nkt, remap_body, 0)
    thr = hi * 65536 + search16()

    def count(pred):
        def body(kt, acc):
            one = jnp.where(pred(key_ref[kt]), 1.0, 0.0)
            return acc + jnp.sum(one.reshape(t // 8, 8, t), axis=0)
        return jnp.sum(lax.fori_loop(0, nkt, body, jnp.zeros((8, t), F32)), axis=0, keepdims=True)

    n_ge = count(lambda keys: keys >= thr)
    has_tie = jnp.max(n_ge) > float(top_k)

    @pl.when(jnp.logical_not(has_tie))
    def _():
        def mask_body(kt, carry):
            keys = key_ref[kt]
            sel = (keys >= thr) & (keys > KEY_NEG_INF)
            mask_ref[0, 0, kt] = jnp.where(sel, 0.0, -jnp.inf).astype(BF16)
            return carry

        lax.fori_loop(0, nkt, mask_body, 0)

    @pl.when(has_tie)
    def _():
        need = float(top_k) - count(lambda keys: keys > thr)

        def mask_body(kt, seen):
            keys = key_ref[kt]
            tie = (keys == thr) & (keys > KEY_NEG_INF)
            tie_b = jnp.where(tie, 1.0, 0.0).astype(BF16)
            before = seen + _dot(lower_ref[...], tie_b)
            sel = (keys > thr) | (tie & (before < need))
            mask_ref[0, 0, kt] = jnp.where(sel, 0.0, -jnp.inf).astype(BF16)
            return seen + jnp.sum(tie_b.astype(F32).reshape(t // 8, 8, t).sum(axis=0), axis=0, keepdims=True)

        lax.fori_loop(0, nkt, mask_body, jnp.zeros((1, t), F32))

    def fill_body(kt, carry):
        mask_ref[0, 0, kt] = jnp.full((t, t), -jnp.inf, BF16)
        return carry

    lax.fori_loop(nkt, nk, fill_body, 0)


def _index_call(iq, iwt, ik4, t, top_k):
    B, S, _ = iq.shape
    n = S // t
    scale = (IDX_HEADS ** -0.5) * (IDX_DIM ** -0.5)
    kern = functools.partial(_index_kernel, t=t, nk=n, top_k=top_k, scale=scale)
    pos = jnp.arange(t)
    lower = (pos[None, :] < pos[:, None]).astype(BF16)
    return pl.pallas_call(
        kern,
        grid=(B, n),
        in_specs=[pl.BlockSpec((1, t, IDX_Q), lambda b, i: (b, i, 0)),
                  pl.BlockSpec((1, IDX_HEADS, t), lambda b, i: (b, 0, i)),
                  pl.BlockSpec((1, S, IDX_PER_BLOCK * LANES), lambda b, i: (b, 0, 0)),
                  pl.BlockSpec((t, t), lambda b, i: (0, 0))],
        out_specs=pl.BlockSpec((1, 1, n, t, t), lambda b, i: (b, i, 0, 0, 0)),
        out_shape=jax.ShapeDtypeStruct((B, n, n, t, t), BF16),
        scratch_shapes=[pltpu.VMEM((n, t, t), I32), pltpu.VMEM((n, t, t), I16)],
        compiler_params=_cparams(("parallel", "arbitrary")),
        name="index",
    )(iq, iwt, ik4, lower)


ONES_ROWS = 16


def _attn_kernel(qi_tab, kt_tab, q_ref, k_ref, vt_ref, mask_ref, bias_ref, rb_ref, o_ref,
                 qz_ref, m_ref, acc_ref, s_ref, *, t):
    s_id = pl.program_id(1)
    qi = qi_tab[s_id]
    kt = kt_tab[s_id]
    dh = ATTN_HEAD_DIM

    @pl.when(kt == 0)
    def _():
        m_ref[...] = jnp.full(m_ref.shape, -jnp.inf, F32)
        acc_ref[...] = jnp.zeros(acc_ref.shape, F32)
        lane = lax.broadcasted_iota(I32, (t, LANES), 1)
        for h in range(N_ATTN_HEADS):
            blk = q_ref[0, :, (h // 2) * LANES:(h // 2 + 1) * LANES]
            keep = (lane < dh) if h % 2 == 0 else (lane >= dh)
            qz_ref[h] = jnp.where(keep, blk, jnp.zeros_like(blk))

    def step(bias_tile, bias_const):
        maskf = mask_ref[0, 0, 0].astype(F32)
        ones = jnp.ones((ONES_ROWS, t), BF16)

        for h in range(N_ATTN_HEADS):
            k_blk = k_ref[0, :, (h // 2) * LANES:(h // 2 + 1) * LANES]
            s = _dot_nt(k_blk, qz_ref[h]) + maskf
            s_ref[h] = s if bias_tile is None else s + bias_tile(h)
        for h in range(N_ATTN_HEADS):
            s = s_ref[h]
            c = bias_const(h)
            m_old = m_ref[h:h + 1, :]
            m_cur = jnp.max(jnp.max(s.reshape(t // 8, 8, t), axis=0), axis=0, keepdims=True) + c
            m_new = jnp.maximum(m_old, m_cur)
            m_safe = jnp.where(m_new == -jnp.inf, 0.0, m_new)
            alpha = jnp.exp2(m_old - m_safe)
            p = jnp.exp2(s - (m_safe - c)).astype(BF16)
            v_aug = jnp.concatenate([vt_ref[0, h * dh:(h + 1) * dh, :], ones], axis=0)
            acc_ref[h] = alpha * acc_ref[h] + _dot(v_aug, p)
            m_ref[h:h + 1, :] = m_new

    @pl.when(kt == qi)
    def _():
        step(lambda h: bias_ref[h, 0], lambda h: 0.0)

    @pl.when(kt == qi - 1)
    def _():
        step(lambda h: bias_ref[h, 1], lambda h: 0.0)

    @pl.when(kt < qi - 1)
    def _():
        step(None, lambda h: rb_ref[N_BUCKETS - 1, h] * LOG2E)

    @pl.when(kt == qi)
    def _():
        outs = []
        for h in range(N_ATTN_HEADS):
            a = acc_ref[h]
            outs.append(a[:dh, :] / a[dh:dh + 1, :])
        o_ref[0] = jnp.concatenate(outs, axis=0).T.astype(BF16)


def _attn_call(q, k, vt, mask, bias_tiles, rel_bias, t):
    B, S, W = q.shape
    n = S // t
    H = N_ATTN_HEADS
    qi_tab = jnp.asarray([i for i in range(n) for _ in range(i + 1)], I32)
    kt_tab = jnp.asarray([j for i in range(n) for j in range(i + 1)], I32)
    kern = functools.partial(_attn_kernel, t=t)
    grid_spec = pltpu.PrefetchScalarGridSpec(
        num_scalar_prefetch=2,
        grid=(B, int(qi_tab.shape[0])),
        in_specs=[pl.BlockSpec((1, t, W), lambda b, s, qt, kt: (b, qt[s], 0)),
                  pl.BlockSpec((1, t, W), lambda b, s, qt, kt: (b, kt[s], 0)),
                  pl.BlockSpec((1, W, t), lambda b, s, qt, kt: (b, 0, kt[s])),
                  pl.BlockSpec((1, 1, 1, t, t), lambda b, s, qt, kt: (b, qt[s], kt[s], 0, 0)),
                  pl.BlockSpec((H, 2, t, t), lambda b, s, qt, kt: (0, 0, 0, 0), pipeline_mode=pl.Buffered(1)),
                  pl.BlockSpec(memory_space=pltpu.SMEM)],
        out_specs=pl.BlockSpec((1, t, W), lambda b, s, qt, kt: (b, qt[s], 0)),
        scratch_shapes=[pltpu.VMEM((H, t, LANES), BF16),
                        pltpu.VMEM((H, t), F32),
                        pltpu.VMEM((H, ATTN_HEAD_DIM + ONES_ROWS, t), F32),
                        pltpu.VMEM((H, t, t), F32)])
    return pl.pallas_call(
        kern,
        grid_spec=grid_spec,
        out_shape=jax.ShapeDtypeStruct((B, S, W), BF16),
        compiler_params=_cparams(("parallel", "arbitrary")),
        name="attn",
    )(qi_tab, kt_tab, q, k, vt, mask, bias_tiles, rel_bias)


def _blockdiag_rows(x):
    lane = lax.broadcasted_iota(I32, x.shape, 1)
    zero = jnp.zeros_like(x)
    return jnp.concatenate([jnp.where(lane < RWKV_HEAD, x, zero),
                            jnp.where(lane >= RWKV_HEAD, x, zero)], axis=0)


def _rwkv_kernel(z_ref, mu_ref, w0_ref, dup_ref, a0_ref, iup_ref, gup_ref, kk_ref, ka_ref, rk_ref,
                 lng_ref, lnb_ref, seg_ref, tri_ref, o_ref,
                 prev_ref, st_ref, at_ref, rt_ref, bt_ref, kt_ref, bh_ref, kh_ref, v_ref, pc_ref, y_ref,
                 la_ref, lb_ref, mak_ref, arb_ref, ark_ref, wa_ref, wb_ref, g1_ref, g2_ref, h1_ref, h2_ref,
                 *, tt):
    j = pl.program_id(1)
    W = RWKV_W
    C = CHUNK
    nchunk = tt // C
    npair = RWKV_HEADS // 2

    @pl.when(j == 0)
    def _():
        prev_ref[...] = jnp.zeros(prev_ref.shape, F32)
        st_ref[...] = jnp.zeros(st_ref.shape, F32)

    z = z_ref[0]
    row = lax.broadcasted_iota(I32, z.shape, 0)
    z_prev = jnp.where(row == 0, prev_ref[...], pltpu.roll(z, 1, axis=0))
    prev_ref[...] = z[tt - 1:tt, :]
    z = z + mu_ref[...] * (z_prev - z)

    r = z[:, 0:W]
    k = z[:, W:2 * W]
    v = z[:, 2 * W:3 * W]
    wdad = z[:, 3 * W:3 * W + 2 * DECAY_LORA]
    gd = z[:, 3 * W + 2 * DECAY_LORA:]

    w_pre = w0_ref[...] + _dot3(jnp.tanh(wdad), dup_ref[...])
    neg = -w_pre
    softplus = jnp.maximum(neg, 0.0) + jnp.log(1.0 + jnp.exp(-jnp.abs(neg)))
    lw = -jnp.exp(-softplus - 0.5)
    a_lr = _sigmoid(a0_ref[...] + _dot(wdad.astype(BF16), iup_ref[...]))
    g = _dot(_sigmoid(gd).astype(BF16), gup_ref[...])

    seg = seg_ref[...]
    kk = k * kk_ref[...]
    kk = kk / jnp.maximum(jnp.sqrt(_dot_exact_rhs(kk * kk, seg)), 1e-12)
    k2 = k * (1.0 + (a_lr - 1.0) * ka_ref[...])
    a_vec = -kk
    b_vec = kk * a_lr

    cum = _dot_exact_rhs_lhs(tri_ref[...], lw)
    tot = jnp.concatenate([jnp.broadcast_to(cum[(c + 1) * C - 1:(c + 1) * C, :], (C, W)) for c in range(nchunk)],
                          axis=0)
    p_inv = jnp.exp(-cum)
    p_out = jnp.exp(tot - cum)
    at_ref[...] = a_vec * jnp.exp(cum - lw)
    rt_ref[...] = r * jnp.exp(cum)
    bt_ref[...] = (b_vec * p_inv).astype(BF16)
    kt_ref[...] = (k2 * p_inv).astype(BF16)
    bh_ref[...] = b_vec * p_out
    kh_ref[...] = k2 * p_out
    v_ref[...] = v
    pc_ref[...] = jnp.exp(tot)

    t_i = lax.broadcasted_iota(I32, (C, LANES), 0)
    s_i = lax.broadcasted_iota(I32, (C, LANES), 1) % C
    strict = s_i < t_i
    incl = s_i <= t_i
    r_i = lax.broadcasted_iota(I32, (LANES, LANES), 0)
    c_i = lax.broadcasted_iota(I32, (LANES, LANES), 1)
    same_head = (r_i < RWKV_HEAD) == (c_i < RWKV_HEAD)
    diag = r_i == c_i
    nstage = int(math.log2(C))
    zero = jnp.zeros((C, LANES), F32)
    zsq = jnp.zeros((LANES, LANES), F32)
    units = [(c, p) for c in range(nchunk) for p in range(npair)]

    def sl(c, p):
        return slice(c * C, (c + 1) * C), slice(p * LANES, (p + 1) * LANES)

    def bd2(w):
        wb = w.astype(BF16)
        return jnp.concatenate([_blockdiag_rows(wb[:, :LANES]), _blockdiag_rows(wb[:, LANES:])], axis=1)

    for i, (c, p) in enumerate(units):
        rows, cols = sl(c, p)
        lhs = jnp.concatenate([at_ref[rows, cols], rt_ref[rows, cols]], axis=0).astype(BF16)
        rhs = jnp.concatenate([_blockdiag_rows(bt_ref[rows, cols]),
                               _blockdiag_rows(kt_ref[rows, cols])], axis=0)
        prod = _dot_nt(lhs, rhs)
        la_ref[i] = jnp.where(strict, prod[:C, :LANES], zero).astype(BF16)
        mak_ref[i] = jnp.where(strict, prod[:C, LANES:], zero).astype(BF16)
        arb_ref[i] = jnp.where(incl, prod[C:, :LANES], zero).astype(BF16)
        ark_ref[i] = jnp.where(incl, prod[C:, LANES:], zero).astype(BF16)
    for i, (c, p) in enumerate(units):
        rows, cols = sl(c, p)
        w2 = _dot(mak_ref[i], _blockdiag_rows(v_ref[rows, cols].astype(BF16)))
        wa_ref[i] = jnp.concatenate([at_ref[rows, cols], w2], axis=1)
    l_bufs, w_bufs = (la_ref, lb_ref), (wa_ref, wb_ref)
    for s in range(nstage):
        l_in, l_out = l_bufs[s % 2], l_bufs[(s + 1) % 2]
        w_in, w_out = w_bufs[s % 2], w_bufs[(s + 1) % 2]
        for i in range(len(units)):
            lmat = l_in[i]
            w = w_in[i]
            w_out[i] = w + _dot(lmat, bd2(w))
            if s < nstage - 1:
                l_out[i] = _dot(lmat, _blockdiag_rows(lmat)).astype(BF16)
    w_fin = w_bufs[nstage % 2]
    for i, (c, p) in enumerate(units):
        rows, cols = sl(c, p)
        w = w_fin[i]
        wb = w.astype(BF16)
        vb = v_ref[rows, cols].astype(BF16)
        gg = _dot(arb_ref[i], bd2(w))
        g1_ref[i] = (rt_ref[rows, cols] + gg[:, :LANES]).astype(BF16)
        g2_ref[i] = gg[:, LANES:] + _dot(ark_ref[i], _blockdiag_rows(vb))
        bk_t = jnp.concatenate([bh_ref[rows, cols], kh_ref[rows, cols]], axis=0).T
        hrhs = jnp.concatenate([wb, jnp.concatenate([jnp.zeros((C, LANES), BF16), vb], axis=1)], axis=0)
        hh = _dot(bk_t.astype(BF16), hrhs)
        pc = pc_ref[c * C:c * C + 1, cols]
        h1 = jnp.where(same_head, hh[:, :LANES], zsq) + jnp.where(diag, jnp.broadcast_to(pc, (LANES, LANES)), zsq)
        h1_ref[i] = h1.astype(BF16)
        h2_ref[i] = jnp.where(same_head, hh[:, LANES:], zsq)
    for c in range(nchunk):
        sts = [st_ref[p].astype(BF16) for p in range(npair)]
        for p in range(npair):
            i = c * npair + p
            rows, cols = sl(c, p)
            y_ref[rows, cols] = _dot(g1_ref[i], sts[p]) + g2_ref[i]
            st_ref[p] = _dot(h1_ref[i], sts[p]) + h2_ref[i]

    y = y_ref[...]
    inv_n = 1.0 / RWKV_HEAD
    mean = _dot(y.astype(BF16), seg) * inv_n
    yc = y - mean
    var = _dot((yc * yc).astype(BF16), seg) * inv_n
    yn = yc * lax.rsqrt(var + GN_EPS) * lng_ref[...] + lnb_ref[...]
    bonus = _dot((r * k2 * rk_ref[...]).astype(BF16), seg) * v
    o_ref[0] = ((yn + bonus) * g).astype(BF16)


def _dot_exact_rhs_lhs(ones_bf16, x):
    hi, mid, lo = _split3(x)
    return _dot(ones_bf16, hi) + _dot(ones_bf16, mid) + _dot(ones_bf16, lo)


def _rwkv_call(zr3, tshift_mu, decay_w0, decay_up, iclr_a0, iclr_up, gate_up, k_k, k_a, r_k, lnx_g, lnx_b):
    B, S, _ = zr3.shape
    tt = min(256, S)
    W = RWKV_W
    row = lambda a: a.reshape(1, -1).astype(F32)
    dup = jnp.concatenate([decay_up, jnp.zeros((ICLR_LORA, W), F32)], axis=0)
    iup = jnp.concatenate([jnp.zeros((DECAY_LORA, W), F32), iclr_up], axis=0)
    idx = jnp.arange(W)
    seg = (idx[:, None] // RWKV_HEAD == idx[None, :] // RWKV_HEAD).astype(BF16)
    t = jnp.arange(tt)
    same_chunk = t[:, None] // CHUNK == t[None, :] // CHUNK
    tri = (same_chunk & (t[None, :] <= t[:, None])).astype(BF16)
    const = lambda shape: pl.BlockSpec(shape, lambda b, j: (0,) * len(shape))
    kern = functools.partial(_rwkv_kernel, tt=tt)
    nu = (tt // CHUNK) * (RWKV_HEADS // 2)
    return pl.pallas_call(
        kern,
        grid=(B, S // tt),
        in_specs=[pl.BlockSpec((1, tt, RWKV_IN), lambda b, j: (b, j, 0)),
                  const((1, RWKV_IN)), const((1, W)), const((2 * DECAY_LORA, W)), const((1, W)),
                  const((2 * ICLR_LORA, W)), const((GATE_LORA, W)), const((1, W)), const((1, W)),
                  const((1, W)), const((1, W)), const((1, W)),
                  const((W, W)), const((tt, tt))],
        out_specs=pl.BlockSpec((1, tt, W), lambda b, j: (b, j, 0)),
        out_shape=jax.ShapeDtypeStruct((B, S, W), BF16),
        scratch_shapes=[pltpu.VMEM((1, RWKV_IN), F32),
                        pltpu.VMEM((RWKV_HEADS // 2, LANES, LANES), F32),
                        pltpu.VMEM((tt, W), F32),
                        pltpu.VMEM((tt, W), F32),
                        pltpu.VMEM((tt, W), BF16),
                        pltpu.VMEM((tt, W), BF16),
                        pltpu.VMEM((tt, W), F32),
                        pltpu.VMEM((tt, W), F32),
                        pltpu.VMEM((tt, W), F32),
                        pltpu.VMEM((tt, W), F32),
                        pltpu.VMEM((tt, W), F32),
                        pltpu.VMEM((nu, CHUNK, LANES), BF16),
                        pltpu.VMEM((nu, CHUNK, LANES), BF16),
                        pltpu.VMEM((nu, CHUNK, LANES), BF16),
                        pltpu.VMEM((nu, CHUNK, LANES), BF16),
                        pltpu.VMEM((nu, CHUNK, LANES), BF16),
                        pltpu.VMEM((nu, CHUNK, 2 * LANES), F32),
                        pltpu.VMEM((nu, CHUNK, 2 * LANES), F32),
                        pltpu.VMEM((nu, CHUNK, LANES), BF16),
                        pltpu.VMEM((nu, CHUNK, LANES), F32),
                        pltpu.VMEM((nu, LANES, LANES), BF16),
                        pltpu.VMEM((nu, LANES, LANES), F32)],
        compiler_params=_cparams(("parallel", "arbitrary")),
        name="rwkv",
    )(zr3, row(tshift_mu), row(decay_w0), dup, row(iclr_a0), iup.astype(BF16), gate_up.astype(BF16), row(k_k),
      row(k_a), row(r_k), row(lnx_g), row(lnx_b), seg, tri)


def _merge_kernel(x_ref, attn_ref, rw_ref, ga_ref, gr_ref, wa_ref, wr_ref, wo_ref, g1_ref,
                  n2_ref, sc_ref, sh_ref, rwt_ref, rb_ref, x1_ref, h2_ref, gt_ref, cnt_ref):
    a = _dot(attn_ref[...], wa_ref[...])
    rr = _dot(rw_ref[...], wr_ref[...])
    mixed = _sigmoid(ga_ref[...].astype(F32)) * a + _sigmoid(gr_ref[...].astype(F32)) * rr
    x1 = x_ref[...] + g1_ref[0] * _dot(mixed.astype(BF16), wo_ref[...])
    x1_ref[...] = x1
    ms = jnp.mean(x1 * x1, axis=-1, keepdims=True)
    h2 = x1 * lax.rsqrt(ms + RMS_EPS) * n2_ref[...]
    h2 = h2 * (1.0 + sc_ref[0]) + sh_ref[0]
    h2_ref[...] = _pack_bf16_pairs(h2)

    tm = x1.shape[0]
    E, G, EG = N_EXPERTS, N_GROUPS, N_EXPERTS // N_GROUPS
    scores = _sigmoid(_dot3(rwt_ref[...], h2, nt=True))
    choice = scores + rb_ref[...]
    c3 = choice.reshape(G, EG, tm)
    e_i = lax.broadcasted_iota(I32, (G, EG, tm), 1)
    m1 = jnp.max(c3, axis=1, keepdims=True)
    first = jnp.min(jnp.where(c3 == m1, e_i, EG), axis=1, keepdims=True)
    m2 = jnp.max(jnp.where(e_i == first, -jnp.inf, c3), axis=1, keepdims=True)
    grp = (m1 + m2).reshape(G, tm)
    g_i = lax.broadcasted_iota(I32, (G, tm), 0)
    rank = jnp.zeros((G, tm), I32)
    for o in range(G):
        other = grp[o:o + 1, :]
        rank = rank + jnp.where((other > grp) | ((other == grp) & (o < g_i)), 1, 0)
    gsel = rank < TOPK_GROUPS
    esel = jnp.broadcast_to(gsel.reshape(G, 1, tm), (G, EG, tm)).reshape(E, tm)
    mc = jnp.where(esel, choice, -jnp.inf)
    x_i = lax.broadcasted_iota(I32, (E, tm), 0)
    erank = jnp.zeros((E, tm), I32)
    for o in range(E):
        other = mc[o:o + 1, :]
        erank = erank + jnp.where((other > mc) | ((other == mc) & (o < x_i)), 1, 0)
    top = erank < MOE_TOPK
    gw = jnp.where(top, scores, 0.0)
    gw = gw / jnp.sum(gw, axis=0, keepdims=True) * ROUTED_SCALE
    gt_ref[...] = gw
    sel = jnp.where(gw > 0.0, 1.0, 0.0)
    cnt_ref[0] = jnp.broadcast_to(jnp.sum(sel, axis=1, keepdims=True), (E, LANES))


def _pack_bf16_pairs(x):
    n = x.shape[1] // 2
    bits = pltpu.bitcast(x.astype(BF16).astype(F32), I32)
    return bits[:, :n] | lax.shift_right_logical(bits[:, n:], 16)


def _unpack_bf16_pairs(p):
    hi = pltpu.bitcast(p & jnp.int32(-65536), F32)
    lo = pltpu.bitcast(lax.shift_left(p, 16), F32)
    return jnp.concatenate([hi, lo], axis=1).astype(BF16)


def _merge_call(x2, attn, rw, ga, gr, wa, wr, wo, g1, norm2_g, sc2, sh2, router_w, router_bias, S):
    T, D = x2.shape
    B = T // S
    tm = min(512, S)
    tpb = S // tm
    nt = T // tm
    E = N_EXPERTS
    row = lambda i: (i, 0)
    per_b = lambda i: (i // tpb, 0, 0)
    const = lambda shape: pl.BlockSpec(shape, lambda i: (0,) * len(shape))
    return pl.pallas_call(
        _merge_kernel,
        grid=(nt,),
        in_specs=[pl.BlockSpec((tm, D), row), pl.BlockSpec((tm, ATTN_W), row), pl.BlockSpec((tm, RWKV_W), row),
                  pl.BlockSpec((tm, D), row), pl.BlockSpec((tm, D), row),
                  const((ATTN_W, D)), const((RWKV_W, D)), const((D, D)),
                  pl.BlockSpec((1, 1, D), per_b), const((1, D)),
                  pl.BlockSpec((1, 1, D), per_b), pl.BlockSpec((1, 1, D), per_b),
                  const((E, D)), const((E, 1))],
        out_specs=[pl.BlockSpec((tm, D), row), pl.BlockSpec((tm, D // 2), row),
                   pl.BlockSpec((E, tm), lambda i: (0, i)), pl.BlockSpec((1, E, LANES), lambda i: (i, 0, 0))],
        out_shape=[jax.ShapeDtypeStruct((T, D), F32), jax.ShapeDtypeStruct((T, D // 2), I32),
                   jax.ShapeDtypeStruct((E, T), F32), jax.ShapeDtypeStruct((nt, E, LANES), F32)],
        compiler_params=_cparams(("parallel",)),
        name="merge",
    )(x2, attn, rw, ga, gr, wa, wr, wo, g1.reshape(B, 1, D), norm2_g.reshape(1, D),
      sc2.reshape(B, 1, D), sh2.reshape(B, 1, D), router_w.T, router_bias.reshape(E, 1))


MOE_BLOCK = 1024
SC_CORES, SC_SUBCORES = 2, 16
SC_WORKERS = SC_CORES * SC_SUBCORES
SC_ROWS = 128


def _plan_kernel(gt_ref, cnt_ref, upper_ref, lowe_ref, dest_ref, gw_ref, be_ref, off_ref, *, tm, n_blocks):
    i = pl.program_id(0)
    E = N_EXPERTS
    lowe = lowe_ref[...]

    @pl.when(i == 0)
    def _():
        total = jnp.sum(cnt_ref[...], axis=0)
        nblk = jnp.floor((total + (MOE_BLOCK - 1)) * (1.0 / MOE_BLOCK))
        start_blk = _dot_exact_rhs_lhs(lowe, nblk)
        off_ref[...] = start_blk * MOE_BLOCK
        end_blk = start_blk + nblk
        b_i = lax.broadcasted_iota(I32, (E, n_blocks), 1).astype(F32)
        e_of_b = jnp.sum(jnp.where(end_blk[:, :1] <= b_i, 1.0, 0.0), axis=0, keepdims=True)
        be_ref[...] = e_of_b.astype(I32)

    gt = gt_ref[...]
    sel = gt > 0.0
    selb = jnp.where(sel, 1.0, 0.0).astype(BF16)
    rank = _dot(selb, upper_ref[...])
    dest = off_ref[:, :1] + rank
    off_ref[...] = off_ref[...] + cnt_ref[i]
    kth = _dot(lowe, selb)
    dests, gws = [], []
    for k in range(MOE_TOPK):
        m = sel & (kth == float(k))
        have = jnp.sum(jnp.where(m, 1.0, 0.0), axis=0, keepdims=True)
        d = jnp.sum(jnp.where(m, dest, 0.0), axis=0, keepdims=True)
        dests.append(jnp.where(have > 0.0, d, float((n_blocks - 1) * MOE_BLOCK)))
        gws.append(jnp.sum(jnp.where(m, gt, 0.0), axis=0, keepdims=True))
    dest_ref[...] = jnp.concatenate(dests, axis=0).astype(I32)
    gpad = jnp.concatenate(gws + [jnp.zeros((LANES - MOE_TOPK, tm), F32)], axis=0)
    gw_ref[...] = gpad.T


def _plan_call(gate_t, cnt, n_blocks):
    E, T = gate_t.shape
    nt = cnt.shape[0]
    tm = T // nt
    idx = jnp.arange(tm)
    upper = (idx[:, None] < idx[None, :]).astype(BF16)
    ei = jnp.arange(E)
    lowe = (ei[None, :] < ei[:, None]).astype(BF16)
    kern = functools.partial(_plan_kernel, tm=tm, n_blocks=n_blocks)
    const = lambda shape: pl.BlockSpec(shape, lambda i: (0,) * len(shape))
    return pl.pallas_call(
        kern,
        grid=(nt,),
        in_specs=[pl.BlockSpec((E, tm), lambda i: (0, i)), const((nt, E, LANES)), const((tm, tm)), const((E, E))],
        out_specs=[pl.BlockSpec((MOE_TOPK, tm), lambda i: (0, i)), pl.BlockSpec((tm, LANES), lambda i: (i, 0)),
                   const((1, n_blocks))],
        out_shape=[jax.ShapeDtypeStruct((MOE_TOPK, T), I32), jax.ShapeDtypeStruct((T, LANES), F32),
                   jax.ShapeDtypeStruct((1, n_blocks), I32)],
        scratch_shapes=[pltpu.VMEM((E, LANES), F32)],
        compiler_params=_cparams(("arbitrary",)),
        name="plan",
    )(gate_t, cnt, upper, lowe)


def _sc_index_layout(dest_t):
    K, T = dest_t.shape
    n_ch = T // (SC_WORKERS * SC_ROWS)
    return dest_t.reshape(K, SC_WORKERS, n_ch, SC_ROWS).transpose(1, 2, 0, 3).reshape(SC_WORKERS, n_ch * K, SC_ROWS)


def _sc_dispatch(rows, idx, n_slots):
    T, W = rows.shape
    n_ch = T // (SC_WORKERS * SC_ROWS)
    tpw = T // SC_WORKERS
    mesh = plsc.VectorSubcoreMesh(core_axis_name="c", subcore_axis_name="s")

    @functools.partial(
        pl.kernel, mesh=mesh,
        out_type=jax.ShapeDtypeStruct((n_slots, W), I32),
        scratch_types=[pltpu.VMEM((n_ch * MOE_TOPK, SC_ROWS), I32), pltpu.VMEM((SC_ROWS, W), I32),
                       pltpu.SemaphoreType.DMA])
    def kern(x_hbm, idx_hbm, o_hbm, idx_v, rows_v, sem):
        wid = lax.axis_index("s") * SC_CORES + lax.axis_index("c")
        pltpu.sync_copy(idx_hbm.at[wid], idx_v)

        @pl.loop(0, n_ch)
        def _(j):
            pltpu.sync_copy(x_hbm.at[pl.ds(wid * tpw + j * SC_ROWS, SC_ROWS)], rows_v)
            copies = [pltpu.async_copy(rows_v, o_hbm.at[idx_v.at[j * MOE_TOPK + k]], sem)
                      for k in range(MOE_TOPK)]
            for cp in copies:
                cp.wait()

    return kern(rows, idx)


def _sc_combine(slots, idx, T):
    _, W = slots.shape
    n_ch = T // (SC_WORKERS * SC_ROWS)
    tpw = T // SC_WORKERS
    mesh = plsc.VectorSubcoreMesh(core_axis_name="c", subcore_axis_name="s")

    @functools.partial(
        pl.kernel, mesh=mesh,
        out_type=jax.ShapeDtypeStruct((MOE_TOPK, T, W), I32),
        scratch_types=[pltpu.VMEM((n_ch * MOE_TOPK, SC_ROWS), I32), pltpu.VMEM((SC_ROWS, W), I32),
                       pltpu.SemaphoreType.DMA])
    def kern(s_hbm, idx_hbm, o_hbm, idx_v, rows_v, sem):
        wid = lax.axis_index("s") * SC_CORES + lax.axis_index("c")
        pltpu.sync_copy(idx_hbm.at[wid], idx_v)

        @pl.loop(0, n_ch)
        def _(j):
            for k in range(MOE_TOPK):
                pltpu.async_copy(s_hbm.at[idx_v.at[j * MOE_TOPK + k]], rows_v, sem).wait()
                pltpu.sync_copy(rows_v, o_hbm.at[k, pl.ds(wid * tpw + j * SC_ROWS, SC_ROWS)])

    return kern(slots, idx)


def _ffn_kernel(be_ref, x_ref, eg_ref, eu_ref, ed_ref, o_ref):
    used = be_ref[pl.program_id(0)] < N_EXPERTS

    @pl.when(used)
    def _():
        x = _unpack_bf16_pairs(x_ref[...])
        a = _dot(x, eg_ref[0])
        u = _dot(x, eu_ref[0])
        o_ref[...] = _pack_bf16_pairs(_dot((a * _sigmoid(a) * u).astype(BF16), ed_ref[0]))

    @pl.when(jnp.logical_not(used))
    def _():
        o_ref[...] = jnp.zeros(o_ref.shape, I32)


def _ffn_call(xs, block_e, eg, eu, ed, n_blocks):
    P, W = xs.shape
    D, FF = 2 * W, EXPERT_FF
    grid_spec = pltpu.PrefetchScalarGridSpec(
        num_scalar_prefetch=1,
        grid=(n_blocks,),
        in_specs=[pl.BlockSpec((MOE_BLOCK, W), lambda b, be: (b, 0)),
                  pl.BlockSpec((1, D, FF), lambda b, be: (jnp.minimum(be[b], N_EXPERTS - 1), 0, 0)),
                  pl.BlockSpec((1, D, FF), lambda b, be: (jnp.minimum(be[b], N_EXPERTS - 1), 0, 0)),
                  pl.BlockSpec((1, FF, D), lambda b, be: (jnp.minimum(be[b], N_EXPERTS - 1), 0, 0))],
        out_specs=pl.BlockSpec((MOE_BLOCK, W), lambda b, be: (b, 0)))
    return pl.pallas_call(
        _ffn_kernel,
        grid_spec=grid_spec,
        out_shape=jax.ShapeDtypeStruct((P, W), I32),
        compiler_params=_cparams(("parallel",)),
        name="ffn",
    )(block_e, xs, eg, eu, ed)


def _final_kernel(h_ref, c_ref, gw_ref, x1_ref, g2_ref, fg_ref, sg_ref, su_ref, sd_ref, o_ref):
    h = _unpack_bf16_pairs(h_ref[...])
    a = _dot(h, sg_ref[...])
    u = _dot(h, su_ref[...])
    moe = _dot((a * _sigmoid(a) * u).astype(BF16), sd_ref[...])
    gw = gw_ref[...]
    for k in range(MOE_TOPK):
        w = gw[:, k:k + 1]
        y = _unpack_bf16_pairs(c_ref[k]).astype(F32)
        moe = moe + jnp.where(w > 0.0, w * y, 0.0)
    x2 = x1_ref[...] + g2_ref[0] * moe
    ms = jnp.mean(x2 * x2, axis=-1, keepdims=True)
    o_ref[...] = x2 * lax.rsqrt(ms + RMS_EPS) * fg_ref[...]


def _final_call(h2p, comb, gw, x1, g2, final_g, sg, su, sd, S):
    T, W = h2p.shape
    D, FF = 2 * W, EXPERT_FF
    B = T // S
    tm = min(512, S)
    tpb = S // tm
    row = lambda i: (i, 0)
    const = lambda shape: pl.BlockSpec(shape, lambda i: (0,) * len(shape))
    return pl.pallas_call(
        _final_kernel,
        grid=(T // tm,),
        in_specs=[pl.BlockSpec((tm, W), row), pl.BlockSpec((MOE_TOPK, tm, W), lambda i: (0, i, 0)),
                  pl.BlockSpec((tm, LANES), row), pl.BlockSpec((tm, D), row),
                  pl.BlockSpec((1, 1, D), lambda i: (i // tpb, 0, 0)), const((1, D)),
                  const((D, FF)), const((D, FF)), const((FF, D))],
        out_specs=pl.BlockSpec((tm, D), row),
        out_shape=jax.ShapeDtypeStruct((T, D), F32),
        compiler_params=_cparams(("parallel",)),
        name="final",
    )(h2p, comb, gw, x1, g2.reshape(B, 1, D), final_g.reshape(1, D), sg, su, sd)


def _moe_call(h2p, gate_t, cnt, x1, g2, final_g, eg, eu, ed, sg, su, sd, S):
    T = h2p.shape[0]
    n_blocks = (T * MOE_TOPK) // MOE_BLOCK + N_EXPERTS + 1
    dest_t, gw, block_e = _plan_call(gate_t, cnt, n_blocks)
    idx = _sc_index_layout(dest_t)
    xs = _sc_dispatch(h2p, idx, n_blocks * MOE_BLOCK)
    ys = _ffn_call(xs, block_e.reshape(n_blocks), eg, eu, ed, n_blocks)
    comb = _sc_combine(ys, idx, T)
    return _final_call(h2p, comb, gw, x1, g2, final_g, sg, su, sd, S)


def _layer(x2, c, S, ada_w, ada_b, norm1_g, w_in, rel_bias, tshift_mu, decay_w0, decay_up, iclr_a0, iclr_up,
           gate_up, k_k, k_a, r_k, lnx_g, lnx_b, w_attn_br, w_rwkv_br, w_out, norm2_g, router_w, router_bias,
           exp_gate, exp_up, exp_down, sh_gate, sh_up, sh_down, final_g):
    T, D = x2.shape
    B = T // S
    mod = _mod_call(c, ada_w, ada_b)
    sh1, sc1, g1, sh2, sc2, g2 = jnp.split(mod, 6, axis=-1)

    q, k, vt, iq, ik4, iwt, zr, ga, gr = _inproj_call(x2, norm1_g, sc1, sh1, w_in, S)

    ta = min(512, S)
    assert ta >= LANES and S % ta == 0
    top_k = min(TOPK_MAX, S // 4)
    seq = lambda a: a.reshape(B, S, a.shape[-1])
    mask = _index_call(seq(iq), iwt, seq(ik4), ta, top_k)
    bias_tiles = _bias_call(rel_bias, ta)
    attn = _attn_call(seq(q), seq(k), vt, mask, bias_tiles, rel_bias, ta).reshape(T, ATTN_W)

    rw = _rwkv_call(zr.reshape(B, S, RWKV_IN), tshift_mu, decay_w0, decay_up, iclr_a0, iclr_up, gate_up,
                    k_k, k_a, r_k, lnx_g, lnx_b).reshape(T, RWKV_W)

    x1, h2p, gate_t, cnt = _merge_call(x2, attn, rw, ga, gr, w_attn_br.astype(BF16), w_rwkv_br.astype(BF16),
                                       w_out.astype(BF16), g1, norm2_g, sc2, sh2, router_w, router_bias, S)
    return _moe_call(h2p, gate_t, cnt, x1, g2, final_g, exp_gate.astype(BF16), exp_up.astype(BF16),
                     exp_down.astype(BF16), sh_gate.astype(BF16), sh_up.astype(BF16), sh_down.astype(BF16), S)


def kernel(x, c, ada_w, ada_b, norm1_g, w_in, rel_bias, tshift_mu, decay_w0, decay_up, iclr_a0, iclr_up, gate_up, k_k, k_a, r_k, lnx_g, lnx_b, w_attn_br, w_rwkv_br, w_out, norm2_g, router_w, router_bias, exp_gate, exp_up, exp_down, sh_gate, sh_up, sh_down, final_g):
    B, S, D = x.shape
    depth = ada_w.shape[0]
    assert depth == 1, "the final RMSNorm is fused into the (single) layer's MoE kernel"
    out = _layer(x.reshape(B * S, D), c, S, ada_w[0], ada_b[0], norm1_g[0], w_in[0], rel_bias, tshift_mu[0],
                 decay_w0[0], decay_up[0], iclr_a0[0], iclr_up[0], gate_up[0], k_k[0], k_a[0], r_k[0],
                 lnx_g[0], lnx_b[0], w_attn_br[0], w_rwkv_br[0], w_out[0], norm2_g[0], router_w[0],
                 router_bias[0], exp_gate[0], exp_up[0], exp_down[0], sh_gate[0], sh_up[0], sh_down[0], final_g)
    return out.reshape(B, S, D)
```

```python
import functools
import math

import jax
import jax.numpy as jnp
from jax import lax
from jax.experimental import pallas as pl
from jax.experimental.pallas import tpu as pltpu
from jax.experimental.pallas import tpu_sc as plsc

F32 = jnp.float32
BF16 = jnp.bfloat16
I32 = jnp.int32
I16 = jnp.int16

RMS_EPS = 1e-6
D_MODEL = 1024
N_ATTN_HEADS = 8
ATTN_HEAD_DIM = 64
ATTN_W = 512
IDX_HEADS = 16
IDX_DIM = 32
IDX_Q = 512
TOPK_MAX = 256
N_BUCKETS = 32
MAX_DISTANCE = 128
RWKV_HEADS = 8
RWKV_HEAD = 64
RWKV_W = 512
DECAY_LORA = 64
ICLR_LORA = 64
GATE_LORA = 128
RWKV_IN = 1792
GN_EPS = 64e-5
N_EXPERTS = 64
N_GROUPS = 8
TOPK_GROUPS = 4
MOE_TOPK = 8
EXPERT_FF = 256
ROUTED_SCALE = 2.5

LANES = 128
VMEM_LIMIT = 56 * 1024 * 1024
CHUNK = 64
LOG2E = 1.4426950408889634
INT_MIN = -2147483648
KEY_NEG_INF = -2139095041

NT_DIMS = (((1,), (1,)), ((), ()))


def _cparams(sem):
    return pltpu.CompilerParams(dimension_semantics=sem, vmem_limit_bytes=VMEM_LIMIT)


def _dot(a, b):
    return jnp.dot(a, b, preferred_element_type=F32)


def _dot_nt(a, b):
    return lax.dot_general(a, b, NT_DIMS, preferred_element_type=F32)


def _split2(x):
    hi = x.astype(BF16)
    lo = (x - hi.astype(F32)).astype(BF16)
    return hi, lo


def _split3(x):
    hi = x.astype(BF16)
    r1 = x - hi.astype(F32)
    mid = r1.astype(BF16)
    lo = (r1 - mid.astype(F32)).astype(BF16)
    return hi, mid, lo


def _dot_exact_rhs(x, ones_bf16, terms=2):
    parts = _split3(x) if terms == 3 else _split2(x)
    out = _dot(parts[0], ones_bf16)
    for p in parts[1:]:
        out = out + _dot(p, ones_bf16)
    return out


def _dot3(a, b, nt=False):
    ah, al = _split2(a)
    bh, bl = _split2(b)
    f = _dot_nt if nt else _dot
    return f(ah, bh) + f(ah, bl) + f(al, bh)


def _sigmoid(x):
    return 1.0 / (1.0 + jnp.exp(-x))


def _mod_kernel(c_ref, w_ref, b_ref, o_ref):
    c = c_ref[...]
    s = c * _sigmoid(c)
    o_ref[...] = _dot3(s, w_ref[...]) + b_ref[...]


def _mod_call(c, ada_w, ada_b):
    B, D = c.shape
    N = ada_w.shape[1]
    tn = 1024
    return pl.pallas_call(
        _mod_kernel,
        grid=(N // tn,),
        in_specs=[pl.BlockSpec((B, D), lambda j: (0, 0)),
                  pl.BlockSpec((D, tn), lambda j: (0, j)),
                  pl.BlockSpec((1, tn), lambda j: (0, j))],
        out_specs=pl.BlockSpec((B, tn), lambda j: (0, j)),
        out_shape=jax.ShapeDtypeStruct((B, N), F32),
        compiler_params=_cparams(("arbitrary",)),
        name="mod",
    )(c, ada_w, ada_b.reshape(1, N))


_OFF_Q, _OFF_K, _OFF_IQ, _OFF_IK4, _OFF_ZR, _OFF_GA, _OFF_GR, _N_PACK = (
    0, 512, 1024, 1536, 2048, 3840, 4864, 5888)
IDX_PER_BLOCK = LANES // IDX_DIM


def _pack_w_in(w_in):
    D = w_in.shape[0]
    w_ik = w_in[:, 2048:2080]
    ik4 = jnp.zeros((D, IDX_PER_BLOCK * LANES), w_in.dtype)
    for j in range(IDX_PER_BLOCK):
        ik4 = lax.dynamic_update_slice(ik4, w_ik, (0, j * LANES + j * IDX_DIM))
    w_pack = jnp.concatenate([w_in[:, 0:1024], w_in[:, 1536:2048], ik4, w_in[:, 2096:]], axis=1).astype(BF16)
    return w_pack, w_in[:, 1024:1536].T.astype(BF16), w_in[:, 2080:2096].T.astype(BF16)


def _inproj_kernel(x_ref, g_ref, sc_ref, sh_ref, w_ref, wvt_ref, wiwt_ref,
                   q_ref, k_ref, vt_ref, iq_ref, ik4_ref, iwt_ref, zr_ref, ga_ref, gr_ref):
    x = x_ref[...]
    ms = jnp.mean(x * x, axis=-1, keepdims=True)
    h = x * lax.rsqrt(ms + RMS_EPS) * g_ref[...]
    h = h * (1.0 + sc_ref[0]) + sh_ref[0]
    hb = h.astype(BF16)

    def proj(lo, hi):
        return _dot(hb, w_ref[:, lo:hi])

    q_ref[...] = (proj(_OFF_Q, _OFF_K) * (ATTN_HEAD_DIM ** -0.5 * LOG2E)).astype(BF16)
    k_ref[...] = proj(_OFF_K, _OFF_IQ).astype(BF16)
    iq_ref[...] = proj(_OFF_IQ, _OFF_IK4).astype(BF16)
    ik4_ref[...] = proj(_OFF_IK4, _OFF_ZR).astype(BF16)
    zr_ref[...] = proj(_OFF_ZR, _OFF_GA)
    ga_ref[...] = proj(_OFF_GA, _OFF_GR).astype(BF16)
    gr_ref[...] = proj(_OFF_GR, _N_PACK).astype(BF16)
    vt_ref[0] = _dot_nt(wvt_ref[...], hb).astype(BF16)
    iwt_ref[0] = _dot_nt(wiwt_ref[...], hb)


def _inproj_call(x2, norm_g, sc, sh, w_in, S):
    T, D = x2.shape
    B = T // S
    tm = min(512, S)
    tpb = S // tm
    w_pack, wvt, wiwt = _pack_w_in(w_in)
    row = lambda i: (i, 0)
    per_b = lambda i: (i // tpb, 0, 0)
    colblk = lambda i: (i // tpb, 0, i % tpb)
    const = lambda shape: pl.BlockSpec(shape, lambda i: (0,) * len(shape), pipeline_mode=pl.Buffered(1))
    rows_out = ((512, BF16), (512, BF16), (512, BF16), (512, BF16), (RWKV_IN, F32), (D, BF16), (D, BF16))
    out_specs = [pl.BlockSpec((tm, w), row) for w, _ in rows_out]
    out_shape = [jax.ShapeDtypeStruct((T, w), dt) for w, dt in rows_out]
    out_specs[2:2] = [pl.BlockSpec((1, ATTN_W, tm), colblk)]
    out_shape[2:2] = [jax.ShapeDtypeStruct((B, ATTN_W, S), BF16)]
    out_specs[5:5] = [pl.BlockSpec((1, IDX_HEADS, tm), colblk)]
    out_shape[5:5] = [jax.ShapeDtypeStruct((B, IDX_HEADS, S), F32)]
    return pl.pallas_call(
        _inproj_kernel,
        grid=(T // tm,),
        in_specs=[pl.BlockSpec((tm, D), row),
                  pl.BlockSpec((1, D), lambda i: (0, 0)),
                  pl.BlockSpec((1, 1, D), per_b),
                  pl.BlockSpec((1, 1, D), per_b),
                  const((D, _N_PACK)), const((ATTN_W, D)), const((IDX_HEADS, D))],
        out_specs=out_specs,
        out_shape=out_shape,
        compiler_params=_cparams(("parallel",)),
        name="inproj",
    )(x2, norm_g.reshape(1, D), sc.reshape(B, 1, D), sh.reshape(B, 1, D), w_pack, wvt, wiwt)


def _t5_bucket(rel):
    n = jnp.maximum(rel, 0)
    max_exact = N_BUCKETS // 2
    nf = jnp.maximum(n, 1).astype(F32)
    large = max_exact + (jnp.log(nf / max_exact) / math.log(MAX_DISTANCE / max_exact)
                         * (N_BUCKETS - max_exact)).astype(I32)
    large = jnp.minimum(large, N_BUCKETS - 1)
    return jnp.where(n < max_exact, n, large)


def _bias_kernel(bucket_ref, rb_ref, o_ref):
    h = pl.program_id(0)
    bk = bucket_ref[...]
    out = jnp.zeros(bk.shape, F32)
    for b in range(N_BUCKETS):
        out = jnp.where(bk == b, rb_ref[b, h] * LOG2E, out)
    o_ref[0] = out


def _bias_call(rel_bias, tq):
    r = jnp.arange(tq, dtype=I32)[None, :]
    c = jnp.arange(tq, dtype=I32)[:, None]
    buckets = jnp.stack([_t5_bucket(r - c), _t5_bucket(tq + r - c)])
    return pl.pallas_call(
        _bias_kernel,
        grid=(N_ATTN_HEADS,),
        in_specs=[pl.BlockSpec((2, tq, tq), lambda h: (0, 0, 0)),
                  pl.BlockSpec(memory_space=pltpu.SMEM)],
        out_specs=pl.BlockSpec((1, 2, tq, tq), lambda h: (h, 0, 0, 0)),
        out_shape=jax.ShapeDtypeStruct((N_ATTN_HEADS, 2, tq, tq), F32),
        compiler_params=_cparams(("arbitrary",)),
        name="bias",
    )(buckets, rel_bias)


def _index_kernel(iq_ref, iwt_ref, ik4_ref, lower_ref, mask_ref, key_ref, k16_ref, *, t, nk, top_k, scale):
    qi = pl.program_id(1)
    nkt = qi + 1
    ksub = LANES
    qpos = qi * t + lax.broadcasted_iota(I32, (ksub, t), 1)

    def score_tile(kt, carry):
        kbase = pl.multiple_of(kt * t, t)
        for ks in range(t // ksub):
            acc = jnp.zeros((ksub, t), F32)
            ik_rows = ik4_ref[0, pl.ds(kbase + ks * ksub, ksub), :]
            ik_stack = jnp.concatenate([ik_rows[:, j * LANES:(j + 1) * LANES] for j in range(IDX_PER_BLOCK)],
                                       axis=0)
            for g in range(IDX_HEADS // IDX_PER_BLOCK):
                d4 = _dot_nt(ik_stack, iq_ref[0, :, g * LANES:(g + 1) * LANES])
                for j in range(IDX_PER_BLOCK):
                    h = g * IDX_PER_BLOCK + j
                    acc = acc + jnp.maximum(d4[j * ksub:(j + 1) * ksub], 0.0) * iwt_ref[0, h:h + 1, :]
            s = acc * scale
            kpos = kt * t + ks * ksub + lax.broadcasted_iota(I32, (ksub, t), 0)
            s = jnp.where(kpos <= qpos, s, -jnp.inf)
            bits = pltpu.bitcast(s, I32)
            key = bits ^ ((bits >> 31) & 0x7FFFFFFF)
            key_ref[kt, ks * ksub:(ks + 1) * ksub, :] = key
            k16_ref[kt, ks * ksub:(ks + 1) * ksub, :] = (key >> 16).astype(I16)
        return carry

    lax.fori_loop(0, nkt, score_tile, 0)

    pack = 16

    def search16():
        def bit_body(i, ans):
            cand = ans | lax.shift_left(jnp.int32(1), 15 - i)
            cand16 = (cand - 32768).astype(I16)

            def cnt_body(kt, acc):
                one = jnp.where(k16_ref[kt] >= cand16, jnp.int16(1), jnp.int16(0))
                for r in range(t // pack):
                    acc = acc + one[r * pack:(r + 1) * pack, :]
                return acc

            acc = lax.fori_loop(0, nkt, cnt_body, jnp.zeros((pack, t), I16))
            cnt = jnp.sum(acc.astype(I32), axis=0, keepdims=True)
            return jnp.where(cnt >= top_k, cand, ans)

        return lax.fori_loop(0, 16, bit_body, jnp.zeros((1, t), I32))

    hi = search16() - 32768

    def remap_body(kt, carry):
        key = key_ref[kt]
        khi = key >> 16
        lo = (key & 0xFFFF) - 32768
        k16_ref[kt] = jnp.where(khi > hi, 32767, jnp.where(khi == hi, lo, -32768)).astype(I16)
        return carry

    lax.fori_loop(0, nkt, remap_body, 0)
    thr = hi * 65536 + search16()

    def count(pred):
        def body(kt, acc):
            one = jnp.where(pred(key_ref[kt]), 1.0, 0.0)
            return acc + jnp.sum(one.reshape(t // 8, 8, t), axis=0)
        return jnp.sum(lax.fori_loop(0, nkt, body, jnp.zeros((8, t), F32)), axis=0, keepdims=True)

    n_ge = count(lambda keys: keys >= thr)
    has_tie = jnp.max(n_ge) > float(top_k)

    @pl.when(jnp.logical_not(has_tie))
    def _():
        def mask_body(kt, carry):
            keys = key_ref[kt]
            sel = (keys >= thr) & (keys > KEY_NEG_INF)
            mask_ref[0, 0, kt] = jnp.where(sel, 0.0, -jnp.inf).astype(BF16)
            return carry

        lax.fori_loop(0, nkt, mask_body, 0)

    @pl.when(has_tie)
    def _():
        need = float(top_k) - count(lambda keys: keys > thr)

        def mask_body(kt, seen):
            keys = key_ref[kt]
            tie = (keys == thr) & (keys > KEY_NEG_INF)
            tie_b = jnp.where(tie, 1.0, 0.0).astype(BF16)
            before = seen + _dot(lower_ref[...], tie_b)
            sel = (keys > thr) | (tie & (before < need))
            mask_ref[0, 0, kt] = jnp.where(sel, 0.0, -jnp.inf).astype(BF16)
            return seen + jnp.sum(tie_b.astype(F32).reshape(t // 8, 8, t).sum(axis=0), axis=0, keepdims=True)

        lax.fori_loop(0, nkt, mask_body, jnp.zeros((1, t), F32))

    def fill_body(kt, carry):
        mask_ref[0, 0, kt] = jnp.full((t, t), -jnp.inf, BF16)
        return carry

    lax.fori_loop(nkt, nk, fill_body, 0)


def _index_call(iq, iwt, ik4, t, top_k):
    B, S, _ = iq.shape
    n = S // t
    scale = (IDX_HEADS ** -0.5) * (IDX_DIM ** -0.5)
    kern = functools.partial(_index_kernel, t=t, nk=n, top_k=top_k, scale=scale)
    pos = jnp.arange(t)
    lower = (pos[None, :] < pos[:, None]).astype(BF16)
    return pl.pallas_call(
        kern,
        grid=(B, n),
        in_specs=[pl.BlockSpec((1, t, IDX_Q), lambda b, i: (b, i, 0)),
                  pl.BlockSpec((1, IDX_HEADS, t), lambda b, i: (b, 0, i)),
                  pl.BlockSpec((1, S, IDX_PER_BLOCK * LANES), lambda b, i: (b, 0, 0)),
                  pl.BlockSpec((t, t), lambda b, i: (0, 0))],
        out_specs=pl.BlockSpec((1, 1, n, t, t), lambda b, i: (b, i, 0, 0, 0)),
        out_shape=jax.ShapeDtypeStruct((B, n, n, t, t), BF16),
        scratch_shapes=[pltpu.VMEM((n, t, t), I32), pltpu.VMEM((n, t, t), I16)],
        compiler_params=_cparams(("parallel", "arbitrary")),
        name="index",
    )(iq, iwt, ik4, lower)


ONES_ROWS = 16


def _attn_kernel(qi_tab, kt_tab, q_ref, k_ref, vt_ref, mask_ref, bias_ref, rb_ref, o_ref,
                 qz_ref, m_ref, acc_ref, s_ref, *, t):
    s_id = pl.program_id(1)
    qi = qi_tab[s_id]
    kt = kt_tab[s_id]
    dh = ATTN_HEAD_DIM

    @pl.when(kt == 0)
    def _():
        m_ref[...] = jnp.full(m_ref.shape, -jnp.inf, F32)
        acc_ref[...] = jnp.zeros(acc_ref.shape, F32)
        lane = lax.broadcasted_iota(I32, (t, LANES), 1)
        for h in range(N_ATTN_HEADS):
            blk = q_ref[0, :, (h // 2) * LANES:(h // 2 + 1) * LANES]
            keep = (lane < dh) if h % 2 == 0 else (lane >= dh)
            qz_ref[h] = jnp.where(keep, blk, jnp.zeros_like(blk))

    def step(bias_tile, bias_const):
        maskf = mask_ref[0, 0, 0].astype(F32)
        ones = jnp.ones((ONES_ROWS, t), BF16)

        for h in range(N_ATTN_HEADS):
            k_blk = k_ref[0, :, (h // 2) * LANES:(h // 2 + 1) * LANES]
            s = _dot_nt(k_blk, qz_ref[h]) + maskf
            s_ref[h] = s if bias_tile is None else s + bias_tile(h)
        for h in range(N_ATTN_HEADS):
            s = s_ref[h]
            c = bias_const(h)
            m_old = m_ref[h:h + 1, :]
            m_cur = jnp.max(jnp.max(s.reshape(t // 8, 8, t), axis=0), axis=0, keepdims=True) + c
            m_new = jnp.maximum(m_old, m_cur)
            m_safe = jnp.where(m_new == -jnp.inf, 0.0, m_new)
            alpha = jnp.exp2(m_old - m_safe)
            p = jnp.exp2(s - (m_safe - c)).astype(BF16)
            v_aug = jnp.concatenate([vt_ref[0, h * dh:(h + 1) * dh, :], ones], axis=0)
            acc_ref[h] = alpha * acc_ref[h] + _dot(v_aug, p)
            m_ref[h:h + 1, :] = m_new

    @pl.when(kt == qi)
    def _():
        step(lambda h: bias_ref[h, 0], lambda h: 0.0)

    @pl.when(kt == qi - 1)
    def _():
        step(lambda h: bias_ref[h, 1], lambda h: 0.0)

    @pl.when(kt < qi - 1)
    def _():
        step(None, lambda h: rb_ref[N_BUCKETS - 1, h] * LOG2E)

    @pl.when(kt == qi)
    def _():
        outs = []
        for h in range(N_ATTN_HEADS):
            a = acc_ref[h]
            outs.append(a[:dh, :] / a[dh:dh + 1, :])
        o_ref[0] = jnp.concatenate(outs, axis=0).T.astype(BF16)


def _attn_call(q, k, vt, mask, bias_tiles, rel_bias, t):
    B, S, W = q.shape
    n = S // t
    H = N_ATTN_HEADS
    qi_tab = jnp.asarray([i for i in range(n) for _ in range(i + 1)], I32)
    kt_tab = jnp.asarray([j for i in range(n) for j in range(i + 1)], I32)
    kern = functools.partial(_attn_kernel, t=t)
    grid_spec = pltpu.PrefetchScalarGridSpec(
        num_scalar_prefetch=2,
        grid=(B, int(qi_tab.shape[0])),
        in_specs=[pl.BlockSpec((1, t, W), lambda b, s, qt, kt: (b, qt[s], 0)),
                  pl.BlockSpec((1, t, W), lambda b, s, qt, kt: (b, kt[s], 0)),
                  pl.BlockSpec((1, W, t), lambda b, s, qt, kt: (b, 0, kt[s])),
                  pl.BlockSpec((1, 1, 1, t, t), lambda b, s, qt, kt: (b, qt[s], kt[s], 0, 0)),
                  pl.BlockSpec((H, 2, t, t), lambda b, s, qt, kt: (0, 0, 0, 0), pipeline_mode=pl.Buffered(1)),
                  pl.BlockSpec(memory_space=pltpu.SMEM)],
        out_specs=pl.BlockSpec((1, t, W), lambda b, s, qt, kt: (b, qt[s], 0)),
        scratch_shapes=[pltpu.VMEM((H, t, LANES), BF16),
                        pltpu.VMEM((H, t), F32),
                        pltpu.VMEM((H, ATTN_HEAD_DIM + ONES_ROWS, t), F32),
                        pltpu.VMEM((H, t, t), F32)])
    return pl.pallas_call(
        kern,
        grid_spec=grid_spec,
        out_shape=jax.ShapeDtypeStruct((B, S, W), BF16),
        compiler_params=_cparams(("parallel", "arbitrary")),
        name="attn",
    )(qi_tab, kt_tab, q, k, vt, mask, bias_tiles, rel_bias)


def _blockdiag_rows(x):
    lane = lax.broadcasted_iota(I32, x.shape, 1)
    zero = jnp.zeros_like(x)
    return jnp.concatenate([jnp.where(lane < RWKV_HEAD, x, zero),
                            jnp.where(lane >= RWKV_HEAD, x, zero)], axis=0)


def _rwkv_kernel(z_ref, mu_ref, w0_ref, dup_ref, a0_ref, iup_ref, gup_ref, kk_ref, ka_ref, rk_ref,
                 lng_ref, lnb_ref, seg_ref, tri_ref, o_ref,
                 prev_ref, st_ref, at_ref, rt_ref, bt_ref, kt_ref, bh_ref, kh_ref, v_ref, pc_ref, y_ref,
                 la_ref, lb_ref, mak_ref, arb_ref, ark_ref, wa_ref, wb_ref, g1_ref, g2_ref, h1_ref, h2_ref,
                 *, tt):
    j = pl.program_id(1)
    W = RWKV_W
    C = CHUNK
    nchunk = tt // C
    npair = RWKV_HEADS // 2

    @pl.when(j == 0)
    def _():
        prev_ref[...] = jnp.zeros(prev_ref.shape, F32)
        st_ref[...] = jnp.zeros(st_ref.shape, F32)

    z = z_ref[0]
    row = lax.broadcasted_iota(I32, z.shape, 0)
    z_prev = jnp.where(row == 0, prev_ref[...], pltpu.roll(z, 1, axis=0))
    prev_ref[...] = z[tt - 1:tt, :]
    z = z + mu_ref[...] * (z_prev - z)

    r = z[:, 0:W]
    k = z[:, W:2 * W]
    v = z[:, 2 * W:3 * W]
    wdad = z[:, 3 * W:3 * W + 2 * DECAY_LORA]
    gd = z[:, 3 * W + 2 * DECAY_LORA:]

    w_pre = w0_ref[...] + _dot3(jnp.tanh(wdad), dup_ref[...])
    neg = -w_pre
    softplus = jnp.maximum(neg, 0.0) + jnp.log(1.0 + jnp.exp(-jnp.abs(neg)))
    lw = -jnp.exp(-softplus - 0.5)
    a_lr = _sigmoid(a0_ref[...] + _dot(wdad.astype(BF16), iup_ref[...]))
    g = _dot(_sigmoid(gd).astype(BF16), gup_ref[...])

    seg = seg_ref[...]

    def head_sum(parts):
        half = seg.shape[0]
        cols = [sum(_dot(p[:, lo:lo + half], seg) for p in parts) for lo in range(0, W, half)]
        return jnp.concatenate(cols, axis=1)

    kk = k * kk_ref[...]
    kk = kk / jnp.maximum(jnp.sqrt(head_sum(_split2(kk * kk))), 1e-12)
    k2 = k * (1.0 + (a_lr - 1.0) * ka_ref[...])
    a_vec = -kk
    b_vec = kk * a_lr

    cum = _dot_exact_rhs_lhs(tri_ref[...], lw)
    tot = jnp.concatenate([jnp.broadcast_to(cum[(c + 1) * C - 1:(c + 1) * C, :], (C, W)) for c in range(nchunk)],
                          axis=0)
    p_inv = jnp.exp(-cum)
    p_out = jnp.exp(tot - cum)
    at_ref[...] = a_vec * jnp.exp(cum - lw)
    rt_ref[...] = r * jnp.exp(cum)
    bt_ref[...] = (b_vec * p_inv).astype(BF16)
    kt_ref[...] = (k2 * p_inv).astype(BF16)
    bh_ref[...] = b_vec * p_out
    kh_ref[...] = k2 * p_out
    v_ref[...] = v
    pc_ref[...] = jnp.exp(tot)

    t_i = lax.broadcasted_iota(I32, (C, LANES), 0)
    s_i = lax.broadcasted_iota(I32, (C, LANES), 1) % C
    strict = s_i < t_i
    incl = s_i <= t_i
    r_i = lax.broadcasted_iota(I32, (LANES, LANES), 0)
    c_i = lax.broadcasted_iota(I32, (LANES, LANES), 1)
    same_head = (r_i < RWKV_HEAD) == (c_i < RWKV_HEAD)
    diag = r_i == c_i
    nstage = int(math.log2(C))
    zero = jnp.zeros((C, LANES), F32)
    zsq = jnp.zeros((LANES, LANES), F32)
    units = [(c, p) for c in range(nchunk) for p in range(npair)]

    def sl(c, p):
        return slice(c * C, (c + 1) * C), slice(p * LANES, (p + 1) * LANES)

    def bd2(w):
        wb = w.astype(BF16)
        return jnp.concatenate([_blockdiag_rows(wb[:, :LANES]), _blockdiag_rows(wb[:, LANES:])], axis=1)

    for i, (c, p) in enumerate(units):
        rows, cols = sl(c, p)
        lhs = jnp.concatenate([at_ref[rows, cols], rt_ref[rows, cols]], axis=0).astype(BF16)
        rhs = jnp.concatenate([_blockdiag_rows(bt_ref[rows, cols]),
                               _blockdiag_rows(kt_ref[rows, cols])], axis=0)
        prod = _dot_nt(lhs, rhs)
        la_ref[i] = jnp.where(strict, prod[:C, :LANES], zero).astype(BF16)
        mak_ref[i] = jnp.where(strict, prod[:C, LANES:], zero).astype(BF16)
        arb_ref[i] = jnp.where(incl, prod[C:, :LANES], zero).astype(BF16)
        ark_ref[i] = jnp.where(incl, prod[C:, LANES:], zero).astype(BF16)
    for i, (c, p) in enumerate(units):
        rows, cols = sl(c, p)
        w2 = _dot(mak_ref[i], _blockdiag_rows(v_ref[rows, cols].astype(BF16)))
        wa_ref[i] = jnp.concatenate([at_ref[rows, cols], w2], axis=1)
    l_bufs, w_bufs = (la_ref, lb_ref), (wa_ref, wb_ref)
    for s in range(nstage):
        l_in, l_out = l_bufs[s % 2], l_bufs[(s + 1) % 2]
        w_in, w_out = w_bufs[s % 2], w_bufs[(s + 1) % 2]
        for i in range(len(units)):
            lmat = l_in[i]
            w = w_in[i]
            w_out[i] = w + _dot(lmat, bd2(w))
            if s < nstage - 1:
                l_out[i] = _dot(lmat, _blockdiag_rows(lmat)).astype(BF16)
    w_fin = w_bufs[nstage % 2]
    for i, (c, p) in enumerate(units):
        rows, cols = sl(c, p)
        w = w_fin[i]
        wb = w.astype(BF16)
        vb = v_ref[rows, cols].astype(BF16)
        gg = _dot(arb_ref[i], bd2(w))
        g1_ref[i] = (rt_ref[rows, cols] + gg[:, :LANES]).astype(BF16)
        g2_ref[i] = gg[:, LANES:] + _dot(ark_ref[i], _blockdiag_rows(vb))
        bk_t = jnp.concatenate([bh_ref[rows, cols], kh_ref[rows, cols]], axis=0).T
        hrhs = jnp.concatenate([wb, jnp.concatenate([jnp.zeros((C, LANES), BF16), vb], axis=1)], axis=0)
        hh = _dot(bk_t.astype(BF16), hrhs)
        pc = pc_ref[c * C:c * C + 1, cols]
        h1 = jnp.where(same_head, hh[:, :LANES], zsq) + jnp.where(diag, jnp.broadcast_to(pc, (LANES, LANES)), zsq)
        h1_ref[i] = h1.astype(BF16)
        h2_ref[i] = jnp.where(same_head, hh[:, LANES:], zsq)
    for c in range(nchunk):
        sts = [st_ref[p].astype(BF16) for p in range(npair)]
        for p in range(npair):
            i = c * npair + p
            rows, cols = sl(c, p)
            y_ref[rows, cols] = _dot(g1_ref[i], sts[p]) + g2_ref[i]
            st_ref[p] = _dot(h1_ref[i], sts[p]) + h2_ref[i]

    y = y_ref[...]
    inv_n = 1.0 / RWKV_HEAD
    mean = head_sum([y.astype(BF16)]) * inv_n
    yc = y - mean
    var = head_sum([(yc * yc).astype(BF16)]) * inv_n
    yn = yc * lax.rsqrt(var + GN_EPS) * lng_ref[...] + lnb_ref[...]
    bonus = head_sum([(r * k2 * rk_ref[...]).astype(BF16)]) * v
    o_ref[0] = ((yn + bonus) * g).astype(BF16)


def _dot_exact_rhs_lhs(ones_bf16, x):
    hi, mid, lo = _split3(x)
    return _dot(ones_bf16, hi) + _dot(ones_bf16, mid) + _dot(ones_bf16, lo)


def _rwkv_call(zr3, tshift_mu, decay_w0, decay_up, iclr_a0, iclr_up, gate_up, k_k, k_a, r_k, lnx_g, lnx_b):
    B, S, _ = zr3.shape
    tt = min(256, S)
    W = RWKV_W
    row = lambda a: a.reshape(1, -1).astype(F32)
    dup = jnp.concatenate([decay_up, jnp.zeros((ICLR_LORA, W), F32)], axis=0)
    iup = jnp.concatenate([jnp.zeros((DECAY_LORA, W), F32), iclr_up], axis=0)
    idx = jnp.arange(2 * LANES)
    seg = (idx[:, None] // RWKV_HEAD == idx[None, :] // RWKV_HEAD).astype(BF16)
    t = jnp.arange(tt)
    same_chunk = t[:, None] // CHUNK == t[None, :] // CHUNK
    tri = (same_chunk & (t[None, :] <= t[:, None])).astype(BF16)
    const = lambda shape: pl.BlockSpec(shape, lambda b, j: (0,) * len(shape))
    kern = functools.partial(_rwkv_kernel, tt=tt)
    nu = (tt // CHUNK) * (RWKV_HEADS // 2)
    return pl.pallas_call(
        kern,
        grid=(B, S // tt),
        in_specs=[pl.BlockSpec((1, tt, RWKV_IN), lambda b, j: (b, j, 0)),
                  const((1, RWKV_IN)), const((1, W)), const((2 * DECAY_LORA, W)), const((1, W)),
                  const((2 * ICLR_LORA, W)), const((GATE_LORA, W)), const((1, W)), const((1, W)),
                  const((1, W)), const((1, W)), const((1, W)),
                  const((2 * LANES, 2 * LANES)), const((tt, tt))],
        out_specs=pl.BlockSpec((1, tt, W), lambda b, j: (b, j, 0)),
        out_shape=jax.ShapeDtypeStruct((B, S, W), BF16),
        scratch_shapes=[pltpu.VMEM((1, RWKV_IN), F32),
                        pltpu.VMEM((RWKV_HEADS // 2, LANES, LANES), F32),
                        pltpu.VMEM((tt, W), F32),
                        pltpu.VMEM((tt, W), F32),
                        pltpu.VMEM((tt, W), BF16),
                        pltpu.VMEM((tt, W), BF16),
                        pltpu.VMEM((tt, W), F32),
                        pltpu.VMEM((tt, W), F32),
                        pltpu.VMEM((tt, W), F32),
                        pltpu.VMEM((tt, W), F32),
                        pltpu.VMEM((tt, W), F32),
                        pltpu.VMEM((nu, CHUNK, LANES), BF16),
                        pltpu.VMEM((nu, CHUNK, LANES), BF16),
                        pltpu.VMEM((nu, CHUNK, LANES), BF16),
                        pltpu.VMEM((nu, CHUNK, LANES), BF16),
                        pltpu.VMEM((nu, CHUNK, LANES), BF16),
                        pltpu.VMEM((nu, CHUNK, 2 * LANES), F32),
                        pltpu.VMEM((nu, CHUNK, 2 * LANES), F32),
                        pltpu.VMEM((nu, CHUNK, LANES), BF16),
                        pltpu.VMEM((nu, CHUNK, LANES), F32),
                        pltpu.VMEM((nu, LANES, LANES), BF16),
                        pltpu.VMEM((nu, LANES, LANES), F32)],
        compiler_params=_cparams(("parallel", "arbitrary")),
        name="rwkv",
    )(zr3, row(tshift_mu), row(decay_w0), dup, row(iclr_a0), iup.astype(BF16), gate_up.astype(BF16), row(k_k),
      row(k_a), row(r_k), row(lnx_g), row(lnx_b), seg, tri)


def _merge_kernel(x_ref, attn_ref, rw_ref, ga_ref, gr_ref, wa_ref, wr_ref, wo_ref, g1_ref,
                  n2_ref, sc_ref, sh_ref, rwt_ref, rb_ref, x1_ref, h2_ref, gt_ref, cnt_ref):
    a = _dot(attn_ref[...], wa_ref[...])
    rr = _dot(rw_ref[...], wr_ref[...])
    mixed = _sigmoid(ga_ref[...].astype(F32)) * a + _sigmoid(gr_ref[...].astype(F32)) * rr
    x1 = x_ref[...] + g1_ref[0] * _dot(mixed.astype(BF16), wo_ref[...])
    x1_ref[...] = x1
    ms = jnp.mean(x1 * x1, axis=-1, keepdims=True)
    h2 = x1 * lax.rsqrt(ms + RMS_EPS) * n2_ref[...]
    h2 = h2 * (1.0 + sc_ref[0]) + sh_ref[0]
    h2_ref[...] = _pack_bf16_pairs(h2)

    tm = x1.shape[0]
    E, G, EG = N_EXPERTS, N_GROUPS, N_EXPERTS // N_GROUPS
    scores = _sigmoid(_dot3(rwt_ref[...], h2, nt=True))
    choice = scores + rb_ref[...]
    c3 = choice.reshape(G, EG, tm)
    e_i = lax.broadcasted_iota(I32, (G, EG, tm), 1)
    m1 = jnp.max(c3, axis=1, keepdims=True)
    first = jnp.min(jnp.where(c3 == m1, e_i, EG), axis=1, keepdims=True)
    m2 = jnp.max(jnp.where(e_i == first, -jnp.inf, c3), axis=1, keepdims=True)
    grp = (m1 + m2).reshape(G, tm)
    g_i = lax.broadcasted_iota(I32, (G, tm), 0)
    rank = jnp.zeros((G, tm), I32)
    for o in range(G):
        other = grp[o:o + 1, :]
        rank = rank + jnp.where((other > grp) | ((other == grp) & (o < g_i)), 1, 0)
    gsel = rank < TOPK_GROUPS
    esel = jnp.broadcast_to(gsel.reshape(G, 1, tm), (G, EG, tm)).reshape(E, tm)
    mc = jnp.where(esel, choice, -jnp.inf)
    x_i = lax.broadcasted_iota(I32, (E, tm), 0)
    erank = jnp.zeros((E, tm), I32)
    for o in range(E):
        other = mc[o:o + 1, :]
        erank = erank + jnp.where((other > mc) | ((other == mc) & (o < x_i)), 1, 0)
    top = erank < MOE_TOPK
    gw = jnp.where(top, scores, 0.0)
    gw = gw / jnp.sum(gw, axis=0, keepdims=True) * ROUTED_SCALE
    gt_ref[...] = gw
    sel = jnp.where(gw > 0.0, 1.0, 0.0)
    cnt_ref[0] = jnp.broadcast_to(jnp.sum(sel, axis=1, keepdims=True), (E, LANES))


def _pack_bf16_pairs(x):
    n = x.shape[1] // 2
    bits = pltpu.bitcast(x.astype(BF16).astype(F32), I32)
    return bits[:, :n] | lax.shift_right_logical(bits[:, n:], 16)


def _unpack_bf16_pairs(p):
    hi = pltpu.bitcast(p & jnp.int32(-65536), F32)
    lo = pltpu.bitcast(lax.shift_left(p, 16), F32)
    return jnp.concatenate([hi, lo], axis=1).astype(BF16)


def _merge_call(x2, attn, rw, ga, gr, wa, wr, wo, g1, norm2_g, sc2, sh2, router_w, router_bias, S):
    T, D = x2.shape
    B = T // S
    tm = min(MERGE_TILE, S)
    tpb = S // tm
    nt = T // tm
    E = N_EXPERTS
    row = lambda i: (i, 0)
    per_b = lambda i: (i // tpb, 0, 0)
    const = lambda shape: pl.BlockSpec(shape, lambda i: (0,) * len(shape))
    return pl.pallas_call(
        _merge_kernel,
        grid=(nt,),
        in_specs=[pl.BlockSpec((tm, D), row), pl.BlockSpec((tm, ATTN_W), row), pl.BlockSpec((tm, RWKV_W), row),
                  pl.BlockSpec((tm, D), row), pl.BlockSpec((tm, D), row),
                  const((ATTN_W, D)), const((RWKV_W, D)), const((D, D)),
                  pl.BlockSpec((1, 1, D), per_b), const((1, D)),
                  pl.BlockSpec((1, 1, D), per_b), pl.BlockSpec((1, 1, D), per_b),
                  const((E, D)), const((E, 1))],
        out_specs=[pl.BlockSpec((tm, D), row), pl.BlockSpec((tm, D // 2), row),
                   pl.BlockSpec((E, tm), lambda i: (0, i)), pl.BlockSpec((1, E, LANES), lambda i: (i, 0, 0))],
        out_shape=[jax.ShapeDtypeStruct((T, D), F32), jax.ShapeDtypeStruct((T, D // 2), I32),
                   jax.ShapeDtypeStruct((E, T), F32), jax.ShapeDtypeStruct((nt, E, LANES), F32)],
        compiler_params=_cparams(("parallel",)),
        name="merge",
    )(x2, attn, rw, ga, gr, wa, wr, wo, g1.reshape(B, 1, D), norm2_g.reshape(1, D),
      sc2.reshape(B, 1, D), sh2.reshape(B, 1, D), router_w.T, router_bias.reshape(E, 1))


MERGE_TILE = 512
MOE_BLOCK = 1024
SC_CORES, SC_SUBCORES = 2, 16
SC_WORKERS = SC_CORES * SC_SUBCORES
SC_ROWS = 128


def _plan_kernel(gt_ref, cnt_ref, upper_ref, lowe_ref, dest_ref, gw_ref, be_ref, off_ref, *, tm, n_blocks):
    i = pl.program_id(0)
    E = N_EXPERTS
    lowe = lowe_ref[...]

    @pl.when(i == 0)
    def _():
        total = jnp.sum(cnt_ref[...], axis=0)
        nblk = jnp.floor((total + (MOE_BLOCK - 1)) * (1.0 / MOE_BLOCK))
        start_blk = _dot_exact_rhs_lhs(lowe, nblk)
        off_ref[...] = start_blk * MOE_BLOCK
        end_blk = start_blk + nblk
        b_i = lax.broadcasted_iota(I32, (E, n_blocks), 1).astype(F32)
        e_of_b = jnp.sum(jnp.where(end_blk[:, :1] <= b_i, 1.0, 0.0), axis=0, keepdims=True)
        be_ref[...] = e_of_b.astype(I32)

    gt = gt_ref[...]
    sel = gt > 0.0
    selb = jnp.where(sel, 1.0, 0.0).astype(BF16)
    rank = _dot(selb, upper_ref[...])
    dest = off_ref[:, :1] + rank
    off_ref[...] = off_ref[...] + cnt_ref[i]
    kth = _dot(lowe, selb)
    dests, gws = [], []
    for k in range(MOE_TOPK):
        m = sel & (kth == float(k))
        have = jnp.sum(jnp.where(m, 1.0, 0.0), axis=0, keepdims=True)
        d = jnp.sum(jnp.where(m, dest, 0.0), axis=0, keepdims=True)
        dests.append(jnp.where(have > 0.0, d, float((n_blocks - 1) * MOE_BLOCK)))
        gws.append(jnp.sum(jnp.where(m, gt, 0.0), axis=0, keepdims=True))
    dest_ref[...] = jnp.concatenate(dests, axis=0).astype(I32)
    gpad = jnp.concatenate(gws + [jnp.zeros((LANES - MOE_TOPK, tm), F32)], axis=0)
    gw_ref[...] = gpad.T


def _plan_call(gate_t, cnt, n_blocks, tile0):
    E = gate_t.shape[0]
    nt = cnt.shape[0]
    tm = MERGE_TILE
    T = nt * tm
    idx = jnp.arange(tm)
    upper = (idx[:, None] < idx[None, :]).astype(BF16)
    ei = jnp.arange(E)
    lowe = (ei[None, :] < ei[:, None]).astype(BF16)
    kern = functools.partial(_plan_kernel, tm=tm, n_blocks=n_blocks)
    const = lambda shape: pl.BlockSpec(shape, lambda i: (0,) * len(shape))
    return pl.pallas_call(
        kern,
        grid=(nt,),
        in_specs=[pl.BlockSpec((E, tm), lambda i: (0, i + tile0)), const((nt, E, LANES)), const((tm, tm)),
                  const((E, E))],
        out_specs=[pl.BlockSpec((MOE_TOPK, tm), lambda i: (0, i)), pl.BlockSpec((tm, LANES), lambda i: (i, 0)),
                   const((1, n_blocks))],
        out_shape=[jax.ShapeDtypeStruct((MOE_TOPK, T), I32), jax.ShapeDtypeStruct((T, LANES), F32),
                   jax.ShapeDtypeStruct((1, n_blocks), I32)],
        scratch_shapes=[pltpu.VMEM((E, LANES), F32)],
        compiler_params=_cparams(("arbitrary",)),
        name="plan",
    )(gate_t, cnt, upper, lowe)


def _sc_index_layout(dest_t):
    K, T = dest_t.shape
    n_ch = T // (SC_WORKERS * SC_ROWS)
    return dest_t.reshape(K, SC_WORKERS, n_ch, SC_ROWS).transpose(1, 2, 0, 3).reshape(SC_WORKERS, n_ch * K, SC_ROWS)


def _sc_dispatch(rows, idx, n_slots, tok0):
    W = rows.shape[1]
    n_ch = idx.shape[1] // MOE_TOPK
    T = n_ch * SC_WORKERS * SC_ROWS
    tpw = T // SC_WORKERS
    mesh = plsc.VectorSubcoreMesh(core_axis_name="c", subcore_axis_name="s")

    @functools.partial(
        pl.kernel, mesh=mesh,
        out_type=jax.ShapeDtypeStruct((n_slots, W), I32),
        scratch_types=[pltpu.VMEM((n_ch * MOE_TOPK, SC_ROWS), I32), pltpu.VMEM((SC_ROWS, W), I32),
                       pltpu.SemaphoreType.DMA])
    def kern(x_hbm, idx_hbm, o_hbm, idx_v, rows_v, sem):
        wid = lax.axis_index("s") * SC_CORES + lax.axis_index("c")
        pltpu.sync_copy(idx_hbm.at[wid], idx_v)

        @pl.loop(0, n_ch)
        def _(j):
            pltpu.sync_copy(x_hbm.at[pl.ds(tok0 + wid * tpw + j * SC_ROWS, SC_ROWS)], rows_v)
            copies = [pltpu.async_copy(rows_v, o_hbm.at[idx_v.at[j * MOE_TOPK + k]], sem)
                      for k in range(MOE_TOPK)]
            for cp in copies:
                cp.wait()

    return kern(rows, idx)


def _sc_combine(slots, idx, T):
    _, W = slots.shape
    n_ch = T // (SC_WORKERS * SC_ROWS)
    tpw = T // SC_WORKERS
    mesh = plsc.VectorSubcoreMesh(core_axis_name="c", subcore_axis_name="s")

    @functools.partial(
        pl.kernel, mesh=mesh,
        out_type=jax.ShapeDtypeStruct((MOE_TOPK, T, W), I32),
        scratch_types=[pltpu.VMEM((n_ch * MOE_TOPK, SC_ROWS), I32), pltpu.VMEM((SC_ROWS, W), I32),
                       pltpu.SemaphoreType.DMA])
    def kern(s_hbm, idx_hbm, o_hbm, idx_v, rows_v, sem):
        wid = lax.axis_index("s") * SC_CORES + lax.axis_index("c")
        pltpu.sync_copy(idx_hbm.at[wid], idx_v)

        @pl.loop(0, n_ch)
        def _(j):
            for k in range(MOE_TOPK):
                pltpu.async_copy(s_hbm.at[idx_v.at[j * MOE_TOPK + k]], rows_v, sem).wait()
                pltpu.sync_copy(rows_v, o_hbm.at[k, pl.ds(wid * tpw + j * SC_ROWS, SC_ROWS)])

    return kern(slots, idx)


def _ffn_kernel(be_ref, x_ref, eg_ref, eu_ref, ed_ref, o_ref):
    used = be_ref[pl.program_id(0)] < N_EXPERTS

    @pl.when(used)
    def _():
        x = _unpack_bf16_pairs(x_ref[...])
        a = _dot(x, eg_ref[0].astype(BF16))
        u = _dot(x, eu_ref[0].astype(BF16))
        o_ref[...] = _pack_bf16_pairs(_dot((a * _sigmoid(a) * u).astype(BF16), ed_ref[0].astype(BF16)))

    @pl.when(jnp.logical_not(used))
    def _():
        o_ref[...] = jnp.zeros(o_ref.shape, I32)


def _ffn_call(xs, block_e, eg, eu, ed, n_blocks):
    P, W = xs.shape
    D, FF = 2 * W, EXPERT_FF
    grid_spec = pltpu.PrefetchScalarGridSpec(
        num_scalar_prefetch=1,
        grid=(n_blocks,),
        in_specs=[pl.BlockSpec((MOE_BLOCK, W), lambda b, be: (b, 0)),
                  pl.BlockSpec((1, D, FF), lambda b, be: (jnp.minimum(be[b], N_EXPERTS - 1), 0, 0)),
                  pl.BlockSpec((1, D, FF), lambda b, be: (jnp.minimum(be[b], N_EXPERTS - 1), 0, 0)),
                  pl.BlockSpec((1, FF, D), lambda b, be: (jnp.minimum(be[b], N_EXPERTS - 1), 0, 0))],
        out_specs=pl.BlockSpec((MOE_BLOCK, W), lambda b, be: (b, 0)))
    return pl.pallas_call(
        _ffn_kernel,
        grid_spec=grid_spec,
        out_shape=jax.ShapeDtypeStruct((P, W), I32),
        compiler_params=_cparams(("parallel",)),
        name="ffn",
    )(block_e, xs, eg, eu, ed)


def _final_kernel(h_ref, c_ref, gw_ref, x1_ref, g2_ref, fg_ref, sg_ref, su_ref, sd_ref, o_ref):
    h = _unpack_bf16_pairs(h_ref[...])
    a = _dot(h, sg_ref[...])
    u = _dot(h, su_ref[...])
    moe = _dot((a * _sigmoid(a) * u).astype(BF16), sd_ref[...])
    gw = gw_ref[...]
    for k in range(MOE_TOPK):
        w = gw[:, k:k + 1]
        y = _unpack_bf16_pairs(c_ref[k]).astype(F32)
        moe = moe + jnp.where(w > 0.0, w * y, 0.0)
    x2 = x1_ref[...] + g2_ref[0] * moe
    ms = jnp.mean(x2 * x2, axis=-1, keepdims=True)
    o_ref[...] = x2 * lax.rsqrt(ms + RMS_EPS) * fg_ref[...]


def _final_call(h2p, comb, gw, x1, g2, final_g, sg, su, sd, S, tile0, prev_out):
    T, W = h2p.shape
    D, FF = 2 * W, EXPERT_FF
    B = T // S
    tm = MERGE_TILE
    tpb = S // tm
    ntile = comb.shape[1] // tm
    row = lambda i: (i, 0)
    full_row = lambda i: (i + tile0, 0)
    const = lambda shape: pl.BlockSpec(shape, lambda i: (0,) * len(shape))
    in_specs = [pl.BlockSpec((tm, W), full_row), pl.BlockSpec((MOE_TOPK, tm, W), lambda i: (0, i, 0)),
                pl.BlockSpec((tm, LANES), row), pl.BlockSpec((tm, D), full_row),
                pl.BlockSpec((1, 1, D), lambda i: ((i + tile0) // tpb, 0, 0)), const((1, D)),
                const((D, FF)), const((D, FF)), const((FF, D))]
    args = [h2p, comb, gw, x1, g2.reshape(B, 1, D), final_g.reshape(1, D), sg, su, sd]
    kern, aliases = _final_kernel, {}
    if prev_out is not None:
        in_specs.append(pl.BlockSpec(memory_space=pl.ANY))
        args.append(prev_out)
        aliases = {len(args) - 1: 0}
        kern = lambda *refs: _final_kernel(*refs[:9], refs[10])
    return pl.pallas_call(
        kern,
        grid=(ntile,),
        in_specs=in_specs,
        out_specs=pl.BlockSpec((tm, D), full_row),
        out_shape=jax.ShapeDtypeStruct((T, D), F32),
        input_output_aliases=aliases,
        compiler_params=_cparams(("parallel",)),
        name="final",
    )(*args)


MOE_GROUPS = 2


def _moe_call(h2p, gate_t, cnt, x1, g2, final_g, eg, eu, ed, sg, su, sd, S):
    T = h2p.shape[0]
    nt = cnt.shape[0]
    groups = MOE_GROUPS if (T // MOE_GROUPS) % (SC_WORKERS * SC_ROWS) == 0 and nt % MOE_GROUPS == 0 else 1
    tg, ntg = T // groups, nt // groups
    n_blocks = (tg * MOE_TOPK) // MOE_BLOCK + N_EXPERTS + 1
    out = None
    for g in range(groups):
        dest_t, gw, block_e = _plan_call(gate_t, cnt[g * ntg:(g + 1) * ntg], n_blocks, g * ntg)
        idx = _sc_index_layout(dest_t)
        xs = _sc_dispatch(h2p, idx, n_blocks * MOE_BLOCK, g * tg)
        ys = _ffn_call(xs, block_e.reshape(n_blocks), eg, eu, ed, n_blocks)
        comb = _sc_combine(ys, idx, tg)
        out = _final_call(h2p, comb, gw, x1, g2, final_g, sg, su, sd, S, g * ntg, out)
    return out


def _layer(x2, c, S, ada_w, ada_b, norm1_g, w_in, rel_bias, tshift_mu, decay_w0, decay_up, iclr_a0, iclr_up,
           gate_up, k_k, k_a, r_k, lnx_g, lnx_b, w_attn_br, w_rwkv_br, w_out, norm2_g, router_w, router_bias,
           exp_gate, exp_up, exp_down, sh_gate, sh_up, sh_down, final_g):
    T, D = x2.shape
    B = T // S
    mod = _mod_call(c, ada_w, ada_b)
    sh1, sc1, g1, sh2, sc2, g2 = jnp.split(mod, 6, axis=-1)

    q, k, vt, iq, ik4, iwt, zr, ga, gr = _inproj_call(x2, norm1_g, sc1, sh1, w_in, S)

    ta = min(512, S)
    assert ta >= LANES and S % ta == 0
    top_k = min(TOPK_MAX, S // 4)
    seq = lambda a: a.reshape(B, S, a.shape[-1])
    mask = _index_call(seq(iq), iwt, seq(ik4), ta, top_k)
    bias_tiles = _bias_call(rel_bias, ta)
    attn = _attn_call(seq(q), seq(k), vt, mask, bias_tiles, rel_bias, ta).reshape(T, ATTN_W)

    rw = _rwkv_call(zr.reshape(B, S, RWKV_IN), tshift_mu, decay_w0, decay_up, iclr_a0, iclr_up, gate_up,
                    k_k, k_a, r_k, lnx_g, lnx_b).reshape(T, RWKV_W)

    x1, h2p, gate_t, cnt = _merge_call(x2, attn, rw, ga, gr, w_attn_br.astype(BF16), w_rwkv_br.astype(BF16),
                                       w_out.astype(BF16), g1, norm2_g, sc2, sh2, router_w, router_bias, S)
    return _moe_call(h2p, gate_t, cnt, x1, g2, final_g, exp_gate, exp_up, exp_down,
                     sh_gate.astype(BF16), sh_up.astype(BF16), sh_down.astype(BF16), S)


def kernel(x, c, ada_w, ada_b, norm1_g, w_in, rel_bias, tshift_mu, decay_w0, decay_up, iclr_a0, iclr_up, gate_up, k_k, k_a, r_k, lnx_g, lnx_b, w_attn_br, w_rwkv_br, w_out, norm2_g, router_w, router_bias, exp_gate, exp_up, exp_down, sh_gate, sh_up, sh_down, final_g):
    B, S, D = x.shape
    depth = ada_w.shape[0]
    assert depth == 1, "the final RMSNorm is fused into the (single) layer's MoE kernel"
    out = _layer(x.reshape(B * S, D), c, S, ada_w[0], ada_b[0], norm1_g[0], w_in[0], rel_bias, tshift_mu[0],
                 decay_w0[0], decay_up[0], iclr_a0[0], iclr_up[0], gate_up[0], k_k[0], k_a[0], r_k[0],
                 lnx_g[0], lnx_b[0], w_attn_br[0], w_rwkv_br[0], w_out[0], norm2_g[0], router_w[0],
                 router_bias[0], exp_gate[0], exp_up[0], exp_down[0], sh_gate[0], sh_up[0], sh_down[0], final_g)
    return out.reshape(B, S, D)
```

```python
import functools
import math

import jax
import jax.numpy as jnp
from jax import lax
from jax.experimental import pallas as pl
from jax.experimental.pallas import tpu as pltpu
from jax.experimental.pallas import tpu_sc as plsc

F32 = jnp.float32
BF16 = jnp.bfloat16
I32 = jnp.int32
I16 = jnp.int16

RMS_EPS = 1e-6
D_MODEL = 1024
N_ATTN_HEADS = 8
ATTN_HEAD_DIM = 64
ATTN_W = 512
IDX_HEADS = 16
IDX_DIM = 32
IDX_Q = 512
TOPK_MAX = 256
N_BUCKETS = 32
MAX_DISTANCE = 128
RWKV_HEADS = 8
RWKV_HEAD = 64
RWKV_W = 512
DECAY_LORA = 64
ICLR_LORA = 64
GATE_LORA = 128
RWKV_IN = 1792
GN_EPS = 64e-5
N_EXPERTS = 64
N_GROUPS = 8
TOPK_GROUPS = 4
MOE_TOPK = 8
EXPERT_FF = 256
ROUTED_SCALE = 2.5

LANES = 128
VMEM_LIMIT = 56 * 1024 * 1024
CHUNK = 64
LOG2E = 1.4426950408889634
INT_MIN = -2147483648
KEY_NEG_INF = -2139095041

NT_DIMS = (((1,), (1,)), ((), ()))


def _cparams(sem):
    return pltpu.CompilerParams(dimension_semantics=sem, vmem_limit_bytes=VMEM_LIMIT)


def _dot(a, b):
    return jnp.dot(a, b, preferred_element_type=F32)


def _dot_nt(a, b):
    return lax.dot_general(a, b, NT_DIMS, preferred_element_type=F32)


def _split2(x):
    hi = x.astype(BF16)
    lo = (x - hi.astype(F32)).astype(BF16)
    return hi, lo


def _split3(x):
    hi = x.astype(BF16)
    r1 = x - hi.astype(F32)
    mid = r1.astype(BF16)
    lo = (r1 - mid.astype(F32)).astype(BF16)
    return hi, mid, lo


def _dot_exact_rhs(x, ones_bf16, terms=2):
    parts = _split3(x) if terms == 3 else _split2(x)
    out = _dot(parts[0], ones_bf16)
    for p in parts[1:]:
        out = out + _dot(p, ones_bf16)
    return out


def _dot3(a, b, nt=False):
    ah, al = _split2(a)
    bh, bl = _split2(b)
    f = _dot_nt if nt else _dot
    return f(ah, bh) + f(ah, bl) + f(al, bh)


def _sigmoid(x):
    return 1.0 / (1.0 + jnp.exp(-x))


def _mod_kernel(c_ref, w_ref, b_ref, o_ref):
    c = c_ref[...]
    s = c * _sigmoid(c)
    o_ref[...] = _dot3(s, w_ref[...]) + b_ref[...]


def _mod_call(c, ada_w, ada_b):
    B, D = c.shape
    N = ada_w.shape[1]
    tn = 1024
    return pl.pallas_call(
        _mod_kernel,
        grid=(N // tn,),
        in_specs=[pl.BlockSpec((B, D), lambda j: (0, 0)),
                  pl.BlockSpec((D, tn), lambda j: (0, j)),
                  pl.BlockSpec((1, tn), lambda j: (0, j))],
        out_specs=pl.BlockSpec((B, tn), lambda j: (0, j)),
        out_shape=jax.ShapeDtypeStruct((B, N), F32),
        compiler_params=_cparams(("arbitrary",)),
        name="mod",
    )(c, ada_w, ada_b.reshape(1, N))


_OFF_Q, _OFF_K, _OFF_IQ, _OFF_IK4, _OFF_ZR, _OFF_GA, _OFF_GR, _N_PACK = (
    0, 512, 1024, 1536, 2048, 3840, 4864, 5888)
IDX_PER_BLOCK = LANES // IDX_DIM


def _pack_w_in(w_in):
    D = w_in.shape[0]
    w_ik = w_in[:, 2048:2080]
    ik4 = jnp.zeros((D, IDX_PER_BLOCK * LANES), w_in.dtype)
    for j in range(IDX_PER_BLOCK):
        ik4 = lax.dynamic_update_slice(ik4, w_ik, (0, j * LANES + j * IDX_DIM))
    w_pack = jnp.concatenate([w_in[:, 0:1024], w_in[:, 1536:2048], ik4, w_in[:, 2096:]], axis=1).astype(BF16)
    return w_pack, w_in[:, 1024:1536].T.astype(BF16), w_in[:, 2080:2096].T.astype(BF16)


def _inproj_kernel(x_ref, g_ref, sc_ref, sh_ref, w_ref, wvt_ref, wiwt_ref,
                   q_ref, k_ref, vt_ref, iq_ref, ik4_ref, iwt_ref, zr_ref, ga_ref, gr_ref):
    x = x_ref[...]
    ms = jnp.mean(x * x, axis=-1, keepdims=True)
    h = x * lax.rsqrt(ms + RMS_EPS) * g_ref[...]
    h = h * (1.0 + sc_ref[0]) + sh_ref[0]
    hb = h.astype(BF16)

    def proj(lo, hi):
        return _dot(hb, w_ref[:, lo:hi])

    q_ref[...] = (proj(_OFF_Q, _OFF_K) * (ATTN_HEAD_DIM ** -0.5 * LOG2E)).astype(BF16)
    k_ref[...] = proj(_OFF_K, _OFF_IQ).astype(BF16)
    iq_ref[...] = proj(_OFF_IQ, _OFF_IK4).astype(BF16)
    ik4_ref[...] = proj(_OFF_IK4, _OFF_ZR).astype(BF16)
    zr_ref[...] = proj(_OFF_ZR, _OFF_GA)
    ga_ref[...] = proj(_OFF_GA, _OFF_GR).astype(BF16)
    gr_ref[...] = proj(_OFF_GR, _N_PACK).astype(BF16)
    vt_ref[0] = _dot_nt(wvt_ref[...], hb).astype(BF16)
    iwt_ref[0] = _dot_nt(wiwt_ref[...], hb)


def _inproj_call(x2, norm_g, sc, sh, w_in, S):
    T, D = x2.shape
    B = T // S
    tm = min(512, S)
    tpb = S // tm
    w_pack, wvt, wiwt = _pack_w_in(w_in)
    row = lambda i: (i, 0)
    per_b = lambda i: (i // tpb, 0, 0)
    colblk = lambda i: (i // tpb, 0, i % tpb)
    const = lambda shape: pl.BlockSpec(shape, lambda i: (0,) * len(shape), pipeline_mode=pl.Buffered(1))
    rows_out = ((512, BF16), (512, BF16), (512, BF16), (512, BF16), (RWKV_IN, F32), (D, BF16), (D, BF16))
    out_specs = [pl.BlockSpec((tm, w), row) for w, _ in rows_out]
    out_shape = [jax.ShapeDtypeStruct((T, w), dt) for w, dt in rows_out]
    out_specs[2:2] = [pl.BlockSpec((1, ATTN_W, tm), colblk)]
    out_shape[2:2] = [jax.ShapeDtypeStruct((B, ATTN_W, S), BF16)]
    out_specs[5:5] = [pl.BlockSpec((1, IDX_HEADS, tm), colblk)]
    out_shape[5:5] = [jax.ShapeDtypeStruct((B, IDX_HEADS, S), F32)]
    return pl.pallas_call(
        _inproj_kernel,
        grid=(T // tm,),
        in_specs=[pl.BlockSpec((tm, D), row),
                  pl.BlockSpec((1, D), lambda i: (0, 0)),
                  pl.BlockSpec((1, 1, D), per_b),
                  pl.BlockSpec((1, 1, D), per_b),
                  const((D, _N_PACK)), const((ATTN_W, D)), const((IDX_HEADS, D))],
        out_specs=out_specs,
        out_shape=out_shape,
        compiler_params=_cparams(("parallel",)),
        name="inproj",
    )(x2, norm_g.reshape(1, D), sc.reshape(B, 1, D), sh.reshape(B, 1, D), w_pack, wvt, wiwt)


def _t5_bucket(rel):
    n = jnp.maximum(rel, 0)
    max_exact = N_BUCKETS // 2
    nf = jnp.maximum(n, 1).astype(F32)
    large = max_exact + (jnp.log(nf / max_exact) / math.log(MAX_DISTANCE / max_exact)
                         * (N_BUCKETS - max_exact)).astype(I32)
    large = jnp.minimum(large, N_BUCKETS - 1)
    return jnp.where(n < max_exact, n, large)


def _bias_kernel(bucket_ref, rb_ref, o_ref):
    h = pl.program_id(0)
    bk = bucket_ref[...]
    out = jnp.zeros(bk.shape, F32)
    for b in range(N_BUCKETS):
        out = jnp.where(bk == b, rb_ref[b, h] * LOG2E, out)
    o_ref[0] = out


def _bias_call(rel_bias, tq):
    r = jnp.arange(tq, dtype=I32)[None, :]
    c = jnp.arange(tq, dtype=I32)[:, None]
    buckets = jnp.stack([_t5_bucket(r - c), _t5_bucket(tq + r - c)])
    return pl.pallas_call(
        _bias_kernel,
        grid=(N_ATTN_HEADS,),
        in_specs=[pl.BlockSpec((2, tq, tq), lambda h: (0, 0, 0)),
                  pl.BlockSpec(memory_space=pltpu.SMEM)],
        out_specs=pl.BlockSpec((1, 2, tq, tq), lambda h: (h, 0, 0, 0)),
        out_shape=jax.ShapeDtypeStruct((N_ATTN_HEADS, 2, tq, tq), F32),
        compiler_params=_cparams(("arbitrary",)),
        name="bias",
    )(buckets, rel_bias)


def _index_kernel(iq_ref, iwt_ref, ik4_ref, lower_ref, mask_ref, key_ref, k16_ref, *, t, nk, top_k, scale):
    qi = pl.program_id(1)
    nkt = qi + 1
    ksub = LANES
    qpos = qi * t + lax.broadcasted_iota(I32, (ksub, t), 1)

    def score_tile(kt, carry):
        kbase = pl.multiple_of(kt * t, t)
        for ks in range(t // ksub):
            acc = jnp.zeros((ksub, t), F32)
            ik_rows = ik4_ref[0, pl.ds(kbase + ks * ksub, ksub), :]
            ik_stack = jnp.concatenate([ik_rows[:, j * LANES:(j + 1) * LANES] for j in range(IDX_PER_BLOCK)],
                                       axis=0)
            for g in range(IDX_HEADS // IDX_PER_BLOCK):
                d4 = _dot_nt(ik_stack, iq_ref[0, :, g * LANES:(g + 1) * LANES])
                for j in range(IDX_PER_BLOCK):
                    h = g * IDX_PER_BLOCK + j
                    acc = acc + jnp.maximum(d4[j * ksub:(j + 1) * ksub], 0.0) * iwt_ref[0, h:h + 1, :]
            s = acc * scale
            kpos = kt * t + ks * ksub + lax.broadcasted_iota(I32, (ksub, t), 0)
            s = jnp.where(kpos <= qpos, s, -jnp.inf)
            bits = pltpu.bitcast(s, I32)
            key = bits ^ ((bits >> 31) & 0x7FFFFFFF)
            key_ref[kt, ks * ksub:(ks + 1) * ksub, :] = key
            k16_ref[kt, ks * ksub:(ks + 1) * ksub, :] = (key >> 16).astype(I16)
        return carry

    lax.fori_loop(0, nkt, score_tile, 0)

    pack = 16

    def search16():
        def bit_body(i, ans):
            cand = ans | lax.shift_left(jnp.int32(1), 15 - i)
            cand16 = (cand - 32768).astype(I16)

            def cnt_body(kt, acc):
                one = jnp.where(k16_ref[kt] >= cand16, jnp.int16(1), jnp.int16(0))
                for r in range(t // pack):
                    acc = acc + one[r * pack:(r + 1) * pack, :]
                return acc

            acc = lax.fori_loop(0, nkt, cnt_body, jnp.zeros((pack, t), I16))
            cnt = jnp.sum(acc.astype(I32), axis=0, keepdims=True)
            return jnp.where(cnt >= top_k, cand, ans)

        return lax.fori_loop(0, 16, bit_body, jnp.zeros((1, t), I32))

    hi = search16() - 32768

    def remap_body(kt, carry):
        key = key_ref[kt]
        khi = key >> 16
        lo = (key & 0xFFFF) - 32768
        k16_ref[kt] = jnp.where(khi > hi, 32767, jnp.where(khi == hi, lo, -32768)).astype(I16)
        return carry

    lax.fori_loop(0, nkt, remap_body, 0)
    thr = hi * 65536 + search16()

    def count(pred):
        def body(kt, acc):
            one = jnp.where(pred(key_ref[kt]), 1.0, 0.0)
            return acc + jnp.sum(one.reshape(t // 8, 8, t), axis=0)
        return jnp.sum(lax.fori_loop(0, nkt, body, jnp.zeros((8, t), F32)), axis=0, keepdims=True)

    n_ge = count(lambda keys: keys >= thr)
    has_tie = jnp.max(n_ge) > float(top_k)

    @pl.when(jnp.logical_not(has_tie))
    def _():
        def mask_body(kt, carry):
            keys = key_ref[kt]
            sel = (keys >= thr) & (keys > KEY_NEG_INF)
            mask_ref[0, 0, kt] = jnp.where(sel, 0.0, -jnp.inf).astype(BF16)
            return carry

        lax.fori_loop(0, nkt, mask_body, 0)

    @pl.when(has_tie)
    def _():
        need = float(top_k) - count(lambda keys: keys > thr)

        def mask_body(kt, seen):
            keys = key_ref[kt]
            tie = (keys == thr) & (keys > KEY_NEG_INF)
            tie_b = jnp.where(tie, 1.0, 0.0).astype(BF16)
            before = seen + _dot(lower_ref[...], tie_b)
            sel = (keys > thr) | (tie & (before < need))
            mask_ref[0, 0, kt] = jnp.where(sel, 0.0, -jnp.inf).astype(BF16)
            return seen + jnp.sum(tie_b.astype(F32).reshape(t // 8, 8, t).sum(axis=0), axis=0, keepdims=True)

        lax.fori_loop(0, nkt, mask_body, jnp.zeros((1, t), F32))

    def fill_body(kt, carry):
        mask_ref[0, 0, kt] = jnp.full((t, t), -jnp.inf, BF16)
        return carry

    lax.fori_loop(nkt, nk, fill_body, 0)


def _index_call(iq, iwt, ik4, t, top_k):
    B, S, _ = iq.shape
    n = S // t
    scale = (IDX_HEADS ** -0.5) * (IDX_DIM ** -0.5)
    kern = functools.partial(_index_kernel, t=t, nk=n, top_k=top_k, scale=scale)
    pos = jnp.arange(t)
    lower = (pos[None, :] < pos[:, None]).astype(BF16)
    return pl.pallas_call(
        kern,
        grid=(B, n),
        in_specs=[pl.BlockSpec((1, t, IDX_Q), lambda b, i: (b, i, 0)),
                  pl.BlockSpec((1, IDX_HEADS, t), lambda b, i: (b, 0, i)),
                  pl.BlockSpec((1, S, IDX_PER_BLOCK * LANES), lambda b, i: (b, 0, 0)),
                  pl.BlockSpec((t, t), lambda b, i: (0, 0))],
        out_specs=pl.BlockSpec((1, 1, n, t, t), lambda b, i: (b, i, 0, 0, 0)),
        out_shape=jax.ShapeDtypeStruct((B, n, n, t, t), BF16),
        scratch_shapes=[pltpu.VMEM((n, t, t), I32), pltpu.VMEM((n, t, t), I16)],
        compiler_params=_cparams(("parallel", "arbitrary")),
        name="index",
    )(iq, iwt, ik4, lower)


ONES_ROWS = 16


def _attn_kernel(qi_tab, kt_tab, q_ref, k_ref, vt_ref, mask_ref, bias_ref, rb_ref, o_ref,
                 qz_ref, m_ref, acc_ref, s_ref, *, t):
    s_id = pl.program_id(1)
    qi = qi_tab[s_id]
    kt = kt_tab[s_id]
    dh = ATTN_HEAD_DIM

    @pl.when(kt == 0)
    def _():
        m_ref[...] = jnp.full(m_ref.shape, -jnp.inf, F32)
        acc_ref[...] = jnp.zeros(acc_ref.shape, F32)
        lane = lax.broadcasted_iota(I32, (t, LANES), 1)
        for h in range(N_ATTN_HEADS):
            blk = q_ref[0, :, (h // 2) * LANES:(h // 2 + 1) * LANES]
            keep = (lane < dh) if h % 2 == 0 else (lane >= dh)
            qz_ref[h] = jnp.where(keep, blk, jnp.zeros_like(blk))

    def step(bias_tile, bias_const):
        maskf = mask_ref[0, 0, 0].astype(F32)
        ones = jnp.ones((ONES_ROWS, t), BF16)

        for h in range(N_ATTN_HEADS):
            k_blk = k_ref[0, :, (h // 2) * LANES:(h // 2 + 1) * LANES]
            s = _dot_nt(k_blk, qz_ref[h]) + maskf
            s_ref[h] = s if bias_tile is None else s + bias_tile(h)
        for h in range(N_ATTN_HEADS):
            s = s_ref[h]
            c = bias_const(h)
            m_old = m_ref[h:h + 1, :]
            m_cur = jnp.max(jnp.max(s.reshape(t // 8, 8, t), axis=0), axis=0, keepdims=True) + c
            m_new = jnp.maximum(m_old, m_cur)
            m_safe = jnp.where(m_new == -jnp.inf, 0.0, m_new)
            alpha = jnp.exp2(m_old - m_safe)
            p = jnp.exp2(s - (m_safe - c)).astype(BF16)
            v_aug = jnp.concatenate([vt_ref[0, h * dh:(h + 1) * dh, :], ones], axis=0)
            acc_ref[h] = alpha * acc_ref[h] + _dot(v_aug, p)
            m_ref[h:h + 1, :] = m_new

    @pl.when(kt == qi)
    def _():
        step(lambda h: bias_ref[h, 0], lambda h: 0.0)

    @pl.when(kt == qi - 1)
    def _():
        step(lambda h: bias_ref[h, 1], lambda h: 0.0)

    @pl.when(kt < qi - 1)
    def _():
        step(None, lambda h: rb_ref[N_BUCKETS - 1, h] * LOG2E)

    @pl.when(kt == qi)
    def _():
        outs = []
        for h in range(N_ATTN_HEADS):
            a = acc_ref[h]
            outs.append(a[:dh, :] / a[dh:dh + 1, :])
        o_ref[0] = jnp.concatenate(outs, axis=0).T.astype(BF16)


def _attn_call(q, k, vt, mask, bias_tiles, rel_bias, t):
    B, S, W = q.shape
    n = S // t
    H = N_ATTN_HEADS
    qi_tab = jnp.asarray([i for i in range(n) for _ in range(i + 1)], I32)
    kt_tab = jnp.asarray([j for i in range(n) for j in range(i + 1)], I32)
    kern = functools.partial(_attn_kernel, t=t)
    grid_spec = pltpu.PrefetchScalarGridSpec(
        num_scalar_prefetch=2,
        grid=(B, int(qi_tab.shape[0])),
        in_specs=[pl.BlockSpec((1, t, W), lambda b, s, qt, kt: (b, qt[s], 0)),
                  pl.BlockSpec((1, t, W), lambda b, s, qt, kt: (b, kt[s], 0)),
                  pl.BlockSpec((1, W, t), lambda b, s, qt, kt: (b, 0, kt[s])),
                  pl.BlockSpec((1, 1, 1, t, t), lambda b, s, qt, kt: (b, qt[s], kt[s], 0, 0)),
                  pl.BlockSpec((H, 2, t, t), lambda b, s, qt, kt: (0, 0, 0, 0), pipeline_mode=pl.Buffered(1)),
                  pl.BlockSpec(memory_space=pltpu.SMEM)],
        out_specs=pl.BlockSpec((1, t, W), lambda b, s, qt, kt: (b, qt[s], 0)),
        scratch_shapes=[pltpu.VMEM((H, t, LANES), BF16),
                        pltpu.VMEM((H, t), F32),
                        pltpu.VMEM((H, ATTN_HEAD_DIM + ONES_ROWS, t), F32),
                        pltpu.VMEM((H, t, t), F32)])
    return pl.pallas_call(
        kern,
        grid_spec=grid_spec,
        out_shape=jax.ShapeDtypeStruct((B, S, W), BF16),
        compiler_params=_cparams(("parallel", "arbitrary")),
        name="attn",
    )(qi_tab, kt_tab, q, k, vt, mask, bias_tiles, rel_bias)


def _blockdiag_rows(x):
    lane = lax.broadcasted_iota(I32, x.shape, 1)
    zero = jnp.zeros_like(x)
    return jnp.concatenate([jnp.where(lane < RWKV_HEAD, x, zero),
                            jnp.where(lane >= RWKV_HEAD, x, zero)], axis=0)


def _rwkv_kernel(z_ref, mu_ref, w0_ref, dup_ref, a0_ref, iup_ref, gup_ref, kk_ref, ka_ref, rk_ref,
                 lng_ref, lnb_ref, seg_ref, tri_ref, o_ref,
                 prev_ref, st_ref, at_ref, rt_ref, bt_ref, kt_ref, bh_ref, kh_ref, v_ref, pc_ref, y_ref,
                 la_ref, lb_ref, mak_ref, arb_ref, ark_ref, wa_ref, wb_ref, g1_ref, g2_ref, h1_ref, h2_ref,
                 *, tt):
    j = pl.program_id(1)
    W = RWKV_W
    C = CHUNK
    nchunk = tt // C
    npair = RWKV_HEADS // 2

    @pl.when(j == 0)
    def _():
        prev_ref[...] = jnp.zeros(prev_ref.shape, F32)
        st_ref[...] = jnp.zeros(st_ref.shape, F32)

    z = z_ref[0]
    row = lax.broadcasted_iota(I32, z.shape, 0)
    z_prev = jnp.where(row == 0, prev_ref[...], pltpu.roll(z, 1, axis=0))
    prev_ref[...] = z[tt - 1:tt, :]
    z = z + mu_ref[...] * (z_prev - z)

    r = z[:, 0:W]
    k = z[:, W:2 * W]
    v = z[:, 2 * W:3 * W]
    wdad = z[:, 3 * W:3 * W + 2 * DECAY_LORA]
    gd = z[:, 3 * W + 2 * DECAY_LORA:]

    w_pre = w0_ref[...] + _dot3(jnp.tanh(wdad), dup_ref[...])
    neg = -w_pre
    softplus = jnp.maximum(neg, 0.0) + jnp.log(1.0 + jnp.exp(-jnp.abs(neg)))
    lw = -jnp.exp(-softplus - 0.5)
    a_lr = _sigmoid(a0_ref[...] + _dot(wdad.astype(BF16), iup_ref[...]))
    g = _dot(_sigmoid(gd).astype(BF16), gup_ref[...])

    seg = seg_ref[...]

    def head_sum(parts):
        half = seg.shape[0]
        cols = [sum(_dot(p[:, lo:lo + half], seg) for p in parts) for lo in range(0, W, half)]
        return jnp.concatenate(cols, axis=1)

    kk = k * kk_ref[...]
    kk = kk / jnp.maximum(jnp.sqrt(head_sum(_split2(kk * kk))), 1e-12)
    k2 = k * (1.0 + (a_lr - 1.0) * ka_ref[...])
    a_vec = -kk
    b_vec = kk * a_lr

    cum = _dot_exact_rhs_lhs(tri_ref[...], lw)
    tot = jnp.concatenate([jnp.broadcast_to(cum[(c + 1) * C - 1:(c + 1) * C, :], (C, W)) for c in range(nchunk)],
                          axis=0)
    p_inv = jnp.exp(-cum)
    p_out = jnp.exp(tot - cum)
    at_ref[...] = a_vec * jnp.exp(cum - lw)
    rt_ref[...] = r * jnp.exp(cum)
    bt_ref[...] = (b_vec * p_inv).astype(BF16)
    kt_ref[...] = (k2 * p_inv).astype(BF16)
    bh_ref[...] = b_vec * p_out
    kh_ref[...] = k2 * p_out
    v_ref[...] = v
    pc_ref[...] = jnp.exp(tot)

    t_i = lax.broadcasted_iota(I32, (C, LANES), 0)
    s_i = lax.broadcasted_iota(I32, (C, LANES), 1) % C
    strict = s_i < t_i
    incl = s_i <= t_i
    r_i = lax.broadcasted_iota(I32, (LANES, LANES), 0)
    c_i = lax.broadcasted_iota(I32, (LANES, LANES), 1)
    same_head = (r_i < RWKV_HEAD) == (c_i < RWKV_HEAD)
    diag = r_i == c_i
    nstage = int(math.log2(C))
    zero = jnp.zeros((C, LANES), F32)
    zsq = jnp.zeros((LANES, LANES), F32)
    units = [(c, p) for c in range(nchunk) for p in range(npair)]

    def sl(c, p):
        return slice(c * C, (c + 1) * C), slice(p * LANES, (p + 1) * LANES)

    def bd2(w):
        wb = w.astype(BF16)
        return jnp.concatenate([_blockdiag_rows(wb[:, :LANES]), _blockdiag_rows(wb[:, LANES:])], axis=1)

    for i, (c, p) in enumerate(units):
        rows, cols = sl(c, p)
        lhs = jnp.concatenate([at_ref[rows, cols], rt_ref[rows, cols]], axis=0).astype(BF16)
        rhs = jnp.concatenate([_blockdiag_rows(bt_ref[rows, cols]),
                               _blockdiag_rows(kt_ref[rows, cols])], axis=0)
        prod = _dot_nt(lhs, rhs)
        la_ref[i] = jnp.where(strict, prod[:C, :LANES], zero).astype(BF16)
        mak_ref[i] = jnp.where(strict, prod[:C, LANES:], zero).astype(BF16)
        arb_ref[i] = jnp.where(incl, prod[C:, :LANES], zero).astype(BF16)
        ark_ref[i] = jnp.where(incl, prod[C:, LANES:], zero).astype(BF16)
    for i, (c, p) in enumerate(units):
        rows, cols = sl(c, p)
        w2 = _dot(mak_ref[i], _blockdiag_rows(v_ref[rows, cols].astype(BF16)))
        wa_ref[i] = jnp.concatenate([at_ref[rows, cols], w2], axis=1)
    l_bufs, w_bufs = (la_ref, lb_ref), (wa_ref, wb_ref)
    for s in range(nstage):
        l_in, l_out = l_bufs[s % 2], l_bufs[(s + 1) % 2]
        w_in, w_out = w_bufs[s % 2], w_bufs[(s + 1) % 2]
        for i in range(len(units)):
            lmat = l_in[i]
            w = w_in[i]
            w_out[i] = w + _dot(lmat, bd2(w))
            if s < nstage - 1:
                l_out[i] = _dot(lmat, _blockdiag_rows(lmat)).astype(BF16)
    w_fin = w_bufs[nstage % 2]
    for i, (c, p) in enumerate(units):
        rows, cols = sl(c, p)
        w = w_fin[i]
        wb = w.astype(BF16)
        vb = v_ref[rows, cols].astype(BF16)
        gg = _dot(arb_ref[i], bd2(w))
        g1_ref[i] = (rt_ref[rows, cols] + gg[:, :LANES]).astype(BF16)
        g2_ref[i] = gg[:, LANES:] + _dot(ark_ref[i], _blockdiag_rows(vb))
        bk_t = jnp.concatenate([bh_ref[rows, cols], kh_ref[rows, cols]], axis=0).T
        hrhs = jnp.concatenate([wb, jnp.concatenate([jnp.zeros((C, LANES), BF16), vb], axis=1)], axis=0)
        hh = _dot(bk_t.astype(BF16), hrhs)
        pc = pc_ref[c * C:c * C + 1, cols]
        h1 = jnp.where(same_head, hh[:, :LANES], zsq) + jnp.where(diag, jnp.broadcast_to(pc, (LANES, LANES)), zsq)
        h1_ref[i] = h1.astype(BF16)
        h2_ref[i] = jnp.where(same_head, hh[:, LANES:], zsq)
    for c in range(nchunk):
        sts = [st_ref[p].astype(BF16) for p in range(npair)]
        for p in range(npair):
            i = c * npair + p
            rows, cols = sl(c, p)
            y_ref[rows, cols] = _dot(g1_ref[i], sts[p]) + g2_ref[i]
            st_ref[p] = _dot(h1_ref[i], sts[p]) + h2_ref[i]

    y = y_ref[...]
    inv_n = 1.0 / RWKV_HEAD
    mean = head_sum([y.astype(BF16)]) * inv_n
    yc = y - mean
    var = head_sum([(yc * yc).astype(BF16)]) * inv_n
    yn = yc * lax.rsqrt(var + GN_EPS) * lng_ref[...] + lnb_ref[...]
    bonus = head_sum([(r * k2 * rk_ref[...]).astype(BF16)]) * v
    o_ref[0] = ((yn + bonus) * g).astype(BF16)


def _dot_exact_rhs_lhs(ones_bf16, x):
    hi, mid, lo = _split3(x)
    return _dot(ones_bf16, hi) + _dot(ones_bf16, mid) + _dot(ones_bf16, lo)


def _rwkv_call(zr3, tshift_mu, decay_w0, decay_up, iclr_a0, iclr_up, gate_up, k_k, k_a, r_k, lnx_g, lnx_b):
    B, S, _ = zr3.shape
    tt = min(256, S)
    W = RWKV_W
    row = lambda a: a.reshape(1, -1).astype(F32)
    dup = jnp.concatenate([decay_up, jnp.zeros((ICLR_LORA, W), F32)], axis=0)
    iup = jnp.concatenate([jnp.zeros((DECAY_LORA, W), F32), iclr_up], axis=0)
    idx = jnp.arange(2 * LANES)
    seg = (idx[:, None] // RWKV_HEAD == idx[None, :] // RWKV_HEAD).astype(BF16)
    t = jnp.arange(tt)
    same_chunk = t[:, None] // CHUNK == t[None, :] // CHUNK
    tri = (same_chunk & (t[None, :] <= t[:, None])).astype(BF16)
    const = lambda shape: pl.BlockSpec(shape, lambda b, j: (0,) * len(shape))
    kern = functools.partial(_rwkv_kernel, tt=tt)
    nu = (tt // CHUNK) * (RWKV_HEADS // 2)
    return pl.pallas_call(
        kern,
        grid=(B, S // tt),
        in_specs=[pl.BlockSpec((1, tt, RWKV_IN), lambda b, j: (b, j, 0)),
                  const((1, RWKV_IN)), const((1, W)), const((2 * DECAY_LORA, W)), const((1, W)),
                  const((2 * ICLR_LORA, W)), const((GATE_LORA, W)), const((1, W)), const((1, W)),
                  const((1, W)), const((1, W)), const((1, W)),
                  const((2 * LANES, 2 * LANES)), const((tt, tt))],
        out_specs=pl.BlockSpec((1, tt, W), lambda b, j: (b, j, 0)),
        out_shape=jax.ShapeDtypeStruct((B, S, W), BF16),
        scratch_shapes=[pltpu.VMEM((1, RWKV_IN), F32),
                        pltpu.VMEM((RWKV_HEADS // 2, LANES, LANES), F32),
                        pltpu.VMEM((tt, W), F32),
                        pltpu.VMEM((tt, W), F32),
                        pltpu.VMEM((tt, W), BF16),
                        pltpu.VMEM((tt, W), BF16),
                        pltpu.VMEM((tt, W), F32),
                        pltpu.VMEM((tt, W), F32),
                        pltpu.VMEM((tt, W), F32),
                        pltpu.VMEM((tt, W), F32),
                        pltpu.VMEM((tt, W), F32),
                        pltpu.VMEM((nu, CHUNK, LANES), BF16),
                        pltpu.VMEM((nu, CHUNK, LANES), BF16),
                        pltpu.VMEM((nu, CHUNK, LANES), BF16),
                        pltpu.VMEM((nu, CHUNK, LANES), BF16),
                        pltpu.VMEM((nu, CHUNK, LANES), BF16),
                        pltpu.VMEM((nu, CHUNK, 2 * LANES), F32),
                        pltpu.VMEM((nu, CHUNK, 2 * LANES), F32),
                        pltpu.VMEM((nu, CHUNK, LANES), BF16),
                        pltpu.VMEM((nu, CHUNK, LANES), F32),
                        pltpu.VMEM((nu, LANES, LANES), BF16),
                        pltpu.VMEM((nu, LANES, LANES), F32)],
        compiler_params=_cparams(("parallel", "arbitrary")),
        name="rwkv",
    )(zr3, row(tshift_mu), row(decay_w0), dup, row(iclr_a0), iup.astype(BF16), gate_up.astype(BF16), row(k_k),
      row(k_a), row(r_k), row(lnx_g), row(lnx_b), seg, tri)


def _merge_kernel(x_ref, attn_ref, rw_ref, ga_ref, gr_ref, wa_ref, wr_ref, wo_ref, g1_ref,
                  n2_ref, sc_ref, sh_ref, rwt_ref, rb_ref, x1_ref, h2_ref, gt_ref, cnt_ref):
    a = _dot(attn_ref[...], wa_ref[...])
    rr = _dot(rw_ref[...], wr_ref[...])
    mixed = _sigmoid(ga_ref[...].astype(F32)) * a + _sigmoid(gr_ref[...].astype(F32)) * rr
    x1 = x_ref[...] + g1_ref[0] * _dot(mixed.astype(BF16), wo_ref[...])
    x1_ref[...] = x1
    ms = jnp.mean(x1 * x1, axis=-1, keepdims=True)
    h2 = x1 * lax.rsqrt(ms + RMS_EPS) * n2_ref[...]
    h2 = h2 * (1.0 + sc_ref[0]) + sh_ref[0]
    h2_ref[...] = _pack_bf16_pairs(h2)

    tm = x1.shape[0]
    E, G, EG = N_EXPERTS, N_GROUPS, N_EXPERTS // N_GROUPS
    scores = _sigmoid(_dot3(rwt_ref[...], h2, nt=True))
    choice = scores + rb_ref[...]
    c3 = choice.reshape(G, EG, tm)
    e_i = lax.broadcasted_iota(I32, (G, EG, tm), 1)
    m1 = jnp.max(c3, axis=1, keepdims=True)
    first = jnp.min(jnp.where(c3 == m1, e_i, EG), axis=1, keepdims=True)
    m2 = jnp.max(jnp.where(e_i == first, -jnp.inf, c3), axis=1, keepdims=True)
    grp = (m1 + m2).reshape(G, tm)
    g_i = lax.broadcasted_iota(I32, (G, tm), 0)
    rank = jnp.zeros((G, tm), I32)
    for o in range(G):
        other = grp[o:o + 1, :]
        rank = rank + jnp.where((other > grp) | ((other == grp) & (o < g_i)), 1, 0)
    gsel = rank < TOPK_GROUPS
    esel = jnp.broadcast_to(gsel.reshape(G, 1, tm), (G, EG, tm)).reshape(E, tm)
    mc = jnp.where(esel, choice, -jnp.inf)
    x_i = lax.broadcasted_iota(I32, (E, tm), 0)
    erank = jnp.zeros((E, tm), I32)
    for o in range(E):
        other = mc[o:o + 1, :]
        erank = erank + jnp.where((other > mc) | ((other == mc) & (o < x_i)), 1, 0)
    top = erank < MOE_TOPK
    gw = jnp.where(top, scores, 0.0)
    gw = gw / jnp.sum(gw, axis=0, keepdims=True) * ROUTED_SCALE
    gt_ref[...] = gw
    sel = jnp.where(gw > 0.0, 1.0, 0.0)
    cnt_ref[0] = jnp.broadcast_to(jnp.sum(sel, axis=1, keepdims=True), (E, LANES))


def _pack_bf16_pairs(x):
    n = x.shape[1] // 2
    bits = pltpu.bitcast(x.astype(BF16).astype(F32), I32)
    return bits[:, :n] | lax.shift_right_logical(bits[:, n:], 16)


def _unpack_bf16_pairs(p):
    hi = pltpu.bitcast(p & jnp.int32(-65536), F32)
    lo = pltpu.bitcast(lax.shift_left(p, 16), F32)
    return jnp.concatenate([hi, lo], axis=1).astype(BF16)


def _merge_call(x2, attn, rw, ga, gr, wa, wr, wo, g1, norm2_g, sc2, sh2, router_w, router_bias, S):
    T, D = x2.shape
    B = T // S
    tm = min(MERGE_TILE, S)
    tpb = S // tm
    nt = T // tm
    E = N_EXPERTS
    row = lambda i: (i, 0)
    per_b = lambda i: (i // tpb, 0, 0)
    const = lambda shape: pl.BlockSpec(shape, lambda i: (0,) * len(shape))
    return pl.pallas_call(
        _merge_kernel,
        grid=(nt,),
        in_specs=[pl.BlockSpec((tm, D), row), pl.BlockSpec((tm, ATTN_W), row), pl.BlockSpec((tm, RWKV_W), row),
                  pl.BlockSpec((tm, D), row), pl.BlockSpec((tm, D), row),
                  const((ATTN_W, D)), const((RWKV_W, D)), const((D, D)),
                  pl.BlockSpec((1, 1, D), per_b), const((1, D)),
                  pl.BlockSpec((1, 1, D), per_b), pl.BlockSpec((1, 1, D), per_b),
                  const((E, D)), const((E, 1))],
        out_specs=[pl.BlockSpec((tm, D), row), pl.BlockSpec((tm, D // 2), row),
                   pl.BlockSpec((E, tm), lambda i: (0, i)), pl.BlockSpec((1, E, LANES), lambda i: (i, 0, 0))],
        out_shape=[jax.ShapeDtypeStruct((T, D), F32), jax.ShapeDtypeStruct((T, D // 2), I32),
                   jax.ShapeDtypeStruct((E, T), F32), jax.ShapeDtypeStruct((nt, E, LANES), F32)],
        compiler_params=_cparams(("parallel",)),
        name="merge",
    )(x2, attn, rw, ga, gr, wa, wr, wo, g1.reshape(B, 1, D), norm2_g.reshape(1, D),
      sc2.reshape(B, 1, D), sh2.reshape(B, 1, D), router_w.T, router_bias.reshape(E, 1))


MERGE_TILE = 512
MOE_BLOCK = 1024
SC_CORES, SC_SUBCORES = 2, 16
SC_WORKERS = SC_CORES * SC_SUBCORES
SC_ROWS = 128


def _plan_kernel(gt_ref, cnt_ref, upper_ref, lowe_ref, dest_ref, gw_ref, be_ref, off_ref, *, tm, n_blocks):
    i = pl.program_id(0)
    E = N_EXPERTS
    lowe = lowe_ref[...]

    @pl.when(i == 0)
    def _():
        total = jnp.sum(cnt_ref[...], axis=0)
        nblk = jnp.floor((total + (MOE_BLOCK - 1)) * (1.0 / MOE_BLOCK))
        start_blk = _dot_exact_rhs_lhs(lowe, nblk)
        off_ref[...] = start_blk * MOE_BLOCK
        end_blk = start_blk + nblk
        b_i = lax.broadcasted_iota(I32, (E, n_blocks), 1).astype(F32)
        e_of_b = jnp.sum(jnp.where(end_blk[:, :1] <= b_i, 1.0, 0.0), axis=0, keepdims=True)
        be_ref[...] = e_of_b.astype(I32)

    gt = gt_ref[...]
    sel = gt > 0.0
    selb = jnp.where(sel, 1.0, 0.0).astype(BF16)
    rank = _dot(selb, upper_ref[...])
    dest = off_ref[:, :1] + rank
    off_ref[...] = off_ref[...] + cnt_ref[i]
    kth = _dot(lowe, selb)
    dests, gws = [], []
    for k in range(MOE_TOPK):
        m = sel & (kth == float(k))
        have = jnp.sum(jnp.where(m, 1.0, 0.0), axis=0, keepdims=True)
        d = jnp.sum(jnp.where(m, dest, 0.0), axis=0, keepdims=True)
        dests.append(jnp.where(have > 0.0, d, float((n_blocks - 1) * MOE_BLOCK)))
        gws.append(jnp.sum(jnp.where(m, gt, 0.0), axis=0, keepdims=True))
    dest_ref[...] = jnp.concatenate(dests, axis=0).astype(I32)
    gpad = jnp.concatenate(gws + [jnp.zeros((LANES - MOE_TOPK, tm), F32)], axis=0)
    gw_ref[...] = gpad.T


def _plan_call(gate_t, cnt, n_blocks, tile0):
    E = gate_t.shape[0]
    nt = cnt.shape[0]
    tm = MERGE_TILE
    T = nt * tm
    idx = jnp.arange(tm)
    upper = (idx[:, None] < idx[None, :]).astype(BF16)
    ei = jnp.arange(E)
    lowe = (ei[None, :] < ei[:, None]).astype(BF16)
    kern = functools.partial(_plan_kernel, tm=tm, n_blocks=n_blocks)
    const = lambda shape: pl.BlockSpec(shape, lambda i: (0,) * len(shape))
    return pl.pallas_call(
        kern,
        grid=(nt,),
        in_specs=[pl.BlockSpec((E, tm), lambda i: (0, i + tile0)), const((nt, E, LANES)), const((tm, tm)),
                  const((E, E))],
        out_specs=[pl.BlockSpec((MOE_TOPK, tm), lambda i: (0, i)), pl.BlockSpec((tm, LANES), lambda i: (i, 0)),
                   const((1, n_blocks))],
        out_shape=[jax.ShapeDtypeStruct((MOE_TOPK, T), I32), jax.ShapeDtypeStruct((T, LANES), F32),
                   jax.ShapeDtypeStruct((1, n_blocks), I32)],
        scratch_shapes=[pltpu.VMEM((E, LANES), F32)],
        compiler_params=_cparams(("arbitrary",)),
        name="plan",
    )(gate_t, cnt, upper, lowe)


def _sc_index_layout(dest_t):
    K, T = dest_t.shape
    n_ch = T // (SC_WORKERS * SC_ROWS)
    return dest_t.reshape(K, SC_WORKERS, n_ch, SC_ROWS).transpose(1, 2, 0, 3).reshape(SC_WORKERS, n_ch * K, SC_ROWS)


def _sc_dispatch(rows, idx, n_slots, tok0):
    W = rows.shape[1]
    n_ch = idx.shape[1] // MOE_TOPK
    T = n_ch * SC_WORKERS * SC_ROWS
    tpw = T // SC_WORKERS
    mesh = plsc.VectorSubcoreMesh(core_axis_name="c", subcore_axis_name="s")

    @functools.partial(
        pl.kernel, mesh=mesh,
        out_type=jax.ShapeDtypeStruct((n_slots, W), I32),
        scratch_types=[pltpu.VMEM((n_ch * MOE_TOPK, SC_ROWS), I32), pltpu.VMEM((SC_ROWS, W), I32),
                       pltpu.SemaphoreType.DMA])
    def kern(x_hbm, idx_hbm, o_hbm, idx_v, rows_v, sem):
        wid = lax.axis_index("s") * SC_CORES + lax.axis_index("c")
        pltpu.sync_copy(idx_hbm.at[wid], idx_v)

        @pl.loop(0, n_ch)
        def _(j):
            pltpu.sync_copy(x_hbm.at[pl.ds(tok0 + wid * tpw + j * SC_ROWS, SC_ROWS)], rows_v)
            copies = [pltpu.async_copy(rows_v, o_hbm.at[idx_v.at[j * MOE_TOPK + k]], sem)
                      for k in range(MOE_TOPK)]
            for cp in copies:
                cp.wait()

    return kern(rows, idx)


def _sc_combine(slots, idx, T):
    _, W = slots.shape
    n_ch = T // (SC_WORKERS * SC_ROWS)
    tpw = T // SC_WORKERS
    mesh = plsc.VectorSubcoreMesh(core_axis_name="c", subcore_axis_name="s")

    @functools.partial(
        pl.kernel, mesh=mesh,
        out_type=jax.ShapeDtypeStruct((MOE_TOPK, T, W), I32),
        scratch_types=[pltpu.VMEM((n_ch * MOE_TOPK, SC_ROWS), I32), pltpu.VMEM((SC_ROWS, W), I32),
                       pltpu.SemaphoreType.DMA])
    def kern(s_hbm, idx_hbm, o_hbm, idx_v, rows_v, sem):
        wid = lax.axis_index("s") * SC_CORES + lax.axis_index("c")
        pltpu.sync_copy(idx_hbm.at[wid], idx_v)

        @pl.loop(0, n_ch)
        def _(j):
            for k in range(MOE_TOPK):
                pltpu.async_copy(s_hbm.at[idx_v.at[j * MOE_TOPK + k]], rows_v, sem).wait()
                pltpu.sync_copy(rows_v, o_hbm.at[k, pl.ds(wid * tpw + j * SC_ROWS, SC_ROWS)])

    return kern(slots, idx)


def _ffn_kernel(be_ref, x_ref, eg_ref, eu_ref, ed_ref, o_ref):
    used = be_ref[pl.program_id(0)] < N_EXPERTS

    @pl.when(used)
    def _():
        x = _unpack_bf16_pairs(x_ref[...])
        a = _dot(x, eg_ref[0])
        u = _dot(x, eu_ref[0])
        o_ref[...] = _pack_bf16_pairs(_dot((a * _sigmoid(a) * u).astype(BF16), ed_ref[0]))

    @pl.when(jnp.logical_not(used))
    def _():
        o_ref[...] = jnp.zeros(o_ref.shape, I32)


def _ffn_call(xs, block_e, eg, eu, ed, n_blocks):
    P, W = xs.shape
    D, FF = 2 * W, EXPERT_FF
    grid_spec = pltpu.PrefetchScalarGridSpec(
        num_scalar_prefetch=1,
        grid=(n_blocks,),
        in_specs=[pl.BlockSpec((MOE_BLOCK, W), lambda b, be: (b, 0)),
                  pl.BlockSpec((1, D, FF), lambda b, be: (jnp.minimum(be[b], N_EXPERTS - 1), 0, 0)),
                  pl.BlockSpec((1, D, FF), lambda b, be: (jnp.minimum(be[b], N_EXPERTS - 1), 0, 0)),
                  pl.BlockSpec((1, FF, D), lambda b, be: (jnp.minimum(be[b], N_EXPERTS - 1), 0, 0))],
        out_specs=pl.BlockSpec((MOE_BLOCK, W), lambda b, be: (b, 0)))
    return pl.pallas_call(
        _ffn_kernel,
        grid_spec=grid_spec,
        out_shape=jax.ShapeDtypeStruct((P, W), I32),
        compiler_params=_cparams(("parallel",)),
        name="ffn",
    )(block_e, xs, eg, eu, ed)


def _final_kernel(h_ref, c_ref, gw_ref, x1_ref, g2_ref, fg_ref, sg_ref, su_ref, sd_ref, o_ref):
    h = _unpack_bf16_pairs(h_ref[...])
    a = _dot(h, sg_ref[...])
    u = _dot(h, su_ref[...])
    moe = _dot((a * _sigmoid(a) * u).astype(BF16), sd_ref[...])
    gw = gw_ref[...]
    for k in range(MOE_TOPK):
        w = gw[:, k:k + 1]
        y = _unpack_bf16_pairs(c_ref[k]).astype(F32)
        moe = moe + jnp.where(w > 0.0, w * y, 0.0)
    x2 = x1_ref[...] + g2_ref[0] * moe
    ms = jnp.mean(x2 * x2, axis=-1, keepdims=True)
    o_ref[...] = x2 * lax.rsqrt(ms + RMS_EPS) * fg_ref[...]


def _final_call(h2p, comb, gw, x1, g2, final_g, sg, su, sd, S, tile0, prev_out):
    T, W = h2p.shape
    D, FF = 2 * W, EXPERT_FF
    B = T // S
    tm = MERGE_TILE
    tpb = S // tm
    ntile = comb.shape[1] // tm
    row = lambda i: (i, 0)
    full_row = lambda i: (i + tile0, 0)
    const = lambda shape: pl.BlockSpec(shape, lambda i: (0,) * len(shape))
    in_specs = [pl.BlockSpec((tm, W), full_row), pl.BlockSpec((MOE_TOPK, tm, W), lambda i: (0, i, 0)),
                pl.BlockSpec((tm, LANES), row), pl.BlockSpec((tm, D), full_row),
                pl.BlockSpec((1, 1, D), lambda i: ((i + tile0) // tpb, 0, 0)), const((1, D)),
                const((D, FF)), const((D, FF)), const((FF, D))]
    args = [h2p, comb, gw, x1, g2.reshape(B, 1, D), final_g.reshape(1, D), sg, su, sd]
    kern, aliases = _final_kernel, {}
    if prev_out is not None:
        in_specs.append(pl.BlockSpec(memory_space=pl.ANY))
        args.append(prev_out)
        aliases = {len(args) - 1: 0}
        kern = lambda *refs: _final_kernel(*refs[:9], refs[10])
    return pl.pallas_call(
        kern,
        grid=(ntile,),
        in_specs=in_specs,
        out_specs=pl.BlockSpec((tm, D), full_row),
        out_shape=jax.ShapeDtypeStruct((T, D), F32),
        input_output_aliases=aliases,
        compiler_params=_cparams(("parallel",)),
        name="final",
    )(*args)


MOE_GROUPS = 2


def _moe_call(h2p, gate_t, cnt, x1, g2, final_g, eg, eu, ed, sg, su, sd, S):
    T = h2p.shape[0]
    nt = cnt.shape[0]
    groups = MOE_GROUPS if (T // MOE_GROUPS) % (SC_WORKERS * SC_ROWS) == 0 and nt % MOE_GROUPS == 0 else 1
    tg, ntg = T // groups, nt // groups
    n_blocks = (tg * MOE_TOPK) // MOE_BLOCK + N_EXPERTS + 1
    out = None
    for g in range(groups):
        dest_t, gw, block_e = _plan_call(gate_t, cnt[g * ntg:(g + 1) * ntg], n_blocks, g * ntg)
        idx = _sc_index_layout(dest_t)
        xs = _sc_dispatch(h2p, idx, n_blocks * MOE_BLOCK, g * tg)
        ys = _ffn_call(xs, block_e.reshape(n_blocks), eg, eu, ed, n_blocks)
        comb = _sc_combine(ys, idx, tg)
        out = _final_call(h2p, comb, gw, x1, g2, final_g, sg, su, sd, S, g * ntg, out)
    return out


def _layer(x2, c, S, ada_w, ada_b, norm1_g, w_in, rel_bias, tshift_mu, decay_w0, decay_up, iclr_a0, iclr_up,
           gate_up, k_k, k_a, r_k, lnx_g, lnx_b, w_attn_br, w_rwkv_br, w_out, norm2_g, router_w, router_bias,
           exp_gate, exp_up, exp_down, sh_gate, sh_up, sh_down, final_g):
    T, D = x2.shape
    B = T // S
    mod = _mod_call(c, ada_w, ada_b)
    sh1, sc1, g1, sh2, sc2, g2 = jnp.split(mod, 6, axis=-1)

    q, k, vt, iq, ik4, iwt, zr, ga, gr = _inproj_call(x2, norm1_g, sc1, sh1, w_in, S)

    ta = min(512, S)
    assert ta >= LANES and S % ta == 0
    top_k = min(TOPK_MAX, S // 4)
    seq = lambda a: a.reshape(B, S, a.shape[-1])
    mask = _index_call(seq(iq), iwt, seq(ik4), ta, top_k)
    bias_tiles = _bias_call(rel_bias, ta)
    attn = _attn_call(seq(q), seq(k), vt, mask, bias_tiles, rel_bias, ta).reshape(T, ATTN_W)

    rw = _rwkv_call(zr.reshape(B, S, RWKV_IN), tshift_mu, decay_w0, decay_up, iclr_a0, iclr_up, gate_up,
                    k_k, k_a, r_k, lnx_g, lnx_b).reshape(T, RWKV_W)

    x1, h2p, gate_t, cnt = _merge_call(x2, attn, rw, ga, gr, w_attn_br.astype(BF16), w_rwkv_br.astype(BF16),
                                       w_out.astype(BF16), g1, norm2_g, sc2, sh2, router_w, router_bias, S)
    return _moe_call(h2p, gate_t, cnt, x1, g2, final_g, exp_gate.astype(BF16), exp_up.astype(BF16),
                     exp_down.astype(BF16), sh_gate.astype(BF16), sh_up.astype(BF16), sh_down.astype(BF16), S)


def kernel(x, c, ada_w, ada_b, norm1_g, w_in, rel_bias, tshift_mu, decay_w0, decay_up, iclr_a0, iclr_up, gate_up, k_k, k_a, r_k, lnx_g, lnx_b, w_attn_br, w_rwkv_br, w_out, norm2_g, router_w, router_bias, exp_gate, exp_up, exp_down, sh_gate, sh_up, sh_down, final_g):
    B, S, D = x.shape
    depth = ada_w.shape[0]
    assert depth == 1, "the final RMSNorm is fused into the (single) layer's MoE kernel"
    out = _layer(x.reshape(B * S, D), c, S, ada_w[0], ada_b[0], norm1_g[0], w_in[0], rel_bias, tshift_mu[0],
                 decay_w0[0], decay_up[0], iclr_a0[0], iclr_up[0], gate_up[0], k_k[0], k_a[0], r_k[0],
                 lnx_g[0], lnx_b[0], w_attn_br[0], w_rwkv_br[0], w_out[0], norm2_g[0], router_w[0],
                 router_bias[0], exp_gate[0], exp_up[0], exp_down[0], sh_gate[0], sh_up[0], sh_down[0], final_g)
    return out.reshape(B, S, D)
```

```python
import functools
import math

import jax
import jax.numpy as jnp
from jax import lax
from jax.experimental import pallas as pl
from jax.experimental.pallas import tpu as pltpu
from jax.experimental.pallas import tpu_sc as plsc

F32 = jnp.float32
BF16 = jnp.bfloat16
I32 = jnp.int32
I16 = jnp.int16

RMS_EPS = 1e-6
D_MODEL = 1024
N_ATTN_HEADS = 8
ATTN_HEAD_DIM = 64
ATTN_W = 512
IDX_HEADS = 16
IDX_DIM = 32
IDX_Q = 512
TOPK_MAX = 256
N_BUCKETS = 32
MAX_DISTANCE = 128
RWKV_HEADS = 8
RWKV_HEAD = 64
RWKV_W = 512
DECAY_LORA = 64
ICLR_LORA = 64
GATE_LORA = 128
RWKV_IN = 1792
GN_EPS = 64e-5
N_EXPERTS = 64
N_GROUPS = 8
TOPK_GROUPS = 4
MOE_TOPK = 8
EXPERT_FF = 256
ROUTED_SCALE = 2.5

LANES = 128
VMEM_LIMIT = 56 * 1024 * 1024
CHUNK = 64
LOG2E = 1.4426950408889634
INT_MIN = -2147483648
KEY_NEG_INF = -2139095041

NT_DIMS = (((1,), (1,)), ((), ()))


def _cparams(sem):
    return pltpu.CompilerParams(dimension_semantics=sem, vmem_limit_bytes=VMEM_LIMIT)


def _dot(a, b):
    return jnp.dot(a, b, preferred_element_type=F32)


def _dot_nt(a, b):
    return lax.dot_general(a, b, NT_DIMS, preferred_element_type=F32)


def _split2(x):
    hi = x.astype(BF16)
    lo = (x - hi.astype(F32)).astype(BF16)
    return hi, lo


def _split3(x):
    hi = x.astype(BF16)
    r1 = x - hi.astype(F32)
    mid = r1.astype(BF16)
    lo = (r1 - mid.astype(F32)).astype(BF16)
    return hi, mid, lo


def _dot_exact_rhs(x, ones_bf16, terms=2):
    parts = _split3(x) if terms == 3 else _split2(x)
    out = _dot(parts[0], ones_bf16)
    for p in parts[1:]:
        out = out + _dot(p, ones_bf16)
    return out


def _dot3(a, b, nt=False):
    ah, al = _split2(a)
    bh, bl = _split2(b)
    f = _dot_nt if nt else _dot
    return f(ah, bh) + f(ah, bl) + f(al, bh)


def _sigmoid(x):
    return 1.0 / (1.0 + jnp.exp(-x))


def _gate_sigmoid(x):
    return 0.5 * jnp.tanh(0.5 * x) + 0.5


def _mod_kernel(c_ref, w_ref, b_ref, o_ref):
    c = c_ref[...]
    s = c * _sigmoid(c)
    o_ref[...] = _dot3(s, w_ref[...]) + b_ref[...]


def _mod_call(c, ada_w, ada_b):
    B, D = c.shape
    N = ada_w.shape[1]
    tn = 1024
    return pl.pallas_call(
        _mod_kernel,
        grid=(N // tn,),
        in_specs=[pl.BlockSpec((B, D), lambda j: (0, 0)),
                  pl.BlockSpec((D, tn), lambda j: (0, j)),
                  pl.BlockSpec((1, tn), lambda j: (0, j))],
        out_specs=pl.BlockSpec((B, tn), lambda j: (0, j)),
        out_shape=jax.ShapeDtypeStruct((B, N), F32),
        compiler_params=_cparams(("arbitrary",)),
        name="mod",
    )(c, ada_w, ada_b.reshape(1, N))


_OFF_Q, _OFF_K, _OFF_IQ, _OFF_IK4, _OFF_ZR, _OFF_GA, _OFF_GR, _N_PACK = (
    0, 512, 1024, 1536, 2048, 3840, 4864, 5888)
IDX_PER_BLOCK = LANES // IDX_DIM


def _pack_w_in(w_in):
    D = w_in.shape[0]
    w_ik = w_in[:, 2048:2080]
    ik4 = jnp.zeros((D, IDX_PER_BLOCK * LANES), w_in.dtype)
    for j in range(IDX_PER_BLOCK):
        ik4 = lax.dynamic_update_slice(ik4, w_ik, (0, j * LANES + j * IDX_DIM))
    w_pack = jnp.concatenate([w_in[:, 0:1024], w_in[:, 1536:2048], ik4, w_in[:, 2096:]], axis=1).astype(BF16)
    return w_pack, w_in[:, 1024:1536].T.astype(BF16), w_in[:, 2080:2096].T.astype(BF16)


def _inproj_kernel(x_ref, g_ref, sc_ref, sh_ref, w_ref, wvt_ref, wiwt_ref,
                   q_ref, k_ref, vt_ref, iq_ref, ik4_ref, iwt_ref, zr_ref, ga_ref, gr_ref):
    x = x_ref[...]
    ms = jnp.mean(x * x, axis=-1, keepdims=True)
    h = x * lax.rsqrt(ms + RMS_EPS) * g_ref[...]
    h = h * (1.0 + sc_ref[0]) + sh_ref[0]
    hb = h.astype(BF16)

    def proj(lo, hi):
        return _dot(hb, w_ref[:, lo:hi])

    q_ref[...] = (proj(_OFF_Q, _OFF_K) * (ATTN_HEAD_DIM ** -0.5 * LOG2E)).astype(BF16)
    k_ref[...] = proj(_OFF_K, _OFF_IQ).astype(BF16)
    iq_ref[...] = proj(_OFF_IQ, _OFF_IK4).astype(BF16)
    ik4_ref[...] = proj(_OFF_IK4, _OFF_ZR).astype(BF16)
    zr_ref[...] = proj(_OFF_ZR, _OFF_GA)
    ga_ref[...] = proj(_OFF_GA, _OFF_GR).astype(BF16)
    gr_ref[...] = proj(_OFF_GR, _N_PACK).astype(BF16)
    vt_ref[0] = _dot_nt(wvt_ref[...], hb).astype(BF16)
    iwt_ref[0] = _dot_nt(wiwt_ref[...], hb)


def _inproj_call(x2, norm_g, sc, sh, w_in, S):
    T, D = x2.shape
    B = T // S
    tm = min(512, S)
    tpb = S // tm
    w_pack, wvt, wiwt = _pack_w_in(w_in)
    row = lambda i: (i, 0)
    per_b = lambda i: (i // tpb, 0, 0)
    colblk = lambda i: (i // tpb, 0, i % tpb)
    const = lambda shape: pl.BlockSpec(shape, lambda i: (0,) * len(shape), pipeline_mode=pl.Buffered(1))
    rows_out = ((512, BF16), (512, BF16), (512, BF16), (512, BF16), (RWKV_IN, F32), (D, BF16), (D, BF16))
    out_specs = [pl.BlockSpec((tm, w), row) for w, _ in rows_out]
    out_shape = [jax.ShapeDtypeStruct((T, w), dt) for w, dt in rows_out]
    out_specs[2:2] = [pl.BlockSpec((1, ATTN_W, tm), colblk)]
    out_shape[2:2] = [jax.ShapeDtypeStruct((B, ATTN_W, S), BF16)]
    out_specs[5:5] = [pl.BlockSpec((1, IDX_HEADS, tm), colblk)]
    out_shape[5:5] = [jax.ShapeDtypeStruct((B, IDX_HEADS, S), F32)]
    return pl.pallas_call(
        _inproj_kernel,
        grid=(T // tm,),
        in_specs=[pl.BlockSpec((tm, D), row),
                  pl.BlockSpec((1, D), lambda i: (0, 0)),
                  pl.BlockSpec((1, 1, D), per_b),
                  pl.BlockSpec((1, 1, D), per_b),
                  const((D, _N_PACK)), const((ATTN_W, D)), const((IDX_HEADS, D))],
        out_specs=out_specs,
        out_shape=out_shape,
        compiler_params=_cparams(("parallel",)),
        name="inproj",
    )(x2, norm_g.reshape(1, D), sc.reshape(B, 1, D), sh.reshape(B, 1, D), w_pack, wvt, wiwt)


def _t5_bucket(rel):
    n = jnp.maximum(rel, 0)
    max_exact = N_BUCKETS // 2
    nf = jnp.maximum(n, 1).astype(F32)
    large = max_exact + (jnp.log(nf / max_exact) / math.log(MAX_DISTANCE / max_exact)
                         * (N_BUCKETS - max_exact)).astype(I32)
    large = jnp.minimum(large, N_BUCKETS - 1)
    return jnp.where(n < max_exact, n, large)


def _bias_kernel(bucket_ref, rb_ref, o_ref):
    h = pl.program_id(0)
    bk = bucket_ref[...]
    out = jnp.zeros(bk.shape, F32)
    for b in range(N_BUCKETS):
        out = jnp.where(bk == b, rb_ref[b, h] * LOG2E, out)
    o_ref[0] = out


def _bias_call(rel_bias, tq):
    r = jnp.arange(tq, dtype=I32)[None, :]
    c = jnp.arange(tq, dtype=I32)[:, None]
    buckets = jnp.stack([_t5_bucket(r - c), _t5_bucket(tq + r - c)])
    return pl.pallas_call(
        _bias_kernel,
        grid=(N_ATTN_HEADS,),
        in_specs=[pl.BlockSpec((2, tq, tq), lambda h: (0, 0, 0)),
                  pl.BlockSpec(memory_space=pltpu.SMEM)],
        out_specs=pl.BlockSpec((1, 2, tq, tq), lambda h: (h, 0, 0, 0)),
        out_shape=jax.ShapeDtypeStruct((N_ATTN_HEADS, 2, tq, tq), F32),
        compiler_params=_cparams(("arbitrary",)),
        name="bias",
    )(buckets, rel_bias)


def _index_kernel(iq_ref, iwt_ref, ik4_ref, lower_ref, mask_ref, key_ref, k16_ref, *, t, nk, top_k, scale):
    qi = pl.program_id(1)
    nkt = qi + 1
    ksub = LANES
    qpos = qi * t + lax.broadcasted_iota(I32, (ksub, t), 1)

    def score_tile(kt, carry):
        kbase = pl.multiple_of(kt * t, t)
        for ks in range(t // ksub):
            acc = jnp.zeros((ksub, t), F32)
            ik_rows = ik4_ref[0, pl.ds(kbase + ks * ksub, ksub), :]
            ik_stack = jnp.concatenate([ik_rows[:, j * LANES:(j + 1) * LANES] for j in range(IDX_PER_BLOCK)],
                                       axis=0)
            for g in range(IDX_HEADS // IDX_PER_BLOCK):
                d4 = _dot_nt(ik_stack, iq_ref[0, :, g * LANES:(g + 1) * LANES])
                for j in range(IDX_PER_BLOCK):
                    h = g * IDX_PER_BLOCK + j
                    acc = acc + jnp.maximum(d4[j * ksub:(j + 1) * ksub], 0.0) * iwt_ref[0, h:h + 1, :]
            s = acc * scale
            kpos = kt * t + ks * ksub + lax.broadcasted_iota(I32, (ksub, t), 0)
            s = jnp.where(kpos <= qpos, s, -jnp.inf)
            bits = pltpu.bitcast(s, I32)
            key = bits ^ ((bits >> 31) & 0x7FFFFFFF)
            key_ref[kt, ks * ksub:(ks + 1) * ksub, :] = key
            k16_ref[kt, ks * ksub:(ks + 1) * ksub, :] = (key >> 16).astype(I16)
        return carry

    lax.fori_loop(0, nkt, score_tile, 0)

    pack = 16

    def search16():
        def bit_body(i, ans):
            cand = ans | lax.shift_left(jnp.int32(1), 15 - i)
            cand16 = (cand - 32768).astype(I16)

            def cnt_body(kt, acc):
                one = jnp.where(k16_ref[kt] >= cand16, jnp.int16(1), jnp.int16(0))
                for r in range(t // pack):
                    acc = acc + one[r * pack:(r + 1) * pack, :]
                return acc

            acc = lax.fori_loop(0, nkt, cnt_body, jnp.zeros((pack, t), I16))
            cnt = jnp.sum(acc.astype(I32), axis=0, keepdims=True)
            return jnp.where(cnt >= top_k, cand, ans)

        return lax.fori_loop(0, 16, bit_body, jnp.zeros((1, t), I32))

    hi = search16() - 32768

    def remap_body(kt, carry):
        key = key_ref[kt]
        khi = key >> 16
        lo = (key & 0xFFFF) - 32768
        k16_ref[kt] = jnp.where(khi > hi, 32767, jnp.where(khi == hi, lo, -32768)).astype(I16)
        return carry

    lax.fori_loop(0, nkt, remap_body, 0)
    thr = hi * 65536 + search16()

    def count(pred):
        def body(kt, acc):
            one = jnp.where(pred(key_ref[kt]), 1.0, 0.0)
            return acc + jnp.sum(one.reshape(t // 8, 8, t), axis=0)
        return jnp.sum(lax.fori_loop(0, nkt, body, jnp.zeros((8, t), F32)), axis=0, keepdims=True)

    n_ge = count(lambda keys: keys >= thr)
    has_tie = jnp.max(n_ge) > float(top_k)

    @pl.when(jnp.logical_not(has_tie))
    def _():
        def mask_body(kt, carry):
            keys = key_ref[kt]
            sel = (keys >= thr) & (keys > KEY_NEG_INF)
            mask_ref[0, 0, kt] = jnp.where(sel, 0.0, -jnp.inf).astype(BF16)
            return carry

        lax.fori_loop(0, nkt, mask_body, 0)

    @pl.when(has_tie)
    def _():
        need = float(top_k) - count(lambda keys: keys > thr)

        def mask_body(kt, seen):
            keys = key_ref[kt]
            tie = (keys == thr) & (keys > KEY_NEG_INF)
            tie_b = jnp.where(tie, 1.0, 0.0).astype(BF16)
            before = seen + _dot(lower_ref[...], tie_b)
            sel = (keys > thr) | (tie & (before < need))
            mask_ref[0, 0, kt] = jnp.where(sel, 0.0, -jnp.inf).astype(BF16)
            return seen + jnp.sum(tie_b.astype(F32).reshape(t // 8, 8, t).sum(axis=0), axis=0, keepdims=True)

        lax.fori_loop(0, nkt, mask_body, jnp.zeros((1, t), F32))

    def fill_body(kt, carry):
        mask_ref[0, 0, kt] = jnp.full((t, t), -jnp.inf, BF16)
        return carry

    lax.fori_loop(nkt, nk, fill_body, 0)


def _index_call(iq, iwt, ik4, t, top_k):
    B, S, _ = iq.shape
    n = S // t
    scale = (IDX_HEADS ** -0.5) * (IDX_DIM ** -0.5)
    kern = functools.partial(_index_kernel, t=t, nk=n, top_k=top_k, scale=scale)
    pos = jnp.arange(t)
    lower = (pos[None, :] < pos[:, None]).astype(BF16)
    return pl.pallas_call(
        kern,
        grid=(B, n),
        in_specs=[pl.BlockSpec((1, t, IDX_Q), lambda b, i: (b, i, 0)),
                  pl.BlockSpec((1, IDX_HEADS, t), lambda b, i: (b, 0, i)),
                  pl.BlockSpec((1, S, IDX_PER_BLOCK * LANES), lambda b, i: (b, 0, 0)),
                  pl.BlockSpec((t, t), lambda b, i: (0, 0))],
        out_specs=pl.BlockSpec((1, 1, n, t, t), lambda b, i: (b, i, 0, 0, 0)),
        out_shape=jax.ShapeDtypeStruct((B, n, n, t, t), BF16),
        scratch_shapes=[pltpu.VMEM((n, t, t), I32), pltpu.VMEM((n, t, t), I16)],
        compiler_params=_cparams(("parallel", "arbitrary")),
        name="index",
    )(iq, iwt, ik4, lower)


ONES_ROWS = 16


def _attn_kernel(qi_tab, kt_tab, q_ref, k_ref, vt_ref, mask_ref, bias_ref, rb_ref, o_ref,
                 qz_ref, m_ref, acc_ref, s_ref, *, t):
    s_id = pl.program_id(1)
    qi = qi_tab[s_id]
    kt = kt_tab[s_id]
    dh = ATTN_HEAD_DIM

    @pl.when(kt == 0)
    def _():
        m_ref[...] = jnp.full(m_ref.shape, -jnp.inf, F32)
        acc_ref[...] = jnp.zeros(acc_ref.shape, F32)
        lane = lax.broadcasted_iota(I32, (t, LANES), 1)
        for h in range(N_ATTN_HEADS):
            blk = q_ref[0, :, (h // 2) * LANES:(h // 2 + 1) * LANES]
            keep = (lane < dh) if h % 2 == 0 else (lane >= dh)
            qz_ref[h] = jnp.where(keep, blk, jnp.zeros_like(blk))

    def step(bias_tile, bias_const):
        maskf = mask_ref[0, 0, 0].astype(F32)
        ones = jnp.ones((ONES_ROWS, t), BF16)

        for h in range(N_ATTN_HEADS):
            k_blk = k_ref[0, :, (h // 2) * LANES:(h // 2 + 1) * LANES]
            s = _dot_nt(k_blk, qz_ref[h]) + maskf
            s_ref[h] = s if bias_tile is None else s + bias_tile(h)
        for h in range(N_ATTN_HEADS):
            s = s_ref[h]
            c = bias_const(h)
            m_old = m_ref[h:h + 1, :]
            m_cur = jnp.max(jnp.max(s.reshape(t // 8, 8, t), axis=0), axis=0, keepdims=True) + c
            m_new = jnp.maximum(m_old, m_cur)
            m_safe = jnp.where(m_new == -jnp.inf, 0.0, m_new)
            alpha = jnp.exp2(m_old - m_safe)
            p = jnp.exp2(s - (m_safe - c)).astype(BF16)
            v_aug = jnp.concatenate([vt_ref[0, h * dh:(h + 1) * dh, :], ones], axis=0)
            acc_ref[h] = alpha * acc_ref[h] + _dot(v_aug, p)
            m_ref[h:h + 1, :] = m_new

    @pl.when(kt == qi)
    def _():
        step(lambda h: bias_ref[h, 0], lambda h: 0.0)

    @pl.when(kt == qi - 1)
    def _():
        step(lambda h: bias_ref[h, 1], lambda h: 0.0)

    @pl.when(kt < qi - 1)
    def _():
        step(None, lambda h: rb_ref[N_BUCKETS - 1, h] * LOG2E)

    @pl.when(kt == qi)
    def _():
        outs = []
        for h in range(N_ATTN_HEADS):
            a = acc_ref[h]
            outs.append(a[:dh, :] / a[dh:dh + 1, :])
        o_ref[0] = jnp.concatenate(outs, axis=0).T.astype(BF16)


def _attn_call(q, k, vt, mask, bias_tiles, rel_bias, t):
    B, S, W = q.shape
    n = S // t
    H = N_ATTN_HEADS
    qi_tab = jnp.asarray([i for i in range(n) for _ in range(i + 1)], I32)
    kt_tab = jnp.asarray([j for i in range(n) for j in range(i + 1)], I32)
    kern = functools.partial(_attn_kernel, t=t)
    grid_spec = pltpu.PrefetchScalarGridSpec(
        num_scalar_prefetch=2,
        grid=(B, int(qi_tab.shape[0])),
        in_specs=[pl.BlockSpec((1, t, W), lambda b, s, qt, kt: (b, qt[s], 0)),
                  pl.BlockSpec((1, t, W), lambda b, s, qt, kt: (b, kt[s], 0)),
                  pl.BlockSpec((1, W, t), lambda b, s, qt, kt: (b, 0, kt[s])),
                  pl.BlockSpec((1, 1, 1, t, t), lambda b, s, qt, kt: (b, qt[s], kt[s], 0, 0)),
                  pl.BlockSpec((H, 2, t, t), lambda b, s, qt, kt: (0, 0, 0, 0), pipeline_mode=pl.Buffered(1)),
                  pl.BlockSpec(memory_space=pltpu.SMEM)],
        out_specs=pl.BlockSpec((1, t, W), lambda b, s, qt, kt: (b, qt[s], 0)),
        scratch_shapes=[pltpu.VMEM((H, t, LANES), BF16),
                        pltpu.VMEM((H, t), F32),
                        pltpu.VMEM((H, ATTN_HEAD_DIM + ONES_ROWS, t), F32),
                        pltpu.VMEM((H, t, t), F32)])
    return pl.pallas_call(
        kern,
        grid_spec=grid_spec,
        out_shape=jax.ShapeDtypeStruct((B, S, W), BF16),
        compiler_params=_cparams(("parallel", "arbitrary")),
        name="attn",
    )(qi_tab, kt_tab, q, k, vt, mask, bias_tiles, rel_bias)


def _blockdiag_rows(x):
    lane = lax.broadcasted_iota(I32, x.shape, 1)
    zero = jnp.zeros_like(x)
    return jnp.concatenate([jnp.where(lane < RWKV_HEAD, x, zero),
                            jnp.where(lane >= RWKV_HEAD, x, zero)], axis=0)


def _rwkv_kernel(z_ref, mu_ref, w0_ref, dup_ref, a0_ref, iup_ref, gup_ref, kk_ref, ka_ref, rk_ref,
                 lng_ref, lnb_ref, seg_ref, tri_ref, o_ref,
                 prev_ref, st_ref, at_ref, rt_ref, bt_ref, kt_ref, bh_ref, kh_ref, v_ref, pc_ref, y_ref,
                 la_ref, lb_ref, mak_ref, arb_ref, ark_ref, wa_ref, wb_ref, g1_ref, g2_ref, h1_ref, h2_ref,
                 *, tt):
    j = pl.program_id(1)
    W = RWKV_W
    C = CHUNK
    nchunk = tt // C
    npair = RWKV_HEADS // 2

    @pl.when(j == 0)
    def _():
        prev_ref[...] = jnp.zeros(prev_ref.shape, F32)
        st_ref[...] = jnp.zeros(st_ref.shape, F32)

    z = z_ref[0]
    row = lax.broadcasted_iota(I32, z.shape, 0)
    z_prev = jnp.where(row == 0, prev_ref[...], pltpu.roll(z, 1, axis=0))
    prev_ref[...] = z[tt - 1:tt, :]
    z = z + mu_ref[...] * (z_prev - z)

    r = z[:, 0:W]
    k = z[:, W:2 * W]
    v = z[:, 2 * W:3 * W]
    wdad = z[:, 3 * W:3 * W + 2 * DECAY_LORA]
    gd = z[:, 3 * W + 2 * DECAY_LORA:]

    w_pre = w0_ref[...] + _dot3(jnp.tanh(wdad), dup_ref[...])
    neg = -w_pre
    softplus = jnp.maximum(neg, 0.0) + jnp.log(1.0 + jnp.exp(-jnp.abs(neg)))
    lw = -jnp.exp(-softplus - 0.5)
    a_lr = _sigmoid(a0_ref[...] + _dot(wdad.astype(BF16), iup_ref[...]))
    g = _dot(_sigmoid(gd).astype(BF16), gup_ref[...])

    seg = seg_ref[...]

    def head_sum(parts):
        half = seg.shape[0]
        cols = [sum(_dot(p[:, lo:lo + half], seg) for p in parts) for lo in range(0, W, half)]
        return jnp.concatenate(cols, axis=1)

    kk = k * kk_ref[...]
    kk = kk / jnp.maximum(jnp.sqrt(head_sum(_split2(kk * kk))), 1e-12)
    k2 = k * (1.0 + (a_lr - 1.0) * ka_ref[...])
    a_vec = -kk
    b_vec = kk * a_lr

    cum = _dot_exact_rhs_lhs(tri_ref[...], lw)
    tot = jnp.concatenate([jnp.broadcast_to(cum[(c + 1) * C - 1:(c + 1) * C, :], (C, W)) for c in range(nchunk)],
                          axis=0)
    p_inv = jnp.exp(-cum)
    p_out = jnp.exp(tot - cum)
    at_ref[...] = a_vec * jnp.exp(cum - lw)
    rt_ref[...] = r * jnp.exp(cum)
    bt_ref[...] = (b_vec * p_inv).astype(BF16)
    kt_ref[...] = (k2 * p_inv).astype(BF16)
    bh_ref[...] = b_vec * p_out
    kh_ref[...] = k2 * p_out
    v_ref[...] = v
    pc_ref[...] = jnp.exp(tot)

    t_i = lax.broadcasted_iota(I32, (C, LANES), 0)
    s_i = lax.broadcasted_iota(I32, (C, LANES), 1) % C
    strict = s_i < t_i
    incl = s_i <= t_i
    r_i = lax.broadcasted_iota(I32, (LANES, LANES), 0)
    c_i = lax.broadcasted_iota(I32, (LANES, LANES), 1)
    same_head = (r_i < RWKV_HEAD) == (c_i < RWKV_HEAD)
    diag = r_i == c_i
    nstage = int(math.log2(C))
    zero = jnp.zeros((C, LANES), F32)
    zsq = jnp.zeros((LANES, LANES), F32)
    units = [(c, p) for c in range(nchunk) for p in range(npair)]

    def sl(c, p):
        return slice(c * C, (c + 1) * C), slice(p * LANES, (p + 1) * LANES)

    def bd2(w):
        wb = w.astype(BF16)
        return jnp.concatenate([_blockdiag_rows(wb[:, :LANES]), _blockdiag_rows(wb[:, LANES:])], axis=1)

    for i, (c, p) in enumerate(units):
        rows, cols = sl(c, p)
        lhs = jnp.concatenate([at_ref[rows, cols], rt_ref[rows, cols]], axis=0).astype(BF16)
        rhs = jnp.concatenate([_blockdiag_rows(bt_ref[rows, cols]),
                               _blockdiag_rows(kt_ref[rows, cols])], axis=0)
        prod = _dot_nt(lhs, rhs)
        la_ref[i] = jnp.where(strict, prod[:C, :LANES], zero).astype(BF16)
        mak_ref[i] = jnp.where(strict, prod[:C, LANES:], zero).astype(BF16)
        arb_ref[i] = jnp.where(incl, prod[C:, :LANES], zero).astype(BF16)
        ark_ref[i] = jnp.where(incl, prod[C:, LANES:], zero).astype(BF16)
    for i, (c, p) in enumerate(units):
        rows, cols = sl(c, p)
        w2 = _dot(mak_ref[i], _blockdiag_rows(v_ref[rows, cols].astype(BF16)))
        wa_ref[i] = jnp.concatenate([at_ref[rows, cols], w2], axis=1)
    l_bufs, w_bufs = (la_ref, lb_ref), (wa_ref, wb_ref)
    for s in range(nstage):
        l_in, l_out = l_bufs[s % 2], l_bufs[(s + 1) % 2]
        w_in, w_out = w_bufs[s % 2], w_bufs[(s + 1) % 2]
        for i in range(len(units)):
            lmat = l_in[i]
            w = w_in[i]
            w_out[i] = w + _dot(lmat, bd2(w))
            if s < nstage - 1:
                l_out[i] = _dot(lmat, _blockdiag_rows(lmat)).astype(BF16)
    w_fin = w_bufs[nstage % 2]
    for i, (c, p) in enumerate(units):
        rows, cols = sl(c, p)
        w = w_fin[i]
        wb = w.astype(BF16)
        vb = v_ref[rows, cols].astype(BF16)
        gg = _dot(arb_ref[i], bd2(w))
        g1_ref[i] = (rt_ref[rows, cols] + gg[:, :LANES]).astype(BF16)
        g2_ref[i] = gg[:, LANES:] + _dot(ark_ref[i], _blockdiag_rows(vb))
        bk_t = jnp.concatenate([bh_ref[rows, cols], kh_ref[rows, cols]], axis=0).T
        hrhs = jnp.concatenate([wb, jnp.concatenate([jnp.zeros((C, LANES), BF16), vb], axis=1)], axis=0)
        hh = _dot(bk_t.astype(BF16), hrhs)
        pc = pc_ref[c * C:c * C + 1, cols]
        h1 = jnp.where(same_head, hh[:, :LANES], zsq) + jnp.where(diag, jnp.broadcast_to(pc, (LANES, LANES)), zsq)
        h1_ref[i] = h1.astype(BF16)
        h2_ref[i] = jnp.where(same_head, hh[:, LANES:], zsq)
    for c in range(nchunk):
        sts = [st_ref[p].astype(BF16) for p in range(npair)]
        for p in range(npair):
            i = c * npair + p
            rows, cols = sl(c, p)
            y_ref[rows, cols] = _dot(g1_ref[i], sts[p]) + g2_ref[i]
            st_ref[p] = _dot(h1_ref[i], sts[p]) + h2_ref[i]

    y = y_ref[...]
    inv_n = 1.0 / RWKV_HEAD
    mean = head_sum([y.astype(BF16)]) * inv_n
    yc = y - mean
    var = head_sum([(yc * yc).astype(BF16)]) * inv_n
    yn = yc * lax.rsqrt(var + GN_EPS) * lng_ref[...] + lnb_ref[...]
    bonus = head_sum([(r * k2 * rk_ref[...]).astype(BF16)]) * v
    o_ref[0] = ((yn + bonus) * g).astype(BF16)


def _dot_exact_rhs_lhs(ones_bf16, x):
    hi, mid, lo = _split3(x)
    return _dot(ones_bf16, hi) + _dot(ones_bf16, mid) + _dot(ones_bf16, lo)


def _rwkv_call(zr3, tshift_mu, decay_w0, decay_up, iclr_a0, iclr_up, gate_up, k_k, k_a, r_k, lnx_g, lnx_b):
    B, S, _ = zr3.shape
    tt = min(256, S)
    W = RWKV_W
    row = lambda a: a.reshape(1, -1).astype(F32)
    dup = jnp.concatenate([decay_up, jnp.zeros((ICLR_LORA, W), F32)], axis=0)
    iup = jnp.concatenate([jnp.zeros((DECAY_LORA, W), F32), iclr_up], axis=0)
    idx = jnp.arange(2 * LANES)
    seg = (idx[:, None] // RWKV_HEAD == idx[None, :] // RWKV_HEAD).astype(BF16)
    t = jnp.arange(tt)
    same_chunk = t[:, None] // CHUNK == t[None, :] // CHUNK
    tri = (same_chunk & (t[None, :] <= t[:, None])).astype(BF16)
    const = lambda shape: pl.BlockSpec(shape, lambda b, j: (0,) * len(shape))
    kern = functools.partial(_rwkv_kernel, tt=tt)
    nu = (tt // CHUNK) * (RWKV_HEADS // 2)
    return pl.pallas_call(
        kern,
        grid=(B, S // tt),
        in_specs=[pl.BlockSpec((1, tt, RWKV_IN), lambda b, j: (b, j, 0)),
                  const((1, RWKV_IN)), const((1, W)), const((2 * DECAY_LORA, W)), const((1, W)),
                  const((2 * ICLR_LORA, W)), const((GATE_LORA, W)), const((1, W)), const((1, W)),
                  const((1, W)), const((1, W)), const((1, W)),
                  const((2 * LANES, 2 * LANES)), const((tt, tt))],
        out_specs=pl.BlockSpec((1, tt, W), lambda b, j: (b, j, 0)),
        out_shape=jax.ShapeDtypeStruct((B, S, W), BF16),
        scratch_shapes=[pltpu.VMEM((1, RWKV_IN), F32),
                        pltpu.VMEM((RWKV_HEADS // 2, LANES, LANES), F32),
                        pltpu.VMEM((tt, W), F32),
                        pltpu.VMEM((tt, W), F32),
                        pltpu.VMEM((tt, W), BF16),
                        pltpu.VMEM((tt, W), BF16),
                        pltpu.VMEM((tt, W), F32),
                        pltpu.VMEM((tt, W), F32),
                        pltpu.VMEM((tt, W), F32),
                        pltpu.VMEM((tt, W), F32),
                        pltpu.VMEM((tt, W), F32),
                        pltpu.VMEM((nu, CHUNK, LANES), BF16),
                        pltpu.VMEM((nu, CHUNK, LANES), BF16),
                        pltpu.VMEM((nu, CHUNK, LANES), BF16),
                        pltpu.VMEM((nu, CHUNK, LANES), BF16),
                        pltpu.VMEM((nu, CHUNK, LANES), BF16),
                        pltpu.VMEM((nu, CHUNK, 2 * LANES), F32),
                        pltpu.VMEM((nu, CHUNK, 2 * LANES), F32),
                        pltpu.VMEM((nu, CHUNK, LANES), BF16),
                        pltpu.VMEM((nu, CHUNK, LANES), F32),
                        pltpu.VMEM((nu, LANES, LANES), BF16),
                        pltpu.VMEM((nu, LANES, LANES), F32)],
        compiler_params=_cparams(("parallel", "arbitrary")),
        name="rwkv",
    )(zr3, row(tshift_mu), row(decay_w0), dup, row(iclr_a0), iup.astype(BF16), gate_up.astype(BF16), row(k_k),
      row(k_a), row(r_k), row(lnx_g), row(lnx_b), seg, tri)


def _merge_kernel(x_ref, attn_ref, rw_ref, ga_ref, gr_ref, wa_ref, wr_ref, wo_ref, g1_ref,
                  n2_ref, sc_ref, sh_ref, rwt_ref, rb_ref, x1_ref, h2_ref, gt_ref, cnt_ref):
    a = _dot(attn_ref[...], wa_ref[...])
    rr = _dot(rw_ref[...], wr_ref[...])
    mixed = _gate_sigmoid(ga_ref[...].astype(F32)) * a + _gate_sigmoid(gr_ref[...].astype(F32)) * rr
    x1 = x_ref[...] + g1_ref[0] * _dot(mixed.astype(BF16), wo_ref[...])
    x1_ref[...] = x1
    ms = jnp.mean(x1 * x1, axis=-1, keepdims=True)
    h2 = x1 * lax.rsqrt(ms + RMS_EPS) * n2_ref[...]
    h2 = h2 * (1.0 + sc_ref[0]) + sh_ref[0]
    h2_ref[...] = _pack_bf16_pairs(h2)

    tm = x1.shape[0]
    E, G, EG = N_EXPERTS, N_GROUPS, N_EXPERTS // N_GROUPS
    scores = _sigmoid(_dot3(rwt_ref[...], h2, nt=True))
    choice = scores + rb_ref[...]
    c3 = choice.reshape(G, EG, tm)
    e_i = lax.broadcasted_iota(I32, (G, EG, tm), 1)
    m1 = jnp.max(c3, axis=1, keepdims=True)
    first = jnp.min(jnp.where(c3 == m1, e_i, EG), axis=1, keepdims=True)
    m2 = jnp.max(jnp.where(e_i == first, -jnp.inf, c3), axis=1, keepdims=True)
    grp = (m1 + m2).reshape(G, tm)
    g_i = lax.broadcasted_iota(I32, (G, tm), 0)
    rank = jnp.zeros((G, tm), I32)
    for o in range(G):
        other = grp[o:o + 1, :]
        rank = rank + jnp.where((other > grp) | ((other == grp) & (o < g_i)), 1, 0)
    gsel = rank < TOPK_GROUPS
    esel = jnp.broadcast_to(gsel.reshape(G, 1, tm), (G, EG, tm)).reshape(E, tm)
    mc = jnp.where(esel, choice, -jnp.inf)
    s_i = lax.broadcasted_iota(I32, (EG, tm), 0)
    rows = [mc[r * EG:(r + 1) * EG, :] for r in range(G)]
    ranks = [jnp.zeros((EG, tm), I32) for _ in range(G)]
    for o in range(E):
        ro, so = divmod(o, EG)
        other = mc[o:o + 1, :]
        for r in range(G):
            ge = jnp.where(other >= rows[r], 1, 0)
            gt = jnp.where(other > rows[r], 1, 0)
            if r > ro:
                beats = ge
            elif r < ro:
                beats = gt
            else:
                beats = jnp.where(s_i > so, ge, gt)
            ranks[r] = ranks[r] + beats
    top = jnp.concatenate(ranks, axis=0) < MOE_TOPK
    gw = jnp.where(top, scores, 0.0)
    gw = gw / jnp.sum(gw, axis=0, keepdims=True) * ROUTED_SCALE
    gt_ref[...] = gw
    sel = jnp.where(gw > 0.0, 1.0, 0.0)
    cnt_ref[0] = jnp.broadcast_to(jnp.sum(sel, axis=1, keepdims=True), (E, LANES))


def _pack_bf16_pairs(x):
    n = x.shape[1] // 2
    bits = pltpu.bitcast(x.astype(BF16).astype(F32), I32)
    return bits[:, :n] | lax.shift_right_logical(bits[:, n:], 16)


def _unpack_bf16_pairs(p):
    hi = pltpu.bitcast(p & jnp.int32(-65536), F32)
    lo = pltpu.bitcast(lax.shift_left(p, 16), F32)
    return jnp.concatenate([hi, lo], axis=1).astype(BF16)


def _merge_call(x2, attn, rw, ga, gr, wa, wr, wo, g1, norm2_g, sc2, sh2, router_w, router_bias, S):
    T, D = x2.shape
    B = T // S
    tm = min(MERGE_TILE, S)
    tpb = S // tm
    nt = T // tm
    E = N_EXPERTS
    row = lambda i: (i, 0)
    per_b = lambda i: (i // tpb, 0, 0)
    const = lambda shape: pl.BlockSpec(shape, lambda i: (0,) * len(shape))
    return pl.pallas_call(
        _merge_kernel,
        grid=(nt,),
        in_specs=[pl.BlockSpec((tm, D), row), pl.BlockSpec((tm, ATTN_W), row), pl.BlockSpec((tm, RWKV_W), row),
                  pl.BlockSpec((tm, D), row), pl.BlockSpec((tm, D), row),
                  const((ATTN_W, D)), const((RWKV_W, D)), const((D, D)),
                  pl.BlockSpec((1, 1, D), per_b), const((1, D)),
                  pl.BlockSpec((1, 1, D), per_b), pl.BlockSpec((1, 1, D), per_b),
                  const((E, D)), const((E, 1))],
        out_specs=[pl.BlockSpec((tm, D), row), pl.BlockSpec((tm, D // 2), row),
                   pl.BlockSpec((E, tm), lambda i: (0, i)), pl.BlockSpec((1, E, LANES), lambda i: (i, 0, 0))],
        out_shape=[jax.ShapeDtypeStruct((T, D), F32), jax.ShapeDtypeStruct((T, D // 2), I32),
                   jax.ShapeDtypeStruct((E, T), F32), jax.ShapeDtypeStruct((nt, E, LANES), F32)],
        compiler_params=_cparams(("parallel",)),
        name="merge",
    )(x2, attn, rw, ga, gr, wa, wr, wo, g1.reshape(B, 1, D), norm2_g.reshape(1, D),
      sc2.reshape(B, 1, D), sh2.reshape(B, 1, D), router_w.T, router_bias.reshape(E, 1))


MERGE_TILE = 512
MOE_BLOCK = 1024
SC_CORES, SC_SUBCORES = 2, 16
SC_WORKERS = SC_CORES * SC_SUBCORES
SC_ROWS = 128


def _plan_kernel(gt_ref, cnt_ref, upper_ref, lowe_ref, dest_ref, gw_ref, be_ref, off_ref, *, tm, n_blocks):
    i = pl.program_id(0)
    E = N_EXPERTS
    lowe = lowe_ref[...]

    @pl.when(i == 0)
    def _():
        total = jnp.sum(cnt_ref[...], axis=0)
        nblk = jnp.floor((total + (MOE_BLOCK - 1)) * (1.0 / MOE_BLOCK))
        start_blk = _dot_exact_rhs_lhs(lowe, nblk)
        off_ref[...] = start_blk * MOE_BLOCK
        end_blk = start_blk + nblk
        b_i = lax.broadcasted_iota(I32, (E, n_blocks), 1).astype(F32)
        e_of_b = jnp.sum(jnp.where(end_blk[:, :1] <= b_i, 1.0, 0.0), axis=0, keepdims=True)
        be_ref[...] = e_of_b.astype(I32)

    gt = gt_ref[...]
    sel = gt > 0.0
    selb = jnp.where(sel, 1.0, 0.0).astype(BF16)
    rank = _dot(selb, upper_ref[...])
    dest = off_ref[:, :1] + rank
    off_ref[...] = off_ref[...] + cnt_ref[i]
    kth = _dot(lowe, selb)
    dests, gws = [], []
    for k in range(MOE_TOPK):
        m = sel & (kth == float(k))
        have = jnp.sum(jnp.where(m, 1.0, 0.0), axis=0, keepdims=True)
        d = jnp.sum(jnp.where(m, dest, 0.0), axis=0, keepdims=True)
        dests.append(jnp.where(have > 0.0, d, float((n_blocks - 1) * MOE_BLOCK)))
        gws.append(jnp.sum(jnp.where(m, gt, 0.0), axis=0, keepdims=True))
    dest_ref[...] = jnp.concatenate(dests, axis=0).astype(I32)
    gpad = jnp.concatenate(gws + [jnp.zeros((LANES - MOE_TOPK, tm), F32)], axis=0)
    gw_ref[...] = gpad.T


def _plan_call(gate_t, cnt, n_blocks, tile0):
    E = gate_t.shape[0]
    nt = cnt.shape[0]
    tm = MERGE_TILE
    T = nt * tm
    idx = jnp.arange(tm)
    upper = (idx[:, None] < idx[None, :]).astype(BF16)
    ei = jnp.arange(E)
    lowe = (ei[None, :] < ei[:, None]).astype(BF16)
    kern = functools.partial(_plan_kernel, tm=tm, n_blocks=n_blocks)
    const = lambda shape: pl.BlockSpec(shape, lambda i: (0,) * len(shape))
    return pl.pallas_call(
        kern,
        grid=(nt,),
        in_specs=[pl.BlockSpec((E, tm), lambda i: (0, i + tile0)), const((nt, E, LANES)), const((tm, tm)),
                  const((E, E))],
        out_specs=[pl.BlockSpec((MOE_TOPK, tm), lambda i: (0, i)), pl.BlockSpec((tm, LANES), lambda i: (i, 0)),
                   const((1, n_blocks))],
        out_shape=[jax.ShapeDtypeStruct((MOE_TOPK, T), I32), jax.ShapeDtypeStruct((T, LANES), F32),
                   jax.ShapeDtypeStruct((1, n_blocks), I32)],
        scratch_shapes=[pltpu.VMEM((E, LANES), F32)],
        compiler_params=_cparams(("arbitrary",)),
        name="plan",
    )(gate_t, cnt, upper, lowe)


def _sc_index_layout(dest_t):
    K, T = dest_t.shape
    n_ch = T // (SC_WORKERS * SC_ROWS)
    return dest_t.reshape(K, SC_WORKERS, n_ch, SC_ROWS).transpose(1, 2, 0, 3).reshape(SC_WORKERS, n_ch * K, SC_ROWS)


def _sc_dispatch(rows, idx, n_slots, tok0):
    W = rows.shape[1]
    n_ch = idx.shape[1] // MOE_TOPK
    T = n_ch * SC_WORKERS * SC_ROWS
    tpw = T // SC_WORKERS
    mesh = plsc.VectorSubcoreMesh(core_axis_name="c", subcore_axis_name="s")

    @functools.partial(
        pl.kernel, mesh=mesh,
        out_type=jax.ShapeDtypeStruct((n_slots, W), I32),
        scratch_types=[pltpu.VMEM((n_ch * MOE_TOPK, SC_ROWS), I32), pltpu.VMEM((SC_ROWS, W), I32),
                       pltpu.SemaphoreType.DMA])
    def kern(x_hbm, idx_hbm, o_hbm, idx_v, rows_v, sem):
        wid = lax.axis_index("s") * SC_CORES + lax.axis_index("c")
        pltpu.sync_copy(idx_hbm.at[wid], idx_v)

        @pl.loop(0, n_ch)
        def _(j):
            pltpu.sync_copy(x_hbm.at[pl.ds(tok0 + wid * tpw + j * SC_ROWS, SC_ROWS)], rows_v)
            copies = [pltpu.async_copy(rows_v, o_hbm.at[idx_v.at[j * MOE_TOPK + k]], sem)
                      for k in range(MOE_TOPK)]
            for cp in copies:
                cp.wait()

    return kern(rows, idx)


def _sc_combine(slots, idx, T):
    _, W = slots.shape
    n_ch = T // (SC_WORKERS * SC_ROWS)
    tpw = T // SC_WORKERS
    mesh = plsc.VectorSubcoreMesh(core_axis_name="c", subcore_axis_name="s")

    @functools.partial(
        pl.kernel, mesh=mesh,
        out_type=jax.ShapeDtypeStruct((MOE_TOPK, T, W), I32),
        scratch_types=[pltpu.VMEM((n_ch * MOE_TOPK, SC_ROWS), I32), pltpu.VMEM((SC_ROWS, W), I32),
                       pltpu.SemaphoreType.DMA])
    def kern(s_hbm, idx_hbm, o_hbm, idx_v, rows_v, sem):
        wid = lax.axis_index("s") * SC_CORES + lax.axis_index("c")
        pltpu.sync_copy(idx_hbm.at[wid], idx_v)

        @pl.loop(0, n_ch)
        def _(j):
            for k in range(MOE_TOPK):
                pltpu.async_copy(s_hbm.at[idx_v.at[j * MOE_TOPK + k]], rows_v, sem).wait()
                pltpu.sync_copy(rows_v, o_hbm.at[k, pl.ds(wid * tpw + j * SC_ROWS, SC_ROWS)])

    return kern(slots, idx)


def _ffn_kernel(be_ref, x_ref, eg_ref, eu_ref, ed_ref, o_ref):
    used = be_ref[pl.program_id(0)] < N_EXPERTS

    @pl.when(used)
    def _():
        x = _unpack_bf16_pairs(x_ref[...])
        a = _dot(x, eg_ref[0])
        u = _dot(x, eu_ref[0])
        o_ref[...] = _pack_bf16_pairs(_dot((a * _gate_sigmoid(a) * u).astype(BF16), ed_ref[0]))

    @pl.when(jnp.logical_not(used))
    def _():
        o_ref[...] = jnp.zeros(o_ref.shape, I32)


def _ffn_call(xs, block_e, eg, eu, ed, n_blocks):
    P, W = xs.shape
    D, FF = 2 * W, EXPERT_FF
    grid_spec = pltpu.PrefetchScalarGridSpec(
        num_scalar_prefetch=1,
        grid=(n_blocks,),
        in_specs=[pl.BlockSpec((MOE_BLOCK, W), lambda b, be: (b, 0)),
                  pl.BlockSpec((1, D, FF), lambda b, be: (jnp.minimum(be[b], N_EXPERTS - 1), 0, 0)),
                  pl.BlockSpec((1, D, FF), lambda b, be: (jnp.minimum(be[b], N_EXPERTS - 1), 0, 0)),
                  pl.BlockSpec((1, FF, D), lambda b, be: (jnp.minimum(be[b], N_EXPERTS - 1), 0, 0))],
        out_specs=pl.BlockSpec((MOE_BLOCK, W), lambda b, be: (b, 0)))
    return pl.pallas_call(
        _ffn_kernel,
        grid_spec=grid_spec,
        out_shape=jax.ShapeDtypeStruct((P, W), I32),
        compiler_params=_cparams(("parallel",)),
        name="ffn",
    )(block_e, xs, eg, eu, ed)


def _final_kernel(h_ref, c_ref, gw_ref, x1_ref, g2_ref, fg_ref, sg_ref, su_ref, sd_ref, o_ref):
    h = _unpack_bf16_pairs(h_ref[...])
    a = _dot(h, sg_ref[...])
    u = _dot(h, su_ref[...])
    moe = _dot((a * _gate_sigmoid(a) * u).astype(BF16), sd_ref[...])
    gw = gw_ref[...]
    for k in range(MOE_TOPK):
        w = gw[:, k:k + 1]
        y = _unpack_bf16_pairs(c_ref[k]).astype(F32)
        moe = moe + jnp.where(w > 0.0, w * y, 0.0)
    x2 = x1_ref[...] + g2_ref[0] * moe
    ms = jnp.mean(x2 * x2, axis=-1, keepdims=True)
    o_ref[...] = x2 * lax.rsqrt(ms + RMS_EPS) * fg_ref[...]


def _final_call(h2p, comb, gw, x1, g2, final_g, sg, su, sd, S, tile0, prev_out):
    T, W = h2p.shape
    D, FF = 2 * W, EXPERT_FF
    B = T // S
    tm = MERGE_TILE
    tpb = S // tm
    ntile = comb.shape[1] // tm
    row = lambda i: (i, 0)
    full_row = lambda i: (i + tile0, 0)
    const = lambda shape: pl.BlockSpec(shape, lambda i: (0,) * len(shape))
    in_specs = [pl.BlockSpec((tm, W), full_row), pl.BlockSpec((MOE_TOPK, tm, W), lambda i: (0, i, 0)),
                pl.BlockSpec((tm, LANES), row), pl.BlockSpec((tm, D), full_row),
                pl.BlockSpec((1, 1, D), lambda i: ((i + tile0) // tpb, 0, 0)), const((1, D)),
                const((D, FF)), const((D, FF)), const((FF, D))]
    args = [h2p, comb, gw, x1, g2.reshape(B, 1, D), final_g.reshape(1, D), sg, su, sd]
    kern, aliases = _final_kernel, {}
    if prev_out is not None:
        in_specs.append(pl.BlockSpec(memory_space=pl.ANY))
        args.append(prev_out)
        aliases = {len(args) - 1: 0}
        kern = lambda *refs: _final_kernel(*refs[:9], refs[10])
    return pl.pallas_call(
        kern,
        grid=(ntile,),
        in_specs=in_specs,
        out_specs=pl.BlockSpec((tm, D), full_row),
        out_shape=jax.ShapeDtypeStruct((T, D), F32),
        input_output_aliases=aliases,
        compiler_params=_cparams(("parallel",)),
        name="final",
    )(*args)


MOE_GROUPS = 2


def _moe_call(h2p, gate_t, cnt, x1, g2, final_g, eg, eu, ed, sg, su, sd, S):
    T = h2p.shape[0]
    nt = cnt.shape[0]
    groups = MOE_GROUPS if (T // MOE_GROUPS) % (SC_WORKERS * SC_ROWS) == 0 and nt % MOE_GROUPS == 0 else 1
    tg, ntg = T // groups, nt // groups
    n_blocks = (tg * MOE_TOPK) // MOE_BLOCK + N_EXPERTS + 1
    out = None
    for g in range(groups):
        dest_t, gw, block_e = _plan_call(gate_t, cnt[g * ntg:(g + 1) * ntg], n_blocks, g * ntg)
        idx = _sc_index_layout(dest_t)
        xs = _sc_dispatch(h2p, idx, n_blocks * MOE_BLOCK, g * tg)
        ys = _ffn_call(xs, block_e.reshape(n_blocks), eg, eu, ed, n_blocks)
        comb = _sc_combine(ys, idx, tg)
        out = _final_call(h2p, comb, gw, x1, g2, final_g, sg, su, sd, S, g * ntg, out)
    return out


def _layer(x2, c, S, ada_w, ada_b, norm1_g, w_in, rel_bias, tshift_mu, decay_w0, decay_up, iclr_a0, iclr_up,
           gate_up, k_k, k_a, r_k, lnx_g, lnx_b, w_attn_br, w_rwkv_br, w_out, norm2_g, router_w, router_bias,
           exp_gate, exp_up, exp_down, sh_gate, sh_up, sh_down, final_g):
    T, D = x2.shape
    B = T // S
    mod = _mod_call(c, ada_w, ada_b)
    sh1, sc1, g1, sh2, sc2, g2 = jnp.split(mod, 6, axis=-1)

    q, k, vt, iq, ik4, iwt, zr, ga, gr = _inproj_call(x2, norm1_g, sc1, sh1, w_in, S)

    ta = min(512, S)
    assert ta >= LANES and S % ta == 0
    top_k = min(TOPK_MAX, S // 4)
    seq = lambda a: a.reshape(B, S, a.shape[-1])
    mask = _index_call(seq(iq), iwt, seq(ik4), ta, top_k)
    bias_tiles = _bias_call(rel_bias, ta)
    attn = _attn_call(seq(q), seq(k), vt, mask, bias_tiles, rel_bias, ta).reshape(T, ATTN_W)

    rw = _rwkv_call(zr.reshape(B, S, RWKV_IN), tshift_mu, decay_w0, decay_up, iclr_a0, iclr_up, gate_up,
                    k_k, k_a, r_k, lnx_g, lnx_b).reshape(T, RWKV_W)

    x1, h2p, gate_t, cnt = _merge_call(x2, attn, rw, ga, gr, w_attn_br.astype(BF16), w_rwkv_br.astype(BF16),
                                       w_out.astype(BF16), g1, norm2_g, sc2, sh2, router_w, router_bias, S)
    return _moe_call(h2p, gate_t, cnt, x1, g2, final_g, exp_gate.astype(BF16), exp_up.astype(BF16),
                     exp_down.astype(BF16), sh_gate.astype(BF16), sh_up.astype(BF16), sh_down.astype(BF16), S)


def kernel(x, c, ada_w, ada_b, norm1_g, w_in, rel_bias, tshift_mu, decay_w0, decay_up, iclr_a0, iclr_up, gate_up, k_k, k_a, r_k, lnx_g, lnx_b, w_attn_br, w_rwkv_br, w_out, norm2_g, router_w, router_bias, exp_gate, exp_up, exp_down, sh_gate, sh_up, sh_down, final_g):
    B, S, D = x.shape
    depth = ada_w.shape[0]
    assert depth == 1, "the final RMSNorm is fused into the (single) layer's MoE kernel"
    out = _layer(x.reshape(B * S, D), c, S, ada_w[0], ada_b[0], norm1_g[0], w_in[0], rel_bias, tshift_mu[0],
                 decay_w0[0], decay_up[0], iclr_a0[0], iclr_up[0], gate_up[0], k_k[0], k_a[0], r_k[0],
                 lnx_g[0], lnx_b[0], w_attn_br[0], w_rwkv_br[0], w_out[0], norm2_g[0], router_w[0],
                 router_bias[0], exp_gate[0], exp_up[0], exp_down[0], sh_gate[0], sh_up[0], sh_down[0], final_g)
    return out.reshape(B, S, D)
```

```python
import functools
import math

import jax
import jax.numpy as jnp
from jax import lax
from jax.experimental import pallas as pl
from jax.experimental.pallas import tpu as pltpu
from jax.experimental.pallas import tpu_sc as plsc

F32 = jnp.float32
BF16 = jnp.bfloat16
I32 = jnp.int32
I16 = jnp.int16

RMS_EPS = 1e-6
D_MODEL = 1024
N_ATTN_HEADS = 8
ATTN_HEAD_DIM = 64
ATTN_W = 512
IDX_HEADS = 16
IDX_DIM = 32
IDX_Q = 512
TOPK_MAX = 256
N_BUCKETS = 32
MAX_DISTANCE = 128
RWKV_HEADS = 8
RWKV_HEAD = 64
RWKV_W = 512
DECAY_LORA = 64
ICLR_LORA = 64
GATE_LORA = 128
RWKV_IN = 1792
GN_EPS = 64e-5
N_EXPERTS = 64
N_GROUPS = 8
TOPK_GROUPS = 4
MOE_TOPK = 8
EXPERT_FF = 256
ROUTED_SCALE = 2.5

LANES = 128
VMEM_LIMIT = 56 * 1024 * 1024
CHUNK = 64
MERGE_TILE = 512
MOE_BLOCK = 1024
MOE_GROUPS = 2
SC_CORES, SC_SUBCORES = 2, 16
SC_WORKERS = SC_CORES * SC_SUBCORES
SC_ROWS = 128
LOG2E = 1.4426950408889634
INT_MIN = -2147483648
KEY_NEG_INF = -2139095041

NT_DIMS = (((1,), (1,)), ((), ()))


def _cparams(sem):
    return pltpu.CompilerParams(dimension_semantics=sem, vmem_limit_bytes=VMEM_LIMIT)


def _dot(a, b):
    return jnp.dot(a, b, preferred_element_type=F32)


def _dot_nt(a, b):
    return lax.dot_general(a, b, NT_DIMS, preferred_element_type=F32)


def _split2(x):
    hi = x.astype(BF16)
    lo = (x - hi.astype(F32)).astype(BF16)
    return hi, lo


def _split3(x):
    hi = x.astype(BF16)
    r1 = x - hi.astype(F32)
    mid = r1.astype(BF16)
    lo = (r1 - mid.astype(F32)).astype(BF16)
    return hi, mid, lo


def _dot3(a, b, nt=False):
    ah, al = _split2(a)
    bh, bl = _split2(b)
    f = _dot_nt if nt else _dot
    return f(ah, bh) + f(ah, bl) + f(al, bh)


def _sigmoid(x):
    return 1.0 / (1.0 + jnp.exp(-x))


def _gate_sigmoid(x):
    return 0.5 * jnp.tanh(0.5 * x) + 0.5


def _mod_kernel(c_ref, w_ref, b_ref, o_ref):
    c = c_ref[...]
    s = c * _sigmoid(c)
    o_ref[...] = _dot3(s, w_ref[...]) + b_ref[...]


def _mod_call(c, ada_w, ada_b):
    B, D = c.shape
    N = ada_w.shape[1]
    tn = 1024
    return pl.pallas_call(
        _mod_kernel,
        grid=(N // tn,),
        in_specs=[pl.BlockSpec((B, D), lambda j: (0, 0)),
                  pl.BlockSpec((D, tn), lambda j: (0, j)),
                  pl.BlockSpec((1, tn), lambda j: (0, j))],
        out_specs=pl.BlockSpec((B, tn), lambda j: (0, j)),
        out_shape=jax.ShapeDtypeStruct((B, N), F32),
        compiler_params=_cparams(("arbitrary",)),
        name="mod",
    )(c, ada_w, ada_b.reshape(1, N))


_OFF_Q, _OFF_K, _OFF_IQ, _OFF_IK4, _OFF_ZR, _OFF_GA, _OFF_GR, _N_PACK = (
    0, 512, 1024, 1536, 2048, 3840, 4864, 5888)
IDX_PER_BLOCK = LANES // IDX_DIM


def _pack_w_in(w_in):
    D = w_in.shape[0]
    w_ik = w_in[:, 2048:2080]
    ik4 = jnp.zeros((D, IDX_PER_BLOCK * LANES), w_in.dtype)
    for j in range(IDX_PER_BLOCK):
        ik4 = lax.dynamic_update_slice(ik4, w_ik, (0, j * LANES + j * IDX_DIM))
    w_pack = jnp.concatenate([w_in[:, 0:1024], w_in[:, 1536:2048], ik4, w_in[:, 2096:]], axis=1).astype(BF16)
    return w_pack, w_in[:, 1024:1536].T.astype(BF16), w_in[:, 2080:2096].T.astype(BF16)


def _inproj_kernel(x_ref, g_ref, sc_ref, sh_ref, w_ref, wvt_ref, wiwt_ref,
                   q_ref, k_ref, vt_ref, iq_ref, ik4_ref, iwt_ref, zr_ref, ga_ref, gr_ref):
    x = x_ref[...]
    ms = jnp.mean(x * x, axis=-1, keepdims=True)
    h = x * lax.rsqrt(ms + RMS_EPS) * g_ref[...]
    h = h * (1.0 + sc_ref[0]) + sh_ref[0]
    hb = h.astype(BF16)

    def proj(lo, hi):
        return _dot(hb, w_ref[:, lo:hi])

    q_ref[...] = (proj(_OFF_Q, _OFF_K) * (ATTN_HEAD_DIM ** -0.5 * LOG2E)).astype(BF16)
    k_ref[...] = proj(_OFF_K, _OFF_IQ).astype(BF16)
    iq_ref[...] = proj(_OFF_IQ, _OFF_IK4).astype(BF16)
    ik4_ref[...] = proj(_OFF_IK4, _OFF_ZR).astype(BF16)
    zr_ref[...] = proj(_OFF_ZR, _OFF_GA)
    ga_ref[...] = proj(_OFF_GA, _OFF_GR).astype(BF16)
    gr_ref[...] = proj(_OFF_GR, _N_PACK).astype(BF16)
    vt_ref[0] = _dot_nt(wvt_ref[...], hb).astype(BF16)
    iwt_ref[0] = _dot_nt(wiwt_ref[...], hb)


def _inproj_call(x2, norm_g, sc, sh, w_in, S):
    T, D = x2.shape
    B = T // S
    tm = min(512, S)
    tpb = S // tm
    w_pack, wvt, wiwt = _pack_w_in(w_in)
    row = lambda i: (i, 0)
    per_b = lambda i: (i // tpb, 0, 0)
    colblk = lambda i: (i // tpb, 0, i % tpb)
    const = lambda shape: pl.BlockSpec(shape, lambda i: (0,) * len(shape), pipeline_mode=pl.Buffered(1))
    rows_out = ((512, BF16), (512, BF16), (512, BF16), (512, BF16), (RWKV_IN, F32), (D, BF16), (D, BF16))
    out_specs = [pl.BlockSpec((tm, w), row) for w, _ in rows_out]
    out_shape = [jax.ShapeDtypeStruct((T, w), dt) for w, dt in rows_out]
    out_specs[2:2] = [pl.BlockSpec((1, ATTN_W, tm), colblk)]
    out_shape[2:2] = [jax.ShapeDtypeStruct((B, ATTN_W, S), BF16)]
    out_specs[5:5] = [pl.BlockSpec((1, IDX_HEADS, tm), colblk)]
    out_shape[5:5] = [jax.ShapeDtypeStruct((B, IDX_HEADS, S), F32)]
    return pl.pallas_call(
        _inproj_kernel,
        grid=(T // tm,),
        in_specs=[pl.BlockSpec((tm, D), row),
                  pl.BlockSpec((1, D), lambda i: (0, 0)),
                  pl.BlockSpec((1, 1, D), per_b),
                  pl.BlockSpec((1, 1, D), per_b),
                  const((D, _N_PACK)), const((ATTN_W, D)), const((IDX_HEADS, D))],
        out_specs=out_specs,
        out_shape=out_shape,
        compiler_params=_cparams(("parallel",)),
        name="inproj",
    )(x2, norm_g.reshape(1, D), sc.reshape(B, 1, D), sh.reshape(B, 1, D), w_pack, wvt, wiwt)


def _t5_bucket(rel):
    n = jnp.maximum(rel, 0)
    max_exact = N_BUCKETS // 2
    nf = jnp.maximum(n, 1).astype(F32)
    large = max_exact + (jnp.log(nf / max_exact) / math.log(MAX_DISTANCE / max_exact)
                         * (N_BUCKETS - max_exact)).astype(I32)
    large = jnp.minimum(large, N_BUCKETS - 1)
    return jnp.where(n < max_exact, n, large)


def _bias_kernel(bucket_ref, rb_ref, o_ref):
    h = pl.program_id(0)
    bk = bucket_ref[...]
    out = jnp.zeros(bk.shape, F32)
    for b in range(N_BUCKETS):
        out = jnp.where(bk == b, rb_ref[b, h] * LOG2E, out)
    o_ref[0] = out


def _bias_call(rel_bias, tq):
    r = jnp.arange(tq, dtype=I32)[None, :]
    c = jnp.arange(tq, dtype=I32)[:, None]
    buckets = jnp.stack([_t5_bucket(r - c), _t5_bucket(tq + r - c)])
    return pl.pallas_call(
        _bias_kernel,
        grid=(N_ATTN_HEADS,),
        in_specs=[pl.BlockSpec((2, tq, tq), lambda h: (0, 0, 0)),
                  pl.BlockSpec(memory_space=pltpu.SMEM)],
        out_specs=pl.BlockSpec((1, 2, tq, tq), lambda h: (h, 0, 0, 0)),
        out_shape=jax.ShapeDtypeStruct((N_ATTN_HEADS, 2, tq, tq), F32),
        compiler_params=_cparams(("arbitrary",)),
        name="bias",
    )(buckets, rel_bias)


def _index_kernel(iq_ref, iwt_ref, ik4_ref, lower_ref, mask_ref, key_ref, k16_ref, *, t, nk, top_k, scale):
    qi = pl.program_id(1)
    nkt = qi + 1
    ksub = LANES
    qpos = qi * t + lax.broadcasted_iota(I32, (ksub, t), 1)

    def score_tile(kt, carry):
        kbase = pl.multiple_of(kt * t, t)
        for ks in range(t // ksub):
            acc = jnp.zeros((ksub, t), F32)
            ik_rows = ik4_ref[0, pl.ds(kbase + ks * ksub, ksub), :]
            ik_stack = jnp.concatenate([ik_rows[:, j * LANES:(j + 1) * LANES] for j in range(IDX_PER_BLOCK)],
                                       axis=0)
            for g in range(IDX_HEADS // IDX_PER_BLOCK):
                d4 = _dot_nt(ik_stack, iq_ref[0, :, g * LANES:(g + 1) * LANES])
                for j in range(IDX_PER_BLOCK):
                    h = g * IDX_PER_BLOCK + j
                    acc = acc + jnp.maximum(d4[j * ksub:(j + 1) * ksub], 0.0) * iwt_ref[0, h:h + 1, :]
            s = acc * scale
            kpos = kt * t + ks * ksub + lax.broadcasted_iota(I32, (ksub, t), 0)
            s = jnp.where(kpos <= qpos, s, -jnp.inf)
            bits = pltpu.bitcast(s, I32)
            key = bits ^ ((bits >> 31) & 0x7FFFFFFF)
            key_ref[kt, ks * ksub:(ks + 1) * ksub, :] = key
            k16_ref[kt, ks * ksub:(ks + 1) * ksub, :] = (key >> 16).astype(I16)
        return carry

    lax.fori_loop(0, nkt, score_tile, 0)

    pack = 16

    def search16():
        def bit_body(i, ans):
            cand = ans | lax.shift_left(jnp.int32(1), 15 - i)
            cand16 = (cand - 32768).astype(I16)

            def cnt_body(kt, acc):
                one = jnp.where(k16_ref[kt] >= cand16, jnp.int16(1), jnp.int16(0))
                for r in range(t // pack):
                    acc = acc + one[r * pack:(r + 1) * pack, :]
                return acc

            acc = lax.fori_loop(0, nkt, cnt_body, jnp.zeros((pack, t), I16))
            cnt = jnp.sum(acc.astype(I32), axis=0, keepdims=True)
            return jnp.where(cnt >= top_k, cand, ans)

        return lax.fori_loop(0, 16, bit_body, jnp.zeros((1, t), I32))

    hi = search16() - 32768

    def remap_body(kt, carry):
        key = key_ref[kt]
        khi = key >> 16
        lo = (key & 0xFFFF) - 32768
        k16_ref[kt] = jnp.where(khi > hi, 32767, jnp.where(khi == hi, lo, -32768)).astype(I16)
        return carry

    lax.fori_loop(0, nkt, remap_body, 0)
    thr = hi * 65536 + search16()

    def count(pred):
        def body(kt, acc):
            one = jnp.where(pred(key_ref[kt]), 1.0, 0.0)
            return acc + jnp.sum(one.reshape(t // 8, 8, t), axis=0)
        return jnp.sum(lax.fori_loop(0, nkt, body, jnp.zeros((8, t), F32)), axis=0, keepdims=True)

    n_ge = count(lambda keys: keys >= thr)
    has_tie = jnp.max(n_ge) > float(top_k)

    @pl.when(jnp.logical_not(has_tie))
    def _():
        def mask_body(kt, carry):
            keys = key_ref[kt]
            sel = (keys >= thr) & (keys > KEY_NEG_INF)
            mask_ref[0, 0, kt] = jnp.where(sel, 0.0, -jnp.inf).astype(BF16)
            return carry

        lax.fori_loop(0, nkt, mask_body, 0)

    @pl.when(has_tie)
    def _():
        need = float(top_k) - count(lambda keys: keys > thr)

        def mask_body(kt, seen):
            keys = key_ref[kt]
            tie = (keys == thr) & (keys > KEY_NEG_INF)
            tie_b = jnp.where(tie, 1.0, 0.0).astype(BF16)
            before = seen + _dot(lower_ref[...], tie_b)
            sel = (keys > thr) | (tie & (before < need))
            mask_ref[0, 0, kt] = jnp.where(sel, 0.0, -jnp.inf).astype(BF16)
            return seen + jnp.sum(tie_b.astype(F32).reshape(t // 8, 8, t).sum(axis=0), axis=0, keepdims=True)

        lax.fori_loop(0, nkt, mask_body, jnp.zeros((1, t), F32))

    def fill_body(kt, carry):
        mask_ref[0, 0, kt] = jnp.full((t, t), -jnp.inf, BF16)
        return carry

    lax.fori_loop(nkt, nk, fill_body, 0)


def _index_call(iq, iwt, ik4, t, top_k):
    B, S, _ = iq.shape
    n = S // t
    scale = (IDX_HEADS ** -0.5) * (IDX_DIM ** -0.5)
    kern = functools.partial(_index_kernel, t=t, nk=n, top_k=top_k, scale=scale)
    pos = jnp.arange(t)
    lower = (pos[None, :] < pos[:, None]).astype(BF16)
    return pl.pallas_call(
        kern,
        grid=(B, n),
        in_specs=[pl.BlockSpec((1, t, IDX_Q), lambda b, i: (b, i, 0)),
                  pl.BlockSpec((1, IDX_HEADS, t), lambda b, i: (b, 0, i)),
                  pl.BlockSpec((1, S, IDX_PER_BLOCK * LANES), lambda b, i: (b, 0, 0)),
                  pl.BlockSpec((t, t), lambda b, i: (0, 0))],
        out_specs=pl.BlockSpec((1, 1, n, t, t), lambda b, i: (b, i, 0, 0, 0)),
        out_shape=jax.ShapeDtypeStruct((B, n, n, t, t), BF16),
        scratch_shapes=[pltpu.VMEM((n, t, t), I32), pltpu.VMEM((n, t, t), I16)],
        compiler_params=_cparams(("parallel", "arbitrary")),
        name="index",
    )(iq, iwt, ik4, lower)


ONES_ROWS = 16


def _attn_kernel(qi_tab, kt_tab, q_ref, k_ref, vt_ref, mask_ref, bias_ref, rb_ref, o_ref,
                 qz_ref, m_ref, acc_ref, s_ref, *, t):
    s_id = pl.program_id(1)
    qi = qi_tab[s_id]
    kt = kt_tab[s_id]
    dh = ATTN_HEAD_DIM

    @pl.when(kt == 0)
    def _():
        m_ref[...] = jnp.full(m_ref.shape, -jnp.inf, F32)
        acc_ref[...] = jnp.zeros(acc_ref.shape, F32)
        lane = lax.broadcasted_iota(I32, (t, LANES), 1)
        for h in range(N_ATTN_HEADS):
            blk = q_ref[0, :, (h // 2) * LANES:(h // 2 + 1) * LANES]
            keep = (lane < dh) if h % 2 == 0 else (lane >= dh)
            qz_ref[h] = jnp.where(keep, blk, jnp.zeros_like(blk))

    def step(bias_tile, bias_const):
        maskf = mask_ref[0, 0, 0].astype(F32)
        ones = jnp.ones((ONES_ROWS, t), BF16)

        for h in range(N_ATTN_HEADS):
            k_blk = k_ref[0, :, (h // 2) * LANES:(h // 2 + 1) * LANES]
            s = _dot_nt(k_blk, qz_ref[h]) + maskf
            s_ref[h] = s if bias_tile is None else s + bias_tile(h)
        for h in range(N_ATTN_HEADS):
            s = s_ref[h]
            c = bias_const(h)
            m_old = m_ref[h:h + 1, :]
            m_cur = jnp.max(jnp.max(s.reshape(t // 8, 8, t), axis=0), axis=0, keepdims=True) + c
            m_new = jnp.maximum(m_old, m_cur)
            m_safe = jnp.where(m_new == -jnp.inf, 0.0, m_new)
            alpha = jnp.exp2(m_old - m_safe)
            p = jnp.exp2(s - (m_safe - c)).astype(BF16)
            v_aug = jnp.concatenate([vt_ref[0, h * dh:(h + 1) * dh, :], ones], axis=0)
            acc_ref[h] = alpha * acc_ref[h] + _dot(v_aug, p)
            m_ref[h:h + 1, :] = m_new

    @pl.when(kt == qi)
    def _():
        step(lambda h: bias_ref[h, 0], lambda h: 0.0)

    @pl.when(kt == qi - 1)
    def _():
        step(lambda h: bias_ref[h, 1], lambda h: 0.0)

    @pl.when(kt < qi - 1)
    def _():
        step(None, lambda h: rb_ref[N_BUCKETS - 1, h] * LOG2E)

    @pl.when(kt == qi)
    def _():
        outs = []
        for h in range(N_ATTN_HEADS):
            a = acc_ref[h]
            outs.append(a[:dh, :] / a[dh:dh + 1, :])
        o_ref[0] = jnp.concatenate(outs, axis=0).T.astype(BF16)


def _attn_call(q, k, vt, mask, bias_tiles, rel_bias, t):
    B, S, W = q.shape
    n = S // t
    H = N_ATTN_HEADS
    qi_tab = jnp.asarray([i for i in range(n) for _ in range(i + 1)], I32)
    kt_tab = jnp.asarray([j for i in range(n) for j in range(i + 1)], I32)
    kern = functools.partial(_attn_kernel, t=t)
    grid_spec = pltpu.PrefetchScalarGridSpec(
        num_scalar_prefetch=2,
        grid=(B, int(qi_tab.shape[0])),
        in_specs=[pl.BlockSpec((1, t, W), lambda b, s, qt, kt: (b, qt[s], 0)),
                  pl.BlockSpec((1, t, W), lambda b, s, qt, kt: (b, kt[s], 0)),
                  pl.BlockSpec((1, W, t), lambda b, s, qt, kt: (b, 0, kt[s])),
                  pl.BlockSpec((1, 1, 1, t, t), lambda b, s, qt, kt: (b, qt[s], kt[s], 0, 0)),
                  pl.BlockSpec((H, 2, t, t), lambda b, s, qt, kt: (0, 0, 0, 0), pipeline_mode=pl.Buffered(1)),
                  pl.BlockSpec(memory_space=pltpu.SMEM)],
        out_specs=pl.BlockSpec((1, t, W), lambda b, s, qt, kt: (b, qt[s], 0)),
        scratch_shapes=[pltpu.VMEM((H, t, LANES), BF16),
                        pltpu.VMEM((H, t), F32),
                        pltpu.VMEM((H, ATTN_HEAD_DIM + ONES_ROWS, t), F32),
                        pltpu.VMEM((H, t, t), F32)])
    return pl.pallas_call(
        kern,
        grid_spec=grid_spec,
        out_shape=jax.ShapeDtypeStruct((B, S, W), BF16),
        compiler_params=_cparams(("parallel", "arbitrary")),
        name="attn",
    )(qi_tab, kt_tab, q, k, vt, mask, bias_tiles, rel_bias)


def _blockdiag_rows(x):
    lane = lax.broadcasted_iota(I32, x.shape, 1)
    zero = jnp.zeros_like(x)
    return jnp.concatenate([jnp.where(lane < RWKV_HEAD, x, zero),
                            jnp.where(lane >= RWKV_HEAD, x, zero)], axis=0)


def _rwkv_kernel(z_ref, mu_ref, w0_ref, dup_ref, a0_ref, iup_ref, gup_ref, kk_ref, ka_ref, rk_ref,
                 lng_ref, lnb_ref, seg_ref, tri_ref, o_ref,
                 prev_ref, st_ref, at_ref, rt_ref, bt_ref, kt_ref, bh_ref, kh_ref, v_ref, pc_ref, y_ref,
                 la_ref, lb_ref, mak_ref, arb_ref, ark_ref, wa_ref, wb_ref, g1_ref, g2_ref, h1_ref, h2_ref,
                 *, tt):
    j = pl.program_id(1)
    W = RWKV_W
    C = CHUNK
    nchunk = tt // C
    npair = RWKV_HEADS // 2

    @pl.when(j == 0)
    def _():
        prev_ref[...] = jnp.zeros(prev_ref.shape, F32)
        st_ref[...] = jnp.zeros(st_ref.shape, F32)

    z = z_ref[0]
    row = lax.broadcasted_iota(I32, z.shape, 0)
    z_prev = jnp.where(row == 0, prev_ref[...], pltpu.roll(z, 1, axis=0))
    prev_ref[...] = z[tt - 1:tt, :]
    z = z + mu_ref[...] * (z_prev - z)

    r = z[:, 0:W]
    k = z[:, W:2 * W]
    v = z[:, 2 * W:3 * W]
    wdad = z[:, 3 * W:3 * W + 2 * DECAY_LORA]
    gd = z[:, 3 * W + 2 * DECAY_LORA:]

    w_pre = w0_ref[...] + _dot3(jnp.tanh(wdad), dup_ref[...])
    neg = -w_pre
    softplus = jnp.maximum(neg, 0.0) + jnp.log(1.0 + jnp.exp(-jnp.abs(neg)))
    lw = -jnp.exp(-softplus - 0.5)
    a_lr = _sigmoid(a0_ref[...] + _dot(wdad.astype(BF16), iup_ref[...]))
    g = _dot(_sigmoid(gd).astype(BF16), gup_ref[...])

    seg = seg_ref[...]

    def head_sum(parts):
        half = seg.shape[0]
        cols = [sum(_dot(p[:, lo:lo + half], seg) for p in parts) for lo in range(0, W, half)]
        return jnp.concatenate(cols, axis=1)

    kk = k * kk_ref[...]
    kk = kk / jnp.maximum(jnp.sqrt(head_sum(_split2(kk * kk))), 1e-12)
    k2 = k * (1.0 + (a_lr - 1.0) * ka_ref[...])
    a_vec = -kk
    b_vec = kk * a_lr

    cum = _dot_exact_rhs_lhs(tri_ref[...], lw)
    tot = jnp.concatenate([jnp.broadcast_to(cum[(c + 1) * C - 1:(c + 1) * C, :], (C, W)) for c in range(nchunk)],
                          axis=0)
    p_inv = jnp.exp(-cum)
    p_out = jnp.exp(tot - cum)
    at_ref[...] = a_vec * jnp.exp(cum - lw)
    rt_ref[...] = r * jnp.exp(cum)
    bt_ref[...] = (b_vec * p_inv).astype(BF16)
    kt_ref[...] = (k2 * p_inv).astype(BF16)
    bh_ref[...] = b_vec * p_out
    kh_ref[...] = k2 * p_out
    v_ref[...] = v
    pc_ref[...] = jnp.exp(tot)

    t_i = lax.broadcasted_iota(I32, (C, LANES), 0)
    s_i = lax.broadcasted_iota(I32, (C, LANES), 1) % C
    strict = s_i < t_i
    incl = s_i <= t_i
    r_i = lax.broadcasted_iota(I32, (LANES, LANES), 0)
    c_i = lax.broadcasted_iota(I32, (LANES, LANES), 1)
    same_head = (r_i < RWKV_HEAD) == (c_i < RWKV_HEAD)
    diag = r_i == c_i
    nstage = int(math.log2(C))
    zero = jnp.zeros((C, LANES), F32)
    zsq = jnp.zeros((LANES, LANES), F32)
    units = [(c, p) for c in range(nchunk) for p in range(npair)]

    def sl(c, p):
        return slice(c * C, (c + 1) * C), slice(p * LANES, (p + 1) * LANES)

    def bd2(w):
        wb = w.astype(BF16)
        return jnp.concatenate([_blockdiag_rows(wb[:, :LANES]), _blockdiag_rows(wb[:, LANES:])], axis=1)

    for i, (c, p) in enumerate(units):
        rows, cols = sl(c, p)
        lhs = jnp.concatenate([at_ref[rows, cols], rt_ref[rows, cols]], axis=0).astype(BF16)
        rhs = jnp.concatenate([_blockdiag_rows(bt_ref[rows, cols]),
                               _blockdiag_rows(kt_ref[rows, cols])], axis=0)
        prod = _dot_nt(lhs, rhs)
        la_ref[i] = jnp.where(strict, prod[:C, :LANES], zero).astype(BF16)
        mak_ref[i] = jnp.where(strict, prod[:C, LANES:], zero).astype(BF16)
        arb_ref[i] = jnp.where(incl, prod[C:, :LANES], zero).astype(BF16)
        ark_ref[i] = jnp.where(incl, prod[C:, LANES:], zero).astype(BF16)
    for i, (c, p) in enumerate(units):
        rows, cols = sl(c, p)
        w2 = _dot(mak_ref[i], _blockdiag_rows(v_ref[rows, cols].astype(BF16)))
        wa_ref[i] = jnp.concatenate([at_ref[rows, cols], w2], axis=1)
    l_bufs, w_bufs = (la_ref, lb_ref), (wa_ref, wb_ref)
    for s in range(nstage):
        l_in, l_out = l_bufs[s % 2], l_bufs[(s + 1) % 2]
        w_in, w_out = w_bufs[s % 2], w_bufs[(s + 1) % 2]
        for i in range(len(units)):
            lmat = l_in[i]
            w = w_in[i]
            w_out[i] = w + _dot(lmat, bd2(w))
            if s < nstage - 1:
                l_out[i] = _dot(lmat, _blockdiag_rows(lmat)).astype(BF16)
    w_fin = w_bufs[nstage % 2]
    for i, (c, p) in enumerate(units):
        rows, cols = sl(c, p)
        w = w_fin[i]
        wb = w.astype(BF16)
        vb = v_ref[rows, cols].astype(BF16)
        gg = _dot(arb_ref[i], bd2(w))
        g1_ref[i] = (rt_ref[rows, cols] + gg[:, :LANES]).astype(BF16)
        g2_ref[i] = gg[:, LANES:] + _dot(ark_ref[i], _blockdiag_rows(vb))
        bk_t = jnp.concatenate([bh_ref[rows, cols], kh_ref[rows, cols]], axis=0).T
        hrhs = jnp.concatenate([wb, jnp.concatenate([jnp.zeros((C, LANES), BF16), vb], axis=1)], axis=0)
        hh = _dot(bk_t.astype(BF16), hrhs)
        pc = pc_ref[c * C:c * C + 1, cols]
        h1 = jnp.where(same_head, hh[:, :LANES], zsq) + jnp.where(diag, jnp.broadcast_to(pc, (LANES, LANES)), zsq)
        h1_ref[i] = h1.astype(BF16)
        h2_ref[i] = jnp.where(same_head, hh[:, LANES:], zsq)
    for c in range(nchunk):
        sts = [st_ref[p].astype(BF16) for p in range(npair)]
        for p in range(npair):
            i = c * npair + p
            rows, cols = sl(c, p)
            y_ref[rows, cols] = _dot(g1_ref[i], sts[p]) + g2_ref[i]
            st_ref[p] = _dot(h1_ref[i], sts[p]) + h2_ref[i]

    y = y_ref[...]
    inv_n = 1.0 / RWKV_HEAD
    mean = head_sum([y.astype(BF16)]) * inv_n
    yc = y - mean
    var = head_sum([(yc * yc).astype(BF16)]) * inv_n
    yn = yc * lax.rsqrt(var + GN_EPS) * lng_ref[...] + lnb_ref[...]
    bonus = head_sum([(r * k2 * rk_ref[...]).astype(BF16)]) * v
    o_ref[0] = ((yn + bonus) * g).astype(BF16)


def _dot_exact_rhs_lhs(ones_bf16, x):
    hi, mid, lo = _split3(x)
    return _dot(ones_bf16, hi) + _dot(ones_bf16, mid) + _dot(ones_bf16, lo)


def _rwkv_call(zr3, tshift_mu, decay_w0, decay_up, iclr_a0, iclr_up, gate_up, k_k, k_a, r_k, lnx_g, lnx_b):
    B, S, _ = zr3.shape
    tt = min(256, S)
    W = RWKV_W
    row = lambda a: a.reshape(1, -1).astype(F32)
    dup = jnp.concatenate([decay_up, jnp.zeros((ICLR_LORA, W), F32)], axis=0)
    iup = jnp.concatenate([jnp.zeros((DECAY_LORA, W), F32), iclr_up], axis=0)
    idx = jnp.arange(2 * LANES)
    seg = (idx[:, None] // RWKV_HEAD == idx[None, :] // RWKV_HEAD).astype(BF16)
    t = jnp.arange(tt)
    same_chunk = t[:, None] // CHUNK == t[None, :] // CHUNK
    tri = (same_chunk & (t[None, :] <= t[:, None])).astype(BF16)
    const = lambda shape: pl.BlockSpec(shape, lambda b, j: (0,) * len(shape))
    kern = functools.partial(_rwkv_kernel, tt=tt)
    nu = (tt // CHUNK) * (RWKV_HEADS // 2)
    return pl.pallas_call(
        kern,
        grid=(B, S // tt),
        in_specs=[pl.BlockSpec((1, tt, RWKV_IN), lambda b, j: (b, j, 0)),
                  const((1, RWKV_IN)), const((1, W)), const((2 * DECAY_LORA, W)), const((1, W)),
                  const((2 * ICLR_LORA, W)), const((GATE_LORA, W)), const((1, W)), const((1, W)),
                  const((1, W)), const((1, W)), const((1, W)),
                  const((2 * LANES, 2 * LANES)), const((tt, tt))],
        out_specs=pl.BlockSpec((1, tt, W), lambda b, j: (b, j, 0)),
        out_shape=jax.ShapeDtypeStruct((B, S, W), BF16),
        scratch_shapes=[pltpu.VMEM((1, RWKV_IN), F32),
                        pltpu.VMEM((RWKV_HEADS // 2, LANES, LANES), F32),
                        pltpu.VMEM((tt, W), F32),
                        pltpu.VMEM((tt, W), F32),
                        pltpu.VMEM((tt, W), BF16),
                        pltpu.VMEM((tt, W), BF16),
                        pltpu.VMEM((tt, W), F32),
                        pltpu.VMEM((tt, W), F32),
                        pltpu.VMEM((tt, W), F32),
                        pltpu.VMEM((tt, W), F32),
                        pltpu.VMEM((tt, W), F32),
                        pltpu.VMEM((nu, CHUNK, LANES), BF16),
                        pltpu.VMEM((nu, CHUNK, LANES), BF16),
                        pltpu.VMEM((nu, CHUNK, LANES), BF16),
                        pltpu.VMEM((nu, CHUNK, LANES), BF16),
                        pltpu.VMEM((nu, CHUNK, LANES), BF16),
                        pltpu.VMEM((nu, CHUNK, 2 * LANES), F32),
                        pltpu.VMEM((nu, CHUNK, 2 * LANES), F32),
                        pltpu.VMEM((nu, CHUNK, LANES), BF16),
                        pltpu.VMEM((nu, CHUNK, LANES), F32),
                        pltpu.VMEM((nu, LANES, LANES), BF16),
                        pltpu.VMEM((nu, LANES, LANES), F32)],
        compiler_params=_cparams(("parallel", "arbitrary")),
        name="rwkv",
    )(zr3, row(tshift_mu), row(decay_w0), dup, row(iclr_a0), iup.astype(BF16), gate_up.astype(BF16), row(k_k),
      row(k_a), row(r_k), row(lnx_g), row(lnx_b), seg, tri)


def _merge_kernel(x_ref, attn_ref, rw_ref, ga_ref, gr_ref, wa_ref, wr_ref, wo_ref, g1_ref,
                  n2_ref, sc_ref, sh_ref, rwt_ref, rb_ref, x1_ref, h2_ref, gt_ref, cnt_ref):
    a = _dot(attn_ref[...], wa_ref[...])
    rr = _dot(rw_ref[...], wr_ref[...])
    mixed = _gate_sigmoid(ga_ref[...].astype(F32)) * a + _gate_sigmoid(gr_ref[...].astype(F32)) * rr
    x1 = x_ref[...] + g1_ref[0] * _dot(mixed.astype(BF16), wo_ref[...])
    x1_ref[...] = x1
    ms = jnp.mean(x1 * x1, axis=-1, keepdims=True)
    h2 = x1 * lax.rsqrt(ms + RMS_EPS) * n2_ref[...]
    h2 = h2 * (1.0 + sc_ref[0]) + sh_ref[0]
    h2_ref[...] = _pack_bf16_pairs(h2)

    tm = x1.shape[0]
    E, G, EG = N_EXPERTS, N_GROUPS, N_EXPERTS // N_GROUPS
    scores = _sigmoid(_dot3(rwt_ref[...], h2, nt=True))
    choice = scores + rb_ref[...]
    c3 = choice.reshape(G, EG, tm)
    e_i = lax.broadcasted_iota(I32, (G, EG, tm), 1)
    m1 = jnp.max(c3, axis=1, keepdims=True)
    first = jnp.min(jnp.where(c3 == m1, e_i, EG), axis=1, keepdims=True)
    m2 = jnp.max(jnp.where(e_i == first, -jnp.inf, c3), axis=1, keepdims=True)
    grp = (m1 + m2).reshape(G, tm)
    g_i = lax.broadcasted_iota(I32, (G, tm), 0)
    rank = jnp.zeros((G, tm), I32)
    for o in range(G):
        other = grp[o:o + 1, :]
        rank = rank + jnp.where((other > grp) | ((other == grp) & (o < g_i)), 1, 0)
    gsel = rank < TOPK_GROUPS
    esel = jnp.broadcast_to(gsel.reshape(G, 1, tm), (G, EG, tm)).reshape(E, tm)
    mc = jnp.where(esel, choice, -jnp.inf)
    s_i = lax.broadcasted_iota(I32, (EG, tm), 0)
    rows = [mc[r * EG:(r + 1) * EG, :] for r in range(G)]
    ranks = [jnp.zeros((EG, tm), I32) for _ in range(G)]
    for o in range(E):
        ro, so = divmod(o, EG)
        other = mc[o:o + 1, :]
        for r in range(G):
            ge = jnp.where(other >= rows[r], 1, 0)
            gt = jnp.where(other > rows[r], 1, 0)
            if r > ro:
                beats = ge
            elif r < ro:
                beats = gt
            else:
                beats = jnp.where(s_i > so, ge, gt)
            ranks[r] = ranks[r] + beats
    top = jnp.concatenate(ranks, axis=0) < MOE_TOPK
    gw = jnp.where(top, scores, 0.0)
    gw = gw / jnp.sum(gw, axis=0, keepdims=True) * ROUTED_SCALE
    gt_ref[...] = gw
    sel = jnp.where(gw > 0.0, 1.0, 0.0)
    cnt_ref[0] = jnp.broadcast_to(jnp.sum(sel, axis=1, keepdims=True), (E, LANES))


def _pack_bf16_pairs(x):
    n = x.shape[1] // 2
    bits = pltpu.bitcast(x.astype(BF16).astype(F32), I32)
    return bits[:, :n] | lax.shift_right_logical(bits[:, n:], 16)


def _unpack_bf16_pairs(p):
    hi = pltpu.bitcast(p & jnp.int32(-65536), F32)
    lo = pltpu.bitcast(lax.shift_left(p, 16), F32)
    return jnp.concatenate([hi, lo], axis=1).astype(BF16)


def _merge_call(x2, attn, rw, ga, gr, wa, wr, wo, g1, norm2_g, sc2, sh2, router_w, router_bias, S):
    T, D = x2.shape
    B = T // S
    tm = min(MERGE_TILE, S)
    tpb = S // tm
    nt = T // tm
    E = N_EXPERTS
    row = lambda i: (i, 0)
    per_b = lambda i: (i // tpb, 0, 0)
    const = lambda shape: pl.BlockSpec(shape, lambda i: (0,) * len(shape))
    return pl.pallas_call(
        _merge_kernel,
        grid=(nt,),
        in_specs=[pl.BlockSpec((tm, D), row), pl.BlockSpec((tm, ATTN_W), row), pl.BlockSpec((tm, RWKV_W), row),
                  pl.BlockSpec((tm, D), row), pl.BlockSpec((tm, D), row),
                  const((ATTN_W, D)), const((RWKV_W, D)), const((D, D)),
                  pl.BlockSpec((1, 1, D), per_b), const((1, D)),
                  pl.BlockSpec((1, 1, D), per_b), pl.BlockSpec((1, 1, D), per_b),
                  const((E, D)), const((E, 1))],
        out_specs=[pl.BlockSpec((tm, D), row), pl.BlockSpec((tm, D // 2), row),
                   pl.BlockSpec((E, tm), lambda i: (0, i)), pl.BlockSpec((1, E, LANES), lambda i: (i, 0, 0))],
        out_shape=[jax.ShapeDtypeStruct((T, D), F32), jax.ShapeDtypeStruct((T, D // 2), I32),
                   jax.ShapeDtypeStruct((E, T), F32), jax.ShapeDtypeStruct((nt, E, LANES), F32)],
        compiler_params=_cparams(("parallel",)),
        name="merge",
    )(x2, attn, rw, ga, gr, wa, wr, wo, g1.reshape(B, 1, D), norm2_g.reshape(1, D),
      sc2.reshape(B, 1, D), sh2.reshape(B, 1, D), router_w.T, router_bias.reshape(E, 1))


def _plan_kernel(gt_ref, cnt_ref, upper_ref, lowe_ref, dest_ref, gw_ref, be_ref, off_ref, *, tm, n_blocks):
    i = pl.program_id(0)
    E = N_EXPERTS
    lowe = lowe_ref[...]

    @pl.when(i == 0)
    def _():
        total = jnp.sum(cnt_ref[...], axis=0)
        nblk = jnp.floor((total + (MOE_BLOCK - 1)) * (1.0 / MOE_BLOCK))
        start_blk = _dot_exact_rhs_lhs(lowe, nblk)
        off_ref[...] = start_blk * MOE_BLOCK
        end_blk = start_blk + nblk
        b_i = lax.broadcasted_iota(I32, (E, n_blocks), 1).astype(F32)
        e_of_b = jnp.sum(jnp.where(end_blk[:, :1] <= b_i, 1.0, 0.0), axis=0, keepdims=True)
        be_ref[...] = e_of_b.astype(I32)

    gt = gt_ref[...]
    sel = gt > 0.0
    selb = jnp.where(sel, 1.0, 0.0).astype(BF16)
    rank = _dot(selb, upper_ref[...])
    dest = off_ref[:, :1] + rank
    off_ref[...] = off_ref[...] + cnt_ref[i]
    kth = _dot(lowe, selb)
    dests, gws = [], []
    for k in range(MOE_TOPK):
        m = sel & (kth == float(k))
        have = jnp.sum(jnp.where(m, 1.0, 0.0), axis=0, keepdims=True)
        d = jnp.sum(jnp.where(m, dest, 0.0), axis=0, keepdims=True)
        dests.append(jnp.where(have > 0.0, d, float((n_blocks - 1) * MOE_BLOCK)))
        gws.append(jnp.sum(jnp.where(m, gt, 0.0), axis=0, keepdims=True))
    dest_ref[...] = jnp.concatenate(dests, axis=0).astype(I32)
    gpad = jnp.concatenate(gws + [jnp.zeros((LANES - MOE_TOPK, tm), F32)], axis=0)
    gw_ref[...] = gpad.T


def _plan_call(gate_t, cnt, n_blocks, tile0):
    E = gate_t.shape[0]
    nt = cnt.shape[0]
    tm = MERGE_TILE
    T = nt * tm
    idx = jnp.arange(tm)
    upper = (idx[:, None] < idx[None, :]).astype(BF16)
    ei = jnp.arange(E)
    lowe = (ei[None, :] < ei[:, None]).astype(BF16)
    kern = functools.partial(_plan_kernel, tm=tm, n_blocks=n_blocks)
    const = lambda shape: pl.BlockSpec(shape, lambda i: (0,) * len(shape))
    return pl.pallas_call(
        kern,
        grid=(nt,),
        in_specs=[pl.BlockSpec((E, tm), lambda i: (0, i + tile0)), const((nt, E, LANES)), const((tm, tm)),
                  const((E, E))],
        out_specs=[pl.BlockSpec((MOE_TOPK, tm), lambda i: (0, i)), pl.BlockSpec((tm, LANES), lambda i: (i, 0)),
                   const((1, n_blocks))],
        out_shape=[jax.ShapeDtypeStruct((MOE_TOPK, T), I32), jax.ShapeDtypeStruct((T, LANES), F32),
                   jax.ShapeDtypeStruct((1, n_blocks), I32)],
        scratch_shapes=[pltpu.VMEM((E, LANES), F32)],
        compiler_params=_cparams(("arbitrary",)),
        name="plan",
    )(gate_t, cnt, upper, lowe)


def _sc_index_layout(dest_t):
    K, T = dest_t.shape
    n_ch = T // (SC_WORKERS * SC_ROWS)
    return dest_t.reshape(K, SC_WORKERS, n_ch, SC_ROWS).transpose(1, 2, 0, 3).reshape(SC_WORKERS, n_ch * K, SC_ROWS)


def _sc_dispatch(rows, idx, n_slots, tok0):
    W = rows.shape[1]
    n_ch = idx.shape[1] // MOE_TOPK
    T = n_ch * SC_WORKERS * SC_ROWS
    tpw = T // SC_WORKERS
    mesh = plsc.VectorSubcoreMesh(core_axis_name="c", subcore_axis_name="s")

    @functools.partial(
        pl.kernel, mesh=mesh,
        out_type=jax.ShapeDtypeStruct((n_slots, W), I32),
        scratch_types=[pltpu.VMEM((n_ch * MOE_TOPK, SC_ROWS), I32), pltpu.VMEM((SC_ROWS, W), I32),
                       pltpu.SemaphoreType.DMA])
    def kern(x_hbm, idx_hbm, o_hbm, idx_v, rows_v, sem):
        wid = lax.axis_index("s") * SC_CORES + lax.axis_index("c")
        pltpu.sync_copy(idx_hbm.at[wid], idx_v)

        @pl.loop(0, n_ch)
        def _(j):
            pltpu.sync_copy(x_hbm.at[pl.ds(tok0 + wid * tpw + j * SC_ROWS, SC_ROWS)], rows_v)
            copies = [pltpu.async_copy(rows_v, o_hbm.at[idx_v.at[j * MOE_TOPK + k]], sem)
                      for k in range(MOE_TOPK)]
            for cp in copies:
                cp.wait()

    return kern(rows, idx)


def _sc_combine(slots, idx, T):
    _, W = slots.shape
    n_ch = T // (SC_WORKERS * SC_ROWS)
    tpw = T // SC_WORKERS
    mesh = plsc.VectorSubcoreMesh(core_axis_name="c", subcore_axis_name="s")

    @functools.partial(
        pl.kernel, mesh=mesh,
        out_type=jax.ShapeDtypeStruct((MOE_TOPK, T, W), I32),
        scratch_types=[pltpu.VMEM((n_ch * MOE_TOPK, SC_ROWS), I32), pltpu.VMEM((SC_ROWS, W), I32),
                       pltpu.SemaphoreType.DMA])
    def kern(s_hbm, idx_hbm, o_hbm, idx_v, rows_v, sem):
        wid = lax.axis_index("s") * SC_CORES + lax.axis_index("c")
        pltpu.sync_copy(idx_hbm.at[wid], idx_v)

        @pl.loop(0, n_ch)
        def _(j):
            for k in range(MOE_TOPK):
                pltpu.async_copy(s_hbm.at[idx_v.at[j * MOE_TOPK + k]], rows_v, sem).wait()
                pltpu.sync_copy(rows_v, o_hbm.at[k, pl.ds(wid * tpw + j * SC_ROWS, SC_ROWS)])

    return kern(slots, idx)


def _ffn_kernel(be_ref, x_ref, eg_ref, eu_ref, ed_ref, o_ref):
    used = be_ref[pl.program_id(0)] < N_EXPERTS

    @pl.when(used)
    def _():
        x = _unpack_bf16_pairs(x_ref[...])
        a = _dot(x, eg_ref[0])
        u = _dot(x, eu_ref[0])
        o_ref[...] = _pack_bf16_pairs(_dot((a * _gate_sigmoid(a) * u).astype(BF16), ed_ref[0]))

    @pl.when(jnp.logical_not(used))
    def _():
        o_ref[...] = jnp.zeros(o_ref.shape, I32)


def _ffn_call(xs, block_e, eg, eu, ed, n_blocks):
    P, W = xs.shape
    D, FF = 2 * W, EXPERT_FF
    grid_spec = pltpu.PrefetchScalarGridSpec(
        num_scalar_prefetch=1,
        grid=(n_blocks,),
        in_specs=[pl.BlockSpec((MOE_BLOCK, W), lambda b, be: (b, 0)),
                  pl.BlockSpec((1, D, FF), lambda b, be: (jnp.minimum(be[b], N_EXPERTS - 1), 0, 0)),
                  pl.BlockSpec((1, D, FF), lambda b, be: (jnp.minimum(be[b], N_EXPERTS - 1), 0, 0)),
                  pl.BlockSpec((1, FF, D), lambda b, be: (jnp.minimum(be[b], N_EXPERTS - 1), 0, 0))],
        out_specs=pl.BlockSpec((MOE_BLOCK, W), lambda b, be: (b, 0)))
    return pl.pallas_call(
        _ffn_kernel,
        grid_spec=grid_spec,
        out_shape=jax.ShapeDtypeStruct((P, W), I32),
        compiler_params=_cparams(("parallel",)),
        name="ffn",
    )(block_e, xs, eg, eu, ed)


def _final_kernel(h_ref, c_ref, gw_ref, x1_ref, g2_ref, fg_ref, sg_ref, su_ref, sd_ref, o_ref):
    h = _unpack_bf16_pairs(h_ref[...])
    a = _dot(h, sg_ref[...])
    u = _dot(h, su_ref[...])
    moe = _dot((a * _gate_sigmoid(a) * u).astype(BF16), sd_ref[...])
    gw = gw_ref[...]
    for k in range(MOE_TOPK):
        w = gw[:, k:k + 1]
        y = _unpack_bf16_pairs(c_ref[k]).astype(F32)
        moe = moe + jnp.where(w > 0.0, w * y, 0.0)
    x2 = x1_ref[...] + g2_ref[0] * moe
    ms = jnp.mean(x2 * x2, axis=-1, keepdims=True)
    o_ref[...] = x2 * lax.rsqrt(ms + RMS_EPS) * fg_ref[...]


def _final_call(h2p, comb, gw, x1, g2, final_g, sg, su, sd, S, tile0, prev_out):
    T, W = h2p.shape
    D, FF = 2 * W, EXPERT_FF
    B = T // S
    tm = MERGE_TILE
    tpb = S // tm
    ntile = comb.shape[1] // tm
    row = lambda i: (i, 0)
    full_row = lambda i: (i + tile0, 0)
    const = lambda shape: pl.BlockSpec(shape, lambda i: (0,) * len(shape))
    in_specs = [pl.BlockSpec((tm, W), full_row), pl.BlockSpec((MOE_TOPK, tm, W), lambda i: (0, i, 0)),
                pl.BlockSpec((tm, LANES), row), pl.BlockSpec((tm, D), full_row),
                pl.BlockSpec((1, 1, D), lambda i: ((i + tile0) // tpb, 0, 0)), const((1, D)),
                const((D, FF)), const((D, FF)), const((FF, D))]
    args = [h2p, comb, gw, x1, g2.reshape(B, 1, D), final_g.reshape(1, D), sg, su, sd]
    kern, aliases = _final_kernel, {}
    if prev_out is not None:
        in_specs.append(pl.BlockSpec(memory_space=pl.ANY))
        args.append(prev_out)
        aliases = {len(args) - 1: 0}
        kern = lambda *refs: _final_kernel(*refs[:9], refs[10])
    return pl.pallas_call(
        kern,
        grid=(ntile,),
        in_specs=in_specs,
        out_specs=pl.BlockSpec((tm, D), full_row),
        out_shape=jax.ShapeDtypeStruct((T, D), F32),
        input_output_aliases=aliases,
        compiler_params=_cparams(("parallel",)),
        name="final",
    )(*args)


def _moe_call(h2p, gate_t, cnt, x1, g2, final_g, eg, eu, ed, sg, su, sd, S):
    T = h2p.shape[0]
    nt = cnt.shape[0]
    groups = MOE_GROUPS if (T // MOE_GROUPS) % (SC_WORKERS * SC_ROWS) == 0 and nt % MOE_GROUPS == 0 else 1
    tg, ntg = T // groups, nt // groups
    n_blocks = (tg * MOE_TOPK) // MOE_BLOCK + N_EXPERTS + 1
    out = None
    for g in range(groups):
        dest_t, gw, block_e = _plan_call(gate_t, cnt[g * ntg:(g + 1) * ntg], n_blocks, g * ntg)
        idx = _sc_index_layout(dest_t)
        xs = _sc_dispatch(h2p, idx, n_blocks * MOE_BLOCK, g * tg)
        ys = _ffn_call(xs, block_e.reshape(n_blocks), eg, eu, ed, n_blocks)
        comb = _sc_combine(ys, idx, tg)
        out = _final_call(h2p, comb, gw, x1, g2, final_g, sg, su, sd, S, g * ntg, out)
    return out


def _layer(x2, c, S, ada_w, ada_b, norm1_g, w_in, rel_bias, tshift_mu, decay_w0, decay_up, iclr_a0, iclr_up,
           gate_up, k_k, k_a, r_k, lnx_g, lnx_b, w_attn_br, w_rwkv_br, w_out, norm2_g, router_w, router_bias,
           exp_gate, exp_up, exp_down, sh_gate, sh_up, sh_down, final_g):
    T, D = x2.shape
    B = T // S
    mod = _mod_call(c, ada_w, ada_b)
    sh1, sc1, g1, sh2, sc2, g2 = jnp.split(mod, 6, axis=-1)

    q, k, vt, iq, ik4, iwt, zr, ga, gr = _inproj_call(x2, norm1_g, sc1, sh1, w_in, S)

    ta = min(512, S)
    assert ta >= LANES and S % ta == 0
    top_k = min(TOPK_MAX, S // 4)
    seq = lambda a: a.reshape(B, S, a.shape[-1])
    mask = _index_call(seq(iq), iwt, seq(ik4), ta, top_k)
    bias_tiles = _bias_call(rel_bias, ta)
    attn = _attn_call(seq(q), seq(k), vt, mask, bias_tiles, rel_bias, ta).reshape(T, ATTN_W)

    rw = _rwkv_call(zr.reshape(B, S, RWKV_IN), tshift_mu, decay_w0, decay_up, iclr_a0, iclr_up, gate_up,
                    k_k, k_a, r_k, lnx_g, lnx_b).reshape(T, RWKV_W)

    x1, h2p, gate_t, cnt = _merge_call(x2, attn, rw, ga, gr, w_attn_br.astype(BF16), w_rwkv_br.astype(BF16),
                                       w_out.astype(BF16), g1, norm2_g, sc2, sh2, router_w, router_bias, S)
    return _moe_call(h2p, gate_t, cnt, x1, g2, final_g, exp_gate.astype(BF16), exp_up.astype(BF16),
                     exp_down.astype(BF16), sh_gate.astype(BF16), sh_up.astype(BF16), sh_down.astype(BF16), S)


def kernel(x, c, ada_w, ada_b, norm1_g, w_in, rel_bias, tshift_mu, decay_w0, decay_up, iclr_a0, iclr_up, gate_up, k_k, k_a, r_k, lnx_g, lnx_b, w_attn_br, w_rwkv_br, w_out, norm2_g, router_w, router_bias, exp_gate, exp_up, exp_down, sh_gate, sh_up, sh_down, final_g):
    B, S, D = x.shape
    depth = ada_w.shape[0]
    assert depth == 1, "the final RMSNorm is fused into the (single) layer's MoE kernel"
    out = _layer(x.reshape(B * S, D), c, S, ada_w[0], ada_b[0], norm1_g[0], w_in[0], rel_bias, tshift_mu[0],
                 decay_w0[0], decay_up[0], iclr_a0[0], iclr_up[0], gate_up[0], k_k[0], k_a[0], r_k[0],
                 lnx_g[0], lnx_b[0], w_attn_br[0], w_rwkv_br[0], w_out[0], norm2_g[0], router_w[0],
                 router_bias[0], exp_gate[0], exp_up[0], exp_down[0], sh_gate[0], sh_up[0], sh_down[0], final_g)
    return out.reshape(B, S, D)
```

```python
import functools
import math

import jax
import jax.numpy as jnp
from jax import lax
from jax.experimental import pallas as pl
from jax.experimental.pallas import tpu as pltpu
from jax.experimental.pallas import tpu_sc as plsc

F32 = jnp.float32
BF16 = jnp.bfloat16
I32 = jnp.int32
I16 = jnp.int16

RMS_EPS = 1e-6
D_MODEL = 1024
N_ATTN_HEADS = 8
ATTN_HEAD_DIM = 64
ATTN_W = 512
IDX_HEADS = 16
IDX_DIM = 32
IDX_Q = 512
TOPK_MAX = 256
N_BUCKETS = 32
MAX_DISTANCE = 128
RWKV_HEADS = 8
RWKV_HEAD = 64
RWKV_W = 512
DECAY_LORA = 64
ICLR_LORA = 64
GATE_LORA = 128
RWKV_IN = 1792
GN_EPS = 64e-5
N_EXPERTS = 64
N_GROUPS = 8
TOPK_GROUPS = 4
MOE_TOPK = 8
EXPERT_FF = 256
ROUTED_SCALE = 2.5

LANES = 128
VMEM_LIMIT = 56 * 1024 * 1024
CHUNK = 64
MERGE_TILE = 512
MOE_BLOCK = 1024
MOE_GROUPS = 2
SC_CORES, SC_SUBCORES = 2, 16
SC_WORKERS = SC_CORES * SC_SUBCORES
SC_ROWS = 128
LOG2E = 1.4426950408889634
INT_MIN = -2147483648
KEY_NEG_INF = -2139095041

NT_DIMS = (((1,), (1,)), ((), ()))


def _cparams(sem):
    return pltpu.CompilerParams(dimension_semantics=sem, vmem_limit_bytes=VMEM_LIMIT)


def _dot(a, b):
    return jnp.dot(a, b, preferred_element_type=F32)


def _dot_nt(a, b):
    return lax.dot_general(a, b, NT_DIMS, preferred_element_type=F32)


def _split2(x):
    hi = x.astype(BF16)
    lo = (x - hi.astype(F32)).astype(BF16)
    return hi, lo


def _split3(x):
    hi = x.astype(BF16)
    r1 = x - hi.astype(F32)
    mid = r1.astype(BF16)
    lo = (r1 - mid.astype(F32)).astype(BF16)
    return hi, mid, lo


def _dot3(a, b, nt=False):
    ah, al = _split2(a)
    bh, bl = _split2(b)
    f = _dot_nt if nt else _dot
    return f(ah, bh) + f(ah, bl) + f(al, bh)


def _sigmoid(x):
    return 1.0 / (1.0 + jnp.exp(-x))


def _gate_sigmoid(x):
    return 0.5 * jnp.tanh(0.5 * x) + 0.5


def _mod_kernel(c_ref, w_ref, b_ref, o_ref):
    c = c_ref[...]
    s = c * _sigmoid(c)
    o_ref[...] = _dot3(s, w_ref[...]) + b_ref[...]


def _mod_call(c, ada_w, ada_b):
    B, D = c.shape
    N = ada_w.shape[1]
    tn = 1024
    return pl.pallas_call(
        _mod_kernel,
        grid=(N // tn,),
        in_specs=[pl.BlockSpec((B, D), lambda j: (0, 0)),
                  pl.BlockSpec((D, tn), lambda j: (0, j)),
                  pl.BlockSpec((1, tn), lambda j: (0, j))],
        out_specs=pl.BlockSpec((B, tn), lambda j: (0, j)),
        out_shape=jax.ShapeDtypeStruct((B, N), F32),
        compiler_params=_cparams(("arbitrary",)),
        name="mod",
    )(c, ada_w, ada_b.reshape(1, N))


_OFF_Q, _OFF_K, _OFF_IQ, _OFF_IK4, _OFF_ZR, _OFF_GA, _OFF_GR, _N_PACK = (
    0, 512, 1024, 1536, 2048, 3840, 4864, 5888)
IDX_PER_BLOCK = LANES // IDX_DIM


def _pack_w_in(w_in):
    D = w_in.shape[0]
    w_ik = w_in[:, 2048:2080]
    ik4 = jnp.zeros((D, IDX_PER_BLOCK * LANES), w_in.dtype)
    for j in range(IDX_PER_BLOCK):
        ik4 = lax.dynamic_update_slice(ik4, w_ik, (0, j * LANES + j * IDX_DIM))
    w_pack = jnp.concatenate([w_in[:, 0:1024], w_in[:, 1536:2048], ik4, w_in[:, 2096:]], axis=1).astype(BF16)
    return w_pack, w_in[:, 1024:1536].T.astype(BF16), w_in[:, 2080:2096].T.astype(BF16)


def _inproj_kernel(x_ref, g_ref, sc_ref, sh_ref, w_ref, wvt_ref, wiwt_ref,
                   q_ref, k_ref, vt_ref, iq_ref, ik4_ref, iwt_ref, zr_ref, ga_ref, gr_ref):
    x = x_ref[...]
    ms = jnp.mean(x * x, axis=-1, keepdims=True)
    h = x * lax.rsqrt(ms + RMS_EPS) * g_ref[...]
    h = h * (1.0 + sc_ref[0]) + sh_ref[0]
    hb = h.astype(BF16)

    def proj(lo, hi):
        return _dot(hb, w_ref[:, lo:hi])

    q_ref[...] = (proj(_OFF_Q, _OFF_K) * (ATTN_HEAD_DIM ** -0.5 * LOG2E)).astype(BF16)
    k_ref[...] = proj(_OFF_K, _OFF_IQ).astype(BF16)
    iq_ref[...] = proj(_OFF_IQ, _OFF_IK4).astype(BF16)
    ik4_ref[...] = proj(_OFF_IK4, _OFF_ZR).astype(BF16)
    zr_ref[...] = proj(_OFF_ZR, _OFF_GA)
    ga_ref[...] = proj(_OFF_GA, _OFF_GR).astype(BF16)
    gr_ref[...] = proj(_OFF_GR, _N_PACK).astype(BF16)
    vt_ref[0] = _dot_nt(wvt_ref[...], hb).astype(BF16)
    iwt_ref[0] = _dot_nt(wiwt_ref[...], hb)


def _inproj_call(x2, norm_g, sc, sh, w_in, S):
    T, D = x2.shape
    B = T // S
    tm = min(512, S)
    tpb = S // tm
    w_pack, wvt, wiwt = _pack_w_in(w_in)
    row = lambda i: (i, 0)
    per_b = lambda i: (i // tpb, 0, 0)
    colblk = lambda i: (i // tpb, 0, i % tpb)
    const = lambda shape: pl.BlockSpec(shape, lambda i: (0,) * len(shape), pipeline_mode=pl.Buffered(1))
    rows_out = ((512, BF16), (512, BF16), (512, BF16), (512, BF16), (RWKV_IN, F32), (D, BF16), (D, BF16))
    out_specs = [pl.BlockSpec((tm, w), row) for w, _ in rows_out]
    out_shape = [jax.ShapeDtypeStruct((T, w), dt) for w, dt in rows_out]
    out_specs[2:2] = [pl.BlockSpec((1, ATTN_W, tm), colblk)]
    out_shape[2:2] = [jax.ShapeDtypeStruct((B, ATTN_W, S), BF16)]
    out_specs[5:5] = [pl.BlockSpec((1, IDX_HEADS, tm), colblk)]
    out_shape[5:5] = [jax.ShapeDtypeStruct((B, IDX_HEADS, S), F32)]
    return pl.pallas_call(
        _inproj_kernel,
        grid=(T // tm,),
        in_specs=[pl.BlockSpec((tm, D), row),
                  pl.BlockSpec((1, D), lambda i: (0, 0)),
                  pl.BlockSpec((1, 1, D), per_b),
                  pl.BlockSpec((1, 1, D), per_b),
                  const((D, _N_PACK)), const((ATTN_W, D)), const((IDX_HEADS, D))],
        out_specs=out_specs,
        out_shape=out_shape,
        compiler_params=_cparams(("parallel",)),
        name="inproj",
    )(x2, norm_g.reshape(1, D), sc.reshape(B, 1, D), sh.reshape(B, 1, D), w_pack, wvt, wiwt)


def _t5_bucket(rel):
    n = jnp.maximum(rel, 0)
    max_exact = N_BUCKETS // 2
    nf = jnp.maximum(n, 1).astype(F32)
    large = max_exact + (jnp.log(nf / max_exact) / math.log(MAX_DISTANCE / max_exact)
                         * (N_BUCKETS - max_exact)).astype(I32)
    large = jnp.minimum(large, N_BUCKETS - 1)
    return jnp.where(n < max_exact, n, large)


def _bias_kernel(bucket_ref, rb_ref, o_ref):
    h = pl.program_id(0)
    bk = bucket_ref[...]
    out = jnp.zeros(bk.shape, F32)
    for b in range(N_BUCKETS):
        out = jnp.where(bk == b, rb_ref[b, h] * LOG2E, out)
    o_ref[0] = out


def _bias_call(rel_bias, tq):
    r = jnp.arange(tq, dtype=I32)[None, :]
    c = jnp.arange(tq, dtype=I32)[:, None]
    buckets = jnp.stack([_t5_bucket(r - c), _t5_bucket(tq + r - c)])
    return pl.pallas_call(
        _bias_kernel,
        grid=(N_ATTN_HEADS,),
        in_specs=[pl.BlockSpec((2, tq, tq), lambda h: (0, 0, 0)),
                  pl.BlockSpec(memory_space=pltpu.SMEM)],
        out_specs=pl.BlockSpec((1, 2, tq, tq), lambda h: (h, 0, 0, 0)),
        out_shape=jax.ShapeDtypeStruct((N_ATTN_HEADS, 2, tq, tq), F32),
        compiler_params=_cparams(("arbitrary",)),
        name="bias",
    )(buckets, rel_bias)


def _index_kernel(iq_ref, iwt_ref, ik4_ref, lower_ref, mask_ref, key_ref, k16_ref, *, t, nk, top_k, scale):
    qi = pl.program_id(1)
    nkt = qi + 1
    ksub = LANES
    qpos = qi * t + lax.broadcasted_iota(I32, (ksub, t), 1)

    def score_tile(kt, carry):
        kbase = pl.multiple_of(kt * t, t)
        for ks in range(t // ksub):
            acc = jnp.zeros((ksub, t), F32)
            ik_rows = ik4_ref[0, pl.ds(kbase + ks * ksub, ksub), :]
            ik_stack = jnp.concatenate([ik_rows[:, j * LANES:(j + 1) * LANES] for j in range(IDX_PER_BLOCK)],
                                       axis=0)
            for g in range(IDX_HEADS // IDX_PER_BLOCK):
                d4 = _dot_nt(ik_stack, iq_ref[0, :, g * LANES:(g + 1) * LANES])
                for j in range(IDX_PER_BLOCK):
                    h = g * IDX_PER_BLOCK + j
                    acc = acc + jnp.maximum(d4[j * ksub:(j + 1) * ksub], 0.0) * iwt_ref[0, h:h + 1, :]
            s = acc * scale
            kpos = kt * t + ks * ksub + lax.broadcasted_iota(I32, (ksub, t), 0)
            s = jnp.where(kpos <= qpos, s, -jnp.inf)
            bits = pltpu.bitcast(s, I32)
            key = bits ^ ((bits >> 31) & 0x7FFFFFFF)
            key_ref[kt, ks * ksub:(ks + 1) * ksub, :] = key
            k16_ref[kt, ks * ksub:(ks + 1) * ksub, :] = (key >> 16).astype(I16)
        return carry

    lax.fori_loop(0, nkt, score_tile, 0)

    pack = 16

    def search16():
        def bit_body(i, ans):
            cand = ans | lax.shift_left(jnp.int32(1), 15 - i)
            cand16 = (cand - 32768).astype(I16)

            def cnt_body(kt, acc):
                one = jnp.where(k16_ref[kt] >= cand16, jnp.int16(1), jnp.int16(0))
                for r in range(t // pack):
                    acc = acc + one[r * pack:(r + 1) * pack, :]
                return acc

            acc = lax.fori_loop(0, nkt, cnt_body, jnp.zeros((pack, t), I16))
            cnt = jnp.sum(acc.astype(I32), axis=0, keepdims=True)
            return jnp.where(cnt >= top_k, cand, ans)

        return lax.fori_loop(0, 16, bit_body, jnp.zeros((1, t), I32))

    hi = search16() - 32768

    def remap_body(kt, carry):
        key = key_ref[kt]
        khi = key >> 16
        lo = (key & 0xFFFF) - 32768
        k16_ref[kt] = jnp.where(khi > hi, 32767, jnp.where(khi == hi, lo, -32768)).astype(I16)
        return carry

    lax.fori_loop(0, nkt, remap_body, 0)
    thr = hi * 65536 + search16()

    def count(pred):
        def body(kt, acc):
            one = jnp.where(pred(key_ref[kt]), 1.0, 0.0)
            return acc + jnp.sum(one.reshape(t // 8, 8, t), axis=0)
        return jnp.sum(lax.fori_loop(0, nkt, body, jnp.zeros((8, t), F32)), axis=0, keepdims=True)

    n_ge = count(lambda keys: keys >= thr)
    has_tie = jnp.max(n_ge) > float(top_k)

    @pl.when(jnp.logical_not(has_tie))
    def _():
        def mask_body(kt, carry):
            keys = key_ref[kt]
            sel = (keys >= thr) & (keys > KEY_NEG_INF)
            mask_ref[0, 0, kt] = jnp.where(sel, 0.0, -jnp.inf).astype(BF16)
            return carry

        lax.fori_loop(0, nkt, mask_body, 0)

    @pl.when(has_tie)
    def _():
        need = float(top_k) - count(lambda keys: keys > thr)

        def mask_body(kt, seen):
            keys = key_ref[kt]
            tie = (keys == thr) & (keys > KEY_NEG_INF)
            tie_b = jnp.where(tie, 1.0, 0.0).astype(BF16)
            before = seen + _dot(lower_ref[...], tie_b)
            sel = (keys > thr) | (tie & (before < need))
            mask_ref[0, 0, kt] = jnp.where(sel, 0.0, -jnp.inf).astype(BF16)
            return seen + jnp.sum(tie_b.astype(F32).reshape(t // 8, 8, t).sum(axis=0), axis=0, keepdims=True)

        lax.fori_loop(0, nkt, mask_body, jnp.zeros((1, t), F32))

    def fill_body(kt, carry):
        mask_ref[0, 0, kt] = jnp.full((t, t), -jnp.inf, BF16)
        return carry

    lax.fori_loop(nkt, nk, fill_body, 0)


def _index_call(iq, iwt, ik4, t, top_k):
    B, S, _ = iq.shape
    n = S // t
    scale = (IDX_HEADS ** -0.5) * (IDX_DIM ** -0.5)
    kern = functools.partial(_index_kernel, t=t, nk=n, top_k=top_k, scale=scale)
    pos = jnp.arange(t)
    lower = (pos[None, :] < pos[:, None]).astype(BF16)
    return pl.pallas_call(
        kern,
        grid=(B, n),
        in_specs=[pl.BlockSpec((1, t, IDX_Q), lambda b, i: (b, i, 0)),
                  pl.BlockSpec((1, IDX_HEADS, t), lambda b, i: (b, 0, i)),
                  pl.BlockSpec((1, S, IDX_PER_BLOCK * LANES), lambda b, i: (b, 0, 0)),
                  pl.BlockSpec((t, t), lambda b, i: (0, 0))],
        out_specs=pl.BlockSpec((1, 1, n, t, t), lambda b, i: (b, i, 0, 0, 0)),
        out_shape=jax.ShapeDtypeStruct((B, n, n, t, t), BF16),
        scratch_shapes=[pltpu.VMEM((n, t, t), I32), pltpu.VMEM((n, t, t), I16)],
        compiler_params=_cparams(("parallel", "arbitrary")),
        name="index",
    )(iq, iwt, ik4, lower)


ONES_ROWS = 16


def _attn_kernel(qi_tab, kt_tab, q_ref, k_ref, vt_ref, mask_ref, bias_ref, rb_ref, o_ref,
                 qz_ref, m_ref, acc_ref, s_ref, *, t):
    s_id = pl.program_id(1)
    qi = qi_tab[s_id]
    kt = kt_tab[s_id]
    dh = ATTN_HEAD_DIM

    @pl.when(kt == 0)
    def _():
        m_ref[...] = jnp.full(m_ref.shape, -jnp.inf, F32)
        acc_ref[...] = jnp.zeros(acc_ref.shape, F32)
        lane = lax.broadcasted_iota(I32, (t, LANES), 1)
        for h in range(N_ATTN_HEADS):
            blk = q_ref[0, :, (h // 2) * LANES:(h // 2 + 1) * LANES]
            keep = (lane < dh) if h % 2 == 0 else (lane >= dh)
            qz_ref[h] = jnp.where(keep, blk, jnp.zeros_like(blk))

    def step(bias_tile, bias_const):
        maskf = mask_ref[0, 0, 0].astype(F32)
        ones = jnp.ones((ONES_ROWS, t), BF16)

        for h in range(N_ATTN_HEADS):
            k_blk = k_ref[0, :, (h // 2) * LANES:(h // 2 + 1) * LANES]
            s = _dot_nt(k_blk, qz_ref[h]) + maskf
            s_ref[h] = s if bias_tile is None else s + bias_tile(h)
        for h in range(N_ATTN_HEADS):
            s = s_ref[h]
            c = bias_const(h)
            m_old = m_ref[h:h + 1, :]
            m_cur = jnp.max(jnp.max(s.reshape(t // 8, 8, t), axis=0), axis=0, keepdims=True) + c
            m_new = jnp.maximum(m_old, m_cur)
            m_safe = jnp.where(m_new == -jnp.inf, 0.0, m_new)
            alpha = jnp.exp2(m_old - m_safe)
            p = jnp.exp2(s - (m_safe - c)).astype(BF16)
            v_aug = jnp.concatenate([vt_ref[0, h * dh:(h + 1) * dh, :], ones], axis=0)
            acc_ref[h] = alpha * acc_ref[h] + _dot(v_aug, p)
            m_ref[h:h + 1, :] = m_new

    @pl.when(kt == qi)
    def _():
        step(lambda h: bias_ref[h, 0], lambda h: 0.0)

    @pl.when(kt == qi - 1)
    def _():
        step(lambda h: bias_ref[h, 1], lambda h: 0.0)

    @pl.when(kt < qi - 1)
    def _():
        step(None, lambda h: rb_ref[N_BUCKETS - 1, h] * LOG2E)

    @pl.when(kt == qi)
    def _():
        outs = []
        for h in range(N_ATTN_HEADS):
            a = acc_ref[h]
            outs.append(a[:dh, :] / a[dh:dh + 1, :])
        o_ref[0] = jnp.concatenate(outs, axis=0).T.astype(BF16)


def _attn_call(q, k, vt, mask, bias_tiles, rel_bias, t):
    B, S, W = q.shape
    n = S // t
    H = N_ATTN_HEADS
    qi_tab = jnp.asarray([i for i in range(n) for _ in range(i + 1)], I32)
    kt_tab = jnp.asarray([j for i in range(n) for j in range(i + 1)], I32)
    kern = functools.partial(_attn_kernel, t=t)
    grid_spec = pltpu.PrefetchScalarGridSpec(
        num_scalar_prefetch=2,
        grid=(B, int(qi_tab.shape[0])),
        in_specs=[pl.BlockSpec((1, t, W), lambda b, s, qt, kt: (b, qt[s], 0)),
                  pl.BlockSpec((1, t, W), lambda b, s, qt, kt: (b, kt[s], 0)),
                  pl.BlockSpec((1, W, t), lambda b, s, qt, kt: (b, 0, kt[s])),
                  pl.BlockSpec((1, 1, 1, t, t), lambda b, s, qt, kt: (b, qt[s], kt[s], 0, 0)),
                  pl.BlockSpec((H, 2, t, t), lambda b, s, qt, kt: (0, 0, 0, 0), pipeline_mode=pl.Buffered(1)),
                  pl.BlockSpec(memory_space=pltpu.SMEM)],
        out_specs=pl.BlockSpec((1, t, W), lambda b, s, qt, kt: (b, qt[s], 0)),
        scratch_shapes=[pltpu.VMEM((H, t, LANES), BF16),
                        pltpu.VMEM((H, t), F32),
                        pltpu.VMEM((H, ATTN_HEAD_DIM + ONES_ROWS, t), F32),
                        pltpu.VMEM((H, t, t), F32)])
    return pl.pallas_call(
        kern,
        grid_spec=grid_spec,
        out_shape=jax.ShapeDtypeStruct((B, S, W), BF16),
        compiler_params=_cparams(("parallel", "arbitrary")),
        name="attn",
    )(qi_tab, kt_tab, q, k, vt, mask, bias_tiles, rel_bias)


def _blockdiag_rows(x):
    lane = lax.broadcasted_iota(I32, x.shape, 1)
    zero = jnp.zeros_like(x)
    return jnp.concatenate([jnp.where(lane < RWKV_HEAD, x, zero),
                            jnp.where(lane >= RWKV_HEAD, x, zero)], axis=0)


def _rwkv_kernel(z_ref, mu_ref, w0_ref, dup_ref, a0_ref, iup_ref, gup_ref, kk_ref, ka_ref, rk_ref,
                 lng_ref, lnb_ref, seg_ref, tri_ref, o_ref,
                 prev_ref, st_ref, at_ref, rt_ref, bt_ref, kt_ref, bh_ref, kh_ref, v_ref, pc_ref, y_ref,
                 la_ref, lb_ref, mak_ref, arb_ref, ark_ref, wa_ref, wb_ref, g1_ref, g2_ref, h1_ref, h2_ref,
                 *, tt):
    j = pl.program_id(1)
    W = RWKV_W
    C = CHUNK
    nchunk = tt // C
    npair = RWKV_HEADS // 2

    @pl.when(j == 0)
    def _():
        prev_ref[...] = jnp.zeros(prev_ref.shape, F32)
        st_ref[...] = jnp.zeros(st_ref.shape, F32)

    z = z_ref[0]
    row = lax.broadcasted_iota(I32, z.shape, 0)
    z_prev = jnp.where(row == 0, prev_ref[...], pltpu.roll(z, 1, axis=0))
    prev_ref[...] = z[tt - 1:tt, :]
    z = z + mu_ref[...] * (z_prev - z)

    r = z[:, 0:W]
    k = z[:, W:2 * W]
    v = z[:, 2 * W:3 * W]
    wdad = z[:, 3 * W:3 * W + 2 * DECAY_LORA]
    gd = z[:, 3 * W + 2 * DECAY_LORA:]

    w_pre = w0_ref[...] + _dot3(jnp.tanh(wdad), dup_ref[...])
    neg = -w_pre
    softplus = jnp.maximum(neg, 0.0) + jnp.log(1.0 + jnp.exp(-jnp.abs(neg)))
    lw = -jnp.exp(-softplus - 0.5)
    a_lr = _sigmoid(a0_ref[...] + _dot(wdad.astype(BF16), iup_ref[...]))
    g = _dot(_sigmoid(gd).astype(BF16), gup_ref[...])

    seg = seg_ref[...]

    def head_sum(parts):
        half = seg.shape[0]
        cols = [sum(_dot(p[:, lo:lo + half], seg) for p in parts) for lo in range(0, W, half)]
        return jnp.concatenate(cols, axis=1)

    kk = k * kk_ref[...]
    kk = kk / jnp.maximum(jnp.sqrt(head_sum(_split2(kk * kk))), 1e-12)
    k2 = k * (1.0 + (a_lr - 1.0) * ka_ref[...])
    a_vec = -kk
    b_vec = kk * a_lr

    cum = _dot_exact_rhs_lhs(tri_ref[...], lw)
    tot = jnp.concatenate([jnp.broadcast_to(cum[(c + 1) * C - 1:(c + 1) * C, :], (C, W)) for c in range(nchunk)],
                          axis=0)
    p_inv = jnp.exp(-cum)
    p_out = jnp.exp(tot - cum)
    at_ref[...] = a_vec * jnp.exp(cum - lw)
    rt_ref[...] = r * jnp.exp(cum)
    bt_ref[...] = (b_vec * p_inv).astype(BF16)
    kt_ref[...] = (k2 * p_inv).astype(BF16)
    bh_ref[...] = b_vec * p_out
    kh_ref[...] = k2 * p_out
    v_ref[...] = v
    pc_ref[...] = jnp.exp(tot)

    t_i = lax.broadcasted_iota(I32, (C, LANES), 0)
    s_i = lax.broadcasted_iota(I32, (C, LANES), 1) % C
    strict = s_i < t_i
    incl = s_i <= t_i
    r_i = lax.broadcasted_iota(I32, (LANES, LANES), 0)
    c_i = lax.broadcasted_iota(I32, (LANES, LANES), 1)
    same_head = (r_i < RWKV_HEAD) == (c_i < RWKV_HEAD)
    diag = r_i == c_i
    nstage = int(math.log2(C))
    zero = jnp.zeros((C, LANES), F32)
    zsq = jnp.zeros((LANES, LANES), F32)
    units = [(c, p) for c in range(nchunk) for p in range(npair)]

    def sl(c, p):
        return slice(c * C, (c + 1) * C), slice(p * LANES, (p + 1) * LANES)

    def bd2(w):
        wb = w.astype(BF16)
        return jnp.concatenate([_blockdiag_rows(wb[:, :LANES]), _blockdiag_rows(wb[:, LANES:])], axis=1)

    for i, (c, p) in enumerate(units):
        rows, cols = sl(c, p)
        lhs = jnp.concatenate([at_ref[rows, cols], rt_ref[rows, cols]], axis=0).astype(BF16)
        rhs = jnp.concatenate([_blockdiag_rows(bt_ref[rows, cols]),
                               _blockdiag_rows(kt_ref[rows, cols])], axis=0)
        prod = _dot_nt(lhs, rhs)
        la_ref[i] = jnp.where(strict, prod[:C, :LANES], zero).astype(BF16)
        mak_ref[i] = jnp.where(strict, prod[:C, LANES:], zero).astype(BF16)
        arb_ref[i] = jnp.where(incl, prod[C:, :LANES], zero).astype(BF16)
        ark_ref[i] = jnp.where(incl, prod[C:, LANES:], zero).astype(BF16)
    for i, (c, p) in enumerate(units):
        rows, cols = sl(c, p)
        w2 = _dot(mak_ref[i], _blockdiag_rows(v_ref[rows, cols].astype(BF16)))
        wa_ref[i] = jnp.concatenate([at_ref[rows, cols], w2], axis=1)
    l_bufs, w_bufs = (la_ref, lb_ref), (wa_ref, wb_ref)
    for s in range(nstage):
        l_in, l_out = l_bufs[s % 2], l_bufs[(s + 1) % 2]
        w_in, w_out = w_bufs[s % 2], w_bufs[(s + 1) % 2]
        for i in range(len(units)):
            lmat = l_in[i]
            w = w_in[i]
            w_out[i] = w + _dot(lmat, bd2(w))
            if s < nstage - 1:
                l_out[i] = _dot(lmat, _blockdiag_rows(lmat)).astype(BF16)
    w_fin = w_bufs[nstage % 2]
    for i, (c, p) in enumerate(units):
        rows, cols = sl(c, p)
        w = w_fin[i]
        wb = w.astype(BF16)
        vb = v_ref[rows, cols].astype(BF16)
        gg = _dot(arb_ref[i], bd2(w))
        g1_ref[i] = (rt_ref[rows, cols] + gg[:, :LANES]).astype(BF16)
        g2_ref[i] = gg[:, LANES:] + _dot(ark_ref[i], _blockdiag_rows(vb))
        bk_t = jnp.concatenate([bh_ref[rows, cols], kh_ref[rows, cols]], axis=0).T
        hrhs = jnp.concatenate([wb, jnp.concatenate([jnp.zeros((C, LANES), BF16), vb], axis=1)], axis=0)
        hh = _dot(bk_t.astype(BF16), hrhs)
        pc = pc_ref[c * C:c * C + 1, cols]
        h1 = jnp.where(same_head, hh[:, :LANES], zsq) + jnp.where(diag, jnp.broadcast_to(pc, (LANES, LANES)), zsq)
        h1_ref[i] = h1.astype(BF16)
        h2_ref[i] = jnp.where(same_head, hh[:, LANES:], zsq)
    for c in range(nchunk):
        sts = [st_ref[p].astype(BF16) for p in range(npair)]
        for p in range(npair):
            i = c * npair + p
            rows, cols = sl(c, p)
            y_ref[rows, cols] = _dot(g1_ref[i], sts[p]) + g2_ref[i]
            st_ref[p] = _dot(h1_ref[i], sts[p]) + h2_ref[i]

    y = y_ref[...]
    inv_n = 1.0 / RWKV_HEAD
    mean = head_sum([y.astype(BF16)]) * inv_n
    yc = y - mean
    var = head_sum([(yc * yc).astype(BF16)]) * inv_n
    yn = yc * lax.rsqrt(var + GN_EPS) * lng_ref[...] + lnb_ref[...]
    bonus = head_sum([(r * k2 * rk_ref[...]).astype(BF16)]) * v
    o_ref[0] = ((yn + bonus) * g).astype(BF16)


def _dot_exact_rhs_lhs(ones_bf16, x):
    hi, mid, lo = _split3(x)
    return _dot(ones_bf16, hi) + _dot(ones_bf16, mid) + _dot(ones_bf16, lo)


def _rwkv_call(zr3, tshift_mu, decay_w0, decay_up, iclr_a0, iclr_up, gate_up, k_k, k_a, r_k, lnx_g, lnx_b):
    B, S, _ = zr3.shape
    tt = min(256, S)
    W = RWKV_W
    row = lambda a: a.reshape(1, -1).astype(F32)
    dup = jnp.concatenate([decay_up, jnp.zeros((ICLR_LORA, W), F32)], axis=0)
    iup = jnp.concatenate([jnp.zeros((DECAY_LORA, W), F32), iclr_up], axis=0)
    idx = jnp.arange(2 * LANES)
    seg = (idx[:, None] // RWKV_HEAD == idx[None, :] // RWKV_HEAD).astype(BF16)
    t = jnp.arange(tt)
    same_chunk = t[:, None] // CHUNK == t[None, :] // CHUNK
    tri = (same_chunk & (t[None, :] <= t[:, None])).astype(BF16)
    const = lambda shape: pl.BlockSpec(shape, lambda b, j: (0,) * len(shape))
    kern = functools.partial(_rwkv_kernel, tt=tt)
    nu = (tt // CHUNK) * (RWKV_HEADS // 2)
    return pl.pallas_call(
        kern,
        grid=(B, S // tt),
        in_specs=[pl.BlockSpec((1, tt, RWKV_IN), lambda b, j: (b, j, 0)),
                  const((1, RWKV_IN)), const((1, W)), const((2 * DECAY_LORA, W)), const((1, W)),
                  const((2 * ICLR_LORA, W)), const((GATE_LORA, W)), const((1, W)), const((1, W)),
                  const((1, W)), const((1, W)), const((1, W)),
                  const((2 * LANES, 2 * LANES)), const((tt, tt))],
        out_specs=pl.BlockSpec((1, tt, W), lambda b, j: (b, j, 0)),
        out_shape=jax.ShapeDtypeStruct((B, S, W), BF16),
        scratch_shapes=[pltpu.VMEM((1, RWKV_IN), F32),
                        pltpu.VMEM((RWKV_HEADS // 2, LANES, LANES), F32),
                        pltpu.VMEM((tt, W), F32),
                        pltpu.VMEM((tt, W), F32),
                        pltpu.VMEM((tt, W), BF16),
                        pltpu.VMEM((tt, W), BF16),
                        pltpu.VMEM((tt, W), F32),
                        pltpu.VMEM((tt, W), F32),
                        pltpu.VMEM((tt, W), F32),
                        pltpu.VMEM((tt, W), F32),
                        pltpu.VMEM((tt, W), F32),
                        pltpu.VMEM((nu, CHUNK, LANES), BF16),
                        pltpu.VMEM((nu, CHUNK, LANES), BF16),
                        pltpu.VMEM((nu, CHUNK, LANES), BF16),
                        pltpu.VMEM((nu, CHUNK, LANES), BF16),
                        pltpu.VMEM((nu, CHUNK, LANES), BF16),
                        pltpu.VMEM((nu, CHUNK, 2 * LANES), F32),
                        pltpu.VMEM((nu, CHUNK, 2 * LANES), F32),
                        pltpu.VMEM((nu, CHUNK, LANES), BF16),
                        pltpu.VMEM((nu, CHUNK, LANES), F32),
                        pltpu.VMEM((nu, LANES, LANES), BF16),
                        pltpu.VMEM((nu, LANES, LANES), F32)],
        compiler_params=_cparams(("parallel", "arbitrary")),
        name="rwkv",
    )(zr3, row(tshift_mu), row(decay_w0), dup, row(iclr_a0), iup.astype(BF16), gate_up.astype(BF16), row(k_k),
      row(k_a), row(r_k), row(lnx_g), row(lnx_b), seg, tri)


def _merge_kernel(x_ref, attn_ref, rw_ref, ga_ref, gr_ref, wa_ref, wr_ref, wo_ref, g1_ref,
                  n2_ref, sc_ref, sh_ref, rwt_ref, rb_ref, x1_ref, h2_ref, gt_ref, cnt_ref):
    a = _dot(attn_ref[...], wa_ref[...])
    rr = _dot(rw_ref[...], wr_ref[...])
    mixed = _gate_sigmoid(ga_ref[...].astype(F32)) * a + _gate_sigmoid(gr_ref[...].astype(F32)) * rr
    x1 = x_ref[...] + g1_ref[0] * _dot(mixed.astype(BF16), wo_ref[...])
    x1_ref[...] = x1
    ms = jnp.mean(x1 * x1, axis=-1, keepdims=True)
    h2 = x1 * lax.rsqrt(ms + RMS_EPS) * n2_ref[...]
    h2 = h2 * (1.0 + sc_ref[0]) + sh_ref[0]
    h2_ref[...] = _pack_bf16_pairs(h2)

    tm = x1.shape[0]
    E, G, EG = N_EXPERTS, N_GROUPS, N_EXPERTS // N_GROUPS
    scores = _sigmoid(_dot3(rwt_ref[...], h2, nt=True))
    choice = scores + rb_ref[...]
    c3 = choice.reshape(G, EG, tm)
    e_i = lax.broadcasted_iota(I32, (G, EG, tm), 1)
    m1 = jnp.max(c3, axis=1, keepdims=True)
    first = jnp.min(jnp.where(c3 == m1, e_i, EG), axis=1, keepdims=True)
    m2 = jnp.max(jnp.where(e_i == first, -jnp.inf, c3), axis=1, keepdims=True)
    grp = (m1 + m2).reshape(G, tm)
    g_i = lax.broadcasted_iota(I32, (G, tm), 0)
    rank = jnp.zeros((G, tm), I32)
    for o in range(G):
        other = grp[o:o + 1, :]
        rank = rank + jnp.where((other > grp) | ((other == grp) & (o < g_i)), 1, 0)
    gsel = rank < TOPK_GROUPS
    esel = jnp.broadcast_to(gsel.reshape(G, 1, tm), (G, EG, tm)).reshape(E, tm)
    mc = jnp.where(esel, choice, -jnp.inf)
    s_i = lax.broadcasted_iota(I32, (EG, tm), 0)
    rows = [mc[r * EG:(r + 1) * EG, :] for r in range(G)]
    ranks = [jnp.zeros((EG, tm), I32) for _ in range(G)]
    for o in range(E):
        ro, so = divmod(o, EG)
        other = mc[o:o + 1, :]
        for r in range(G):
            ge = jnp.where(other >= rows[r], 1, 0)
            gt = jnp.where(other > rows[r], 1, 0)
            if r > ro:
                beats = ge
            elif r < ro:
                beats = gt
            else:
                beats = jnp.where(s_i > so, ge, gt)
            ranks[r] = ranks[r] + beats
    top = jnp.concatenate(ranks, axis=0) < MOE_TOPK
    gw = jnp.where(top, scores, 0.0)
    gw = gw / jnp.sum(gw, axis=0, keepdims=True) * ROUTED_SCALE
    gt_ref[...] = gw
    sel = jnp.where(gw > 0.0, 1.0, 0.0)
    cnt_ref[0] = jnp.broadcast_to(jnp.sum(sel, axis=1, keepdims=True), (E, LANES))


def _pack_bf16_pairs(x):
    n = x.shape[1] // 2
    bits = pltpu.bitcast(x.astype(BF16).astype(F32), I32)
    return bits[:, :n] | lax.shift_right_logical(bits[:, n:], 16)


def _unpack_bf16_pairs(p):
    hi = pltpu.bitcast(p & jnp.int32(-65536), F32)
    lo = pltpu.bitcast(lax.shift_left(p, 16), F32)
    return jnp.concatenate([hi, lo], axis=1).astype(BF16)


def _merge_call(x2, attn, rw, ga, gr, wa, wr, wo, g1, norm2_g, sc2, sh2, router_w, router_bias, S):
    T, D = x2.shape
    B = T // S
    tm = min(MERGE_TILE, S)
    tpb = S // tm
    nt = T // tm
    E = N_EXPERTS
    row = lambda i: (i, 0)
    per_b = lambda i: (i // tpb, 0, 0)
    const = lambda shape: pl.BlockSpec(shape, lambda i: (0,) * len(shape))
    return pl.pallas_call(
        _merge_kernel,
        grid=(nt,),
        in_specs=[pl.BlockSpec((tm, D), row), pl.BlockSpec((tm, ATTN_W), row), pl.BlockSpec((tm, RWKV_W), row),
                  pl.BlockSpec((tm, D), row), pl.BlockSpec((tm, D), row),
                  const((ATTN_W, D)), const((RWKV_W, D)), const((D, D)),
                  pl.BlockSpec((1, 1, D), per_b), const((1, D)),
                  pl.BlockSpec((1, 1, D), per_b), pl.BlockSpec((1, 1, D), per_b),
                  const((E, D)), const((E, 1))],
        out_specs=[pl.BlockSpec((tm, D), row), pl.BlockSpec((tm, D // 2), row),
                   pl.BlockSpec((E, tm), lambda i: (0, i)), pl.BlockSpec((1, E, LANES), lambda i: (i, 0, 0))],
        out_shape=[jax.ShapeDtypeStruct((T, D), F32), jax.ShapeDtypeStruct((T, D // 2), I32),
                   jax.ShapeDtypeStruct((E, T), F32), jax.ShapeDtypeStruct((nt, E, LANES), F32)],
        compiler_params=_cparams(("parallel",)),
        name="merge",
    )(x2, attn, rw, ga, gr, wa, wr, wo, g1.reshape(B, 1, D), norm2_g.reshape(1, D),
      sc2.reshape(B, 1, D), sh2.reshape(B, 1, D), router_w.T, router_bias.reshape(E, 1))


def _plan_kernel(gt_ref, cnt_ref, upper_ref, lowe_ref, dest_ref, gw_ref, be_ref, off_ref, *, tm, n_blocks):
    i = pl.program_id(0)
    E = N_EXPERTS
    lowe = lowe_ref[...]

    @pl.when(i == 0)
    def _():
        total = jnp.sum(cnt_ref[...], axis=0)
        nblk = jnp.floor((total + (MOE_BLOCK - 1)) * (1.0 / MOE_BLOCK))
        start_blk = _dot_exact_rhs_lhs(lowe, nblk)
        off_ref[...] = start_blk * MOE_BLOCK
        end_blk = start_blk + nblk
        b_i = lax.broadcasted_iota(I32, (E, n_blocks), 1).astype(F32)
        e_of_b = jnp.sum(jnp.where(end_blk[:, :1] <= b_i, 1.0, 0.0), axis=0, keepdims=True)
        be_ref[...] = e_of_b.astype(I32)

    gt = gt_ref[...]
    sel = gt > 0.0
    selb = jnp.where(sel, 1.0, 0.0).astype(BF16)
    rank = _dot(selb, upper_ref[...])
    dest = off_ref[:, :1] + rank
    off_ref[...] = off_ref[...] + cnt_ref[i]
    kth = _dot(lowe, selb)
    dests, gws = [], []
    for k in range(MOE_TOPK):
        m = sel & (kth == float(k))
        have = jnp.sum(jnp.where(m, 1.0, 0.0), axis=0, keepdims=True)
        d = jnp.sum(jnp.where(m, dest, 0.0), axis=0, keepdims=True)
        dests.append(jnp.where(have > 0.0, d, float((n_blocks - 1) * MOE_BLOCK)))
        gws.append(jnp.sum(jnp.where(m, gt, 0.0), axis=0, keepdims=True))
    dest_ref[...] = jnp.concatenate(dests, axis=0).astype(I32)
    gpad = jnp.concatenate(gws + [jnp.zeros((LANES - MOE_TOPK, tm), F32)], axis=0)
    gw_ref[...] = gpad.T


def _plan_call(gate_t, cnt, n_blocks, tile0):
    E = gate_t.shape[0]
    nt = cnt.shape[0]
    tm = MERGE_TILE
    T = nt * tm
    idx = jnp.arange(tm)
    upper = (idx[:, None] < idx[None, :]).astype(BF16)
    ei = jnp.arange(E)
    lowe = (ei[None, :] < ei[:, None]).astype(BF16)
    kern = functools.partial(_plan_kernel, tm=tm, n_blocks=n_blocks)
    const = lambda shape: pl.BlockSpec(shape, lambda i: (0,) * len(shape))
    return pl.pallas_call(
        kern,
        grid=(nt,),
        in_specs=[pl.BlockSpec((E, tm), lambda i: (0, i + tile0)), const((nt, E, LANES)), const((tm, tm)),
                  const((E, E))],
        out_specs=[pl.BlockSpec((MOE_TOPK, tm), lambda i: (0, i)), pl.BlockSpec((tm, LANES), lambda i: (i, 0)),
                   const((1, n_blocks))],
        out_shape=[jax.ShapeDtypeStruct((MOE_TOPK, T), I32), jax.ShapeDtypeStruct((T, LANES), F32),
                   jax.ShapeDtypeStruct((1, n_blocks), I32)],
        scratch_shapes=[pltpu.VMEM((E, LANES), F32)],
        compiler_params=_cparams(("arbitrary",)),
        name="plan",
    )(gate_t, cnt, upper, lowe)


def _sc_index_layout(dest_t):
    K, T = dest_t.shape
    n_ch = T // (SC_WORKERS * SC_ROWS)
    return dest_t.reshape(K, SC_WORKERS, n_ch, SC_ROWS).transpose(1, 2, 0, 3).reshape(SC_WORKERS, n_ch * K, SC_ROWS)


def _sc_dispatch(rows, idx, n_slots, tok0):
    W = rows.shape[1]
    n_ch = idx.shape[1] // MOE_TOPK
    T = n_ch * SC_WORKERS * SC_ROWS
    tpw = T // SC_WORKERS
    mesh = plsc.VectorSubcoreMesh(core_axis_name="c", subcore_axis_name="s")

    @functools.partial(
        pl.kernel, mesh=mesh,
        out_type=jax.ShapeDtypeStruct((n_slots, W), I32),
        scratch_types=[pltpu.VMEM((n_ch * MOE_TOPK, SC_ROWS), I32), pltpu.VMEM((SC_ROWS, W), I32),
                       pltpu.SemaphoreType.DMA])
    def kern(x_hbm, idx_hbm, o_hbm, idx_v, rows_v, sem):
        wid = lax.axis_index("s") * SC_CORES + lax.axis_index("c")
        pltpu.sync_copy(idx_hbm.at[wid], idx_v)

        @pl.loop(0, n_ch)
        def _(j):
            pltpu.sync_copy(x_hbm.at[pl.ds(tok0 + wid * tpw + j * SC_ROWS, SC_ROWS)], rows_v)
            copies = [pltpu.async_copy(rows_v, o_hbm.at[idx_v.at[j * MOE_TOPK + k]], sem)
                      for k in range(MOE_TOPK)]
            for cp in copies:
                cp.wait()

    return kern(rows, idx)


def _sc_combine(slots, idx, T):
    _, W = slots.shape
    n_ch = T // (SC_WORKERS * SC_ROWS)
    tpw = T // SC_WORKERS
    mesh = plsc.VectorSubcoreMesh(core_axis_name="c", subcore_axis_name="s")

    @functools.partial(
        pl.kernel, mesh=mesh,
        out_type=jax.ShapeDtypeStruct((MOE_TOPK, T, W), I32),
        scratch_types=[pltpu.VMEM((n_ch * MOE_TOPK, SC_ROWS), I32), pltpu.VMEM((SC_ROWS, W), I32),
                       pltpu.SemaphoreType.DMA])
    def kern(s_hbm, idx_hbm, o_hbm, idx_v, rows_v, sem):
        wid = lax.axis_index("s") * SC_CORES + lax.axis_index("c")
        pltpu.sync_copy(idx_hbm.at[wid], idx_v)

        @pl.loop(0, n_ch)
        def _(j):
            for k in range(MOE_TOPK):
                pltpu.async_copy(s_hbm.at[idx_v.at[j * MOE_TOPK + k]], rows_v, sem).wait()
                pltpu.sync_copy(rows_v, o_hbm.at[k, pl.ds(wid * tpw + j * SC_ROWS, SC_ROWS)])

    return kern(slots, idx)


def _ffn_kernel(be_ref, x_ref, eg_ref, eu_ref, ed_ref, o_ref):
    used = be_ref[pl.program_id(0)] < N_EXPERTS

    @pl.when(used)
    def _():
        x = _unpack_bf16_pairs(x_ref[...])
        a = _dot(x, eg_ref[0].astype(BF16))
        u = _dot(x, eu_ref[0].astype(BF16))
        o_ref[...] = _pack_bf16_pairs(_dot((a * _gate_sigmoid(a) * u).astype(BF16), ed_ref[0].astype(BF16)))

    @pl.when(jnp.logical_not(used))
    def _():
        o_ref[...] = jnp.zeros(o_ref.shape, I32)


def _ffn_call(xs, block_e, eg, eu, ed, n_blocks):
    P, W = xs.shape
    D, FF = 2 * W, EXPERT_FF
    grid_spec = pltpu.PrefetchScalarGridSpec(
        num_scalar_prefetch=1,
        grid=(n_blocks,),
        in_specs=[pl.BlockSpec((MOE_BLOCK, W), lambda b, be: (b, 0)),
                  pl.BlockSpec((1, D, FF), lambda b, be: (jnp.minimum(be[b], N_EXPERTS - 1), 0, 0)),
                  pl.BlockSpec((1, D, FF), lambda b, be: (jnp.minimum(be[b], N_EXPERTS - 1), 0, 0)),
                  pl.BlockSpec((1, FF, D), lambda b, be: (jnp.minimum(be[b], N_EXPERTS - 1), 0, 0))],
        out_specs=pl.BlockSpec((MOE_BLOCK, W), lambda b, be: (b, 0)))
    return pl.pallas_call(
        _ffn_kernel,
        grid_spec=grid_spec,
        out_shape=jax.ShapeDtypeStruct((P, W), I32),
        compiler_params=_cparams(("parallel",)),
        name="ffn",
    )(block_e, xs, eg, eu, ed)


def _final_kernel(h_ref, c_ref, gw_ref, x1_ref, g2_ref, fg_ref, sg_ref, su_ref, sd_ref, o_ref):
    h = _unpack_bf16_pairs(h_ref[...])
    a = _dot(h, sg_ref[...])
    u = _dot(h, su_ref[...])
    moe = _dot((a * _gate_sigmoid(a) * u).astype(BF16), sd_ref[...])
    gw = gw_ref[...]
    for k in range(MOE_TOPK):
        w = gw[:, k:k + 1]
        y = _unpack_bf16_pairs(c_ref[k]).astype(F32)
        moe = moe + jnp.where(w > 0.0, w * y, 0.0)
    x2 = x1_ref[...] + g2_ref[0] * moe
    ms = jnp.mean(x2 * x2, axis=-1, keepdims=True)
    o_ref[...] = x2 * lax.rsqrt(ms + RMS_EPS) * fg_ref[...]


def _final_call(h2p, comb, gw, x1, g2, final_g, sg, su, sd, S, tile0, prev_out):
    T, W = h2p.shape
    D, FF = 2 * W, EXPERT_FF
    B = T // S
    tm = MERGE_TILE
    tpb = S // tm
    ntile = comb.shape[1] // tm
    row = lambda i: (i, 0)
    full_row = lambda i: (i + tile0, 0)
    const = lambda shape: pl.BlockSpec(shape, lambda i: (0,) * len(shape))
    in_specs = [pl.BlockSpec((tm, W), full_row), pl.BlockSpec((MOE_TOPK, tm, W), lambda i: (0, i, 0)),
                pl.BlockSpec((tm, LANES), row), pl.BlockSpec((tm, D), full_row),
                pl.BlockSpec((1, 1, D), lambda i: ((i + tile0) // tpb, 0, 0)), const((1, D)),
                const((D, FF)), const((D, FF)), const((FF, D))]
    args = [h2p, comb, gw, x1, g2.reshape(B, 1, D), final_g.reshape(1, D), sg, su, sd]
    kern, aliases = _final_kernel, {}
    if prev_out is not None:
        in_specs.append(pl.BlockSpec(memory_space=pl.ANY))
        args.append(prev_out)
        aliases = {len(args) - 1: 0}
        kern = lambda *refs: _final_kernel(*refs[:9], refs[10])
    return pl.pallas_call(
        kern,
        grid=(ntile,),
        in_specs=in_specs,
        out_specs=pl.BlockSpec((tm, D), full_row),
        out_shape=jax.ShapeDtypeStruct((T, D), F32),
        input_output_aliases=aliases,
        compiler_params=_cparams(("parallel",)),
        name="final",
    )(*args)


def _moe_call(h2p, gate_t, cnt, x1, g2, final_g, eg, eu, ed, sg, su, sd, S):
    T = h2p.shape[0]
    nt = cnt.shape[0]
    groups = MOE_GROUPS if (T // MOE_GROUPS) % (SC_WORKERS * SC_ROWS) == 0 and nt % MOE_GROUPS == 0 else 1
    tg, ntg = T // groups, nt // groups
    n_blocks = (tg * MOE_TOPK) // MOE_BLOCK + N_EXPERTS + 1
    out = None
    for g in range(groups):
        dest_t, gw, block_e = _plan_call(gate_t, cnt[g * ntg:(g + 1) * ntg], n_blocks, g * ntg)
        idx = _sc_index_layout(dest_t)
        xs = _sc_dispatch(h2p, idx, n_blocks * MOE_BLOCK, g * tg)
        ys = _ffn_call(xs, block_e.reshape(n_blocks), eg, eu, ed, n_blocks)
        comb = _sc_combine(ys, idx, tg)
        out = _final_call(h2p, comb, gw, x1, g2, final_g, sg, su, sd, S, g * ntg, out)
    return out


def _layer(x2, c, S, ada_w, ada_b, norm1_g, w_in, rel_bias, tshift_mu, decay_w0, decay_up, iclr_a0, iclr_up,
           gate_up, k_k, k_a, r_k, lnx_g, lnx_b, w_attn_br, w_rwkv_br, w_out, norm2_g, router_w, router_bias,
           exp_gate, exp_up, exp_down, sh_gate, sh_up, sh_down, final_g):
    T, D = x2.shape
    B = T // S
    mod = _mod_call(c, ada_w, ada_b)
    sh1, sc1, g1, sh2, sc2, g2 = jnp.split(mod, 6, axis=-1)

    q, k, vt, iq, ik4, iwt, zr, ga, gr = _inproj_call(x2, norm1_g, sc1, sh1, w_in, S)

    ta = min(512, S)
    assert ta >= LANES and S % ta == 0
    top_k = min(TOPK_MAX, S // 4)
    seq = lambda a: a.reshape(B, S, a.shape[-1])
    mask = _index_call(seq(iq), iwt, seq(ik4), ta, top_k)
    bias_tiles = _bias_call(rel_bias, ta)
    attn = _attn_call(seq(q), seq(k), vt, mask, bias_tiles, rel_bias, ta).reshape(T, ATTN_W)

    rw = _rwkv_call(zr.reshape(B, S, RWKV_IN), tshift_mu, decay_w0, decay_up, iclr_a0, iclr_up, gate_up,
                    k_k, k_a, r_k, lnx_g, lnx_b).reshape(T, RWKV_W)

    x1, h2p, gate_t, cnt = _merge_call(x2, attn, rw, ga, gr, w_attn_br.astype(BF16), w_rwkv_br.astype(BF16),
                                       w_out.astype(BF16), g1, norm2_g, sc2, sh2, router_w, router_bias, S)
    return _moe_call(h2p, gate_t, cnt, x1, g2, final_g, exp_gate, exp_up, exp_down,
                     sh_gate.astype(BF16), sh_up.astype(BF16), sh_down.astype(BF16), S)


def kernel(x, c, ada_w, ada_b, norm1_g, w_in, rel_bias, tshift_mu, decay_w0, decay_up, iclr_a0, iclr_up, gate_up, k_k, k_a, r_k, lnx_g, lnx_b, w_attn_br, w_rwkv_br, w_out, norm2_g, router_w, router_bias, exp_gate, exp_up, exp_down, sh_gate, sh_up, sh_down, final_g):
    B, S, D = x.shape
    depth = ada_w.shape[0]
    assert depth == 1, "the final RMSNorm is fused into the (single) layer's MoE kernel"
    out = _layer(x.reshape(B * S, D), c, S, ada_w[0], ada_b[0], norm1_g[0], w_in[0], rel_bias, tshift_mu[0],
                 decay_w0[0], decay_up[0], iclr_a0[0], iclr_up[0], gate_up[0], k_k[0], k_a[0], r_k[0],
                 lnx_g[0], lnx_b[0], w_attn_br[0], w_rwkv_br[0], w_out[0], norm2_g[0], router_w[0],
                 router_bias[0], exp_gate[0], exp_up[0], exp_down[0], sh_gate[0], sh_up[0], sh_down[0], final_g)
    return out.reshape(B, S, D)
```

```python
import functools
import math

import jax
import jax.numpy as jnp
from jax import lax
from jax.experimental import pallas as pl
from jax.experimental.pallas import tpu as pltpu
from jax.experimental.pallas import tpu_sc as plsc

F32 = jnp.float32
BF16 = jnp.bfloat16
I32 = jnp.int32
I16 = jnp.int16

RMS_EPS = 1e-6
D_MODEL = 1024
N_ATTN_HEADS = 8
ATTN_HEAD_DIM = 64
ATTN_W = 512
IDX_HEADS = 16
IDX_DIM = 32
IDX_Q = 512
TOPK_MAX = 256
N_BUCKETS = 32
MAX_DISTANCE = 128
RWKV_HEADS = 8
RWKV_HEAD = 64
RWKV_W = 512
DECAY_LORA = 64
ICLR_LORA = 64
GATE_LORA = 128
RWKV_IN = 1792
GN_EPS = 64e-5
N_EXPERTS = 64
N_GROUPS = 8
TOPK_GROUPS = 4
MOE_TOPK = 8
EXPERT_FF = 256
ROUTED_SCALE = 2.5

LANES = 128
VMEM_LIMIT = 56 * 1024 * 1024
CHUNK = 64
MERGE_TILE = 512
MOE_BLOCK = 1024
MOE_GROUPS = 2
SC_CORES, SC_SUBCORES = 2, 16
SC_WORKERS = SC_CORES * SC_SUBCORES
SC_ROWS = 128
LOG2E = 1.4426950408889634
INT_MIN = -2147483648
KEY_NEG_INF = -2139095041

NT_DIMS = (((1,), (1,)), ((), ()))


def _cparams(sem):
    return pltpu.CompilerParams(dimension_semantics=sem, vmem_limit_bytes=VMEM_LIMIT)


def _dot(a, b):
    return jnp.dot(a, b, preferred_element_type=F32)


def _dot_nt(a, b):
    return lax.dot_general(a, b, NT_DIMS, preferred_element_type=F32)


def _split2(x):
    hi = x.astype(BF16)
    lo = (x - hi.astype(F32)).astype(BF16)
    return hi, lo


def _split3(x):
    hi = x.astype(BF16)
    r1 = x - hi.astype(F32)
    mid = r1.astype(BF16)
    lo = (r1 - mid.astype(F32)).astype(BF16)
    return hi, mid, lo


def _dot3(a, b, nt=False):
    ah, al = _split2(a)
    bh, bl = _split2(b)
    f = _dot_nt if nt else _dot
    return f(ah, bh) + f(ah, bl) + f(al, bh)


def _sigmoid(x):
    return 1.0 / (1.0 + jnp.exp(-x))


def _gate_sigmoid(x):
    return 0.5 * jnp.tanh(0.5 * x) + 0.5


def _mod_kernel(c_ref, w_ref, b_ref, o_ref):
    c = c_ref[...]
    s = c * _sigmoid(c)
    o_ref[...] = _dot3(s, w_ref[...]) + b_ref[...]


def _mod_call(c, ada_w, ada_b):
    B, D = c.shape
    N = ada_w.shape[1]
    tn = 1024
    return pl.pallas_call(
        _mod_kernel,
        grid=(N // tn,),
        in_specs=[pl.BlockSpec((B, D), lambda j: (0, 0)),
                  pl.BlockSpec((D, tn), lambda j: (0, j)),
                  pl.BlockSpec((1, tn), lambda j: (0, j))],
        out_specs=pl.BlockSpec((B, tn), lambda j: (0, j)),
        out_shape=jax.ShapeDtypeStruct((B, N), F32),
        compiler_params=_cparams(("arbitrary",)),
        name="mod",
    )(c, ada_w, ada_b.reshape(1, N))


_OFF_Q, _OFF_K, _OFF_IQ, _OFF_IK4, _OFF_ZR, _OFF_GA, _OFF_GR, _N_PACK = (
    0, 512, 1024, 1536, 2048, 3840, 4864, 5888)
IDX_PER_BLOCK = LANES // IDX_DIM


def _pack_w_in(w_in):
    D = w_in.shape[0]
    w_ik = w_in[:, 2048:2080]
    ik4 = jnp.zeros((D, IDX_PER_BLOCK * LANES), w_in.dtype)
    for j in range(IDX_PER_BLOCK):
        ik4 = lax.dynamic_update_slice(ik4, w_ik, (0, j * LANES + j * IDX_DIM))
    w_pack = jnp.concatenate([w_in[:, 0:1024], w_in[:, 1536:2048], ik4, w_in[:, 2096:]], axis=1).astype(BF16)
    return w_pack, w_in[:, 1024:1536].T.astype(BF16), w_in[:, 2080:2096].T.astype(BF16)


def _inproj_kernel(x_ref, g_ref, sc_ref, sh_ref, w_ref, wvt_ref, wiwt_ref,
                   q_ref, k_ref, vt_ref, iq_ref, ik4_ref, iwt_ref, zr_ref, ga_ref, gr_ref):
    x = x_ref[...]
    ms = jnp.mean(x * x, axis=-1, keepdims=True)
    h = x * lax.rsqrt(ms + RMS_EPS) * g_ref[...]
    h = h * (1.0 + sc_ref[0]) + sh_ref[0]
    hb = h.astype(BF16)

    def proj(lo, hi):
        return _dot(hb, w_ref[:, lo:hi])

    q_ref[...] = (proj(_OFF_Q, _OFF_K) * (ATTN_HEAD_DIM ** -0.5 * LOG2E)).astype(BF16)
    k_ref[...] = proj(_OFF_K, _OFF_IQ).astype(BF16)
    iq_ref[...] = proj(_OFF_IQ, _OFF_IK4).astype(BF16)
    ik4_ref[...] = proj(_OFF_IK4, _OFF_ZR).astype(BF16)
    zr_ref[...] = proj(_OFF_ZR, _OFF_GA)
    ga_ref[...] = proj(_OFF_GA, _OFF_GR).astype(BF16)
    gr_ref[...] = proj(_OFF_GR, _N_PACK).astype(BF16)
    vt_ref[0] = _dot_nt(wvt_ref[...], hb).astype(BF16)
    iwt_ref[0] = _dot_nt(wiwt_ref[...], hb)


def _inproj_call(x2, norm_g, sc, sh, w_in, S):
    T, D = x2.shape
    B = T // S
    tm = min(512, S)
    tpb = S // tm
    w_pack, wvt, wiwt = _pack_w_in(w_in)
    row = lambda i: (i, 0)
    per_b = lambda i: (i // tpb, 0, 0)
    colblk = lambda i: (i // tpb, 0, i % tpb)
    const = lambda shape: pl.BlockSpec(shape, lambda i: (0,) * len(shape), pipeline_mode=pl.Buffered(1))
    rows_out = ((512, BF16), (512, BF16), (512, BF16), (512, BF16), (RWKV_IN, F32), (D, BF16), (D, BF16))
    out_specs = [pl.BlockSpec((tm, w), row) for w, _ in rows_out]
    out_shape = [jax.ShapeDtypeStruct((T, w), dt) for w, dt in rows_out]
    out_specs[2:2] = [pl.BlockSpec((1, ATTN_W, tm), colblk)]
    out_shape[2:2] = [jax.ShapeDtypeStruct((B, ATTN_W, S), BF16)]
    out_specs[5:5] = [pl.BlockSpec((1, IDX_HEADS, tm), colblk)]
    out_shape[5:5] = [jax.ShapeDtypeStruct((B, IDX_HEADS, S), F32)]
    return pl.pallas_call(
        _inproj_kernel,
        grid=(T // tm,),
        in_specs=[pl.BlockSpec((tm, D), row),
                  pl.BlockSpec((1, D), lambda i: (0, 0)),
                  pl.BlockSpec((1, 1, D), per_b),
                  pl.BlockSpec((1, 1, D), per_b),
                  const((D, _N_PACK)), const((ATTN_W, D)), const((IDX_HEADS, D))],
        out_specs=out_specs,
        out_shape=out_shape,
        compiler_params=_cparams(("parallel",)),
        name="inproj",
    )(x2, norm_g.reshape(1, D), sc.reshape(B, 1, D), sh.reshape(B, 1, D), w_pack, wvt, wiwt)


def _t5_bucket(rel):
    n = jnp.maximum(rel, 0)
    max_exact = N_BUCKETS // 2
    nf = jnp.maximum(n, 1).astype(F32)
    large = max_exact + (jnp.log(nf / max_exact) / math.log(MAX_DISTANCE / max_exact)
                         * (N_BUCKETS - max_exact)).astype(I32)
    large = jnp.minimum(large, N_BUCKETS - 1)
    return jnp.where(n < max_exact, n, large)


def _bias_kernel(bucket_ref, rb_ref, o_ref):
    h = pl.program_id(0)
    bk = bucket_ref[...]
    out = jnp.zeros(bk.shape, F32)
    for b in range(N_BUCKETS):
        out = jnp.where(bk == b, rb_ref[b, h] * LOG2E, out)
    o_ref[0] = out


def _bias_call(rel_bias, tq):
    r = jnp.arange(tq, dtype=I32)[None, :]
    c = jnp.arange(tq, dtype=I32)[:, None]
    buckets = jnp.stack([_t5_bucket(r - c), _t5_bucket(tq + r - c)])
    return pl.pallas_call(
        _bias_kernel,
        grid=(N_ATTN_HEADS,),
        in_specs=[pl.BlockSpec((2, tq, tq), lambda h: (0, 0, 0)),
                  pl.BlockSpec(memory_space=pltpu.SMEM)],
        out_specs=pl.BlockSpec((1, 2, tq, tq), lambda h: (h, 0, 0, 0)),
        out_shape=jax.ShapeDtypeStruct((N_ATTN_HEADS, 2, tq, tq), F32),
        compiler_params=_cparams(("arbitrary",)),
        name="bias",
    )(buckets, rel_bias)


def _index_kernel(iq_ref, iwt_ref, ik4_ref, lower_ref, mask_ref, key_ref, k16_ref, *, t, nk, top_k, scale):
    qi = pl.program_id(1)
    nkt = qi + 1
    ksub = LANES
    qpos = qi * t + lax.broadcasted_iota(I32, (ksub, t), 1)

    def score_tile(kt, carry):
        kbase = pl.multiple_of(kt * t, t)
        for ks in range(t // ksub):
            acc = jnp.zeros((ksub, t), F32)
            ik_rows = ik4_ref[0, pl.ds(kbase + ks * ksub, ksub), :]
            ik_stack = jnp.concatenate([ik_rows[:, j * LANES:(j + 1) * LANES] for j in range(IDX_PER_BLOCK)],
                                       axis=0)
            for g in range(IDX_HEADS // IDX_PER_BLOCK):
                d4 = _dot_nt(ik_stack, iq_ref[0, :, g * LANES:(g + 1) * LANES])
                for j in range(IDX_PER_BLOCK):
                    h = g * IDX_PER_BLOCK + j
                    acc = acc + jnp.maximum(d4[j * ksub:(j + 1) * ksub], 0.0) * iwt_ref[0, h:h + 1, :]
            s = acc * scale
            kpos = kt * t + ks * ksub + lax.broadcasted_iota(I32, (ksub, t), 0)
            s = jnp.where(kpos <= qpos, s, -jnp.inf)
            bits = pltpu.bitcast(s, I32)
            key = bits ^ ((bits >> 31) & 0x7FFFFFFF)
            key_ref[kt, ks * ksub:(ks + 1) * ksub, :] = key
            k16_ref[kt, ks * ksub:(ks + 1) * ksub, :] = (key >> 16).astype(I16)
        return carry

    lax.fori_loop(0, nkt, score_tile, 0)

    pack = 16

    def search16():
        def bit_body(i, ans):
            cand = ans | lax.shift_left(jnp.int32(1), 15 - i)
            cand16 = (cand - 32768).astype(I16)

            def cnt_body(kt, acc):
                one = jnp.where(k16_ref[kt] >= cand16, jnp.int16(1), jnp.int16(0))
                for r in range(t // pack):
                    acc = acc + one[r * pack:(r + 1) * pack, :]
                return acc

            acc = lax.fori_loop(0, nkt, cnt_body, jnp.zeros((pack, t), I16))
            cnt = jnp.sum(acc.astype(I32), axis=0, keepdims=True)
            return jnp.where(cnt >= top_k, cand, ans)

        return lax.fori_loop(0, 16, bit_body, jnp.zeros((1, t), I32))

    hi = search16() - 32768

    def remap_body(kt, carry):
        key = key_ref[kt]
        khi = key >> 16
        lo = (key & 0xFFFF) - 32768
        k16_ref[kt] = jnp.where(khi > hi, 32767, jnp.where(khi == hi, lo, -32768)).astype(I16)
        return carry

    lax.fori_loop(0, nkt, remap_body, 0)
    thr = hi * 65536 + search16()

    def count(pred):
        def body(kt, acc):
            one = jnp.where(pred(key_ref[kt]), 1.0, 0.0)
            return acc + jnp.sum(one.reshape(t // 8, 8, t), axis=0)
        return jnp.sum(lax.fori_loop(0, nkt, body, jnp.zeros((8, t), F32)), axis=0, keepdims=True)

    n_ge = count(lambda keys: keys >= thr)
    has_tie = jnp.max(n_ge) > float(top_k)

    @pl.when(jnp.logical_not(has_tie))
    def _():
        def mask_body(kt, carry):
            keys = key_ref[kt]
            sel = (keys >= thr) & (keys > KEY_NEG_INF)
            mask_ref[0, 0, kt] = jnp.where(sel, 0.0, -jnp.inf).astype(BF16)
            return carry

        lax.fori_loop(0, nkt, mask_body, 0)

    @pl.when(has_tie)
    def _():
        need = float(top_k) - count(lambda keys: keys > thr)

        def mask_body(kt, seen):
            keys = key_ref[kt]
            tie = (keys == thr) & (keys > KEY_NEG_INF)
            tie_b = jnp.where(tie, 1.0, 0.0).astype(BF16)
            before = seen + _dot(lower_ref[...], tie_b)
            sel = (keys > thr) | (tie & (before < need))
            mask_ref[0, 0, kt] = jnp.where(sel, 0.0, -jnp.inf).astype(BF16)
            return seen + jnp.sum(tie_b.astype(F32).reshape(t // 8, 8, t).sum(axis=0), axis=0, keepdims=True)

        lax.fori_loop(0, nkt, mask_body, jnp.zeros((1, t), F32))

    def fill_body(kt, carry):
        mask_ref[0, 0, kt] = jnp.full((t, t), -jnp.inf, BF16)
        return carry

    lax.fori_loop(nkt, nk, fill_body, 0)


def _index_call(iq, iwt, ik4, t, top_k):
    B, S, _ = iq.shape
    n = S // t
    scale = (IDX_HEADS ** -0.5) * (IDX_DIM ** -0.5)
    kern = functools.partial(_index_kernel, t=t, nk=n, top_k=top_k, scale=scale)
    pos = jnp.arange(t)
    lower = (pos[None, :] < pos[:, None]).astype(BF16)
    return pl.pallas_call(
        kern,
        grid=(B, n),
        in_specs=[pl.BlockSpec((1, t, IDX_Q), lambda b, i: (b, i, 0)),
                  pl.BlockSpec((1, IDX_HEADS, t), lambda b, i: (b, 0, i)),
                  pl.BlockSpec((1, S, IDX_PER_BLOCK * LANES), lambda b, i: (b, 0, 0)),
                  pl.BlockSpec((t, t), lambda b, i: (0, 0))],
        out_specs=pl.BlockSpec((1, 1, n, t, t), lambda b, i: (b, i, 0, 0, 0)),
        out_shape=jax.ShapeDtypeStruct((B, n, n, t, t), BF16),
        scratch_shapes=[pltpu.VMEM((n, t, t), I32), pltpu.VMEM((n, t, t), I16)],
        compiler_params=_cparams(("parallel", "arbitrary")),
        name="index",
    )(iq, iwt, ik4, lower)


ONES_ROWS = 16


def _attn_kernel(qi_tab, kt_tab, q_ref, k_ref, vt_ref, mask_ref, bias_ref, rb_ref, o_ref,
                 qz_ref, m_ref, acc_ref, s_ref, *, t):
    s_id = pl.program_id(1)
    qi = qi_tab[s_id]
    kt = kt_tab[s_id]
    dh = ATTN_HEAD_DIM

    @pl.when(kt == 0)
    def _():
        m_ref[...] = jnp.full(m_ref.shape, -jnp.inf, F32)
        acc_ref[...] = jnp.zeros(acc_ref.shape, F32)
        lane = lax.broadcasted_iota(I32, (t, LANES), 1)
        for h in range(N_ATTN_HEADS):
            blk = q_ref[0, :, (h // 2) * LANES:(h // 2 + 1) * LANES]
            keep = (lane < dh) if h % 2 == 0 else (lane >= dh)
            qz_ref[h] = jnp.where(keep, blk, jnp.zeros_like(blk))

    def step(bias_tile, bias_const):
        maskf = mask_ref[0, 0, 0].astype(F32)
        ones = jnp.ones((ONES_ROWS, t), BF16)

        for h in range(N_ATTN_HEADS):
            k_blk = k_ref[0, :, (h // 2) * LANES:(h // 2 + 1) * LANES]
            s = _dot_nt(k_blk, qz_ref[h]) + maskf
            s_ref[h] = s if bias_tile is None else s + bias_tile(h)
        for h in range(N_ATTN_HEADS):
            s = s_ref[h]
            c = bias_const(h)
            m_old = m_ref[h:h + 1, :]
            m_cur = jnp.max(jnp.max(s.reshape(t // 8, 8, t), axis=0), axis=0, keepdims=True) + c
            m_new = jnp.maximum(m_old, m_cur)
            m_safe = jnp.where(m_new == -jnp.inf, 0.0, m_new)
            alpha = jnp.exp2(m_old - m_safe)
            p = jnp.exp2(s - (m_safe - c)).astype(BF16)
            v_aug = jnp.concatenate([vt_ref[0, h * dh:(h + 1) * dh, :], ones], axis=0)
            acc_ref[h] = alpha * acc_ref[h] + _dot(v_aug, p)
            m_ref[h:h + 1, :] = m_new

    @pl.when(kt == qi)
    def _():
        step(lambda h: bias_ref[h, 0], lambda h: 0.0)

    @pl.when(kt == qi - 1)
    def _():
        step(lambda h: bias_ref[h, 1], lambda h: 0.0)

    @pl.when(kt < qi - 1)
    def _():
        step(None, lambda h: rb_ref[N_BUCKETS - 1, h] * LOG2E)

    @pl.when(kt == qi)
    def _():
        outs = []
        for h in range(N_ATTN_HEADS):
            a = acc_ref[h]
            outs.append(a[:dh, :] / a[dh:dh + 1, :])
        o_ref[0] = jnp.concatenate(outs, axis=0).T.astype(BF16)


def _attn_call(q, k, vt, mask, bias_tiles, rel_bias, t):
    B, S, W = q.shape
    n = S // t
    H = N_ATTN_HEADS
    qi_tab = jnp.asarray([i for i in range(n) for _ in range(i + 1)], I32)
    kt_tab = jnp.asarray([j for i in range(n) for j in range(i + 1)], I32)
    kern = functools.partial(_attn_kernel, t=t)
    grid_spec = pltpu.PrefetchScalarGridSpec(
        num_scalar_prefetch=2,
        grid=(B, int(qi_tab.shape[0])),
        in_specs=[pl.BlockSpec((1, t, W), lambda b, s, qt, kt: (b, qt[s], 0)),
                  pl.BlockSpec((1, t, W), lambda b, s, qt, kt: (b, kt[s], 0)),
                  pl.BlockSpec((1, W, t), lambda b, s, qt, kt: (b, 0, kt[s])),
                  pl.BlockSpec((1, 1, 1, t, t), lambda b, s, qt, kt: (b, qt[s], kt[s], 0, 0)),
                  pl.BlockSpec((H, 2, t, t), lambda b, s, qt, kt: (0, 0, 0, 0), pipeline_mode=pl.Buffered(1)),
                  pl.BlockSpec(memory_space=pltpu.SMEM)],
        out_specs=pl.BlockSpec((1, t, W), lambda b, s, qt, kt: (b, qt[s], 0)),
        scratch_shapes=[pltpu.VMEM((H, t, LANES), BF16),
                        pltpu.VMEM((H, t), F32),
                        pltpu.VMEM((H, ATTN_HEAD_DIM + ONES_ROWS, t), F32),
                        pltpu.VMEM((H, t, t), F32)])
    return pl.pallas_call(
        kern,
        grid_spec=grid_spec,
        out_shape=jax.ShapeDtypeStruct((B, S, W), BF16),
        compiler_params=_cparams(("parallel", "arbitrary")),
        name="attn",
    )(qi_tab, kt_tab, q, k, vt, mask, bias_tiles, rel_bias)


def _blockdiag_rows(x):
    lane = lax.broadcasted_iota(I32, x.shape, 1)
    zero = jnp.zeros_like(x)
    return jnp.concatenate([jnp.where(lane < RWKV_HEAD, x, zero),
                            jnp.where(lane >= RWKV_HEAD, x, zero)], axis=0)


def _rwkv_kernel(z_ref, mu_ref, w0_ref, dup_ref, a0_ref, iup_ref, gup_ref, kk_ref, ka_ref, rk_ref,
                 lng_ref, lnb_ref, seg_ref, tri_ref, o_ref,
                 prev_ref, st_ref, at_ref, rt_ref, bt_ref, kt_ref, bh_ref, kh_ref, v_ref, pc_ref, y_ref,
                 la_ref, lb_ref, mak_ref, arb_ref, ark_ref, wa_ref, wb_ref, g1_ref, g2_ref, h1_ref, h2_ref,
                 *, tt):
    j = pl.program_id(1)
    W = RWKV_W
    C = CHUNK
    nchunk = tt // C
    npair = RWKV_HEADS // 2

    @pl.when(j == 0)
    def _():
        prev_ref[...] = jnp.zeros(prev_ref.shape, F32)
        st_ref[...] = jnp.zeros(st_ref.shape, F32)

    z = z_ref[0]
    row = lax.broadcasted_iota(I32, z.shape, 0)
    z_prev = jnp.where(row == 0, prev_ref[...], pltpu.roll(z, 1, axis=0))
    prev_ref[...] = z[tt - 1:tt, :]
    z = z + mu_ref[...] * (z_prev - z)

    r = z[:, 0:W]
    k = z[:, W:2 * W]
    v = z[:, 2 * W:3 * W]
    wdad = z[:, 3 * W:3 * W + 2 * DECAY_LORA]
    gd = z[:, 3 * W + 2 * DECAY_LORA:]

    w_pre = w0_ref[...] + _dot3(jnp.tanh(wdad), dup_ref[...])
    neg = -w_pre
    softplus = jnp.maximum(neg, 0.0) + jnp.log(1.0 + jnp.exp(-jnp.abs(neg)))
    lw = -jnp.exp(-softplus - 0.5)
    a_lr = _sigmoid(a0_ref[...] + _dot(wdad.astype(BF16), iup_ref[...]))
    g = _dot(_sigmoid(gd).astype(BF16), gup_ref[...])

    seg = seg_ref[...]

    def head_sum(parts):
        half = seg.shape[0]
        cols = [sum(_dot(p[:, lo:lo + half], seg) for p in parts) for lo in range(0, W, half)]
        return jnp.concatenate(cols, axis=1)

    kk = k * kk_ref[...]
    kk = kk / jnp.maximum(jnp.sqrt(head_sum(_split2(kk * kk))), 1e-12)
    k2 = k * (1.0 + (a_lr - 1.0) * ka_ref[...])
    a_vec = -kk
    b_vec = kk * a_lr

    cum = _dot_exact_rhs_lhs(tri_ref[...], lw)
    tot = jnp.concatenate([jnp.broadcast_to(cum[(c + 1) * C - 1:(c + 1) * C, :], (C, W)) for c in range(nchunk)],
                          axis=0)
    p_inv = jnp.exp(-cum)
    p_out = jnp.exp(tot - cum)
    at_ref[...] = a_vec * jnp.exp(cum - lw)
    rt_ref[...] = r * jnp.exp(cum)
    bt_ref[...] = (b_vec * p_inv).astype(BF16)
    kt_ref[...] = (k2 * p_inv).astype(BF16)
    bh_ref[...] = b_vec * p_out
    kh_ref[...] = k2 * p_out
    v_ref[...] = v
    pc_ref[...] = jnp.exp(tot)

    t_i = lax.broadcasted_iota(I32, (C, LANES), 0)
    s_i = lax.broadcasted_iota(I32, (C, LANES), 1) % C
    strict = s_i < t_i
    incl = s_i <= t_i
    r_i = lax.broadcasted_iota(I32, (LANES, LANES), 0)
    c_i = lax.broadcasted_iota(I32, (LANES, LANES), 1)
    same_head = (r_i < RWKV_HEAD) == (c_i < RWKV_HEAD)
    diag = r_i == c_i
    nstage = int(math.log2(C))
    zero = jnp.zeros((C, LANES), F32)
    zsq = jnp.zeros((LANES, LANES), F32)
    units = [(c, p) for c in range(nchunk) for p in range(npair)]

    def sl(c, p):
        return slice(c * C, (c + 1) * C), slice(p * LANES, (p + 1) * LANES)

    def bd2(w):
        wb = w.astype(BF16)
        return jnp.concatenate([_blockdiag_rows(wb[:, :LANES]), _blockdiag_rows(wb[:, LANES:])], axis=1)

    for i, (c, p) in enumerate(units):
        rows, cols = sl(c, p)
        lhs = jnp.concatenate([at_ref[rows, cols], rt_ref[rows, cols]], axis=0).astype(BF16)
        rhs = jnp.concatenate([_blockdiag_rows(bt_ref[rows, cols]),
                               _blockdiag_rows(kt_ref[rows, cols])], axis=0)
        prod = _dot_nt(lhs, rhs)
        la_ref[i] = jnp.where(strict, prod[:C, :LANES], zero).astype(BF16)
        mak_ref[i] = jnp.where(strict, prod[:C, LANES:], zero).astype(BF16)
        arb_ref[i] = jnp.where(incl, prod[C:, :LANES], zero).astype(BF16)
        ark_ref[i] = jnp.where(incl, prod[C:, LANES:], zero).astype(BF16)
    for i, (c, p) in enumerate(units):
        rows, cols = sl(c, p)
        w2 = _dot(mak_ref[i], _blockdiag_rows(v_ref[rows, cols].astype(BF16)))
        wa_ref[i] = jnp.concatenate([at_ref[rows, cols], w2], axis=1)
    l_bufs, w_bufs = (la_ref, lb_ref), (wa_ref, wb_ref)
    for s in range(nstage):
        l_in, l_out = l_bufs[s % 2], l_bufs[(s + 1) % 2]
        w_in, w_out = w_bufs[s % 2], w_bufs[(s + 1) % 2]
        for i in range(len(units)):
            lmat = l_in[i]
            w = w_in[i]
            w_out[i] = w + _dot(lmat, bd2(w))
            if s < nstage - 1:
                l_out[i] = _dot(lmat, _blockdiag_rows(lmat)).astype(BF16)
    w_fin = w_bufs[nstage % 2]
    for i, (c, p) in enumerate(units):
        rows, cols = sl(c, p)
        w = w_fin[i]
        wb = w.astype(BF16)
        vb = v_ref[rows, cols].astype(BF16)
        gg = _dot(arb_ref[i], bd2(w))
        g1_ref[i] = (rt_ref[rows, cols] + gg[:, :LANES]).astype(BF16)
        g2_ref[i] = gg[:, LANES:] + _dot(ark_ref[i], _blockdiag_rows(vb))
        bk_t = jnp.concatenate([bh_ref[rows, cols], kh_ref[rows, cols]], axis=0).T
        hrhs = jnp.concatenate([wb, jnp.concatenate([jnp.zeros((C, LANES), BF16), vb], axis=1)], axis=0)
        hh = _dot(bk_t.astype(BF16), hrhs)
        pc = pc_ref[c * C:c * C + 1, cols]
        h1 = jnp.where(same_head, hh[:, :LANES], zsq) + jnp.where(diag, jnp.broadcast_to(pc, (LANES, LANES)), zsq)
        h1_ref[i] = h1.astype(BF16)
        h2_ref[i] = jnp.where(same_head, hh[:, LANES:], zsq)
    for c in range(nchunk):
        sts = [st_ref[p].astype(BF16) for p in range(npair)]
        for p in range(npair):
            i = c * npair + p
            rows, cols = sl(c, p)
            y_ref[rows, cols] = _dot(g1_ref[i], sts[p]) + g2_ref[i]
            st_ref[p] = _dot(h1_ref[i], sts[p]) + h2_ref[i]

    y = y_ref[...]
    inv_n = 1.0 / RWKV_HEAD
    mean = head_sum([y.astype(BF16)]) * inv_n
    yc = y - mean
    var = head_sum([(yc * yc).astype(BF16)]) * inv_n
    yn = yc * lax.rsqrt(var + GN_EPS) * lng_ref[...] + lnb_ref[...]
    bonus = head_sum([(r * k2 * rk_ref[...]).astype(BF16)]) * v
    o_ref[0] = ((yn + bonus) * g).astype(BF16)


def _dot_exact_rhs_lhs(ones_bf16, x):
    hi, mid, lo = _split3(x)
    return _dot(ones_bf16, hi) + _dot(ones_bf16, mid) + _dot(ones_bf16, lo)


def _rwkv_call(zr3, tshift_mu, decay_w0, decay_up, iclr_a0, iclr_up, gate_up, k_k, k_a, r_k, lnx_g, lnx_b):
    B, S, _ = zr3.shape
    tt = min(256, S)
    W = RWKV_W
    row = lambda a: a.reshape(1, -1).astype(F32)
    dup = jnp.concatenate([decay_up, jnp.zeros((ICLR_LORA, W), F32)], axis=0)
    iup = jnp.concatenate([jnp.zeros((DECAY_LORA, W), F32), iclr_up], axis=0)
    idx = jnp.arange(2 * LANES)
    seg = (idx[:, None] // RWKV_HEAD == idx[None, :] // RWKV_HEAD).astype(BF16)
    t = jnp.arange(tt)
    same_chunk = t[:, None] // CHUNK == t[None, :] // CHUNK
    tri = (same_chunk & (t[None, :] <= t[:, None])).astype(BF16)
    const = lambda shape: pl.BlockSpec(shape, lambda b, j: (0,) * len(shape))
    kern = functools.partial(_rwkv_kernel, tt=tt)
    nu = (tt // CHUNK) * (RWKV_HEADS // 2)
    return pl.pallas_call(
        kern,
        grid=(B, S // tt),
        in_specs=[pl.BlockSpec((1, tt, RWKV_IN), lambda b, j: (b, j, 0)),
                  const((1, RWKV_IN)), const((1, W)), const((2 * DECAY_LORA, W)), const((1, W)),
                  const((2 * ICLR_LORA, W)), const((GATE_LORA, W)), const((1, W)), const((1, W)),
                  const((1, W)), const((1, W)), const((1, W)),
                  const((2 * LANES, 2 * LANES)), const((tt, tt))],
        out_specs=pl.BlockSpec((1, tt, W), lambda b, j: (b, j, 0)),
        out_shape=jax.ShapeDtypeStruct((B, S, W), BF16),
        scratch_shapes=[pltpu.VMEM((1, RWKV_IN), F32),
                        pltpu.VMEM((RWKV_HEADS // 2, LANES, LANES), F32),
                        pltpu.VMEM((tt, W), F32),
                        pltpu.VMEM((tt, W), F32),
                        pltpu.VMEM((tt, W), BF16),
                        pltpu.VMEM((tt, W), BF16),
                        pltpu.VMEM((tt, W), F32),
                        pltpu.VMEM((tt, W), F32),
                        pltpu.VMEM((tt, W), F32),
                        pltpu.VMEM((tt, W), F32),
                        pltpu.VMEM((tt, W), F32),
                        pltpu.VMEM((nu, CHUNK, LANES), BF16),
                        pltpu.VMEM((nu, CHUNK, LANES), BF16),
                        pltpu.VMEM((nu, CHUNK, LANES), BF16),
                        pltpu.VMEM((nu, CHUNK, LANES), BF16),
                        pltpu.VMEM((nu, CHUNK, LANES), BF16),
                        pltpu.VMEM((nu, CHUNK, 2 * LANES), F32),
                        pltpu.VMEM((nu, CHUNK, 2 * LANES), F32),
                        pltpu.VMEM((nu, CHUNK, LANES), BF16),
                        pltpu.VMEM((nu, CHUNK, LANES), F32),
                        pltpu.VMEM((nu, LANES, LANES), BF16),
                        pltpu.VMEM((nu, LANES, LANES), F32)],
        compiler_params=_cparams(("parallel", "arbitrary")),
        name="rwkv",
    )(zr3, row(tshift_mu), row(decay_w0), dup, row(iclr_a0), iup.astype(BF16), gate_up.astype(BF16), row(k_k),
      row(k_a), row(r_k), row(lnx_g), row(lnx_b), seg, tri)


def _merge_kernel(x_ref, attn_ref, rw_ref, ga_ref, gr_ref, wa_ref, wr_ref, wo_ref, g1_ref,
                  n2_ref, sc_ref, sh_ref, rwt_ref, rb_ref, x1_ref, h2_ref, gt_ref, cnt_ref):
    a = _dot(attn_ref[...], wa_ref[...])
    rr = _dot(rw_ref[...], wr_ref[...])
    mixed = _gate_sigmoid(ga_ref[...].astype(F32)) * a + _gate_sigmoid(gr_ref[...].astype(F32)) * rr
    x1 = x_ref[...] + g1_ref[0] * _dot(mixed.astype(BF16), wo_ref[...])
    x1_ref[...] = x1
    ms = jnp.mean(x1 * x1, axis=-1, keepdims=True)
    h2 = x1 * lax.rsqrt(ms + RMS_EPS) * n2_ref[...]
    h2 = h2 * (1.0 + sc_ref[0]) + sh_ref[0]
    h2_ref[...] = _pack_bf16_pairs(h2)

    tm = x1.shape[0]
    E, G, EG = N_EXPERTS, N_GROUPS, N_EXPERTS // N_GROUPS
    scores = _sigmoid(_dot3(rwt_ref[...], h2, nt=True))
    choice = scores + rb_ref[...]
    c3 = choice.reshape(G, EG, tm)
    e_i = lax.broadcasted_iota(I32, (G, EG, tm), 1)
    m1 = jnp.max(c3, axis=1, keepdims=True)
    first = jnp.min(jnp.where(c3 == m1, e_i, EG), axis=1, keepdims=True)
    m2 = jnp.max(jnp.where(e_i == first, -jnp.inf, c3), axis=1, keepdims=True)
    grp = (m1 + m2).reshape(G, tm)
    g_i = lax.broadcasted_iota(I32, (G, tm), 0)
    rank = jnp.zeros((G, tm), I32)
    for o in range(G):
        other = grp[o:o + 1, :]
        rank = rank + jnp.where((other > grp) | ((other == grp) & (o < g_i)), 1, 0)
    gsel = rank < TOPK_GROUPS
    esel = jnp.broadcast_to(gsel.reshape(G, 1, tm), (G, EG, tm)).reshape(E, tm)
    mc = jnp.where(esel, choice, -jnp.inf)
    x_i = lax.broadcasted_iota(I32, (E, tm), 0)
    cur = mc
    picked = jnp.zeros((E, tm), F32)
    for _ in range(MOE_TOPK):
        best = jnp.max(cur, axis=0, keepdims=True)
        first = jnp.min(jnp.where(cur == best, x_i, E), axis=0, keepdims=True)
        hit = x_i == first
        picked = jnp.where(hit, 1.0, picked)
        cur = jnp.where(hit, -jnp.inf, cur)
    gw = jnp.where(picked > 0.0, scores, 0.0)
    gw = gw / jnp.sum(gw, axis=0, keepdims=True) * ROUTED_SCALE
    gt_ref[...] = gw
    sel = jnp.where(gw > 0.0, 1.0, 0.0)
    cnt_ref[0] = jnp.broadcast_to(jnp.sum(sel, axis=1, keepdims=True), (E, LANES))


def _pack_bf16_pairs(x):
    n = x.shape[1] // 2
    bits = pltpu.bitcast(x.astype(BF16).astype(F32), I32)
    return bits[:, :n] | lax.shift_right_logical(bits[:, n:], 16)


def _unpack_bf16_pairs(p):
    hi = pltpu.bitcast(p & jnp.int32(-65536), F32)
    lo = pltpu.bitcast(lax.shift_left(p, 16), F32)
    return jnp.concatenate([hi, lo], axis=1).astype(BF16)


def _merge_call(x2, attn, rw, ga, gr, wa, wr, wo, g1, norm2_g, sc2, sh2, router_w, router_bias, S):
    T, D = x2.shape
    B = T // S
    tm = min(MERGE_TILE, S)
    tpb = S // tm
    nt = T // tm
    E = N_EXPERTS
    row = lambda i: (i, 0)
    per_b = lambda i: (i // tpb, 0, 0)
    const = lambda shape: pl.BlockSpec(shape, lambda i: (0,) * len(shape))
    return pl.pallas_call(
        _merge_kernel,
        grid=(nt,),
        in_specs=[pl.BlockSpec((tm, D), row), pl.BlockSpec((tm, ATTN_W), row), pl.BlockSpec((tm, RWKV_W), row),
                  pl.BlockSpec((tm, D), row), pl.BlockSpec((tm, D), row),
                  const((ATTN_W, D)), const((RWKV_W, D)), const((D, D)),
                  pl.BlockSpec((1, 1, D), per_b), const((1, D)),
                  pl.BlockSpec((1, 1, D), per_b), pl.BlockSpec((1, 1, D), per_b),
                  const((E, D)), const((E, 1))],
        out_specs=[pl.BlockSpec((tm, D), row), pl.BlockSpec((tm, D // 2), row),
                   pl.BlockSpec((E, tm), lambda i: (0, i)), pl.BlockSpec((1, E, LANES), lambda i: (i, 0, 0))],
        out_shape=[jax.ShapeDtypeStruct((T, D), F32), jax.ShapeDtypeStruct((T, D // 2), I32),
                   jax.ShapeDtypeStruct((E, T), F32), jax.ShapeDtypeStruct((nt, E, LANES), F32)],
        compiler_params=_cparams(("parallel",)),
        name="merge",
    )(x2, attn, rw, ga, gr, wa, wr, wo, g1.reshape(B, 1, D), norm2_g.reshape(1, D),
      sc2.reshape(B, 1, D), sh2.reshape(B, 1, D), router_w.T, router_bias.reshape(E, 1))


def _plan_kernel(gt_ref, cnt_ref, upper_ref, lowe_ref, dest_ref, gw_ref, be_ref, off_ref, *, tm, n_blocks):
    i = pl.program_id(0)
    E = N_EXPERTS
    lowe = lowe_ref[...]

    @pl.when(i == 0)
    def _():
        total = jnp.sum(cnt_ref[...], axis=0)
        nblk = jnp.floor((total + (MOE_BLOCK - 1)) * (1.0 / MOE_BLOCK))
        start_blk = _dot_exact_rhs_lhs(lowe, nblk)
        off_ref[...] = start_blk * MOE_BLOCK
        end_blk = start_blk + nblk
        b_i = lax.broadcasted_iota(I32, (E, n_blocks), 1).astype(F32)
        e_of_b = jnp.sum(jnp.where(end_blk[:, :1] <= b_i, 1.0, 0.0), axis=0, keepdims=True)
        be_ref[...] = e_of_b.astype(I32)

    gt = gt_ref[...]
    sel = gt > 0.0
    selb = jnp.where(sel, 1.0, 0.0).astype(BF16)
    rank = _dot(selb, upper_ref[...])
    dest = off_ref[:, :1] + rank
    off_ref[...] = off_ref[...] + cnt_ref[i]
    kth = _dot(lowe, selb)
    dests, gws = [], []
    for k in range(MOE_TOPK):
        m = sel & (kth == float(k))
        have = jnp.sum(jnp.where(m, 1.0, 0.0), axis=0, keepdims=True)
        d = jnp.sum(jnp.where(m, dest, 0.0), axis=0, keepdims=True)
        dests.append(jnp.where(have > 0.0, d, float((n_blocks - 1) * MOE_BLOCK)))
        gws.append(jnp.sum(jnp.where(m, gt, 0.0), axis=0, keepdims=True))
    dest_ref[...] = jnp.concatenate(dests, axis=0).astype(I32)
    gpad = jnp.concatenate(gws + [jnp.zeros((LANES - MOE_TOPK, tm), F32)], axis=0)
    gw_ref[...] = gpad.T


def _plan_call(gate_t, cnt, n_blocks, tile0):
    E = gate_t.shape[0]
    nt = cnt.shape[0]
    tm = MERGE_TILE
    T = nt * tm
    idx = jnp.arange(tm)
    upper = (idx[:, None] < idx[None, :]).astype(BF16)
    ei = jnp.arange(E)
    lowe = (ei[None, :] < ei[:, None]).astype(BF16)
    kern = functools.partial(_plan_kernel, tm=tm, n_blocks=n_blocks)
    const = lambda shape: pl.BlockSpec(shape, lambda i: (0,) * len(shape))
    return pl.pallas_call(
        kern,
        grid=(nt,),
        in_specs=[pl.BlockSpec((E, tm), lambda i: (0, i + tile0)), const((nt, E, LANES)), const((tm, tm)),
                  const((E, E))],
        out_specs=[pl.BlockSpec((MOE_TOPK, tm), lambda i: (0, i)), pl.BlockSpec((tm, LANES), lambda i: (i, 0)),
                   const((1, n_blocks))],
        out_shape=[jax.ShapeDtypeStruct((MOE_TOPK, T), I32), jax.ShapeDtypeStruct((T, LANES), F32),
                   jax.ShapeDtypeStruct((1, n_blocks), I32)],
        scratch_shapes=[pltpu.VMEM((E, LANES), F32)],
        compiler_params=_cparams(("arbitrary",)),
        name="plan",
    )(gate_t, cnt, upper, lowe)


def _sc_index_layout(dest_t):
    K, T = dest_t.shape
    n_ch = T // (SC_WORKERS * SC_ROWS)
    return dest_t.reshape(K, SC_WORKERS, n_ch, SC_ROWS).transpose(1, 2, 0, 3).reshape(SC_WORKERS, n_ch * K, SC_ROWS)


def _sc_dispatch(rows, idx, n_slots, tok0):
    W = rows.shape[1]
    n_ch = idx.shape[1] // MOE_TOPK
    T = n_ch * SC_WORKERS * SC_ROWS
    tpw = T // SC_WORKERS
    mesh = plsc.VectorSubcoreMesh(core_axis_name="c", subcore_axis_name="s")

    @functools.partial(
        pl.kernel, mesh=mesh,
        out_type=jax.ShapeDtypeStruct((n_slots, W), I32),
        scratch_types=[pltpu.VMEM((n_ch * MOE_TOPK, SC_ROWS), I32), pltpu.VMEM((SC_ROWS, W), I32),
                       pltpu.SemaphoreType.DMA])
    def kern(x_hbm, idx_hbm, o_hbm, idx_v, rows_v, sem):
        wid = lax.axis_index("s") * SC_CORES + lax.axis_index("c")
        pltpu.sync_copy(idx_hbm.at[wid], idx_v)

        @pl.loop(0, n_ch)
        def _(j):
            pltpu.sync_copy(x_hbm.at[pl.ds(tok0 + wid * tpw + j * SC_ROWS, SC_ROWS)], rows_v)
            copies = [pltpu.async_copy(rows_v, o_hbm.at[idx_v.at[j * MOE_TOPK + k]], sem)
                      for k in range(MOE_TOPK)]
            for cp in copies:
                cp.wait()

    return kern(rows, idx)


def _sc_combine(slots, idx, T):
    _, W = slots.shape
    n_ch = T // (SC_WORKERS * SC_ROWS)
    tpw = T // SC_WORKERS
    mesh = plsc.VectorSubcoreMesh(core_axis_name="c", subcore_axis_name="s")

    @functools.partial(
        pl.kernel, mesh=mesh,
        out_type=jax.ShapeDtypeStruct((MOE_TOPK, T, W), I32),
        scratch_types=[pltpu.VMEM((n_ch * MOE_TOPK, SC_ROWS), I32), pltpu.VMEM((SC_ROWS, W), I32),
                       pltpu.SemaphoreType.DMA])
    def kern(s_hbm, idx_hbm, o_hbm, idx_v, rows_v, sem):
        wid = lax.axis_index("s") * SC_CORES + lax.axis_index("c")
        pltpu.sync_copy(idx_hbm.at[wid], idx_v)

        @pl.loop(0, n_ch)
        def _(j):
            for k in range(MOE_TOPK):
                pltpu.async_copy(s_hbm.at[idx_v.at[j * MOE_TOPK + k]], rows_v, sem).wait()
                pltpu.sync_copy(rows_v, o_hbm.at[k, pl.ds(wid * tpw + j * SC_ROWS, SC_ROWS)])

    return kern(slots, idx)


def _ffn_kernel(be_ref, x_ref, eg_ref, eu_ref, ed_ref, o_ref):
    used = be_ref[pl.program_id(0)] < N_EXPERTS

    @pl.when(used)
    def _():
        x = _unpack_bf16_pairs(x_ref[...])
        a = _dot(x, eg_ref[0].astype(BF16))
        u = _dot(x, eu_ref[0].astype(BF16))
        o_ref[...] = _pack_bf16_pairs(_dot((a * _gate_sigmoid(a) * u).astype(BF16), ed_ref[0].astype(BF16)))

    @pl.when(jnp.logical_not(used))
    def _():
        o_ref[...] = jnp.zeros(o_ref.shape, I32)


def _ffn_call(xs, block_e, eg, eu, ed, n_blocks):
    P, W = xs.shape
    D, FF = 2 * W, EXPERT_FF
    grid_spec = pltpu.PrefetchScalarGridSpec(
        num_scalar_prefetch=1,
        grid=(n_blocks,),
        in_specs=[pl.BlockSpec((MOE_BLOCK, W), lambda b, be: (b, 0)),
                  pl.BlockSpec((1, D, FF), lambda b, be: (jnp.minimum(be[b], N_EXPERTS - 1), 0, 0)),
                  pl.BlockSpec((1, D, FF), lambda b, be: (jnp.minimum(be[b], N_EXPERTS - 1), 0, 0)),
                  pl.BlockSpec((1, FF, D), lambda b, be: (jnp.minimum(be[b], N_EXPERTS - 1), 0, 0))],
        out_specs=pl.BlockSpec((MOE_BLOCK, W), lambda b, be: (b, 0)))
    return pl.pallas_call(
        _ffn_kernel,
        grid_spec=grid_spec,
        out_shape=jax.ShapeDtypeStruct((P, W), I32),
        compiler_params=_cparams(("parallel",)),
        name="ffn",
    )(block_e, xs, eg, eu, ed)


def _final_kernel(h_ref, c_ref, gw_ref, x1_ref, g2_ref, fg_ref, sg_ref, su_ref, sd_ref, o_ref):
    h = _unpack_bf16_pairs(h_ref[...])
    a = _dot(h, sg_ref[...])
    u = _dot(h, su_ref[...])
    moe = _dot((a * _gate_sigmoid(a) * u).astype(BF16), sd_ref[...])
    gw = gw_ref[...]
    for k in range(MOE_TOPK):
        w = gw[:, k:k + 1]
        y = _unpack_bf16_pairs(c_ref[k]).astype(F32)
        moe = moe + jnp.where(w > 0.0, w * y, 0.0)
    x2 = x1_ref[...] + g2_ref[0] * moe
    ms = jnp.mean(x2 * x2, axis=-1, keepdims=True)
    o_ref[...] = x2 * lax.rsqrt(ms + RMS_EPS) * fg_ref[...]


def _final_call(h2p, comb, gw, x1, g2, final_g, sg, su, sd, S, tile0, prev_out):
    T, W = h2p.shape
    D, FF = 2 * W, EXPERT_FF
    B = T // S
    tm = MERGE_TILE
    tpb = S // tm
    ntile = comb.shape[1] // tm
    row = lambda i: (i, 0)
    full_row = lambda i: (i + tile0, 0)
    const = lambda shape: pl.BlockSpec(shape, lambda i: (0,) * len(shape))
    in_specs = [pl.BlockSpec((tm, W), full_row), pl.BlockSpec((MOE_TOPK, tm, W), lambda i: (0, i, 0)),
                pl.BlockSpec((tm, LANES), row), pl.BlockSpec((tm, D), full_row),
                pl.BlockSpec((1, 1, D), lambda i: ((i + tile0) // tpb, 0, 0)), const((1, D)),
                const((D, FF)), const((D, FF)), const((FF, D))]
    args = [h2p, comb, gw, x1, g2.reshape(B, 1, D), final_g.reshape(1, D), sg, su, sd]
    kern, aliases = _final_kernel, {}
    if prev_out is not None:
        in_specs.append(pl.BlockSpec(memory_space=pl.ANY))
        args.append(prev_out)
        aliases = {len(args) - 1: 0}
        kern = lambda *refs: _final_kernel(*refs[:9], refs[10])
    return pl.pallas_call(
        kern,
        grid=(ntile,),
        in_specs=in_specs,
        out_specs=pl.BlockSpec((tm, D), full_row),
        out_shape=jax.ShapeDtypeStruct((T, D), F32),
        input_output_aliases=aliases,
        compiler_params=_cparams(("parallel",)),
        name="final",
    )(*args)


def _moe_call(h2p, gate_t, cnt, x1, g2, final_g, eg, eu, ed, sg, su, sd, S):
    T = h2p.shape[0]
    nt = cnt.shape[0]
    groups = MOE_GROUPS if (T // MOE_GROUPS) % (SC_WORKERS * SC_ROWS) == 0 and nt % MOE_GROUPS == 0 else 1
    tg, ntg = T // groups, nt // groups
    n_blocks = (tg * MOE_TOPK) // MOE_BLOCK + N_EXPERTS + 1
    out = None
    for g in range(groups):
        dest_t, gw, block_e = _plan_call(gate_t, cnt[g * ntg:(g + 1) * ntg], n_blocks, g * ntg)
        idx = _sc_index_layout(dest_t)
        xs = _sc_dispatch(h2p, idx, n_blocks * MOE_BLOCK, g * tg)
        ys = _ffn_call(xs, block_e.reshape(n_blocks), eg, eu, ed, n_blocks)
        comb = _sc_combine(ys, idx, tg)
        out = _final_call(h2p, comb, gw, x1, g2, final_g, sg, su, sd, S, g * ntg, out)
    return out


def _layer(x2, c, S, ada_w, ada_b, norm1_g, w_in, rel_bias, tshift_mu, decay_w0, decay_up, iclr_a0, iclr_up,
           gate_up, k_k, k_a, r_k, lnx_g, lnx_b, w_attn_br, w_rwkv_br, w_out, norm2_g, router_w, router_bias,
           exp_gate, exp_up, exp_down, sh_gate, sh_up, sh_down, final_g):
    T, D = x2.shape
    B = T // S
    mod = _mod_call(c, ada_w, ada_b)
    sh1, sc1, g1, sh2, sc2, g2 = jnp.split(mod, 6, axis=-1)

    q, k, vt, iq, ik4, iwt, zr, ga, gr = _inproj_call(x2, norm1_g, sc1, sh1, w_in, S)

    ta = min(512, S)
    assert ta >= LANES and S % ta == 0
    top_k = min(TOPK_MAX, S // 4)
    seq = lambda a: a.reshape(B, S, a.shape[-1])
    mask = _index_call(seq(iq), iwt, seq(ik4), ta, top_k)
    bias_tiles = _bias_call(rel_bias, ta)
    attn = _attn_call(seq(q), seq(k), vt, mask, bias_tiles, rel_bias, ta).reshape(T, ATTN_W)

    rw = _rwkv_call(zr.reshape(B, S, RWKV_IN), tshift_mu, decay_w0, decay_up, iclr_a0, iclr_up, gate_up,
                    k_k, k_a, r_k, lnx_g, lnx_b).reshape(T, RWKV_W)

    x1, h2p, gate_t, cnt = _merge_call(x2, attn, rw, ga, gr, w_attn_br.astype(BF16), w_rwkv_br.astype(BF16),
                                       w_out.astype(BF16), g1, norm2_g, sc2, sh2, router_w, router_bias, S)
    return _moe_call(h2p, gate_t, cnt, x1, g2, final_g, exp_gate, exp_up, exp_down,
                     sh_gate.astype(BF16), sh_up.astype(BF16), sh_down.astype(BF16), S)


def kernel(x, c, ada_w, ada_b, norm1_g, w_in, rel_bias, tshift_mu, decay_w0, decay_up, iclr_a0, iclr_up, gate_up, k_k, k_a, r_k, lnx_g, lnx_b, w_attn_br, w_rwkv_br, w_out, norm2_g, router_w, router_bias, exp_gate, exp_up, exp_down, sh_gate, sh_up, sh_down, final_g):
    B, S, D = x.shape
    depth = ada_w.shape[0]
    assert depth == 1, "the final RMSNorm is fused into the (single) layer's MoE kernel"
    out = _layer(x.reshape(B * S, D), c, S, ada_w[0], ada_b[0], norm1_g[0], w_in[0], rel_bias, tshift_mu[0],
                 decay_w0[0], decay_up[0], iclr_a0[0], iclr_up[0], gate_up[0], k_k[0], k_a[0], r_k[0],
                 lnx_g[0], lnx_b[0], w_attn_br[0], w_rwkv_br[0], w_out[0], norm2_g[0], router_w[0],
                 router_bias[0], exp_gate[0], exp_up[0], exp_down[0], sh_gate[0], sh_up[0], sh_down[0], final_g)
    return out.reshape(B, S, D)
```

```python
import functools
import math

import jax
import jax.numpy as jnp
from jax import lax
from jax.experimental import pallas as pl
from jax.experimental.pallas import tpu as pltpu
from jax.experimental.pallas import tpu_sc as plsc

F32 = jnp.float32
BF16 = jnp.bfloat16
I32 = jnp.int32
I16 = jnp.int16

RMS_EPS = 1e-6
D_MODEL = 1024
N_ATTN_HEADS = 8
ATTN_HEAD_DIM = 64
ATTN_W = 512
IDX_HEADS = 16
IDX_DIM = 32
IDX_Q = 512
TOPK_MAX = 256
N_BUCKETS = 32
MAX_DISTANCE = 128
RWKV_HEADS = 8
RWKV_HEAD = 64
RWKV_W = 512
DECAY_LORA = 64
ICLR_LORA = 64
GATE_LORA = 128
RWKV_IN = 1792
GN_EPS = 64e-5
N_EXPERTS = 64
N_GROUPS = 8
TOPK_GROUPS = 4
MOE_TOPK = 8
EXPERT_FF = 256
ROUTED_SCALE = 2.5

LANES = 128
VMEM_LIMIT = 56 * 1024 * 1024
CHUNK = 64
MERGE_TILE = 512
MOE_BLOCK = 1024
MOE_GROUPS = 2
SC_CORES, SC_SUBCORES = 2, 16
SC_WORKERS = SC_CORES * SC_SUBCORES
SC_ROWS = 128
LOG2E = 1.4426950408889634
INT_MIN = -2147483648
KEY_NEG_INF = -2139095041

NT_DIMS = (((1,), (1,)), ((), ()))


def _cparams(sem):
    return pltpu.CompilerParams(dimension_semantics=sem, vmem_limit_bytes=VMEM_LIMIT)


def _dot(a, b):
    return jnp.dot(a, b, preferred_element_type=F32)


def _dot_nt(a, b):
    return lax.dot_general(a, b, NT_DIMS, preferred_element_type=F32)


def _split2(x):
    hi = x.astype(BF16)
    lo = (x - hi.astype(F32)).astype(BF16)
    return hi, lo


def _split3(x):
    hi = x.astype(BF16)
    r1 = x - hi.astype(F32)
    mid = r1.astype(BF16)
    lo = (r1 - mid.astype(F32)).astype(BF16)
    return hi, mid, lo


def _dot3(a, b, nt=False):
    ah, al = _split2(a)
    bh, bl = _split2(b)
    f = _dot_nt if nt else _dot
    return f(ah, bh) + f(ah, bl) + f(al, bh)


def _sigmoid(x):
    return 1.0 / (1.0 + jnp.exp(-x))


def _gate_sigmoid(x):
    return 0.5 * jnp.tanh(0.5 * x) + 0.5


def _mod_kernel(c_ref, w_ref, b_ref, o_ref):
    c = c_ref[...]
    s = c * _sigmoid(c)
    o_ref[...] = _dot3(s, w_ref[...]) + b_ref[...]


def _mod_call(c, ada_w, ada_b):
    B, D = c.shape
    N = ada_w.shape[1]
    tn = 1024
    return pl.pallas_call(
        _mod_kernel,
        grid=(N // tn,),
        in_specs=[pl.BlockSpec((B, D), lambda j: (0, 0)),
                  pl.BlockSpec((D, tn), lambda j: (0, j)),
                  pl.BlockSpec((1, tn), lambda j: (0, j))],
        out_specs=pl.BlockSpec((B, tn), lambda j: (0, j)),
        out_shape=jax.ShapeDtypeStruct((B, N), F32),
        compiler_params=_cparams(("arbitrary",)),
        name="mod",
    )(c, ada_w, ada_b.reshape(1, N))


_OFF_Q, _OFF_K, _OFF_IQ, _OFF_IK4, _OFF_ZR, _OFF_GA, _OFF_GR, _N_PACK = (
    0, 512, 1024, 1536, 2048, 3840, 4864, 5888)
IDX_PER_BLOCK = LANES // IDX_DIM


def _pack_w_in(w_in):
    D = w_in.shape[0]
    w_ik = w_in[:, 2048:2080]
    ik4 = jnp.zeros((D, IDX_PER_BLOCK * LANES), w_in.dtype)
    for j in range(IDX_PER_BLOCK):
        ik4 = lax.dynamic_update_slice(ik4, w_ik, (0, j * LANES + j * IDX_DIM))
    w_pack = jnp.concatenate([w_in[:, 0:1024], w_in[:, 1536:2048], ik4, w_in[:, 2096:]], axis=1).astype(BF16)
    return w_pack, w_in[:, 1024:1536].T.astype(BF16), w_in[:, 2080:2096].T.astype(BF16)


def _inproj_kernel(x_ref, g_ref, sc_ref, sh_ref, w_ref, wvt_ref, wiwt_ref,
                   q_ref, k_ref, vt_ref, iq_ref, ik4_ref, iwt_ref, zr_ref, ga_ref, gr_ref):
    x = x_ref[...]
    ms = jnp.mean(x * x, axis=-1, keepdims=True)
    h = x * lax.rsqrt(ms + RMS_EPS) * g_ref[...]
    h = h * (1.0 + sc_ref[0]) + sh_ref[0]
    hb = h.astype(BF16)

    def proj(lo, hi):
        return _dot(hb, w_ref[:, lo:hi])

    q_ref[...] = (proj(_OFF_Q, _OFF_K) * (ATTN_HEAD_DIM ** -0.5 * LOG2E)).astype(BF16)
    k_ref[...] = proj(_OFF_K, _OFF_IQ).astype(BF16)
    iq_ref[...] = proj(_OFF_IQ, _OFF_IK4).astype(BF16)
    ik4_ref[...] = proj(_OFF_IK4, _OFF_ZR).astype(BF16)
    zr_ref[...] = proj(_OFF_ZR, _OFF_GA)
    ga_ref[...] = proj(_OFF_GA, _OFF_GR).astype(BF16)
    gr_ref[...] = proj(_OFF_GR, _N_PACK).astype(BF16)
    vt_ref[0] = _dot_nt(wvt_ref[...], hb).astype(BF16)
    iwt_ref[0] = _dot_nt(wiwt_ref[...], hb)


def _inproj_call(x2, norm_g, sc, sh, w_in, S):
    T, D = x2.shape
    B = T // S
    tm = min(512, S)
    tpb = S // tm
    w_pack, wvt, wiwt = _pack_w_in(w_in)
    row = lambda i: (i, 0)
    per_b = lambda i: (i // tpb, 0, 0)
    colblk = lambda i: (i // tpb, 0, i % tpb)
    const = lambda shape: pl.BlockSpec(shape, lambda i: (0,) * len(shape), pipeline_mode=pl.Buffered(1))
    rows_out = ((512, BF16), (512, BF16), (512, BF16), (512, BF16), (RWKV_IN, F32), (D, BF16), (D, BF16))
    out_specs = [pl.BlockSpec((tm, w), row) for w, _ in rows_out]
    out_shape = [jax.ShapeDtypeStruct((T, w), dt) for w, dt in rows_out]
    out_specs[2:2] = [pl.BlockSpec((1, ATTN_W, tm), colblk)]
    out_shape[2:2] = [jax.ShapeDtypeStruct((B, ATTN_W, S), BF16)]
    out_specs[5:5] = [pl.BlockSpec((1, IDX_HEADS, tm), colblk)]
    out_shape[5:5] = [jax.ShapeDtypeStruct((B, IDX_HEADS, S), F32)]
    return pl.pallas_call(
        _inproj_kernel,
        grid=(T // tm,),
        in_specs=[pl.BlockSpec((tm, D), row),
                  pl.BlockSpec((1, D), lambda i: (0, 0)),
                  pl.BlockSpec((1, 1, D), per_b),
                  pl.BlockSpec((1, 1, D), per_b),
                  const((D, _N_PACK)), const((ATTN_W, D)), const((IDX_HEADS, D))],
        out_specs=out_specs,
        out_shape=out_shape,
        compiler_params=_cparams(("parallel",)),
        name="inproj",
    )(x2, norm_g.reshape(1, D), sc.reshape(B, 1, D), sh.reshape(B, 1, D), w_pack, wvt, wiwt)


def _t5_bucket(rel):
    n = jnp.maximum(rel, 0)
    max_exact = N_BUCKETS // 2
    nf = jnp.maximum(n, 1).astype(F32)
    large = max_exact + (jnp.log(nf / max_exact) / math.log(MAX_DISTANCE / max_exact)
                         * (N_BUCKETS - max_exact)).astype(I32)
    large = jnp.minimum(large, N_BUCKETS - 1)
    return jnp.where(n < max_exact, n, large)


def _bias_kernel(bucket_ref, rb_ref, o_ref):
    h = pl.program_id(0)
    bk = bucket_ref[...]
    out = jnp.zeros(bk.shape, F32)
    for b in range(N_BUCKETS):
        out = jnp.where(bk == b, rb_ref[b, h] * LOG2E, out)
    o_ref[0] = out


def _bias_call(rel_bias, tq):
    r = jnp.arange(tq, dtype=I32)[None, :]
    c = jnp.arange(tq, dtype=I32)[:, None]
    buckets = jnp.stack([_t5_bucket(r - c), _t5_bucket(tq + r - c)])
    return pl.pallas_call(
        _bias_kernel,
        grid=(N_ATTN_HEADS,),
        in_specs=[pl.BlockSpec((2, tq, tq), lambda h: (0, 0, 0)),
                  pl.BlockSpec(memory_space=pltpu.SMEM)],
        out_specs=pl.BlockSpec((1, 2, tq, tq), lambda h: (h, 0, 0, 0)),
        out_shape=jax.ShapeDtypeStruct((N_ATTN_HEADS, 2, tq, tq), F32),
        compiler_params=_cparams(("arbitrary",)),
        name="bias",
    )(buckets, rel_bias)


def _index_kernel(iq_ref, iwt_ref, ik4_ref, lower_ref, mask_ref, key_ref, k16_ref, lo16_ref, *, t, nk, top_k, scale):
    qi = pl.program_id(1)
    nkt = qi + 1
    ksub = LANES
    qpos = qi * t + lax.broadcasted_iota(I32, (ksub, t), 1)

    def score_tile(kt, carry):
        kbase = pl.multiple_of(kt * t, t)
        for ks in range(t // ksub):
            acc = jnp.zeros((ksub, t), F32)
            ik_rows = ik4_ref[0, pl.ds(kbase + ks * ksub, ksub), :]
            ik_stack = jnp.concatenate([ik_rows[:, j * LANES:(j + 1) * LANES] for j in range(IDX_PER_BLOCK)],
                                       axis=0)
            for g in range(IDX_HEADS // IDX_PER_BLOCK):
                d4 = _dot_nt(ik_stack, iq_ref[0, :, g * LANES:(g + 1) * LANES])
                for j in range(IDX_PER_BLOCK):
                    h = g * IDX_PER_BLOCK + j
                    acc = acc + jnp.maximum(d4[j * ksub:(j + 1) * ksub], 0.0) * iwt_ref[0, h:h + 1, :]
            s = acc * scale
            kpos = kt * t + ks * ksub + lax.broadcasted_iota(I32, (ksub, t), 0)
            s = jnp.where(kpos <= qpos, s, -jnp.inf)
            bits = pltpu.bitcast(s, I32)
            key = bits ^ ((bits >> 31) & 0x7FFFFFFF)
            key_ref[kt, ks * ksub:(ks + 1) * ksub, :] = key
            k16_ref[kt, ks * ksub:(ks + 1) * ksub, :] = (key >> 16).astype(I16)
            lo16_ref[kt, ks * ksub:(ks + 1) * ksub, :] = ((key & 0xFFFF) - 32768).astype(I16)
        return carry

    lax.fori_loop(0, nkt, score_tile, 0)

    pack = 16

    def search16():
        def bit_body(i, ans):
            cand = ans | lax.shift_left(jnp.int32(1), 15 - i)
            cand16 = (cand - 32768).astype(I16)

            def cnt_body(kt, acc):
                one = jnp.where(k16_ref[kt] >= cand16, jnp.int16(1), jnp.int16(0))
                for r in range(t // pack):
                    acc = acc + one[r * pack:(r + 1) * pack, :]
                return acc

            acc = lax.fori_loop(0, nkt, cnt_body, jnp.zeros((pack, t), I16))
            cnt = jnp.sum(acc.astype(I32), axis=0, keepdims=True)
            return jnp.where(cnt >= top_k, cand, ans)

        return lax.fori_loop(0, 16, bit_body, jnp.zeros((1, t), I32))

    hi = search16() - 32768

    hi16 = hi.astype(I16)

    def remap_body(kt, carry):
        khi = k16_ref[kt]
        k16_ref[kt] = jnp.where(khi > hi16, jnp.int16(32767),
                                jnp.where(khi == hi16, lo16_ref[kt], jnp.int16(-32768)))
        return carry

    lax.fori_loop(0, nkt, remap_body, 0)
    thr = hi * 65536 + search16()

    def count(pred):
        def body(kt, acc):
            one = jnp.where(pred(key_ref[kt]), 1.0, 0.0)
            return acc + jnp.sum(one.reshape(t // 8, 8, t), axis=0)
        return jnp.sum(lax.fori_loop(0, nkt, body, jnp.zeros((8, t), F32)), axis=0, keepdims=True)

    def mask_body(kt, acc):
        keys = key_ref[kt]
        ge = keys >= thr
        mask_ref[0, 0, kt] = jnp.where(ge & (keys > KEY_NEG_INF), 0.0, -jnp.inf).astype(BF16)
        return acc + jnp.sum(jnp.where(ge, 1.0, 0.0).reshape(t // 8, 8, t), axis=0)

    n_ge = jnp.sum(lax.fori_loop(0, nkt, mask_body, jnp.zeros((8, t), F32)), axis=0, keepdims=True)
    has_tie = jnp.max(n_ge) > float(top_k)

    @pl.when(has_tie)
    def _():
        need = float(top_k) - count(lambda keys: keys > thr)

        def mask_body(kt, seen):
            keys = key_ref[kt]
            tie = (keys == thr) & (keys > KEY_NEG_INF)
            tie_b = jnp.where(tie, 1.0, 0.0).astype(BF16)
            before = seen + _dot(lower_ref[...], tie_b)
            sel = (keys > thr) | (tie & (before < need))
            mask_ref[0, 0, kt] = jnp.where(sel, 0.0, -jnp.inf).astype(BF16)
            return seen + jnp.sum(tie_b.astype(F32).reshape(t // 8, 8, t).sum(axis=0), axis=0, keepdims=True)

        lax.fori_loop(0, nkt, mask_body, jnp.zeros((1, t), F32))

    def fill_body(kt, carry):
        mask_ref[0, 0, kt] = jnp.full((t, t), -jnp.inf, BF16)
        return carry

    lax.fori_loop(nkt, nk, fill_body, 0)


def _index_call(iq, iwt, ik4, t, top_k):
    B, S, _ = iq.shape
    n = S // t
    scale = (IDX_HEADS ** -0.5) * (IDX_DIM ** -0.5)
    kern = functools.partial(_index_kernel, t=t, nk=n, top_k=top_k, scale=scale)
    pos = jnp.arange(t)
    lower = (pos[None, :] < pos[:, None]).astype(BF16)
    return pl.pallas_call(
        kern,
        grid=(B, n),
        in_specs=[pl.BlockSpec((1, t, IDX_Q), lambda b, i: (b, i, 0)),
                  pl.BlockSpec((1, IDX_HEADS, t), lambda b, i: (b, 0, i)),
                  pl.BlockSpec((1, S, IDX_PER_BLOCK * LANES), lambda b, i: (b, 0, 0)),
                  pl.BlockSpec((t, t), lambda b, i: (0, 0))],
        out_specs=pl.BlockSpec((1, 1, n, t, t), lambda b, i: (b, i, 0, 0, 0)),
        out_shape=jax.ShapeDtypeStruct((B, n, n, t, t), BF16),
        scratch_shapes=[pltpu.VMEM((n, t, t), I32), pltpu.VMEM((n, t, t), I16), pltpu.VMEM((n, t, t), I16)],
        compiler_params=_cparams(("parallel", "arbitrary")),
        name="index",
    )(iq, iwt, ik4, lower)


ONES_ROWS = 16


def _attn_kernel(qi_tab, kt_tab, q_ref, k_ref, vt_ref, mask_ref, bias_ref, rb_ref, o_ref,
                 qz_ref, m_ref, acc_ref, s_ref, *, t):
    s_id = pl.program_id(1)
    qi = qi_tab[s_id]
    kt = kt_tab[s_id]
    dh = ATTN_HEAD_DIM

    @pl.when(kt == 0)
    def _():
        m_ref[...] = jnp.full(m_ref.shape, -jnp.inf, F32)
        acc_ref[...] = jnp.zeros(acc_ref.shape, F32)
        lane = lax.broadcasted_iota(I32, (t, LANES), 1)
        for h in range(N_ATTN_HEADS):
            blk = q_ref[0, :, (h // 2) * LANES:(h // 2 + 1) * LANES]
            keep = (lane < dh) if h % 2 == 0 else (lane >= dh)
            qz_ref[h] = jnp.where(keep, blk, jnp.zeros_like(blk))

    def step(bias_tile, bias_const):
        maskf = mask_ref[0, 0, 0].astype(F32)
        ones = jnp.ones((ONES_ROWS, t), BF16)

        for h in range(N_ATTN_HEADS):
            k_blk = k_ref[0, :, (h // 2) * LANES:(h // 2 + 1) * LANES]
            s = _dot_nt(k_blk, qz_ref[h]) + maskf
            s_ref[h] = s if bias_tile is None else s + bias_tile(h)
        for h in range(N_ATTN_HEADS):
            s = s_ref[h]
            c = bias_const(h)
            m_old = m_ref[h:h + 1, :]
            m_cur = jnp.max(jnp.max(s.reshape(t // 8, 8, t), axis=0), axis=0, keepdims=True) + c
            m_new = jnp.maximum(m_old, m_cur)
            m_safe = jnp.where(m_new == -jnp.inf, 0.0, m_new)
            alpha = jnp.exp2(m_old - m_safe)
            p = jnp.exp2(s - (m_safe - c)).astype(BF16)
            v_aug = jnp.concatenate([vt_ref[0, h * dh:(h + 1) * dh, :], ones], axis=0)
            acc_ref[h] = alpha * acc_ref[h] + _dot(v_aug, p)
            m_ref[h:h + 1, :] = m_new

    @pl.when(kt == qi)
    def _():
        step(lambda h: bias_ref[h, 0], lambda h: 0.0)

    @pl.when(kt == qi - 1)
    def _():
        step(lambda h: bias_ref[h, 1], lambda h: 0.0)

    @pl.when(kt < qi - 1)
    def _():
        step(None, lambda h: rb_ref[N_BUCKETS - 1, h] * LOG2E)

    @pl.when(kt == qi)
    def _():
        outs = []
        for h in range(N_ATTN_HEADS):
            a = acc_ref[h]
            outs.append(a[:dh, :] / a[dh:dh + 1, :])
        o_ref[0] = jnp.concatenate(outs, axis=0).T.astype(BF16)


def _attn_call(q, k, vt, mask, bias_tiles, rel_bias, t):
    B, S, W = q.shape
    n = S // t
    H = N_ATTN_HEADS
    qi_tab = jnp.asarray([i for i in range(n) for _ in range(i + 1)], I32)
    kt_tab = jnp.asarray([j for i in range(n) for j in range(i + 1)], I32)
    kern = functools.partial(_attn_kernel, t=t)
    grid_spec = pltpu.PrefetchScalarGridSpec(
        num_scalar_prefetch=2,
        grid=(B, int(qi_tab.shape[0])),
        in_specs=[pl.BlockSpec((1, t, W), lambda b, s, qt, kt: (b, qt[s], 0)),
                  pl.BlockSpec((1, t, W), lambda b, s, qt, kt: (b, kt[s], 0)),
                  pl.BlockSpec((1, W, t), lambda b, s, qt, kt: (b, 0, kt[s])),
                  pl.BlockSpec((1, 1, 1, t, t), lambda b, s, qt, kt: (b, qt[s], kt[s], 0, 0)),
                  pl.BlockSpec((H, 2, t, t), lambda b, s, qt, kt: (0, 0, 0, 0), pipeline_mode=pl.Buffered(1)),
                  pl.BlockSpec(memory_space=pltpu.SMEM)],
        out_specs=pl.BlockSpec((1, t, W), lambda b, s, qt, kt: (b, qt[s], 0)),
        scratch_shapes=[pltpu.VMEM((H, t, LANES), BF16),
                        pltpu.VMEM((H, t), F32),
                        pltpu.VMEM((H, ATTN_HEAD_DIM + ONES_ROWS, t), F32),
                        pltpu.VMEM((H, t, t), F32)])
    return pl.pallas_call(
        kern,
        grid_spec=grid_spec,
        out_shape=jax.ShapeDtypeStruct((B, S, W), BF16),
        compiler_params=_cparams(("parallel", "arbitrary")),
        name="attn",
    )(qi_tab, kt_tab, q, k, vt, mask, bias_tiles, rel_bias)


def _blockdiag_rows(x):
    lane = lax.broadcasted_iota(I32, x.shape, 1)
    zero = jnp.zeros_like(x)
    return jnp.concatenate([jnp.where(lane < RWKV_HEAD, x, zero),
                            jnp.where(lane >= RWKV_HEAD, x, zero)], axis=0)


def _rwkv_kernel(z_ref, mu_ref, w0_ref, dup_ref, a0_ref, iup_ref, gup_ref, kk_ref, ka_ref, rk_ref,
                 lng_ref, lnb_ref, seg_ref, tri_ref, o_ref,
                 prev_ref, st_ref, at_ref, rt_ref, bt_ref, kt_ref, bh_ref, kh_ref, v_ref, pc_ref, y_ref,
                 la_ref, lb_ref, mak_ref, arb_ref, ark_ref, wa_ref, wb_ref, g1_ref, g2_ref, h1_ref, h2_ref,
                 *, tt):
    j = pl.program_id(1)
    W = RWKV_W
    C = CHUNK
    nchunk = tt // C
    npair = RWKV_HEADS // 2

    @pl.when(j == 0)
    def _():
        prev_ref[...] = jnp.zeros(prev_ref.shape, F32)
        st_ref[...] = jnp.zeros(st_ref.shape, F32)

    z = z_ref[0]
    row = lax.broadcasted_iota(I32, z.shape, 0)
    z_prev = jnp.where(row == 0, prev_ref[...], pltpu.roll(z, 1, axis=0))
    prev_ref[...] = z[tt - 1:tt, :]
    z = z + mu_ref[...] * (z_prev - z)

    r = z[:, 0:W]
    k = z[:, W:2 * W]
    v = z[:, 2 * W:3 * W]
    wdad = z[:, 3 * W:3 * W + 2 * DECAY_LORA]
    gd = z[:, 3 * W + 2 * DECAY_LORA:]

    w_pre = w0_ref[...] + _dot3(jnp.tanh(wdad), dup_ref[...])
    neg = -w_pre
    softplus = jnp.maximum(neg, 0.0) + jnp.log(1.0 + jnp.exp(-jnp.abs(neg)))
    lw = -jnp.exp(-softplus - 0.5)
    a_lr = _sigmoid(a0_ref[...] + _dot(wdad.astype(BF16), iup_ref[...]))
    g = _dot(_sigmoid(gd).astype(BF16), gup_ref[...])

    seg = seg_ref[...]

    def head_sum(parts):
        half = seg.shape[0]
        cols = [sum(_dot(p[:, lo:lo + half], seg) for p in parts) for lo in range(0, W, half)]
        return jnp.concatenate(cols, axis=1)

    kk = k * kk_ref[...]
    kk = kk / jnp.maximum(jnp.sqrt(head_sum(_split2(kk * kk))), 1e-12)
    k2 = k * (1.0 + (a_lr - 1.0) * ka_ref[...])
    a_vec = -kk
    b_vec = kk * a_lr

    cum = _dot_exact_rhs_lhs(tri_ref[...], lw)
    tot = jnp.concatenate([jnp.broadcast_to(cum[(c + 1) * C - 1:(c + 1) * C, :], (C, W)) for c in range(nchunk)],
                          axis=0)
    p_inv = jnp.exp(-cum)
    p_out = jnp.exp(tot - cum)
    at_ref[...] = a_vec * jnp.exp(cum - lw)
    rt_ref[...] = r * jnp.exp(cum)
    bt_ref[...] = (b_vec * p_inv).astype(BF16)
    kt_ref[...] = (k2 * p_inv).astype(BF16)
    bh_ref[...] = b_vec * p_out
    kh_ref[...] = k2 * p_out
    v_ref[...] = v
    pc_ref[...] = jnp.exp(tot)

    t_i = lax.broadcasted_iota(I32, (C, LANES), 0)
    s_i = lax.broadcasted_iota(I32, (C, LANES), 1) % C
    strict = s_i < t_i
    incl = s_i <= t_i
    r_i = lax.broadcasted_iota(I32, (LANES, LANES), 0)
    c_i = lax.broadcasted_iota(I32, (LANES, LANES), 1)
    same_head = (r_i < RWKV_HEAD) == (c_i < RWKV_HEAD)
    diag = r_i == c_i
    nstage = int(math.log2(C))
    zero = jnp.zeros((C, LANES), F32)
    zsq = jnp.zeros((LANES, LANES), F32)
    units = [(c, p) for c in range(nchunk) for p in range(npair)]

    def sl(c, p):
        return slice(c * C, (c + 1) * C), slice(p * LANES, (p + 1) * LANES)

    def bd2(w):
        wb = w.astype(BF16)
        return jnp.concatenate([_blockdiag_rows(wb[:, :LANES]), _blockdiag_rows(wb[:, LANES:])], axis=1)

    for i, (c, p) in enumerate(units):
        rows, cols = sl(c, p)
        lhs = jnp.concatenate([at_ref[rows, cols], rt_ref[rows, cols]], axis=0).astype(BF16)
        rhs = jnp.concatenate([_blockdiag_rows(bt_ref[rows, cols]),
                               _blockdiag_rows(kt_ref[rows, cols])], axis=0)
        prod = _dot_nt(lhs, rhs)
        la_ref[i] = jnp.where(strict, prod[:C, :LANES], zero).astype(BF16)
        mak_ref[i] = jnp.where(strict, prod[:C, LANES:], zero).astype(BF16)
        arb_ref[i] = jnp.where(incl, prod[C:, :LANES], zero).astype(BF16)
        ark_ref[i] = jnp.where(incl, prod[C:, LANES:], zero).astype(BF16)
    for i, (c, p) in enumerate(units):
        rows, cols = sl(c, p)
        w2 = _dot(mak_ref[i], _blockdiag_rows(v_ref[rows, cols].astype(BF16)))
        wa_ref[i] = jnp.concatenate([at_ref[rows, cols], w2], axis=1)
    l_bufs, w_bufs = (la_ref, lb_ref), (wa_ref, wb_ref)
    for s in range(nstage):
        l_in, l_out = l_bufs[s % 2], l_bufs[(s + 1) % 2]
        w_in, w_out = w_bufs[s % 2], w_bufs[(s + 1) % 2]
        for i in range(len(units)):
            lmat = l_in[i]
            w = w_in[i]
            w_out[i] = w + _dot(lmat, bd2(w))
            if s < nstage - 1:
                l_out[i] = _dot(lmat, _blockdiag_rows(lmat)).astype(BF16)
    w_fin = w_bufs[nstage % 2]
    for i, (c, p) in enumerate(units):
        rows, cols = sl(c, p)
        w = w_fin[i]
        wb = w.astype(BF16)
        vb = v_ref[rows, cols].astype(BF16)
        gg = _dot(arb_ref[i], bd2(w))
        g1_ref[i] = (rt_ref[rows, cols] + gg[:, :LANES]).astype(BF16)
        g2_ref[i] = gg[:, LANES:] + _dot(ark_ref[i], _blockdiag_rows(vb))
        bk_t = jnp.concatenate([bh_ref[rows, cols], kh_ref[rows, cols]], axis=0).T
        hrhs = jnp.concatenate([wb, jnp.concatenate([jnp.zeros((C, LANES), BF16), vb], axis=1)], axis=0)
        hh = _dot(bk_t.astype(BF16), hrhs)
        pc = pc_ref[c * C:c * C + 1, cols]
        h1 = jnp.where(same_head, hh[:, :LANES], zsq) + jnp.where(diag, jnp.broadcast_to(pc, (LANES, LANES)), zsq)
        h1_ref[i] = h1.astype(BF16)
        h2_ref[i] = jnp.where(same_head, hh[:, LANES:], zsq)
    for c in range(nchunk):
        sts = [st_ref[p].astype(BF16) for p in range(npair)]
        for p in range(npair):
            i = c * npair + p
            rows, cols = sl(c, p)
            y_ref[rows, cols] = _dot(g1_ref[i], sts[p]) + g2_ref[i]
            st_ref[p] = _dot(h1_ref[i], sts[p]) + h2_ref[i]

    y = y_ref[...]
    inv_n = 1.0 / RWKV_HEAD
    mean = head_sum([y.astype(BF16)]) * inv_n
    yc = y - mean
    var = head_sum([(yc * yc).astype(BF16)]) * inv_n
    yn = yc * lax.rsqrt(var + GN_EPS) * lng_ref[...] + lnb_ref[...]
    bonus = head_sum([(r * k2 * rk_ref[...]).astype(BF16)]) * v
    o_ref[0] = ((yn + bonus) * g).astype(BF16)


def _dot_exact_rhs_lhs(ones_bf16, x):
    hi, mid, lo = _split3(x)
    return _dot(ones_bf16, hi) + _dot(ones_bf16, mid) + _dot(ones_bf16, lo)


def _rwkv_call(zr3, tshift_mu, decay_w0, decay_up, iclr_a0, iclr_up, gate_up, k_k, k_a, r_k, lnx_g, lnx_b):
    B, S, _ = zr3.shape
    tt = min(256, S)
    W = RWKV_W
    row = lambda a: a.reshape(1, -1).astype(F32)
    dup = jnp.concatenate([decay_up, jnp.zeros((ICLR_LORA, W), F32)], axis=0)
    iup = jnp.concatenate([jnp.zeros((DECAY_LORA, W), F32), iclr_up], axis=0)
    idx = jnp.arange(2 * LANES)
    seg = (idx[:, None] // RWKV_HEAD == idx[None, :] // RWKV_HEAD).astype(BF16)
    t = jnp.arange(tt)
    same_chunk = t[:, None] // CHUNK == t[None, :] // CHUNK
    tri = (same_chunk & (t[None, :] <= t[:, None])).astype(BF16)
    const = lambda shape: pl.BlockSpec(shape, lambda b, j: (0,) * len(shape))
    kern = functools.partial(_rwkv_kernel, tt=tt)
    nu = (tt // CHUNK) * (RWKV_HEADS // 2)
    return pl.pallas_call(
        kern,
        grid=(B, S // tt),
        in_specs=[pl.BlockSpec((1, tt, RWKV_IN), lambda b, j: (b, j, 0)),
                  const((1, RWKV_IN)), const((1, W)), const((2 * DECAY_LORA, W)), const((1, W)),
                  const((2 * ICLR_LORA, W)), const((GATE_LORA, W)), const((1, W)), const((1, W)),
                  const((1, W)), const((1, W)), const((1, W)),
                  const((2 * LANES, 2 * LANES)), const((tt, tt))],
        out_specs=pl.BlockSpec((1, tt, W), lambda b, j: (b, j, 0)),
        out_shape=jax.ShapeDtypeStruct((B, S, W), BF16),
        scratch_shapes=[pltpu.VMEM((1, RWKV_IN), F32),
                        pltpu.VMEM((RWKV_HEADS // 2, LANES, LANES), F32),
                        pltpu.VMEM((tt, W), F32),
                        pltpu.VMEM((tt, W), F32),
                        pltpu.VMEM((tt, W), BF16),
                        pltpu.VMEM((tt, W), BF16),
                        pltpu.VMEM((tt, W), F32),
                        pltpu.VMEM((tt, W), F32),
                        pltpu.VMEM((tt, W), F32),
                        pltpu.VMEM((tt, W), F32),
                        pltpu.VMEM((tt, W), F32),
                        pltpu.VMEM((nu, CHUNK, LANES), BF16),
                        pltpu.VMEM((nu, CHUNK, LANES), BF16),
                        pltpu.VMEM((nu, CHUNK, LANES), BF16),
                        pltpu.VMEM((nu, CHUNK, LANES), BF16),
                        pltpu.VMEM((nu, CHUNK, LANES), BF16),
                        pltpu.VMEM((nu, CHUNK, 2 * LANES), F32),
                        pltpu.VMEM((nu, CHUNK, 2 * LANES), F32),
                        pltpu.VMEM((nu, CHUNK, LANES), BF16),
                        pltpu.VMEM((nu, CHUNK, LANES), F32),
                        pltpu.VMEM((nu, LANES, LANES), BF16),
                        pltpu.VMEM((nu, LANES, LANES), F32)],
        compiler_params=_cparams(("parallel", "arbitrary")),
        name="rwkv",
    )(zr3, row(tshift_mu), row(decay_w0), dup, row(iclr_a0), iup.astype(BF16), gate_up.astype(BF16), row(k_k),
      row(k_a), row(r_k), row(lnx_g), row(lnx_b), seg, tri)


def _merge_kernel(x_ref, attn_ref, rw_ref, ga_ref, gr_ref, wa_ref, wr_ref, wo_ref, g1_ref,
                  n2_ref, sc_ref, sh_ref, rwt_ref, rb_ref, x1_ref, h2_ref, gt_ref, cnt_ref):
    a = _dot(attn_ref[...], wa_ref[...])
    rr = _dot(rw_ref[...], wr_ref[...])
    mixed = _gate_sigmoid(ga_ref[...].astype(F32)) * a + _gate_sigmoid(gr_ref[...].astype(F32)) * rr
    x1 = x_ref[...] + g1_ref[0] * _dot(mixed.astype(BF16), wo_ref[...])
    x1_ref[...] = x1
    ms = jnp.mean(x1 * x1, axis=-1, keepdims=True)
    h2 = x1 * lax.rsqrt(ms + RMS_EPS) * n2_ref[...]
    h2 = h2 * (1.0 + sc_ref[0]) + sh_ref[0]
    h2_ref[...] = _pack_bf16_pairs(h2)

    tm = x1.shape[0]
    E, G, EG = N_EXPERTS, N_GROUPS, N_EXPERTS // N_GROUPS
    scores = _sigmoid(_dot3(rwt_ref[...], h2, nt=True))
    choice = scores + rb_ref[...]
    c3 = choice.reshape(G, EG, tm)
    e_i = lax.broadcasted_iota(I32, (G, EG, tm), 1)
    m1 = jnp.max(c3, axis=1, keepdims=True)
    first = jnp.min(jnp.where(c3 == m1, e_i, EG), axis=1, keepdims=True)
    m2 = jnp.max(jnp.where(e_i == first, -jnp.inf, c3), axis=1, keepdims=True)
    grp = (m1 + m2).reshape(G, tm)
    g_i = lax.broadcasted_iota(I32, (G, tm), 0)
    rank = jnp.zeros((G, tm), I32)
    for o in range(G):
        other = grp[o:o + 1, :]
        rank = rank + jnp.where((other > grp) | ((other == grp) & (o < g_i)), 1, 0)
    gsel = rank < TOPK_GROUPS
    esel = jnp.broadcast_to(gsel.reshape(G, 1, tm), (G, EG, tm)).reshape(E, tm)
    mc = jnp.where(esel, choice, -jnp.inf)
    x_i = lax.broadcasted_iota(I32, (E, tm), 0)
    cur = mc
    picked = jnp.zeros((E, tm), F32)
    for _ in range(MOE_TOPK):
        best = jnp.max(cur, axis=0, keepdims=True)
        first = jnp.min(jnp.where(cur == best, x_i, E), axis=0, keepdims=True)
        hit = x_i == first
        picked = jnp.where(hit, 1.0, picked)
        cur = jnp.where(hit, -jnp.inf, cur)
    gw = jnp.where(picked > 0.0, scores, 0.0)
    gw = gw / jnp.sum(gw, axis=0, keepdims=True) * ROUTED_SCALE
    gt_ref[...] = gw
    sel = jnp.where(gw > 0.0, 1.0, 0.0)
    cnt_ref[0] = jnp.broadcast_to(jnp.sum(sel, axis=1, keepdims=True), (E, LANES))


def _pack_bf16_pairs(x):
    n = x.shape[1] // 2
    bits = pltpu.bitcast(x.astype(BF16).astype(F32), I32)
    return bits[:, :n] | lax.shift_right_logical(bits[:, n:], 16)


def _unpack_bf16_pairs(p):
    hi = pltpu.bitcast(p & jnp.int32(-65536), F32)
    lo = pltpu.bitcast(lax.shift_left(p, 16), F32)
    return jnp.concatenate([hi, lo], axis=1).astype(BF16)


def _merge_call(x2, attn, rw, ga, gr, wa, wr, wo, g1, norm2_g, sc2, sh2, router_w, router_bias, S):
    T, D = x2.shape
    B = T // S
    tm = min(MERGE_TILE, S)
    tpb = S // tm
    nt = T // tm
    E = N_EXPERTS
    row = lambda i: (i, 0)
    per_b = lambda i: (i // tpb, 0, 0)
    const = lambda shape: pl.BlockSpec(shape, lambda i: (0,) * len(shape))
    return pl.pallas_call(
        _merge_kernel,
        grid=(nt,),
        in_specs=[pl.BlockSpec((tm, D), row), pl.BlockSpec((tm, ATTN_W), row), pl.BlockSpec((tm, RWKV_W), row),
                  pl.BlockSpec((tm, D), row), pl.BlockSpec((tm, D), row),
                  const((ATTN_W, D)), const((RWKV_W, D)), const((D, D)),
                  pl.BlockSpec((1, 1, D), per_b), const((1, D)),
                  pl.BlockSpec((1, 1, D), per_b), pl.BlockSpec((1, 1, D), per_b),
                  const((E, D)), const((E, 1))],
        out_specs=[pl.BlockSpec((tm, D), row), pl.BlockSpec((tm, D // 2), row),
                   pl.BlockSpec((E, tm), lambda i: (0, i)), pl.BlockSpec((1, E, LANES), lambda i: (i, 0, 0))],
        out_shape=[jax.ShapeDtypeStruct((T, D), F32), jax.ShapeDtypeStruct((T, D // 2), I32),
                   jax.ShapeDtypeStruct((E, T), F32), jax.ShapeDtypeStruct((nt, E, LANES), F32)],
        compiler_params=_cparams(("parallel",)),
        name="merge",
    )(x2, attn, rw, ga, gr, wa, wr, wo, g1.reshape(B, 1, D), norm2_g.reshape(1, D),
      sc2.reshape(B, 1, D), sh2.reshape(B, 1, D), router_w.T, router_bias.reshape(E, 1))


def _plan_kernel(gt_ref, cnt_ref, upper_ref, lowe_ref, dest_ref, gw_ref, be_ref, off_ref, *, tm, n_blocks):
    i = pl.program_id(0)
    E = N_EXPERTS
    lowe = lowe_ref[...]

    @pl.when(i == 0)
    def _():
        total = jnp.sum(cnt_ref[...], axis=0)
        nblk = jnp.floor((total + (MOE_BLOCK - 1)) * (1.0 / MOE_BLOCK))
        start_blk = _dot_exact_rhs_lhs(lowe, nblk)
        off_ref[...] = start_blk * MOE_BLOCK
        end_blk = start_blk + nblk
        b_i = lax.broadcasted_iota(I32, (E, n_blocks), 1).astype(F32)
        e_of_b = jnp.sum(jnp.where(end_blk[:, :1] <= b_i, 1.0, 0.0), axis=0, keepdims=True)
        be_ref[...] = e_of_b.astype(I32)

    gt = gt_ref[...]
    sel = gt > 0.0
    selb = jnp.where(sel, 1.0, 0.0).astype(BF16)
    rank = _dot(selb, upper_ref[...])
    dest = off_ref[:, :1] + rank
    off_ref[...] = off_ref[...] + cnt_ref[i]
    kth = _dot(lowe, selb)
    dests, gws = [], []
    for k in range(MOE_TOPK):
        m = sel & (kth == float(k))
        have = jnp.sum(jnp.where(m, 1.0, 0.0), axis=0, keepdims=True)
        d = jnp.sum(jnp.where(m, dest, 0.0), axis=0, keepdims=True)
        dests.append(jnp.where(have > 0.0, d, float((n_blocks - 1) * MOE_BLOCK)))
        gws.append(jnp.sum(jnp.where(m, gt, 0.0), axis=0, keepdims=True))
    dest_ref[...] = jnp.concatenate(dests, axis=0).astype(I32)
    gpad = jnp.concatenate(gws + [jnp.zeros((LANES - MOE_TOPK, tm), F32)], axis=0)
    gw_ref[...] = gpad.T


def _plan_call(gate_t, cnt, n_blocks, tile0):
    E = gate_t.shape[0]
    nt = cnt.shape[0]
    tm = MERGE_TILE
    T = nt * tm
    idx = jnp.arange(tm)
    upper = (idx[:, None] < idx[None, :]).astype(BF16)
    ei = jnp.arange(E)
    lowe = (ei[None, :] < ei[:, None]).astype(BF16)
    kern = functools.partial(_plan_kernel, tm=tm, n_blocks=n_blocks)
    const = lambda shape: pl.BlockSpec(shape, lambda i: (0,) * len(shape))
    return pl.pallas_call(
        kern,
        grid=(nt,),
        in_specs=[pl.BlockSpec((E, tm), lambda i: (0, i + tile0)), const((nt, E, LANES)), const((tm, tm)),
                  const((E, E))],
        out_specs=[pl.BlockSpec((MOE_TOPK, tm), lambda i: (0, i)), pl.BlockSpec((tm, LANES), lambda i: (i, 0)),
                   const((1, n_blocks))],
        out_shape=[jax.ShapeDtypeStruct((MOE_TOPK, T), I32), jax.ShapeDtypeStruct((T, LANES), F32),
                   jax.ShapeDtypeStruct((1, n_blocks), I32)],
        scratch_shapes=[pltpu.VMEM((E, LANES), F32)],
        compiler_params=_cparams(("arbitrary",)),
        name="plan",
    )(gate_t, cnt, upper, lowe)


def _sc_index_layout(dest_t):
    K, T = dest_t.shape
    n_ch = T // (SC_WORKERS * SC_ROWS)
    return dest_t.reshape(K, SC_WORKERS, n_ch, SC_ROWS).transpose(1, 2, 0, 3).reshape(SC_WORKERS, n_ch * K, SC_ROWS)


def _sc_dispatch(rows, idx, n_slots, tok0):
    W = rows.shape[1]
    n_ch = idx.shape[1] // MOE_TOPK
    T = n_ch * SC_WORKERS * SC_ROWS
    tpw = T // SC_WORKERS
    mesh = plsc.VectorSubcoreMesh(core_axis_name="c", subcore_axis_name="s")

    @functools.partial(
        pl.kernel, mesh=mesh,
        out_type=jax.ShapeDtypeStruct((n_slots, W), I32),
        scratch_types=[pltpu.VMEM((n_ch * MOE_TOPK, SC_ROWS), I32), pltpu.VMEM((SC_ROWS, W), I32),
                       pltpu.SemaphoreType.DMA])
    def kern(x_hbm, idx_hbm, o_hbm, idx_v, rows_v, sem):
        wid = lax.axis_index("s") * SC_CORES + lax.axis_index("c")
        pltpu.sync_copy(idx_hbm.at[wid], idx_v)

        @pl.loop(0, n_ch)
        def _(j):
            pltpu.sync_copy(x_hbm.at[pl.ds(tok0 + wid * tpw + j * SC_ROWS, SC_ROWS)], rows_v)
            copies = [pltpu.async_copy(rows_v, o_hbm.at[idx_v.at[j * MOE_TOPK + k]], sem)
                      for k in range(MOE_TOPK)]
            for cp in copies:
                cp.wait()

    return kern(rows, idx)


def _sc_combine(slots, idx, T):
    _, W = slots.shape
    n_ch = T // (SC_WORKERS * SC_ROWS)
    tpw = T // SC_WORKERS
    mesh = plsc.VectorSubcoreMesh(core_axis_name="c", subcore_axis_name="s")

    @functools.partial(
        pl.kernel, mesh=mesh,
        out_type=jax.ShapeDtypeStruct((MOE_TOPK, T, W), I32),
        scratch_types=[pltpu.VMEM((n_ch * MOE_TOPK, SC_ROWS), I32), pltpu.VMEM((SC_ROWS, W), I32),
                       pltpu.SemaphoreType.DMA])
    def kern(s_hbm, idx_hbm, o_hbm, idx_v, rows_v, sem):
        wid = lax.axis_index("s") * SC_CORES + lax.axis_index("c")
        pltpu.sync_copy(idx_hbm.at[wid], idx_v)

        @pl.loop(0, n_ch)
        def _(j):
            for k in range(MOE_TOPK):
                pltpu.async_copy(s_hbm.at[idx_v.at[j * MOE_TOPK + k]], rows_v, sem).wait()
                pltpu.sync_copy(rows_v, o_hbm.at[k, pl.ds(wid * tpw + j * SC_ROWS, SC_ROWS)])

    return kern(slots, idx)


def _ffn_kernel(be_ref, x_ref, eg_ref, eu_ref, ed_ref, o_ref):
    used = be_ref[pl.program_id(0)] < N_EXPERTS

    @pl.when(used)
    def _():
        x = _unpack_bf16_pairs(x_ref[...])
        a = _dot(x, eg_ref[0].astype(BF16))
        u = _dot(x, eu_ref[0].astype(BF16))
        o_ref[...] = _pack_bf16_pairs(_dot((a * _gate_sigmoid(a) * u).astype(BF16), ed_ref[0].astype(BF16)))

    @pl.when(jnp.logical_not(used))
    def _():
        o_ref[...] = jnp.zeros(o_ref.shape, I32)


def _ffn_call(xs, block_e, eg, eu, ed, n_blocks):
    P, W = xs.shape
    D, FF = 2 * W, EXPERT_FF
    grid_spec = pltpu.PrefetchScalarGridSpec(
        num_scalar_prefetch=1,
        grid=(n_blocks,),
        in_specs=[pl.BlockSpec((MOE_BLOCK, W), lambda b, be: (b, 0)),
                  pl.BlockSpec((1, D, FF), lambda b, be: (jnp.minimum(be[b], N_EXPERTS - 1), 0, 0)),
                  pl.BlockSpec((1, D, FF), lambda b, be: (jnp.minimum(be[b], N_EXPERTS - 1), 0, 0)),
                  pl.BlockSpec((1, FF, D), lambda b, be: (jnp.minimum(be[b], N_EXPERTS - 1), 0, 0))],
        out_specs=pl.BlockSpec((MOE_BLOCK, W), lambda b, be: (b, 0)))
    return pl.pallas_call(
        _ffn_kernel,
        grid_spec=grid_spec,
        out_shape=jax.ShapeDtypeStruct((P, W), I32),
        compiler_params=_cparams(("parallel",)),
        name="ffn",
    )(block_e, xs, eg, eu, ed)


def _final_kernel(h_ref, c_ref, gw_ref, x1_ref, g2_ref, fg_ref, sg_ref, su_ref, sd_ref, o_ref):
    h = _unpack_bf16_pairs(h_ref[...])
    a = _dot(h, sg_ref[...])
    u = _dot(h, su_ref[...])
    moe = _dot((a * _gate_sigmoid(a) * u).astype(BF16), sd_ref[...])
    gw = gw_ref[...]
    for k in range(MOE_TOPK):
        w = gw[:, k:k + 1]
        y = _unpack_bf16_pairs(c_ref[k]).astype(F32)
        moe = moe + jnp.where(w > 0.0, w * y, 0.0)
    x2 = x1_ref[...] + g2_ref[0] * moe
    ms = jnp.mean(x2 * x2, axis=-1, keepdims=True)
    o_ref[...] = x2 * lax.rsqrt(ms + RMS_EPS) * fg_ref[...]


def _final_call(h2p, comb, gw, x1, g2, final_g, sg, su, sd, S, tile0, prev_out):
    T, W = h2p.shape
    D, FF = 2 * W, EXPERT_FF
    B = T // S
    tm = MERGE_TILE
    tpb = S // tm
    ntile = comb.shape[1] // tm
    row = lambda i: (i, 0)
    full_row = lambda i: (i + tile0, 0)
    const = lambda shape: pl.BlockSpec(shape, lambda i: (0,) * len(shape))
    in_specs = [pl.BlockSpec((tm, W), full_row), pl.BlockSpec((MOE_TOPK, tm, W), lambda i: (0, i, 0)),
                pl.BlockSpec((tm, LANES), row), pl.BlockSpec((tm, D), full_row),
                pl.BlockSpec((1, 1, D), lambda i: ((i + tile0) // tpb, 0, 0)), const((1, D)),
                const((D, FF)), const((D, FF)), const((FF, D))]
    args = [h2p, comb, gw, x1, g2.reshape(B, 1, D), final_g.reshape(1, D), sg, su, sd]
    kern, aliases = _final_kernel, {}
    if prev_out is not None:
        in_specs.append(pl.BlockSpec(memory_space=pl.ANY))
        args.append(prev_out)
        aliases = {len(args) - 1: 0}
        kern = lambda *refs: _final_kernel(*refs[:9], refs[10])
    return pl.pallas_call(
        kern,
        grid=(ntile,),
        in_specs=in_specs,
        out_specs=pl.BlockSpec((tm, D), full_row),
        out_shape=jax.ShapeDtypeStruct((T, D), F32),
        input_output_aliases=aliases,
        compiler_params=_cparams(("parallel",)),
        name="final",
    )(*args)


def _moe_call(h2p, gate_t, cnt, x1, g2, final_g, eg, eu, ed, sg, su, sd, S):
    T = h2p.shape[0]
    nt = cnt.shape[0]
    groups = MOE_GROUPS if (T // MOE_GROUPS) % (SC_WORKERS * SC_ROWS) == 0 and nt % MOE_GROUPS == 0 else 1
    tg, ntg = T // groups, nt // groups
    n_blocks = (tg * MOE_TOPK) // MOE_BLOCK + N_EXPERTS + 1
    out = None
    for g in range(groups):
        dest_t, gw, block_e = _plan_call(gate_t, cnt[g * ntg:(g + 1) * ntg], n_blocks, g * ntg)
        idx = _sc_index_layout(dest_t)
        xs = _sc_dispatch(h2p, idx, n_blocks * MOE_BLOCK, g * tg)
        ys = _ffn_call(xs, block_e.reshape(n_blocks), eg, eu, ed, n_blocks)
        comb = _sc_combine(ys, idx, tg)
        out = _final_call(h2p, comb, gw, x1, g2, final_g, sg, su, sd, S, g * ntg, out)
    return out


def _layer(x2, c, S, ada_w, ada_b, norm1_g, w_in, rel_bias, tshift_mu, decay_w0, decay_up, iclr_a0, iclr_up,
           gate_up, k_k, k_a, r_k, lnx_g, lnx_b, w_attn_br, w_rwkv_br, w_out, norm2_g, router_w, router_bias,
           exp_gate, exp_up, exp_down, sh_gate, sh_up, sh_down, final_g):
    T, D = x2.shape
    B = T // S
    mod = _mod_call(c, ada_w, ada_b)
    sh1, sc1, g1, sh2, sc2, g2 = jnp.split(mod, 6, axis=-1)

    q, k, vt, iq, ik4, iwt, zr, ga, gr = _inproj_call(x2, norm1_g, sc1, sh1, w_in, S)

    ta = min(512, S)
    assert ta >= LANES and S % ta == 0
    top_k = min(TOPK_MAX, S // 4)
    seq = lambda a: a.reshape(B, S, a.shape[-1])
    mask = _index_call(seq(iq), iwt, seq(ik4), ta, top_k)
    bias_tiles = _bias_call(rel_bias, ta)
    attn = _attn_call(seq(q), seq(k), vt, mask, bias_tiles, rel_bias, ta).reshape(T, ATTN_W)

    rw = _rwkv_call(zr.reshape(B, S, RWKV_IN), tshift_mu, decay_w0, decay_up, iclr_a0, iclr_up, gate_up,
                    k_k, k_a, r_k, lnx_g, lnx_b).reshape(T, RWKV_W)

    x1, h2p, gate_t, cnt = _merge_call(x2, attn, rw, ga, gr, w_attn_br.astype(BF16), w_rwkv_br.astype(BF16),
                                       w_out.astype(BF16), g1, norm2_g, sc2, sh2, router_w, router_bias, S)
    return _moe_call(h2p, gate_t, cnt, x1, g2, final_g, exp_gate, exp_up, exp_down,
                     sh_gate.astype(BF16), sh_up.astype(BF16), sh_down.astype(BF16), S)


def kernel(x, c, ada_w, ada_b, norm1_g, w_in, rel_bias, tshift_mu, decay_w0, decay_up, iclr_a0, iclr_up, gate_up, k_k, k_a, r_k, lnx_g, lnx_b, w_attn_br, w_rwkv_br, w_out, norm2_g, router_w, router_bias, exp_gate, exp_up, exp_down, sh_gate, sh_up, sh_down, final_g):
    B, S, D = x.shape
    depth = ada_w.shape[0]
    assert depth == 1, "the final RMSNorm is fused into the (single) layer's MoE kernel"
    out = _layer(x.reshape(B * S, D), c, S, ada_w[0], ada_b[0], norm1_g[0], w_in[0], rel_bias, tshift_mu[0],
                 decay_w0[0], decay_up[0], iclr_a0[0], iclr_up[0], gate_up[0], k_k[0], k_a[0], r_k[0],
                 lnx_g[0], lnx_b[0], w_attn_br[0], w_rwkv_br[0], w_out[0], norm2_g[0], router_w[0],
                 router_bias[0], exp_gate[0], exp_up[0], exp_down[0], sh_gate[0], sh_up[0], sh_down[0], final_g)
    return out.reshape(B, S, D)
```
